```python
import math
import jax, jax.numpy as jnp
from jax import lax
import numpy as np

D_MODEL = 1024
BATCH = 8
SEQ = 2048
DEPTH = 4
DEC_BATCH = 32
DEC_SEQ = 64
PAST_LEN = 2048

CHUNK = 64
N_MIXERS = 3
EPS = 1e-6
NEG = -1e30

A_HEADS = 16
A_HEAD_DIM = 64
IDX_HEADS = 8
IDX_DIM = 64
TOPK_MAX = 256
Q_BLOCK = 128
N_BUCKETS = 32
MAX_DISTANCE = 128
A_SCALE = A_HEAD_DIM ** -0.5
IDX_SCALE = (IDX_DIM ** -0.5) * (IDX_HEADS ** -0.5)
A_IN = 3 * A_HEADS * A_HEAD_DIM + IDX_HEADS * IDX_DIM + IDX_DIM + IDX_HEADS

B_HEADS = 8
B_QK_DIM = 64
B_V_DIM = 128
B_MAIN = B_HEADS * (2 * B_QK_DIM + 2 * B_V_DIM)
B_IN = B_MAIN + 2 * B_HEADS

C_HEADS = 8
C_DK = 128
C_DV = 128
CONV_W = 4
C_CONV_DIM = C_HEADS * (2 * C_DK + C_DV)
C_MAIN = C_CONV_DIM + C_HEADS * C_DV
C_IN = C_MAIN + 2 * C_HEADS

MEM_LEN = 256
MEM_HEADS = 4
MEM_HEAD_DIM = 128
MEM_WIDTH = MEM_HEADS * MEM_HEAD_DIM

FFN_HIDDEN = -(-8 * D_MODEL // (3 * 256)) * 256

kernel_name = "hybrid_dsa_mlstm_gdn_streaming_step"


def rms_norm(x, g):
    xf = x.astype(jnp.float32)
    y = xf * lax.rsqrt(jnp.mean(xf * xf, axis=-1, keepdims=True) + EPS)
    return (y * g.astype(jnp.float32)).astype(x.dtype)


def l2_norm(x):
    return x * lax.rsqrt(jnp.sum(x * x, axis=-1, keepdims=True) + EPS)


def t5_bucket(rel):
    half = N_BUCKETS // 2
    max_exact = half // 2
    n = jnp.abs(rel)
    nf = jnp.maximum(n, 1).astype(jnp.float32)
    large = max_exact + (jnp.log(nf / max_exact) / math.log(MAX_DISTANCE / max_exact)
                         * (half - max_exact)).astype(jnp.int32)
    large = jnp.minimum(large, half - 1)
    return jnp.where(rel > 0, half, 0) + jnp.where(n < max_exact, n, large)


def to_chunks(t, L):
    B, H, T = t.shape[:3]
    return jnp.moveaxis(t.reshape(B, H, T // L, L, *t.shape[3:]), 2, 0)


def from_chunks(t):
    nc, B, H, L = t.shape[:4]
    return jnp.moveaxis(t, 0, 2).reshape(B, H, nc * L, *t.shape[4:])


def swap_th(t):
    return jnp.swapaxes(t, 1, 2)


def dsa_sequence(q, k, v, qi, ki, wi, q_pos, k_pos, rel_bias, topk):
    T = q.shape[0]
    qb = min(Q_BLOCK, T)
    nb = T // qb
    k_chunk = k_pos // CHUNK
    kif = ki.astype(jnp.float32)
    bias_tab = rel_bias.astype(jnp.float32)

    def block(args):
        q_b, qi_b, wi_b, qp_b = args
        q_chunk = qp_b // CHUNK
        rel = jnp.einsum("qhd,sd->qhs", qi_b.astype(jnp.float32), kif)
        score = jnp.einsum("qh,qhs->qs", wi_b.astype(jnp.float32), jax.nn.relu(rel)) * IDX_SCALE
        score = jnp.where(k_chunk[None, :] <= q_chunk[:, None], score, NEG)
        _, idx = lax.top_k(score, topk)
        kg = k[idx]
        vg = v[idx]
        kp = k_pos[idx]
        logits = jnp.einsum("qhd,qkhd->qhk", q_b, kg).astype(jnp.float32) * A_SCALE
        bias = bias_tab[t5_bucket(kp - qp_b[:, None])]
        logits = logits + jnp.swapaxes(bias, 1, 2)
        ok = (kp // CHUNK) <= q_chunk[:, None]
        logits = jnp.where(ok[:, None, :], logits, NEG)
        p = jax.nn.softmax(logits, axis=-1).astype(vg.dtype)
        return jnp.einsum("qhk,qkhd->qhd", p, vg)

    out = lax.map(block, (q.reshape(nb, qb, *q.shape[1:]),
                          qi.reshape(nb, qb, *qi.shape[1:]),
                          wi.reshape(nb, qb, wi.shape[-1]),
                          q_pos.reshape(nb, qb)))
    return out.reshape(q.shape)


def dsa_mixer(h, q_pos, k_cache, v_cache, ki_cache, w_in, w_out, q_gain, k_gain, ki_gain, rel_bias):
    B, T, _ = h.shape
    hq = A_HEADS * A_HEAD_DIM
    o3 = 3 * hq
    o4 = o3 + IDX_HEADS * IDX_DIM
    o5 = o4 + IDX_DIM
    proj = h @ w_in
    q = rms_norm(proj[..., :hq].reshape(B, T, A_HEADS, A_HEAD_DIM), q_gain)
    k = rms_norm(proj[..., hq:2 * hq].reshape(B, T, A_HEADS, A_HEAD_DIM), k_gain)
    v = proj[..., 2 * hq:o3].reshape(B, T, A_HEADS, A_HEAD_DIM)
    qi = proj[..., o3:o4].reshape(B, T, IDX_HEADS, IDX_DIM)
    ki = rms_norm(proj[..., o4:o5], ki_gain)
    wi = proj[..., o5:]
    if k_cache is None:
        k_all, v_all, ki_all = k, v, ki
    else:
        k_all = jnp.concatenate([k_cache.astype(k.dtype), k], axis=1)
        v_all = jnp.concatenate([v_cache.astype(v.dtype), v], axis=1)
        ki_all = jnp.concatenate([ki_cache.astype(ki.dtype), ki], axis=1)
    L = k_all.shape[1]
    k_pos = jnp.arange(L, dtype=jnp.int32)
    topk = min(TOPK_MAX, L // 4)
    out = lax.map(lambda a: dsa_sequence(*a, q_pos, k_pos, rel_bias, topk),
                  (q, k_all, v_all, qi, ki_all, wi))
    y = out.reshape(B, T, hq) @ w_out
    return y, k, v, ki


def mlstm_chunk_scan(q, k, v, logi, logf, C0, n0, m0):
    T = q.shape[2]
    L = min(CHUNK, T)
    tril = jnp.tril(jnp.ones((L, L), dtype=bool))

    def step(carry, xs):
        C, n, m = carry
        qc, kc, vc, ic, fc = xs
        b = jnp.cumsum(fc, axis=-1)
        d = jnp.where(tril, b[..., :, None] - b[..., None, :] + ic[..., None, :], NEG)
        inter = b + m[..., None]
        mt = jnp.maximum(inter, jnp.max(d, axis=-1))
        s = jnp.einsum("bhtd,bhsd->bhts", qc, kc) * jnp.exp(d - mt[..., None])
        w_state = jnp.exp(inter - mt)
        num = jnp.einsum("bhts,bhsv->bhtv", s, vc) + w_state[..., None] * jnp.einsum("bhtd,bhdv->bhtv", qc, C)
        den = jnp.sum(s, axis=-1) + w_state * jnp.einsum("bhtd,bhd->bht", qc, n)
        hs = num / jnp.maximum(jnp.abs(den), jnp.exp(-mt))[..., None]
        b_last = b[..., -1]
        dec = b_last[..., None] - b + ic
        m_new = jnp.maximum(b_last + m, jnp.max(dec, axis=-1))
        wk = jnp.exp(dec - m_new[..., None])
        ws = jnp.exp(b_last + m - m_new)
        C_new = ws[..., None, None] * C + jnp.einsum("bhs,bhsd,bhsv->bhdv", wk, kc, vc)
        n_new = ws[..., None] * n + jnp.einsum("bhs,bhsd->bhd", wk, kc)
        return (C_new, n_new, m_new), hs

    (C, n, m), hs = lax.scan(step, (C0, n0, m0),
                             tuple(to_chunks(t, L) for t in (q, k, v, logi, logf)))
    return from_chunks(hs), C, n, m


def mlstm_mixer(h, C0, n0, m0, w_in, gate_bias, h_gain, w_out):
    B, T, _ = h.shape
    nq = B_HEADS * B_QK_DIM
    nv = B_HEADS * B_V_DIM
    proj = (h @ w_in).astype(jnp.float32)
    q = proj[..., :nq].reshape(B, T, B_HEADS, B_QK_DIM)
    k = proj[..., nq:2 * nq].reshape(B, T, B_HEADS, B_QK_DIM) * (B_QK_DIM ** -0.5)
    v = proj[..., 2 * nq:2 * nq + nv].reshape(B, T, B_HEADS, B_V_DIM)
    o = proj[..., 2 * nq + nv:B_MAIN].reshape(B, T, B_HEADS, B_V_DIM)
    gates = proj[..., B_MAIN:] + gate_bias.astype(jnp.float32)
    logi = gates[..., :B_HEADS]
    logf = jax.nn.log_sigmoid(gates[..., B_HEADS:])
    hs, C, n, m = mlstm_chunk_scan(swap_th(q), swap_th(k), swap_th(v), swap_th(logi), swap_th(logf),
                                   C0.astype(jnp.float32), n0.astype(jnp.float32), m0.astype(jnp.float32))
    hs = rms_norm(swap_th(hs), h_gain) * jax.nn.sigmoid(o)
    y = hs.reshape(B, T, nv).astype(h.dtype) @ w_out
    return y, C, n, m


def gdn_chunk_scan(q, k, v, g, beta, S0):
    T = q.shape[2]
    L = min(CHUNK, T)
    incl = jnp.tril(jnp.ones((L, L), dtype=bool))
    strict = jnp.tril(jnp.ones((L, L), dtype=bool), -1)
    eye = jnp.eye(L, dtype=jnp.float32)

    def step(S, xs):
        qc, kc, vc, gc, bc = xs
        gcum = jnp.cumsum(gc, axis=-1)
        diff = gcum[..., :, None] - gcum[..., None, :]
        dm = jnp.where(incl, jnp.exp(jnp.where(incl, diff, 0.0)), 0.0)
        kb = kc * bc[..., None]
        a_mat = eye + jnp.where(strict, jnp.einsum("bhid,bhjd->bhij", kb, kc) * dm, 0.0)
        rhs = jnp.concatenate([vc * bc[..., None], kb * jnp.exp(gcum)[..., None]], axis=-1)
        sol = lax.linalg.triangular_solve(a_mat, rhs, left_side=True, lower=True, unit_diagonal=True)
        u, w = sol[..., :C_DV], sol[..., C_DV:]
        v_new = u - jnp.einsum("bhid,bhdv->bhiv", w, S)
        attn = jnp.einsum("bhid,bhjd->bhij", qc, kc) * dm
        o = (jnp.einsum("bhid,bhdv->bhiv", qc * jnp.exp(gcum)[..., None], S)
             + jnp.einsum("bhij,bhjv->bhiv", attn, v_new))
        g_last = gcum[..., -1:]
        S_new = (S * jnp.exp(g_last)[..., None]
                 + jnp.einsum("bhjd,bhjv->bhdv", kc * jnp.exp(g_last - gcum)[..., None], v_new))
        return S_new, o

    S, o = lax.scan(step, S0, tuple(to_chunks(t, L) for t in (q, k, v, g, beta)))
    return from_chunks(o), S


def gdn_mixer(h, S0, conv_buf, w_in, conv_w, a_log, dt_bias, o_gain, w_out):
    B, T, _ = h.shape
    proj = h @ w_in
    qkv = proj[..., :C_CONV_DIM]
    z = proj[..., C_CONV_DIM:C_MAIN]
    beta_pre = proj[..., C_MAIN:C_MAIN + C_HEADS]
    a_pre = proj[..., C_MAIN + C_HEADS:]
    xp = jnp.concatenate([conv_buf.astype(qkv.dtype), qkv], axis=1)
    conv = sum(xp[:, j:j + T] * conv_w[j] for j in range(CONV_W))
    new_buf = xp[:, T:]
    cf = jax.nn.silu(conv).astype(jnp.float32)
    hk = C_HEADS * C_DK
    q = l2_norm(cf[..., :hk].reshape(B, T, C_HEADS, C_DK)) * (C_DK ** -0.5)
    k = l2_norm(cf[..., hk:2 * hk].reshape(B, T, C_HEADS, C_DK))
    v = cf[..., 2 * hk:].reshape(B, T, C_HEADS, C_DV)
    beta = jax.nn.sigmoid(beta_pre.astype(jnp.float32))
    g = -jnp.exp(a_log.astype(jnp.float32)) * jax.nn.softplus(a_pre.astype(jnp.float32) + dt_bias.astype(jnp.float32))
    o, S = gdn_chunk_scan(swap_th(q), swap_th(k), swap_th(v), swap_th(g), swap_th(beta), S0.astype(jnp.float32))
    o = rms_norm(swap_th(o), o_gain) * jax.nn.silu(z.astype(jnp.float32).reshape(B, T, C_HEADS, C_DV))
    y = o.reshape(B, T, C_HEADS * C_DV).astype(h.dtype) @ w_out
    return y, S, new_buf


def mem_kv(mem, mem_gain, w_mk, w_mv, mk_gain):
    B, M, _ = mem.shape
    m = rms_norm(mem, mem_gain)
    k = rms_norm((m @ w_mk).reshape(B, M, MEM_HEADS, MEM_HEAD_DIM), mk_gain)
    v = (m @ w_mv).reshape(B, M, MEM_HEADS, MEM_HEAD_DIM)
    return k, v


def mem_attend(h, mk, mv, w_mq, mq_gain, w_mo):
    B, T, _ = h.shape
    q = rms_norm((h @ w_mq).reshape(B, T, MEM_HEADS, MEM_HEAD_DIM), mq_gain)
    logits = jnp.einsum("bthd,bmhd->bhtm", q, mk.astype(q.dtype)).astype(jnp.float32) * (MEM_HEAD_DIM ** -0.5)
    p = jax.nn.softmax(logits, axis=-1).astype(h.dtype)
    o = jnp.einsum("bhtm,bmhd->bthd", p, mv.astype(h.dtype))
    return o.reshape(B, T, MEM_WIDTH) @ w_mo


def swiglu(h, w1, w3, w2):
    return (jax.nn.silu(h @ w1) * (h @ w3)) @ w2


def trunk(x, q_pos, mem_k, mem_v, st, P):
    new = {}
    for i in range(DEPTH):
        h = rms_norm(x, P["norm_mix"][i])
        kind = i % N_MIXERS
        if kind == 0:
            y, *ns = dsa_mixer(h, q_pos, *st[i], *P["mixer"][i], P["rel_bias"])
        elif kind == 1:
            y, *ns = mlstm_mixer(h, *st[i], *P["mixer"][i])
        else:
            y, *ns = gdn_mixer(h, *st[i], *P["mixer"][i])
        new[i] = [t.astype(x.dtype) for t in ns]
        x = x + y.astype(x.dtype)
        h = rms_norm(x, P["norm_mem"][i])
        x = x + mem_attend(h, mem_k[i], mem_v[i], P["w_mq"][i], P["mq_gain"][i], P["w_mo"][i])
        h = rms_norm(x, P["norm_ffn"][i])
        x = x + swiglu(h, P["w_ffn1"][i], P["w_ffn3"][i], P["w_ffn2"][i])
    return x, new


def setup_inputs(seed: int = 0) -> dict:
    key = jax.random.key(seed)
    ks = iter(jax.random.split(key, 96))
    D = D_MODEL

    def nrm(shape, scale=1.0):
        return jax.random.normal(next(ks), shape, jnp.float32) * scale

    def gain(shape):
        return 1.0 + 0.05 * nrm(shape)

    def dsa_weights():
        return (nrm((D, A_IN), D ** -0.5), nrm((A_HEADS * A_HEAD_DIM, D), (A_HEADS * A_HEAD_DIM) ** -0.5),
                gain((A_HEAD_DIM,)), gain((A_HEAD_DIM,)), gain((IDX_DIM,)))

    inp = {}
    inp["x_prompt"] = nrm((BATCH, SEQ, D))
    inp["x_sample"] = nrm((DEC_BATCH, DEC_SEQ, D))
    inp["mem_prompt"] = nrm((BATCH, MEM_LEN, D))
    inp["cache_l0_k"] = nrm((DEC_BATCH, PAST_LEN, A_HEADS, A_HEAD_DIM))
    inp["cache_l0_v"] = nrm((DEC_BATCH, PAST_LEN, A_HEADS, A_HEAD_DIM))
    inp["cache_l0_kidx"] = nrm((DEC_BATCH, PAST_LEN, IDX_DIM))
    inp["state_l1_C"] = nrm((DEC_BATCH, B_HEADS, B_QK_DIM, B_V_DIM), 0.3)
    inp["state_l1_n"] = nrm((DEC_BATCH, B_HEADS, B_QK_DIM), 0.3)
    inp["state_l1_m"] = nrm((DEC_BATCH, B_HEADS))
    inp["state_l2_S"] = nrm((DEC_BATCH, C_HEADS, C_DK, C_DV), 0.3)
    inp["state_l2_conv"] = nrm((DEC_BATCH, CONV_W - 1, C_CONV_DIM))
    inp["cache_l3_k"] = nrm((DEC_BATCH, PAST_LEN, A_HEADS, A_HEAD_DIM))
    inp["cache_l3_v"] = nrm((DEC_BATCH, PAST_LEN, A_HEADS, A_HEAD_DIM))
    inp["cache_l3_kidx"] = nrm((DEC_BATCH, PAST_LEN, IDX_DIM))
    inp["cache_mem_k"] = nrm((DEPTH, DEC_BATCH, MEM_LEN, MEM_HEADS, MEM_HEAD_DIM))
    inp["cache_mem_v"] = nrm((DEPTH, DEC_BATCH, MEM_LEN, MEM_HEADS, MEM_HEAD_DIM))
    inp["rel_bias"] = nrm((N_BUCKETS, A_HEADS), 0.5)
    inp["norm_mix"] = gain((DEPTH, D))
    inp["norm_mem"] = gain((DEPTH, D))
    inp["norm_ffn"] = gain((DEPTH, D))
    inp["mem_norm"] = gain((DEPTH, D))
    inp["w_mq"] = nrm((DEPTH, D, MEM_WIDTH), D ** -0.5)
    inp["w_mk"] = nrm((DEPTH, D, MEM_WIDTH), D ** -0.5)
    inp["w_mv"] = nrm((DEPTH, D, MEM_WIDTH), D ** -0.5)
    inp["w_mo"] = nrm((DEPTH, MEM_WIDTH, D), MEM_WIDTH ** -0.5)
    inp["mq_gain"] = gain((DEPTH, MEM_HEAD_DIM))
    inp["mk_gain"] = gain((DEPTH, MEM_HEAD_DIM))
    inp["w_ffn1"] = nrm((DEPTH, D, FFN_HIDDEN), D ** -0.5)
    inp["w_ffn3"] = nrm((DEPTH, D, FFN_HIDDEN), D ** -0.5)
    inp["w_ffn2"] = nrm((DEPTH, FFN_HIDDEN, D), FFN_HIDDEN ** -0.5)
    (inp["a0_w_in"], inp["a0_w_out"], inp["a0_q_gain"], inp["a0_k_gain"], inp["a0_kidx_gain"]) = dsa_weights()
    inp["b1_w_in"] = jnp.concatenate([nrm((D, B_MAIN), D ** -0.5), nrm((D, 2 * B_HEADS), 0.1 * D ** -0.5)], axis=1)
    inp["b1_gate_bias"] = jnp.concatenate([0.1 * nrm((B_HEADS,)),
                                           3.0 + 3.0 * jax.random.uniform(next(ks), (B_HEADS,), jnp.float32)])
    inp["b1_h_gain"] = gain((B_V_DIM,))
    inp["b1_w_out"] = nrm((B_HEADS * B_V_DIM, D), (B_HEADS * B_V_DIM) ** -0.5)
    inp["c2_w_in"] = jnp.concatenate([nrm((D, C_MAIN), D ** -0.5), nrm((D, 2 * C_HEADS), 0.1 * D ** -0.5)], axis=1)
    inp["c2_conv_w"] = nrm((CONV_W, C_CONV_DIM), CONV_W ** -0.5)
    inp["c2_a_log"] = jnp.log(jax.random.uniform(next(ks), (C_HEADS,), jnp.float32, 1.0, 16.0))
    dt = jnp.exp(jax.random.uniform(next(ks), (C_HEADS,), jnp.float32, math.log(1e-3), math.log(1e-1)))
    inp["c2_dt_bias"] = dt + jnp.log(-jnp.expm1(-dt))
    inp["c2_o_gain"] = gain((C_DV,))
    inp["c2_w_out"] = nrm((C_HEADS * C_DV, D), (C_HEADS * C_DV) ** -0.5)
    (inp["a3_w_in"], inp["a3_w_out"], inp["a3_q_gain"], inp["a3_k_gain"], inp["a3_kidx_gain"]) = dsa_weights()
    return inp


def reference(x_prompt, x_sample, mem_prompt,
              cache_l0_k, cache_l0_v, cache_l0_kidx,
              state_l1_C, state_l1_n, state_l1_m,
              state_l2_S, state_l2_conv,
              cache_l3_k, cache_l3_v, cache_l3_kidx,
              cache_mem_k, cache_mem_v,
              rel_bias, norm_mix, norm_mem, norm_ffn, mem_norm,
              w_mq, w_mk, w_mv, w_mo, mq_gain, mk_gain,
              w_ffn1, w_ffn3, w_ffn2,
              a0_w_in, a0_w_out, a0_q_gain, a0_k_gain, a0_kidx_gain,
              b1_w_in, b1_gate_bias, b1_h_gain, b1_w_out,
              c2_w_in, c2_conv_w, c2_a_log, c2_dt_bias, c2_o_gain, c2_w_out,
              a3_w_in, a3_w_out, a3_q_gain, a3_k_gain, a3_kidx_gain):
    P = {
        "rel_bias": rel_bias, "norm_mix": norm_mix, "norm_mem": norm_mem, "norm_ffn": norm_ffn,
        "w_mq": w_mq, "w_mo": w_mo, "mq_gain": mq_gain,
        "w_ffn1": w_ffn1, "w_ffn3": w_ffn3, "w_ffn2": w_ffn2,
        "mixer": {
            0: (a0_w_in, a0_w_out, a0_q_gain, a0_k_gain, a0_kidx_gain),
            1: (b1_w_in, b1_gate_bias, b1_h_gain, b1_w_out),
            2: (c2_w_in, c2_conv_w, c2_a_log, c2_dt_bias, c2_o_gain, c2_w_out),
            3: (a3_w_in, a3_w_out, a3_q_gain, a3_k_gain, a3_kidx_gain),
        },
    }

    B, T = x_prompt.shape[0], x_prompt.shape[1]
    mem_kv_p = [mem_kv(mem_prompt, mem_norm[i], w_mk[i], w_mv[i], mk_gain[i]) for i in range(DEPTH)]
    mem_k_p = [kv[0] for kv in mem_kv_p]
    mem_v_p = [kv[1] for kv in mem_kv_p]
    st_p = {
        0: (None, None, None),
        1: (jnp.zeros((B, B_HEADS, B_QK_DIM, B_V_DIM), jnp.float32),
            jnp.zeros((B, B_HEADS, B_QK_DIM), jnp.float32),
            jnp.full((B, B_HEADS), NEG, jnp.float32)),
        2: (jnp.zeros((B, C_HEADS, C_DK, C_DV), jnp.float32),
            jnp.zeros((B, CONV_W - 1, C_CONV_DIM), x_prompt.dtype)),
        3: (None, None, None),
    }
    y_prompt, np_ = trunk(x_prompt, jnp.arange(T, dtype=jnp.int32), mem_k_p, mem_v_p, st_p, P)
    p_mem_k = jnp.stack(mem_k_p, axis=0)
    p_mem_v = jnp.stack(mem_v_p, axis=0)

    past = cache_l0_k.shape[1]
    Ts = x_sample.shape[1]
    st_s = {
        0: (cache_l0_k, cache_l0_v, cache_l0_kidx),
        1: (state_l1_C, state_l1_n, state_l1_m),
        2: (state_l2_S, state_l2_conv),
        3: (cache_l3_k, cache_l3_v, cache_l3_kidx),
    }
    y_sample, ns_ = trunk(x_sample, past + jnp.arange(Ts, dtype=jnp.int32), cache_mem_k, cache_mem_v, st_s, P)

    p_l0_k, p_l0_v, p_l0_kidx = np_[0]
    p_l1_C, p_l1_n, p_l1_m = np_[1]
    p_l2_S, p_l2_conv = np_[2]
    p_l3_k, p_l3_v, p_l3_kidx = np_[3]
    s_l0_k, s_l0_v, s_l0_kidx = ns_[0]
    s_l1_C, s_l1_n, s_l1_m = ns_[1]
    s_l2_S, s_l2_conv = ns_[2]
    s_l3_k, s_l3_v, s_l3_kidx = ns_[3]
    return (y_prompt, y_sample,
            p_l0_k, p_l0_v, p_l0_kidx, p_l1_C, p_l1_n, p_l1_m, p_l2_S, p_l2_conv,
            p_l3_k, p_l3_v, p_l3_kidx, p_mem_k, p_mem_v,
            s_l0_k, s_l0_v, s_l0_kidx, s_l1_C, s_l1_n, s_l1_m, s_l2_S, s_l2_conv,
            s_l3_k, s_l3_v, s_l3_kidx)
```

```python
import functools
import math

import jax
import jax.numpy as jnp
from jax import lax
from jax.experimental import pallas as pl
from jax.experimental.pallas import tpu as pltpu

F32 = jnp.float32
BF16 = jnp.bfloat16
I32 = jnp.int32

EPS = 1e-6
NEG = -1e30
CHUNK = 64
CHUNK_SHIFT = 6
LANES = 128
TILE = 128
VMEM_LIMIT = 56 * 1024 * 1024

A_HEADS, A_HEAD_DIM = 16, 64
IDX_HEADS, IDX_DIM = 8, 64
TOPK_MAX = 256
N_BUCKETS, MAX_DISTANCE = 32, 128
B_HEADS, B_QK_DIM, B_V_DIM = 8, 64, 128
C_HEADS, C_DK, C_DV = 8, 128, 128
CONV_W = 4
MEM_HEADS, MEM_HEAD_DIM = 4, 128


def _bf(x):
    return x.astype(BF16)


def _bf01(mask):
    return mask.astype(F32).astype(BF16)


def _dot(a, b):
    return jnp.dot(a, b, preferred_element_type=F32)


def _dot_nt(a, b):
    return lax.dot_general(a, b, (((1,), (1,)), ((), ())), preferred_element_type=F32)


def _dot_tn(a, b):
    return lax.dot_general(a, b, (((0,), (0,)), ((), ())), preferred_element_type=F32)


def _split3(x):
    hi = _bf(x)
    r1 = x - hi.astype(F32)
    mid = _bf(r1)
    lo = _bf(r1 - mid.astype(F32))
    return hi, mid, lo


def _dot_exact_rhs01(x, m01):
    hi, mid, lo = _split3(x)
    return _dot(hi, m01) + _dot(mid, m01) + _dot(lo, m01)


def _dot_exact_lhs01(m01, x):
    hi, mid, lo = _split3(x)
    return _dot(m01, hi) + _dot(m01, mid) + _dot(m01, lo)


def _dot_f32(a, b):
    ah, am, al = _split3(a)
    bh, bm, bl = _split3(b)
    return (_dot(ah, bh) + (_dot(ah, bm) + _dot(am, bh))
            + (_dot(am, bm) + _dot(ah, bl) + _dot(al, bh)))


def _rms(x, g):
    ms = jnp.mean(x * x, axis=-1, keepdims=True)
    return x * lax.rsqrt(ms + EPS) * g


def _sigmoid(x):
    return 1.0 / (1.0 + jnp.exp(-x))


def _silu(x):
    return x * _sigmoid(x)


def _softplus(x):
    return jnp.maximum(x, 0.0) + jnp.log1p(jnp.exp(-jnp.abs(x)))


def _params(*sem):
    return pltpu.CompilerParams(dimension_semantics=sem, vmem_limit_bytes=VMEM_LIMIT)


def _full(shape):
    n = len(shape)
    return pl.BlockSpec(shape, lambda *_: (0,) * n)


def _proj_kernel(x_ref, g_ref, w_ref, o_ref, *, col_chunk):
    h = _bf(_rms(x_ref[...], g_ref[...]))
    m = w_ref.shape[1]
    for c in range(0, m, col_chunk):
        e = min(c + col_chunk, m)
        o_ref[:, c:e] = _dot(h, w_ref[:, c:e])


def proj(x, g, w, tm):
    n, d = x.shape
    m = w.shape[1]
    return pl.pallas_call(
        functools.partial(_proj_kernel, col_chunk=512),
        grid=(n // tm,),
        in_specs=[pl.BlockSpec((tm, d), lambda i: (i, 0)), _full((1, d)), _full((d, m))],
        out_specs=pl.BlockSpec((tm, m), lambda i: (i, 0)),
        out_shape=jax.ShapeDtypeStruct((n, m), F32),
        compiler_params=_params("parallel"),
        name="proj",
    )(x, g.reshape(1, d), w)


def _memkv_kernel(x_ref, g_ref, wk_ref, wv_ref, kg_ref, k_ref, v_ref):
    h = _bf(_rms(x_ref[...], g_ref[0]))
    k = _dot(h, wk_ref[0])
    v_ref[0] = _dot(h, wv_ref[0])
    kg = kg_ref[0]
    for hd in range(MEM_HEADS):
        sl = slice(hd * MEM_HEAD_DIM, (hd + 1) * MEM_HEAD_DIM)
        k_ref[0, :, sl] = _rms(k[:, sl], kg)


def mem_kv(mem2d, mem_norm, w_mk, w_mv, mk_gain, tm=256):
    n, d = mem2d.shape
    depth = w_mk.shape[0]
    mw = w_mk.shape[2]
    return pl.pallas_call(
        _memkv_kernel,
        grid=(depth, n // tm),
        in_specs=[pl.BlockSpec((tm, d), lambda l, i: (i, 0)),
                  pl.BlockSpec((1, 1, d), lambda l, i: (l, 0, 0)),
                  pl.BlockSpec((1, d, mw), lambda l, i: (l, 0, 0)),
                  pl.BlockSpec((1, d, mw), lambda l, i: (l, 0, 0)),
                  pl.BlockSpec((1, 1, MEM_HEAD_DIM), lambda l, i: (l, 0, 0))],
        out_specs=[pl.BlockSpec((1, tm, mw), lambda l, i: (l, i, 0)),
                   pl.BlockSpec((1, tm, mw), lambda l, i: (l, i, 0))],
        out_shape=[jax.ShapeDtypeStruct((depth, n, mw), F32)] * 2,
        compiler_params=_params("parallel", "parallel"),
        name="mem_kv",
    )(mem2d, mem_norm.reshape(depth, 1, d), _bf(w_mk), _bf(w_mv), mk_gain.reshape(depth, 1, MEM_HEAD_DIM))


def _memattn_kernel(x_ref, o_ref, wo_ref, g_ref, wq_ref, qg_ref, mk_ref, mv_ref, wmo_ref, y_ref):
    x1 = x_ref[...] + _dot(o_ref[...], wo_ref[...])
    h = _bf(_rms(x1, g_ref[...]))
    q = _dot(h, wq_ref[...])
    qg = qg_ref[...]
    scale = MEM_HEAD_DIM ** -0.5
    outs = []
    for hd in range(MEM_HEADS):
        sl = slice(hd * MEM_HEAD_DIM, (hd + 1) * MEM_HEAD_DIM)
        qh = _bf(_rms(q[:, sl], qg))
        kh = _bf(mk_ref[0, :, sl])
        logits = _dot_nt(qh, kh) * scale
        m = jnp.max(logits, axis=-1, keepdims=True)
        p = jnp.exp(logits - m)
        p = p / jnp.sum(p, axis=-1, keepdims=True)
        outs.append(_bf(_dot(_bf(p), _bf(mv_ref[0, :, sl]))))
    att = jnp.concatenate(outs, axis=-1)
    y_ref[...] = x1 + _dot(att, wmo_ref[...])


def mem_attend(x, o, w_out, g, w_mq, mq_gain, mk, mv, w_mo, tm, tiles_per_seq):
    n, d = x.shape
    mlen, mw = mk.shape[1], mk.shape[2]
    return pl.pallas_call(
        _memattn_kernel,
        grid=(n // tm,),
        in_specs=[pl.BlockSpec((tm, d), lambda i: (i, 0)),
                  pl.BlockSpec((tm, o.shape[1]), lambda i: (i, 0)),
                  _full(w_out.shape), _full((1, d)), _full(w_mq.shape), _full((1, MEM_HEAD_DIM)),
                  pl.BlockSpec((1, mlen, mw), lambda i: (i // tiles_per_seq, 0, 0)),
                  pl.BlockSpec((1, mlen, mw), lambda i: (i // tiles_per_seq, 0, 0)),
                  _full(w_mo.shape)],
        out_specs=pl.BlockSpec((tm, d), lambda i: (i, 0)),
        out_shape=jax.ShapeDtypeStruct((n, d), F32),
        compiler_params=_params("parallel"),
        name="mem_attend",
    )(x, o, w_out, g.reshape(1, d), w_mq, mq_gain.reshape(1, MEM_HEAD_DIM), mk, mv, w_mo)


def _ffn_kernel(x_ref, g_ref, w1_ref, w3_ref, w2_ref, y_ref, *, hid_chunk):
    x = x_ref[...]
    h = _bf(_rms(x, g_ref[...]))
    hidden = w1_ref.shape[1]
    y_ref[...] = x
    for c in range(0, hidden, hid_chunk):
        a = _dot(h, w1_ref[:, c:c + hid_chunk])
        b = _dot(h, w3_ref[:, c:c + hid_chunk])
        y_ref[...] += _dot(_bf(_silu(a) * b), w2_ref[c:c + hid_chunk, :])


def ffn(x, g, w1, w3, w2, tm):
    n, d = x.shape
    hidden = w1.shape[1]
    return pl.pallas_call(
        functools.partial(_ffn_kernel, hid_chunk=256),
        grid=(n // tm,),
        in_specs=[pl.BlockSpec((tm, d), lambda i: (i, 0)), _full((1, d)),
                  _full((d, hidden)), _full((d, hidden)), _full((hidden, d))],
        out_specs=pl.BlockSpec((tm, d), lambda i: (i, 0)),
        out_shape=jax.ShapeDtypeStruct((n, d), F32),
        compiler_params=_params("parallel"),
        name="ffn",
    )(x, g.reshape(1, d), w1, w3, w2)


def _proj_dsa_kernel(x_ref, g_ref, w_ref, qg_ref, kg_ref, kig_ref, bd_ref,
                     q_o, k_o, kb_o, v_o, vb_o, qi_o, kiwi_o):
    h = _bf(_rms(x_ref[...], g_ref[...]))
    hq = A_HEADS * A_HEAD_DIM
    bd = bd_ref[...]
    inv_hd = 1.0 / A_HEAD_DIM

    def head_norm(p, gain):
        ss = _dot_exact_rhs01(p * p, bd)
        return p * lax.rsqrt(ss * inv_hd + EPS) * gain

    step = 512
    for c in range(0, hq, step):
        pq = _dot(h, w_ref[:, c:c + step])
        pk = _dot(h, w_ref[:, hq + c:hq + c + step])
        pv = _dot(h, w_ref[:, 2 * hq + c:2 * hq + c + step])
        for j in range(0, step, LANES):
            sl = slice(c + j, c + j + LANES)
            qn = head_norm(pq[:, j:j + LANES], qg_ref[:, sl])
            q_o[:, sl] = _bf(qn * (A_HEAD_DIM ** -0.5))
            kn = head_norm(pk[:, j:j + LANES], kg_ref[:, sl])
            k_o[:, sl] = kn
            kb_o[:, sl] = _bf(kn)
        v_o[:, c:c + step] = pv
        vb_o[:, c:c + step] = _bf(pv)
    qiw = IDX_HEADS * LANES
    for c in range(0, qiw, step):
        qi_o[:, c:c + step] = _bf(_dot(h, w_ref[:, 3 * hq + c:3 * hq + c + step]))
    p = _dot(h, w_ref[:, 3 * hq + qiw:3 * hq + qiw + LANES])
    lane = lax.broadcasted_iota(I32, p.shape, 1)
    is_ki = lane < IDX_DIM
    ss = jnp.sum(jnp.where(is_ki, p * p, 0.0), axis=-1, keepdims=True)
    kin = p * lax.rsqrt(ss * (1.0 / IDX_DIM) + EPS) * kig_ref[...]
    kiwi_o[...] = jnp.where(is_ki, kin, p)


def proj_dsa(x, g, wa, q_gain, k_gain, ki_gain, tm):
    n, d = x.shape
    hq = A_HEADS * A_HEAD_DIM
    m = wa.shape[1]
    qg = jnp.tile(q_gain, A_HEADS).reshape(1, hq)
    kg = jnp.tile(k_gain, A_HEADS).reshape(1, hq)
    kig = jnp.concatenate([ki_gain, jnp.ones((LANES - IDX_DIM,), F32)]).reshape(1, LANES)
    r = jnp.arange(LANES)
    bd = _bf((r[:, None] // A_HEAD_DIM) == (r[None, :] // A_HEAD_DIM))
    row = lambda w: pl.BlockSpec((tm, w), lambda i: (i, 0))
    return pl.pallas_call(
        _proj_dsa_kernel,
        grid=(n // tm,),
        in_specs=[row(d), _full((1, d)), _full((d, m)), _full((1, hq)), _full((1, hq)),
                  _full((1, LANES)), _full((LANES, LANES))],
        out_specs=[row(hq), row(hq), row(hq), row(hq), row(hq), row(IDX_HEADS * LANES), row(LANES)],
        out_shape=[jax.ShapeDtypeStruct((n, hq), BF16), jax.ShapeDtypeStruct((n, hq), F32),
                   jax.ShapeDtypeStruct((n, hq), BF16), jax.ShapeDtypeStruct((n, hq), F32),
                   jax.ShapeDtypeStruct((n, hq), BF16),
                   jax.ShapeDtypeStruct((n, IDX_HEADS * LANES), BF16),
                   jax.ShapeDtypeStruct((n, LANES), F32)],
        compiler_params=_params("parallel"),
        name="proj_dsa",
    )(x, g.reshape(1, d), wa, qg, kg, kig, bd)


def dsa_weight(w_in):
    hq = A_HEADS * A_HEAD_DIM
    o3 = 3 * hq
    o4 = o3 + IDX_HEADS * IDX_DIM
    d = w_in.shape[0]
    wqi = w_in[:, o3:o4].reshape(d, IDX_HEADS, IDX_DIM)
    wqi = jnp.pad(wqi, ((0, 0), (0, 0), (0, LANES - IDX_DIM))).reshape(d, IDX_HEADS * LANES)
    tail = jnp.pad(w_in[:, o4:], ((0, 0), (0, LANES - (w_in.shape[1] - o4))))
    return _bf(jnp.concatenate([w_in[:, :o3], wqi, tail], axis=1))


def _sortable(s):
    b = pltpu.bitcast(s, I32)
    b = jnp.where(b == jnp.int32(-2 ** 31), 0, b)
    return jnp.where(b < 0, b ^ jnp.int32(0x7FFFFFFF), b)


def _index_mask_kernel(qi_ref, wit_ref, *rest, seg_rows, seg_width, topk, causal, idx_bits):
    nseg = len(seg_rows)
    ki_refs = rest[:nseg]
    o_ref = rest[nseg]
    key_ref, sel_ref = rest[nseg + 1:]
    tq = qi_ref.shape[0]
    ltot = sum(seg_rows)
    i = pl.program_id(1)

    wit = wit_ref[0]
    row0 = 0
    for ki_ref, rows, width in zip(ki_refs, seg_rows, seg_width):
        ki = _bf(ki_ref[0] if len(ki_ref.shape) == 3 else ki_ref[...])
        acc = jnp.zeros((rows, tq), F32)
        for hd in range(IDX_HEADS):
            qh = qi_ref[:, hd * LANES:hd * LANES + width]
            rel = _dot_nt(ki, qh)
            acc = acc + wit[hd:hd + 1, :] * jnp.maximum(rel, 0.0)
        s = acc * ((IDX_DIM ** -0.5) * (IDX_HEADS ** -0.5))
        if causal:
            kpos = row0 + lax.broadcasted_iota(I32, (rows, tq), 0)
            qpos = i * tq + lax.broadcasted_iota(I32, (rows, tq), 1)
            s = jnp.where((kpos >> CHUNK_SHIFT) <= (qpos >> CHUNK_SHIFT), s, NEG)
        key_ref[row0:row0 + rows, :] = _sortable(s)
        row0 += rows

    def count(pred):
        return jnp.sum(pred.astype(I32), axis=0, keepdims=True)

    def ge_count(cand):
        return count(key_ref[...] >= cand)

    t0 = jnp.full((1, tq), -2 ** 31, I32)
    t = jnp.where(ge_count(jnp.zeros((1, tq), I32)) >= topk, 0, t0)

    def vbody(it, t):
        cand = t + (jnp.int32(1) << (30 - it))
        return jnp.where(ge_count(cand) >= topk, cand, t)

    t = lax.fori_loop(0, 31, vbody, t)
    keys = key_ref[...]
    gt = keys > t
    eq = keys == t
    need = topk - count(gt)
    rowi = lax.broadcasted_iota(I32, (ltot, tq), 0)

    def jbody(it, jv):
        cand = jv + (jnp.int32(1) << (idx_bits - 1 - it))
        below = count(eq & (rowi < cand))
        return jnp.where(below < need, cand, jv)

    jv = lax.fori_loop(0, idx_bits, jbody, jnp.zeros((1, tq), I32))
    sel = gt | (eq & (rowi <= jv))
    if causal:
        qpos = i * tq + lax.broadcasted_iota(I32, (ltot, tq), 1)
        sel = sel & ((rowi >> CHUNK_SHIFT) <= (qpos >> CHUNK_SHIFT))
    sel_ref[0:ltot, :] = jnp.where(sel, 0.0, NEG)
    pad = sel_ref.shape[0] - ltot
    if pad:
        sel_ref[ltot:, :] = jnp.full((pad, tq), NEG, F32)
    for kt in range(sel_ref.shape[0] // TILE):
        o_ref[0, 0, kt] = sel_ref[kt * TILE:(kt + 1) * TILE, :].T


def index_mask(qi, wit, ki_segs, seg_specs, seg_rows, seg_width, nseq, nqb, causal):
    ltot = sum(seg_rows)
    lpad = -(-ltot // TILE) * TILE
    nt = lpad // TILE
    topk = min(TOPK_MAX, ltot // 4)
    idx_bits = max(1, (ltot - 1).bit_length())
    kern = functools.partial(_index_mask_kernel, seg_rows=tuple(seg_rows), seg_width=tuple(seg_width),
                             topk=topk, causal=causal, idx_bits=idx_bits)
    return pl.pallas_call(
        kern,
        grid=(nseq, nqb),
        in_specs=[pl.BlockSpec((TILE, qi.shape[1]), lambda b, i: (b * nqb + i, 0)),
                  pl.BlockSpec((1, IDX_HEADS, TILE), lambda b, i: (b * nqb + i, 0, 0))] + list(seg_specs),
        out_specs=pl.BlockSpec((1, 1, nt, TILE, TILE), lambda b, i: (b, i, 0, 0, 0)),
        out_shape=jax.ShapeDtypeStruct((nseq, nqb, nt, TILE, TILE), F32),
        scratch_shapes=[pltpu.VMEM((ltot, TILE), I32), pltpu.VMEM((lpad, TILE), F32)],
        compiler_params=_params("parallel", "parallel"),
        name="index_mask",
    )(qi, wit, *ki_segs)


def _attn_heads(q_ref, o_ref, l_ref, groups, tq):
    lane = lax.broadcasted_iota(I32, (1, LANES), 1)
    keep_lo = _bf((lane < A_HEAD_DIM).astype(F32))
    keep_hi = _bf((lane >= A_HEAD_DIM).astype(F32))
    lane_full = lax.broadcasted_iota(I32, (tq, LANES), 1)

    def over_tiles(count, body, init):
        if isinstance(count, int) and count == 1:
            return body(0, init)
        return lax.fori_loop(0, count, body, init)

    for pair in range(A_HEADS // 2):
        sl = slice(pair * LANES, (pair + 1) * LANES)
        qpair = q_ref[:, sl]
        halves = []
        for half, keep, fill in ((0, keep_lo, keep_hi), (1, keep_hi, keep_lo)):
            hd = 2 * pair + half
            qm = qpair * keep
            mx = jnp.full((tq, LANES), NEG, F32)
            for count, base, k_tile, _, mask_tile, bias_tile in groups:
                def p1(kt, mx, base=base, k_tile=k_tile, mask_tile=mask_tile, bias_tile=bias_tile):
                    l = _dot_nt(qm, k_tile(kt, sl)) + mask_tile(kt) + bias_tile(kt, hd)
                    l_ref[base + kt] = l
                    return jnp.maximum(mx, l)

                mx = over_tiles(count, p1, mx)
            m = jnp.max(mx, axis=-1, keepdims=True)
            acc = jnp.zeros((tq, LANES), F32)
            for count, base, _, v_tile, _, _ in groups:
                def p2(kt, acc, base=base, v_tile=v_tile):
                    p = jnp.exp(l_ref[base + kt] - m)
                    vm = v_tile(kt, sl) * keep + fill
                    return acc + _dot(_bf(p), vm)

                acc = over_tiles(count, p2, acc)
            halves.append(acc / pltpu.roll(acc, A_HEAD_DIM, 1))
        o_ref[:, sl] = _bf(jnp.where(lane_full < A_HEAD_DIM, halves[0], halves[1]))


def _attn_prompt_kernel(q_ref, k_ref, v_ref, mask_ref, bias_ref, o_ref, l_ref):
    i = pl.program_id(1)

    def rows(kt):
        return pl.ds(pl.multiple_of(kt * TILE, TILE), TILE)

    def bias_tile(kt, hd):
        sel = jnp.where(kt == i, 2, jnp.where(kt == i - 1, 1, 0))
        return bias_ref[sel, hd]

    group = (i + 1, 0,
             lambda kt, sl: k_ref[rows(kt), sl],
             lambda kt, sl: v_ref[rows(kt), sl],
             lambda kt: mask_ref[0, 0, kt],
             bias_tile)
    _attn_heads(q_ref, o_ref, l_ref, [group], TILE)


def attn_prompt(q, kb, vb, mask, bias, nseq, seqlen):
    n, hq = q.shape
    nqb = seqlen // TILE
    nt = mask.shape[2]
    return pl.pallas_call(
        _attn_prompt_kernel,
        grid=(nseq, nqb),
        in_specs=[pl.BlockSpec((TILE, hq), lambda b, i: (b * nqb + i, 0)),
                  pl.BlockSpec((seqlen, hq), lambda b, i: (b, 0)),
                  pl.BlockSpec((seqlen, hq), lambda b, i: (b, 0)),
                  pl.BlockSpec((1, 1, nt, TILE, TILE), lambda b, i: (b, i, 0, 0, 0)),
                  _full(bias.shape)],
        out_specs=pl.BlockSpec((TILE, hq), lambda b, i: (b * nqb + i, 0)),
        out_shape=jax.ShapeDtypeStruct((n, hq), BF16),
        scratch_shapes=[pltpu.VMEM((nt, TILE, TILE), F32)],
        compiler_params=_params("parallel", "arbitrary"),
        name="attn_prompt",
    )(q, kb, vb, mask, bias)


def _attn_sample_kernel(q_ref, kc_ref, vc_ref, kn_ref, vn_ref, mask_ref, bias_ref, o_ref, l_ref, *, ncache):
    tq = q_ref.shape[0]

    def rows(kt):
        return pl.ds(pl.multiple_of(kt * TILE, TILE), TILE)

    def cache_bias(kt, hd):
        return bias_ref[jnp.where(kt == ncache - 1, 1, 0), hd, 0:tq, :]

    cache = (ncache, 0,
             lambda kt, sl: _bf(kc_ref[0, rows(kt), sl]),
             lambda kt, sl: _bf(vc_ref[0, rows(kt), sl]),
             lambda kt: mask_ref[0, 0, kt, 0:tq, :],
             cache_bias)
    fresh = (1, ncache,
             lambda kt, sl: kn_ref[0, :, sl],
             lambda kt, sl: vn_ref[0, :, sl],
             lambda kt: mask_ref[0, 0, ncache, 0:tq, :],
             lambda kt, hd: bias_ref[2, hd, 0:tq, :])
    _attn_heads(q_ref, o_ref, l_ref, [cache, fresh], tq)


def attn_sample(q, kc, vc, kn, vn, mask, bias, nseq, tq):
    n, hq = q.shape
    past = kc.shape[1]
    ncache = past // TILE
    nt = mask.shape[2]
    return pl.pallas_call(
        functools.partial(_attn_sample_kernel, ncache=ncache),
        grid=(nseq,),
        in_specs=[pl.BlockSpec((tq, hq), lambda s: (s, 0)),
                  pl.BlockSpec((1, past, hq), lambda s: (s, 0, 0)),
                  pl.BlockSpec((1, past, hq), lambda s: (s, 0, 0)),
                  pl.BlockSpec((1, TILE, hq), lambda s: (s, 0, 0)),
                  pl.BlockSpec((1, TILE, hq), lambda s: (s, 0, 0)),
                  pl.BlockSpec((1, 1, nt, TILE, TILE), lambda s: (s, 0, 0, 0, 0)),
                  _full(bias.shape)],
        out_specs=pl.BlockSpec((tq, hq), lambda s: (s, 0)),
        out_shape=jax.ShapeDtypeStruct((n, hq), BF16),
        scratch_shapes=[pltpu.VMEM((nt, tq, TILE), F32)],
        compiler_params=_params("parallel"),
        name="attn_sample",
    )(q, kc, vc, kn, vn, mask, bias)


def _t5_bucket(rel):
    half = N_BUCKETS // 2
    max_exact = half // 2
    n = jnp.abs(rel)
    nf = jnp.maximum(n, 1).astype(F32)
    large = max_exact + (jnp.log(nf / max_exact) / math.log(MAX_DISTANCE / max_exact)
                         * (half - max_exact)).astype(I32)
    large = jnp.minimum(large, half - 1)
    return jnp.where(rel > 0, half, 0) + jnp.where(n < max_exact, n, large)


def bias_tiles(rel_bias):
    r = jnp.arange(TILE, dtype=I32)[:, None]
    c = jnp.arange(TILE, dtype=I32)[None, :]
    rels = jnp.stack([jnp.full((TILE, TILE), -(TILE + 1), I32), c - r - TILE, c - r])
    tab = rel_bias.astype(F32)[_t5_bucket(rels)]
    return jnp.moveaxis(tab, -1, 1)


def dsa_core_prompt(pr, nseq, seqlen, bias):
    q, _, kb, _, vb, qi, kiwi = pr
    nqb = seqlen // TILE
    wit = kiwi[:, IDX_DIM:IDX_DIM + IDX_HEADS].reshape(nseq * nqb, TILE, IDX_HEADS).swapaxes(1, 2)
    mask = index_mask(qi, wit, [kiwi], [pl.BlockSpec((seqlen, LANES), lambda b, i: (b, 0))],
                      [seqlen], [LANES], nseq, nqb, causal=True)
    return attn_prompt(q, kb, vb, mask, bias, nseq, seqlen)


def dsa_core_sample(pr, k_cache, v_cache, ki_cache, nseq, tq, bias):
    q, _, kb, _, vb, qi, kiwi = pr
    past = k_cache.shape[1]
    hq = q.shape[1]
    pad = TILE - tq
    qi2 = jnp.concatenate([qi.reshape(nseq, tq, -1)] * (TILE // tq), axis=1).reshape(nseq * TILE, -1)
    wit = kiwi[:, IDX_DIM:IDX_DIM + IDX_HEADS].reshape(nseq, tq, IDX_HEADS).swapaxes(1, 2)
    wit = jnp.concatenate([wit] * (TILE // tq), axis=2)
    mask = index_mask(
        qi2, wit, [ki_cache, kiwi],
        [pl.BlockSpec((1, past, IDX_DIM), lambda b, i: (b, 0, 0)),
         pl.BlockSpec((tq, LANES), lambda b, i: (b, 0))],
        [past, tq], [IDX_DIM, LANES], nseq, 1, causal=False)
    kn = jnp.pad(kb.reshape(nseq, tq, hq), ((0, 0), (0, pad), (0, 0)))
    vn = jnp.pad(vb.reshape(nseq, tq, hq), ((0, 0), (0, pad), (0, 0)))
    return attn_sample(q, k_cache.reshape(nseq, past, hq), v_cache.reshape(nseq, past, hq),
                       kn, vn, mask, bias, nseq, tq)


def _mlstm_kernel(p_ref, gt_ref, gb_ref, gbt_ref, hg_ref, c0_ref, n0_ref, m0_ref,
                  hs_ref, c_ref, n_ref, m_ref, cs, ns, ms):
    c_id = pl.program_id(1)
    nh = B_HEADS
    hw = LANES

    @pl.when(c_id == 0)
    def _():
        cs[...] = jnp.zeros(cs.shape, F32)
        cs[:, 0:B_QK_DIM, :] = c0_ref[0]
        ns[...] = jnp.zeros(ns.shape, F32)
        ns[:, 0:B_QK_DIM] = n0_ref[0]
        ms[...] = m0_ref[0]

    L = p_ref.shape[0]
    r_id = lax.broadcasted_iota(I32, (L, L), 0)
    c_id2 = lax.broadcasted_iota(I32, (L, L), 1)
    tril = c_id2 <= r_id
    tril_b = _bf01(tril)
    triu_b = _bf01(r_id <= c_id2)
    g = p_ref[:, 4 * nh * hw:4 * nh * hw + 2 * nh] + gb_ref[...]
    gt = gt_ref[0] + gbt_ref[...]
    logf = jnp.minimum(g[:, nh:], 0.0) - jnp.log1p(jnp.exp(-jnp.abs(g[:, nh:])))
    logft = jnp.minimum(gt[nh:, :], 0.0) - jnp.log1p(jnp.exp(-jnp.abs(gt[nh:, :])))
    bcol = _dot_exact_lhs01(tril_b, logf)
    brow = _dot_exact_rhs01(logft, triu_b)
    icol = g[:, :nh]
    irow = gt[:nh, :]
    hg = hg_ref[...]
    mall = ms[...]
    for hd in range(nh):
        sl = slice(hd * hw, (hd + 1) * hw)
        qf = p_ref[:, sl]
        qh = _bf(qf)
        kh = p_ref[:, nh * hw + hd * hw:nh * hw + (hd + 1) * hw] * (B_QK_DIM ** -0.5)
        vh = _bf(p_ref[:, 2 * nh * hw + hd * hw:2 * nh * hw + (hd + 1) * hw])
        oh = p_ref[:, 3 * nh * hw + hd * hw:3 * nh * hw + (hd + 1) * hw]
        bc = bcol[:, hd:hd + 1]
        br = brow[hd:hd + 1, :]
        ic = icol[:, hd:hd + 1]
        ir = irow[hd:hd + 1, :]
        m = mall[:, hd:hd + 1]
        d = jnp.where(tril, bc - br + ir, NEG)
        inter = bc + m
        mt = jnp.maximum(inter, jnp.max(d, axis=-1, keepdims=True))
        s = _dot_nt(qh, _bf(kh)) * jnp.exp(d - mt)
        wst = jnp.exp(inter - mt)
        cmat = cs[hd]
        nrow = ns[hd:hd + 1, :]
        num = _dot(_bf(s), vh) + wst * _dot(qh, _bf(cmat))
        den = jnp.sum(s, axis=-1, keepdims=True) + wst * jnp.sum(qf * nrow, axis=-1, keepdims=True)
        hs = num / jnp.maximum(jnp.abs(den), jnp.exp(-mt))
        bl = bc[L - 1:L, :]
        dec = bl - bc + ic
        mnew = jnp.maximum(bl + m, jnp.max(dec, axis=0, keepdims=True))
        wk = jnp.exp(dec - mnew)
        ws = jnp.exp(bl + m - mnew)
        kw = kh * wk
        cs[hd] = ws * cmat + _dot_tn(_bf(kw), vh)
        ns[hd:hd + 1, :] = ws * nrow + jnp.sum(kw, axis=0, keepdims=True)
        ms[:, hd:hd + 1] = mnew
        hs_ref[:, sl] = _bf(_rms(hs, hg) * _sigmoid(oh))

    @pl.when(c_id == pl.num_programs(1) - 1)
    def _():
        c_ref[0] = cs[:, 0:B_QK_DIM, :]
        n_ref[0] = ns[:, 0:B_QK_DIM]
        m_ref[0] = ms[...]


def mlstm_weight(w_in):
    d = w_in.shape[0]
    nq = B_HEADS * B_QK_DIM
    nv = B_HEADS * B_V_DIM

    def padh(w):
        w = w.reshape(d, B_HEADS, B_QK_DIM)
        return jnp.pad(w, ((0, 0), (0, 0), (0, LANES - B_QK_DIM))).reshape(d, B_HEADS * LANES)

    main = 2 * nq + 2 * nv
    tail = jnp.pad(w_in[:, main:], ((0, 0), (0, LANES - 2 * B_HEADS)))
    return _bf(jnp.concatenate([padh(w_in[:, :nq]), padh(w_in[:, nq:2 * nq]), w_in[:, 2 * nq:main], tail], axis=1))


def mlstm_core(p, gate_bias, h_gain, c0, n0, m0, nseq, nchunk):
    n, m = p.shape
    nh = B_HEADS
    gcol = 4 * nh * LANES
    gt = p[:, gcol:gcol + 2 * nh].reshape(nseq * nchunk, CHUNK, 2 * nh).swapaxes(1, 2)
    return pl.pallas_call(
        _mlstm_kernel,
        grid=(nseq, nchunk),
        in_specs=[pl.BlockSpec((CHUNK, m), lambda s, c: (s * nchunk + c, 0)),
                  pl.BlockSpec((1, 2 * nh, CHUNK), lambda s, c: (s * nchunk + c, 0, 0)),
                  _full((1, 2 * nh)), _full((2 * nh, 1)), _full((1, B_V_DIM)),
                  pl.BlockSpec((1, nh, B_QK_DIM, B_V_DIM), lambda s, c: (s, 0, 0, 0)),
                  pl.BlockSpec((1, nh, B_QK_DIM), lambda s, c: (s, 0, 0)),
                  pl.BlockSpec((1, 1, nh), lambda s, c: (s, 0, 0))],
        out_specs=[pl.BlockSpec((CHUNK, nh * B_V_DIM), lambda s, c: (s * nchunk + c, 0)),
                   pl.BlockSpec((1, nh, B_QK_DIM, B_V_DIM), lambda s, c: (s, 0, 0, 0)),
                   pl.BlockSpec((1, nh, B_QK_DIM), lambda s, c: (s, 0, 0)),
                   pl.BlockSpec((1, 1, nh), lambda s, c: (s, 0, 0))],
        out_shape=[jax.ShapeDtypeStruct((n, nh * B_V_DIM), BF16),
                   jax.ShapeDtypeStruct((nseq, nh, B_QK_DIM, B_V_DIM), F32),
                   jax.ShapeDtypeStruct((nseq, nh, B_QK_DIM), F32),
                   jax.ShapeDtypeStruct((nseq, 1, nh), F32)],
        scratch_shapes=[pltpu.VMEM((nh, LANES, B_V_DIM), F32), pltpu.VMEM((nh, LANES), F32),
                        pltpu.VMEM((1, nh), F32)],
        compiler_params=_params("parallel", "arbitrary"),
        name="mlstm",
    )(p, gt, gate_bias.reshape(1, 2 * nh), gate_bias.reshape(2 * nh, 1), h_gain.reshape(1, B_V_DIM),
      c0, n0, m0.reshape(nseq, 1, nh))


def _gdn_kernel(p_ref, gt_ref, cw_ref, alog_ref, alogt_ref, dtb_ref, dtbt_ref, og_ref, s0_ref, cb0_ref,
                o_ref, s_ref, cb_ref, ss, tail):
    c_id = pl.program_id(1)
    nh = C_HEADS
    cdim = nh * (2 * C_DK + C_DV)
    L = p_ref.shape[0]
    nprev = CONV_W - 1

    @pl.when(c_id == 0)
    def _():
        ss[...] = s0_ref[0]
        tail[...] = jnp.zeros(tail.shape, F32)
        tail[8 - nprev:8, :] = cb0_ref[0]

    x = p_ref[:, 0:cdim]
    ext = jnp.concatenate([tail[...], x], axis=0)
    conv = ext[8:8 + L] * cw_ref[CONV_W - 1:CONV_W, :]
    for j in range(CONV_W - 1):
        conv = conv + ext[8 - nprev + j:8 - nprev + j + L] * cw_ref[j:j + 1, :]
    tail[...] = x[L - 8:L, :]
    cf = _silu(conv)

    gcolumn = cdim + nh * C_DV
    pre = p_ref[:, gcolumn:gcolumn + 2 * nh]
    pret = gt_ref[0]
    beta = _sigmoid(pre[:, :nh])
    g = -jnp.exp(alog_ref[...]) * _softplus(pre[:, nh:] + dtb_ref[...])
    gtr = -jnp.exp(alogt_ref[...]) * _softplus(pret[nh:, :] + dtbt_ref[...])
    r = lax.broadcasted_iota(I32, (L, L), 0)
    c = lax.broadcasted_iota(I32, (L, L), 1)
    incl = c <= r
    strict = c < r
    eye = (c == r).astype(F32)
    gc_col = _dot_exact_lhs01(_bf01(incl), g)
    gc_row = _dot_exact_rhs01(gtr, _bf01(r <= c))
    og = og_ref[...]
    for hd in range(nh):
        qraw = cf[:, hd * C_DK:(hd + 1) * C_DK]
        kraw = cf[:, nh * C_DK + hd * C_DK:nh * C_DK + (hd + 1) * C_DK]
        vc = cf[:, 2 * nh * C_DK + hd * C_DV:2 * nh * C_DK + (hd + 1) * C_DV]
        z = p_ref[:, cdim + hd * C_DV:cdim + (hd + 1) * C_DV]
        qc = qraw * lax.rsqrt(jnp.sum(qraw * qraw, axis=-1, keepdims=True) + EPS) * (C_DK ** -0.5)
        kc = kraw * lax.rsqrt(jnp.sum(kraw * kraw, axis=-1, keepdims=True) + EPS)
        bc = beta[:, hd:hd + 1]
        gcol = gc_col[:, hd:hd + 1]
        grow = gc_row[hd:hd + 1, :]
        dm = jnp.where(incl, jnp.exp(jnp.where(incl, gcol - grow, 0.0)), 0.0)
        kb = kc * bc
        kcb = _bf(kc)
        amat = jnp.where(strict, _dot_nt(_bf(kb), kcb) * dm, 0.0)
        egc = jnp.exp(gcol)
        rhs = jnp.concatenate([vc * bc, kb * egc], axis=-1)
        nm = -amat
        tinv = eye + nm
        pw = nm
        for _ in range(int(math.log2(L)) - 1):
            pw = _dot_f32(pw, pw)
            tinv = tinv + _dot_f32(tinv, pw)
        sol = _dot_f32(tinv, rhs)
        u = sol[:, :C_DV]
        w = sol[:, C_DV:]
        smat = ss[hd]
        sb = _bf(smat)
        vnew = u - _dot(_bf(w), sb)
        attn = _dot_nt(_bf(qc), kcb) * dm
        o = _dot(_bf(qc * egc), sb) + _dot(_bf(attn), _bf(vnew))
        glast = gcol[L - 1:L, :]
        ss[hd] = smat * jnp.exp(glast) + _dot_tn(_bf(kc * jnp.exp(glast - gcol)), _bf(vnew))
        o_ref[:, hd * C_DV:(hd + 1) * C_DV] = _bf(_rms(o, og) * _silu(z))

    @pl.when(c_id == pl.num_programs(1) - 1)
    def _():
        s_ref[0] = ss[...]
        cb_ref[0] = tail[8 - nprev:8, :]


def gdn_weight(w_in):
    main = C_HEADS * (2 * C_DK + C_DV) + C_HEADS * C_DV
    tail = jnp.pad(w_in[:, main:], ((0, 0), (0, LANES - 2 * C_HEADS)))
    return _bf(jnp.concatenate([w_in[:, :main], tail], axis=1))


def gdn_core(p, conv_w, a_log, dt_bias, o_gain, s0, cb0, nseq, nchunk):
    n, m = p.shape
    nh = C_HEADS
    cdim = nh * (2 * C_DK + C_DV)
    gcolumn = cdim + nh * C_DV
    gt = p[:, gcolumn:gcolumn + 2 * nh].reshape(nseq * nchunk, CHUNK, 2 * nh).swapaxes(1, 2)
    return pl.pallas_call(
        _gdn_kernel,
        grid=(nseq, nchunk),
        in_specs=[pl.BlockSpec((CHUNK, m), lambda s, c: (s * nchunk + c, 0)),
                  pl.BlockSpec((1, 2 * nh, CHUNK), lambda s, c: (s * nchunk + c, 0, 0)),
                  _full((CONV_W, cdim)), _full((1, nh)), _full((nh, 1)), _full((1, nh)), _full((nh, 1)),
                  _full((1, C_DV)),
                  pl.BlockSpec((1, nh, C_DK, C_DV), lambda s, c: (s, 0, 0, 0)),
                  pl.BlockSpec((1, CONV_W - 1, cdim), lambda s, c: (s, 0, 0))],
        out_specs=[pl.BlockSpec((CHUNK, nh * C_DV), lambda s, c: (s * nchunk + c, 0)),
                   pl.BlockSpec((1, nh, C_DK, C_DV), lambda s, c: (s, 0, 0, 0)),
                   pl.BlockSpec((1, CONV_W - 1, cdim), lambda s, c: (s, 0, 0))],
        out_shape=[jax.ShapeDtypeStruct((n, nh * C_DV), BF16),
                   jax.ShapeDtypeStruct((nseq, nh, C_DK, C_DV), F32),
                   jax.ShapeDtypeStruct((nseq, CONV_W - 1, cdim), F32)],
        scratch_shapes=[pltpu.VMEM((nh, C_DK, C_DV), F32), pltpu.VMEM((8, cdim), F32)],
        compiler_params=_params("parallel", "arbitrary"),
        name="gdn",
    )(p, gt, conv_w, a_log.reshape(1, nh), a_log.reshape(nh, 1), dt_bias.reshape(1, nh),
      dt_bias.reshape(nh, 1), o_gain.reshape(1, C_DV), s0, cb0)


def _trunk(x, nseq, seqlen, mem_k, mem_v, st, W, bias, is_prompt):
    d = x.shape[-1]
    n = nseq * seqlen
    x = x.reshape(n, d)
    tm = 256
    tm_mem = min(256, seqlen)
    new = {}
    for i in range(4):
        kind = i % 3
        mx = W["mixer"][i]
        if kind == 0:
            pr = proj_dsa(x, W["norm_mix"][i], mx["w_in"], mx["q_gain"], mx["k_gain"], mx["ki_gain"], tm)
            if is_prompt:
                o = dsa_core_prompt(pr, nseq, seqlen, bias)
            else:
                o = dsa_core_sample(pr, *st[i], nseq, seqlen, bias)
            new[i] = (pr[1].reshape(nseq, seqlen, A_HEADS, A_HEAD_DIM),
                      pr[3].reshape(nseq, seqlen, A_HEADS, A_HEAD_DIM),
                      pr[6][:, :IDX_DIM].reshape(nseq, seqlen, IDX_DIM))
        elif kind == 1:
            p = proj(x, W["norm_mix"][i], mx["w_in"], tm)
            o, c_new, n_new, m_new = mlstm_core(p, mx["gate_bias"], mx["h_gain"], *st[i], nseq, seqlen // CHUNK)
            new[i] = (c_new, n_new, m_new.reshape(nseq, B_HEADS))
        else:
            p = proj(x, W["norm_mix"][i], mx["w_in"], tm)
            o, s_new, cb_new = gdn_core(p, mx["conv_w"], mx["a_log"], mx["dt_bias"], mx["o_gain"], *st[i],
                                        nseq, seqlen // CHUNK)
            new[i] = (s_new, cb_new)
        x = mem_attend(x, o, mx["w_out"], W["norm_mem"][i], W["w_mq"][i], W["mq_gain"][i],
                       mem_k[i], mem_v[i], W["w_mo"][i], tm_mem, seqlen // tm_mem)
        x = ffn(x, W["norm_ffn"][i], W["w_ffn1"][i], W["w_ffn3"][i], W["w_ffn2"][i], tm)
    return x.reshape(nseq, seqlen, d), new


def kernel(x_prompt, x_sample, mem_prompt, cache_l0_k, cache_l0_v, cache_l0_kidx, state_l1_C, state_l1_n, state_l1_m, state_l2_S, state_l2_conv, cache_l3_k, cache_l3_v, cache_l3_kidx, cache_mem_k, cache_mem_v, rel_bias, norm_mix, norm_mem, norm_ffn, mem_norm, w_mq, w_mk, w_mv, w_mo, mq_gain, mk_gain, w_ffn1, w_ffn3, w_ffn2, a0_w_in, a0_w_out, a0_q_gain, a0_k_gain, a0_kidx_gain, b1_w_in, b1_gate_bias, b1_h_gain, b1_w_out, c2_w_in, c2_conv_w, c2_a_log, c2_dt_bias, c2_o_gain, c2_w_out, a3_w_in, a3_w_out, a3_q_gain, a3_k_gain, a3_kidx_gain):
    B, T, D = x_prompt.shape
    S, Ts, _ = x_sample.shape
    depth = w_mq.shape[0]
    mlen = mem_prompt.shape[1]
    mw = MEM_HEADS * MEM_HEAD_DIM

    def dsa_w(w_in, w_out, qg, kg, kig):
        return dict(w_in=dsa_weight(w_in), w_out=_bf(w_out), q_gain=qg, k_gain=kg, ki_gain=kig)

    W = dict(
        norm_mix=norm_mix, norm_mem=norm_mem, norm_ffn=norm_ffn,
        w_mq=_bf(w_mq), w_mo=_bf(w_mo), mq_gain=mq_gain,
        w_ffn1=_bf(w_ffn1), w_ffn3=_bf(w_ffn3), w_ffn2=_bf(w_ffn2),
        mixer={
            0: dsa_w(a0_w_in, a0_w_out, a0_q_gain, a0_k_gain, a0_kidx_gain),
            1: dict(w_in=mlstm_weight(b1_w_in), gate_bias=b1_gate_bias, h_gain=b1_h_gain, w_out=_bf(b1_w_out)),
            2: dict(w_in=gdn_weight(c2_w_in), conv_w=c2_conv_w, a_log=c2_a_log, dt_bias=c2_dt_bias,
                    o_gain=c2_o_gain, w_out=_bf(c2_w_out)),
            3: dsa_w(a3_w_in, a3_w_out, a3_q_gain, a3_k_gain, a3_kidx_gain),
        },
    )
    bias = bias_tiles(rel_bias)

    mk_p, mv_p = mem_kv(mem_prompt.reshape(B * mlen, D), mem_norm, w_mk, w_mv, mk_gain)
    mk_p = mk_p.reshape(depth, B, mlen, mw)
    mv_p = mv_p.reshape(depth, B, mlen, mw)
    st_p = {
        0: None,
        1: (jnp.zeros((B, B_HEADS, B_QK_DIM, B_V_DIM), F32), jnp.zeros((B, B_HEADS, B_QK_DIM), F32),
            jnp.full((B, B_HEADS), NEG, F32)),
        2: (jnp.zeros((B, C_HEADS, C_DK, C_DV), F32), jnp.zeros((B, CONV_W - 1, state_l2_conv.shape[-1]), F32)),
        3: None,
    }
    y_p, np_ = _trunk(x_prompt, B, T, mk_p, mv_p, st_p, W, bias, True)

    st_s = {
        0: (cache_l0_k, cache_l0_v, cache_l0_kidx),
        1: (state_l1_C, state_l1_n, state_l1_m),
        2: (state_l2_S, state_l2_conv),
        3: (cache_l3_k, cache_l3_v, cache_l3_kidx),
    }
    mk_s = cache_mem_k.reshape(depth, S, mlen, mw)
    mv_s = cache_mem_v.reshape(depth, S, mlen, mw)
    y_s, ns_ = _trunk(x_sample, S, Ts, mk_s, mv_s, st_s, W, bias, False)

    shp = (depth, B, mlen, MEM_HEADS, MEM_HEAD_DIM)
    return (y_p, y_s,
            *np_[0], *np_[1], *np_[2], *np_[3], mk_p.reshape(shp), mv_p.reshape(shp),
            *ns_[0], *ns_[1], *ns_[2], *ns_[3])
```

```python
import functools
import math

import jax
import jax.numpy as jnp
from jax import lax
from jax.experimental import pallas as pl
from jax.experimental.pallas import tpu as pltpu

F32 = jnp.float32
BF16 = jnp.bfloat16
I32 = jnp.int32

EPS = 1e-6
NEG = -1e30
CHUNK = 64
CHUNK_SHIFT = 6
LANES = 128
TILE = 128
KTILE = 256
COUNT_CHAINS = 8
VMEM_LIMIT = 56 * 1024 * 1024

A_HEADS, A_HEAD_DIM = 16, 64
IDX_HEADS, IDX_DIM = 8, 64
TOPK_MAX = 256
N_BUCKETS, MAX_DISTANCE = 32, 128
B_HEADS, B_QK_DIM, B_V_DIM = 8, 64, 128
C_HEADS, C_DK, C_DV = 8, 128, 128
CONV_W = 4
MEM_HEADS, MEM_HEAD_DIM = 4, 128


def _bf(x):
    return x.astype(BF16)


def _bf01(mask):
    return mask.astype(F32).astype(BF16)


def _dot(a, b):
    return jnp.dot(a, b, preferred_element_type=F32)


def _dot_nt(a, b):
    return lax.dot_general(a, b, (((1,), (1,)), ((), ())), preferred_element_type=F32)


def _dot_tn(a, b):
    return lax.dot_general(a, b, (((0,), (0,)), ((), ())), preferred_element_type=F32)


def _split3(x):
    hi = _bf(x)
    r1 = x - hi.astype(F32)
    mid = _bf(r1)
    lo = _bf(r1 - mid.astype(F32))
    return hi, mid, lo


def _dot_exact_rhs01(x, m01):
    hi, mid, lo = _split3(x)
    return _dot(hi, m01) + _dot(mid, m01) + _dot(lo, m01)


def _dot_exact_lhs01(m01, x):
    hi, mid, lo = _split3(x)
    return _dot(m01, hi) + _dot(m01, mid) + _dot(m01, lo)


def _dot_f32(a, b):
    ah, am, al = _split3(a)
    bh, bm, bl = _split3(b)
    return (_dot(ah, bh) + (_dot(ah, bm) + _dot(am, bh))
            + (_dot(am, bm) + _dot(ah, bl) + _dot(al, bh)))


def _rms(x, g):
    ms = jnp.mean(x * x, axis=-1, keepdims=True)
    return x * lax.rsqrt(ms + EPS) * g


def _sigmoid(x):
    return 1.0 / (1.0 + jnp.exp(-x))


def _silu(x):
    return x * _sigmoid(x)


def _softplus(x):
    return jnp.maximum(x, 0.0) + jnp.log1p(jnp.exp(-jnp.abs(x)))


def _params(*sem):
    return pltpu.CompilerParams(dimension_semantics=sem, vmem_limit_bytes=VMEM_LIMIT)


def _full(shape):
    n = len(shape)
    return pl.BlockSpec(shape, lambda *_: (0,) * n)


def _proj_kernel(x_ref, g_ref, w_ref, o_ref, *, col_chunk):
    h = _bf(_rms(x_ref[...], g_ref[...]))
    m = w_ref.shape[1]
    for c in range(0, m, col_chunk):
        e = min(c + col_chunk, m)
        o_ref[:, c:e] = _dot(h, w_ref[:, c:e])


def proj(x, g, w, tm):
    n, d = x.shape
    m = w.shape[1]
    return pl.pallas_call(
        functools.partial(_proj_kernel, col_chunk=512),
        grid=(n // tm,),
        in_specs=[pl.BlockSpec((tm, d), lambda i: (i, 0)), _full((1, d)), _full((d, m))],
        out_specs=pl.BlockSpec((tm, m), lambda i: (i, 0)),
        out_shape=jax.ShapeDtypeStruct((n, m), F32),
        compiler_params=_params("parallel"),
        name="proj",
    )(x, g.reshape(1, d), w)


def _memkv_kernel(x_ref, g_ref, wk_ref, wv_ref, kg_ref, k_ref, v_ref):
    h = _bf(_rms(x_ref[...], g_ref[0]))
    k = _dot(h, wk_ref[0])
    v_ref[0] = _dot(h, wv_ref[0])
    kg = kg_ref[0]
    for hd in range(MEM_HEADS):
        sl = slice(hd * MEM_HEAD_DIM, (hd + 1) * MEM_HEAD_DIM)
        k_ref[0, :, sl] = _rms(k[:, sl], kg)


def mem_kv(mem2d, mem_norm, w_mk, w_mv, mk_gain, tm=256):
    n, d = mem2d.shape
    depth = w_mk.shape[0]
    mw = w_mk.shape[2]
    return pl.pallas_call(
        _memkv_kernel,
        grid=(depth, n // tm),
        in_specs=[pl.BlockSpec((tm, d), lambda l, i: (i, 0)),
                  pl.BlockSpec((1, 1, d), lambda l, i: (l, 0, 0)),
                  pl.BlockSpec((1, d, mw), lambda l, i: (l, 0, 0)),
                  pl.BlockSpec((1, d, mw), lambda l, i: (l, 0, 0)),
                  pl.BlockSpec((1, 1, MEM_HEAD_DIM), lambda l, i: (l, 0, 0))],
        out_specs=[pl.BlockSpec((1, tm, mw), lambda l, i: (l, i, 0)),
                   pl.BlockSpec((1, tm, mw), lambda l, i: (l, i, 0))],
        out_shape=[jax.ShapeDtypeStruct((depth, n, mw), F32)] * 2,
        compiler_params=_params("parallel", "parallel"),
        name="mem_kv",
    )(mem2d, mem_norm.reshape(depth, 1, d), _bf(w_mk), _bf(w_mv), mk_gain.reshape(depth, 1, MEM_HEAD_DIM))


def _memattn_kernel(x_ref, o_ref, wo_ref, g_ref, wq_ref, qg_ref, mk_ref, mv_ref, wmo_ref, y_ref):
    x1 = x_ref[...] + _dot(o_ref[...], wo_ref[...])
    h = _bf(_rms(x1, g_ref[...]))
    q = _dot(h, wq_ref[...])
    qg = qg_ref[...]
    scale = MEM_HEAD_DIM ** -0.5
    outs = []
    for hd in range(MEM_HEADS):
        sl = slice(hd * MEM_HEAD_DIM, (hd + 1) * MEM_HEAD_DIM)
        qh = _bf(_rms(q[:, sl], qg))
        kh = _bf(mk_ref[0, :, sl])
        logits = _dot_nt(qh, kh) * scale
        m = jnp.max(logits, axis=-1, keepdims=True)
        p = jnp.exp(logits - m)
        p = p / jnp.sum(p, axis=-1, keepdims=True)
        outs.append(_bf(_dot(_bf(p), _bf(mv_ref[0, :, sl]))))
    att = jnp.concatenate(outs, axis=-1)
    y_ref[...] = x1 + _dot(att, wmo_ref[...])


def mem_attend(x, o, w_out, g, w_mq, mq_gain, mk, mv, w_mo, tm, tiles_per_seq):
    n, d = x.shape
    mlen, mw = mk.shape[1], mk.shape[2]
    return pl.pallas_call(
        _memattn_kernel,
        grid=(n // tm,),
        in_specs=[pl.BlockSpec((tm, d), lambda i: (i, 0)),
                  pl.BlockSpec((tm, o.shape[1]), lambda i: (i, 0)),
                  _full(w_out.shape), _full((1, d)), _full(w_mq.shape), _full((1, MEM_HEAD_DIM)),
                  pl.BlockSpec((1, mlen, mw), lambda i: (i // tiles_per_seq, 0, 0)),
                  pl.BlockSpec((1, mlen, mw), lambda i: (i // tiles_per_seq, 0, 0)),
                  _full(w_mo.shape)],
        out_specs=pl.BlockSpec((tm, d), lambda i: (i, 0)),
        out_shape=jax.ShapeDtypeStruct((n, d), F32),
        compiler_params=_params("parallel"),
        name="mem_attend",
    )(x, o, w_out, g.reshape(1, d), w_mq, mq_gain.reshape(1, MEM_HEAD_DIM), mk, mv, w_mo)


def _ffn_kernel(x_ref, g_ref, w1_ref, w3_ref, w2_ref, y_ref, *, hid_chunk):
    x = x_ref[...]
    h = _bf(_rms(x, g_ref[...]))
    hidden = w1_ref.shape[1]
    y_ref[...] = x
    for c in range(0, hidden, hid_chunk):
        a = _dot(h, w1_ref[:, c:c + hid_chunk])
        b = _dot(h, w3_ref[:, c:c + hid_chunk])
        y_ref[...] += _dot(_bf(_silu(a) * b), w2_ref[c:c + hid_chunk, :])


def ffn(x, g, w1, w3, w2, tm):
    n, d = x.shape
    hidden = w1.shape[1]
    return pl.pallas_call(
        functools.partial(_ffn_kernel, hid_chunk=256),
        grid=(n // tm,),
        in_specs=[pl.BlockSpec((tm, d), lambda i: (i, 0)), _full((1, d)),
                  _full((d, hidden)), _full((d, hidden)), _full((hidden, d))],
        out_specs=pl.BlockSpec((tm, d), lambda i: (i, 0)),
        out_shape=jax.ShapeDtypeStruct((n, d), F32),
        compiler_params=_params("parallel"),
        name="ffn",
    )(x, g.reshape(1, d), w1, w3, w2)


def _proj_dsa_kernel(x_ref, g_ref, w_ref, qg_ref, kg_ref, kig_ref, bd_ref,
                     q_o, k_o, kb_o, v_o, vb_o, qi_o, kiwi_o):
    h = _bf(_rms(x_ref[...], g_ref[...]))
    hq = A_HEADS * A_HEAD_DIM
    bd = bd_ref[...]
    inv_hd = 1.0 / A_HEAD_DIM

    def head_norm(p, gain):
        ss = _dot_exact_rhs01(p * p, bd)
        return p * lax.rsqrt(ss * inv_hd + EPS) * gain

    step = 512
    for c in range(0, hq, step):
        pq = _dot(h, w_ref[:, c:c + step])
        pk = _dot(h, w_ref[:, hq + c:hq + c + step])
        pv = _dot(h, w_ref[:, 2 * hq + c:2 * hq + c + step])
        for j in range(0, step, LANES):
            sl = slice(c + j, c + j + LANES)
            qn = head_norm(pq[:, j:j + LANES], qg_ref[:, sl])
            q_o[:, sl] = _bf(qn * (A_HEAD_DIM ** -0.5))
            kn = head_norm(pk[:, j:j + LANES], kg_ref[:, sl])
            k_o[:, sl] = kn
            kb_o[:, sl] = _bf(kn)
        v_o[:, c:c + step] = pv
        vb_o[:, c:c + step] = _bf(pv)
    qiw = IDX_HEADS * LANES
    for c in range(0, qiw, step):
        qi_o[:, c:c + step] = _bf(_dot(h, w_ref[:, 3 * hq + c:3 * hq + c + step]))
    p = _dot(h, w_ref[:, 3 * hq + qiw:3 * hq + qiw + LANES])
    lane = lax.broadcasted_iota(I32, p.shape, 1)
    is_ki = lane < IDX_DIM
    ss = jnp.sum(jnp.where(is_ki, p * p, 0.0), axis=-1, keepdims=True)
    kin = p * lax.rsqrt(ss * (1.0 / IDX_DIM) + EPS) * kig_ref[...]
    kiwi_o[...] = jnp.where(is_ki, kin, p)


def proj_dsa(x, g, wa, q_gain, k_gain, ki_gain, tm):
    n, d = x.shape
    hq = A_HEADS * A_HEAD_DIM
    m = wa.shape[1]
    qg = jnp.tile(q_gain, A_HEADS).reshape(1, hq)
    kg = jnp.tile(k_gain, A_HEADS).reshape(1, hq)
    kig = jnp.concatenate([ki_gain, jnp.ones((LANES - IDX_DIM,), F32)]).reshape(1, LANES)
    r = jnp.arange(LANES)
    bd = _bf((r[:, None] // A_HEAD_DIM) == (r[None, :] // A_HEAD_DIM))
    row = lambda w: pl.BlockSpec((tm, w), lambda i: (i, 0))
    return pl.pallas_call(
        _proj_dsa_kernel,
        grid=(n // tm,),
        in_specs=[row(d), _full((1, d)), _full((d, m)), _full((1, hq)), _full((1, hq)),
                  _full((1, LANES)), _full((LANES, LANES))],
        out_specs=[row(hq), row(hq), row(hq), row(hq), row(hq), row(IDX_HEADS * LANES), row(LANES)],
        out_shape=[jax.ShapeDtypeStruct((n, hq), BF16), jax.ShapeDtypeStruct((n, hq), F32),
                   jax.ShapeDtypeStruct((n, hq), BF16), jax.ShapeDtypeStruct((n, hq), F32),
                   jax.ShapeDtypeStruct((n, hq), BF16),
                   jax.ShapeDtypeStruct((n, IDX_HEADS * LANES), BF16),
                   jax.ShapeDtypeStruct((n, LANES), F32)],
        compiler_params=_params("parallel"),
        name="proj_dsa",
    )(x, g.reshape(1, d), wa, qg, kg, kig, bd)


def dsa_weight(w_in):
    hq = A_HEADS * A_HEAD_DIM
    o3 = 3 * hq
    o4 = o3 + IDX_HEADS * IDX_DIM
    d = w_in.shape[0]
    wqi = w_in[:, o3:o4].reshape(d, IDX_HEADS, IDX_DIM)
    wqi = jnp.pad(wqi, ((0, 0), (0, 0), (0, LANES - IDX_DIM))).reshape(d, IDX_HEADS * LANES)
    tail = jnp.pad(w_in[:, o4:], ((0, 0), (0, LANES - (w_in.shape[1] - o4))))
    return _bf(jnp.concatenate([w_in[:, :o3], wqi, tail], axis=1))


def _sortable(s):
    b = pltpu.bitcast(s, I32)
    b = jnp.where(b == jnp.int32(-2 ** 31), 0, b)
    return jnp.where(b < 0, b ^ jnp.int32(0x7FFFFFFF), b)


def _index_mask_kernel(qi_ref, wit_ref, *rest, seg_rows, seg_width, topk, causal, idx_bits):
    nseg = len(seg_rows)
    ki_refs = rest[:nseg]
    o_ref = rest[nseg]
    key_ref, sel_ref, jv_ref = rest[nseg + 1:]
    tq = qi_ref.shape[0]
    ltot = sum(seg_rows)
    i = pl.program_id(1)

    wit = wit_ref[0]
    row0 = 0
    for ki_ref, rows, width in zip(ki_refs, seg_rows, seg_width):
        ki = _bf(ki_ref[0] if len(ki_ref.shape) == 3 else ki_ref[...])
        acc = jnp.zeros((rows, tq), F32)
        for hd in range(IDX_HEADS):
            qh = qi_ref[:, hd * LANES:hd * LANES + width]
            rel = _dot_nt(ki, qh)
            acc = acc + wit[hd:hd + 1, :] * jnp.maximum(rel, 0.0)
        s = acc * ((IDX_DIM ** -0.5) * (IDX_HEADS ** -0.5))
        if causal:
            kpos = row0 + lax.broadcasted_iota(I32, (rows, tq), 0)
            qpos = i * tq + lax.broadcasted_iota(I32, (rows, tq), 1)
            s = jnp.where((kpos >> CHUNK_SHIFT) <= (qpos >> CHUNK_SHIFT), s, NEG)
        key_ref[row0:row0 + rows, :] = _sortable(s)
        row0 += rows

    def count(pred):
        c = pred.astype(I32).reshape(COUNT_CHAINS, ltot // COUNT_CHAINS, tq)
        return jnp.sum(jnp.sum(c, axis=1), axis=0, keepdims=True)

    def ge_count(cand):
        return count(key_ref[...] >= cand)

    t0 = jnp.full((1, tq), -2 ** 31, I32)
    t = jnp.where(ge_count(jnp.zeros((1, tq), I32)) >= topk, 0, t0)

    def vbody(it, t):
        cand = t + (jnp.int32(1) << (30 - it))
        return jnp.where(ge_count(cand) >= topk, cand, t)

    t = lax.fori_loop(0, 31, vbody, t)
    keys = key_ref[...]
    gt = keys > t
    eq = keys == t
    need = topk - count(gt)
    rowi = lax.broadcasted_iota(I32, (ltot, tq), 0)

    def jbody(it, jv):
        cand = jv + (jnp.int32(1) << (idx_bits - 1 - it))
        below = count(eq & (rowi < cand))
        return jnp.where(below < need, cand, jv)

    jv_ref[...] = jnp.full((1, tq), ltot, I32)
    has_tie = jnp.max(count(eq) - need) > 0

    @pl.when(has_tie)
    def _():
        jv_ref[...] = lax.fori_loop(0, idx_bits, jbody, jnp.zeros((1, tq), I32))

    sel = gt | (eq & (rowi <= jv_ref[...]))
    if causal:
        qpos = i * tq + lax.broadcasted_iota(I32, (ltot, tq), 1)
        sel = sel & ((rowi >> CHUNK_SHIFT) <= (qpos >> CHUNK_SHIFT))
    sel_ref[0:ltot, :] = jnp.where(sel, 0.0, NEG)
    pad = sel_ref.shape[0] - ltot
    if pad:
        sel_ref[ltot:, :] = jnp.full((pad, tq), NEG, F32)
    for kt in range(sel_ref.shape[0] // TILE):
        o_ref[0, 0, kt] = sel_ref[kt * TILE:(kt + 1) * TILE, :].T


def index_mask(qi, wit, ki_segs, seg_specs, seg_rows, seg_width, nseq, nqb, causal):
    ltot = sum(seg_rows)
    lpad = -(-ltot // KTILE) * KTILE
    nt = lpad // TILE
    topk = min(TOPK_MAX, ltot // 4)
    idx_bits = max(1, (ltot - 1).bit_length())
    kern = functools.partial(_index_mask_kernel, seg_rows=tuple(seg_rows), seg_width=tuple(seg_width),
                             topk=topk, causal=causal, idx_bits=idx_bits)
    return pl.pallas_call(
        kern,
        grid=(nseq, nqb),
        in_specs=[pl.BlockSpec((TILE, qi.shape[1]), lambda b, i: (b * nqb + i, 0)),
                  pl.BlockSpec((1, IDX_HEADS, TILE), lambda b, i: (b * nqb + i, 0, 0))] + list(seg_specs),
        out_specs=pl.BlockSpec((1, 1, nt, TILE, TILE), lambda b, i: (b, i, 0, 0, 0)),
        out_shape=jax.ShapeDtypeStruct((nseq, nqb, nt, TILE, TILE), F32),
        scratch_shapes=[pltpu.VMEM((ltot, TILE), I32), pltpu.VMEM((lpad, TILE), F32),
                        pltpu.VMEM((1, TILE), I32)],
        compiler_params=_params("parallel", "parallel"),
        name="index_mask",
    )(qi, wit, *ki_segs)


def _attn_core(q_ref, o_ref, qs_ref, l_ref, mx_ref, acc_ref, groups, tq):
    npairs = A_HEADS // 2
    lane = lax.broadcasted_iota(I32, (1, LANES), 1)
    keep_lo = _bf((lane < A_HEAD_DIM).astype(F32))
    keep_hi = _bf((lane >= A_HEAD_DIM).astype(F32))
    for pair in range(npairs):
        qp = q_ref[:, pair * LANES:(pair + 1) * LANES]
        qs_ref[pair, 0:tq, :] = qp * keep_lo
        qs_ref[pair, tq:2 * tq, :] = qp * keep_hi
    mx_ref[...] = jnp.full(mx_ref.shape, NEG, F32)
    acc_ref[...] = jnp.zeros(acc_ref.shape, F32)
    ones = jnp.ones((KTILE, LANES), BF16)

    def over_tiles(count, body):
        if isinstance(count, int) and count == 1:
            body(0, 0)
        else:
            lax.fori_loop(0, count, body, 0)

    for count, base, k_tile, _, mask_sub, bias_sub in groups:
        def p1(kt, carry, base=base, k_tile=k_tile, mask_sub=mask_sub, bias_sub=bias_sub):
            masks = [mask_sub(kt, sub) for sub in range(KTILE // LANES)]
            for pair in range(npairs):
                sl = slice(pair * LANES, (pair + 1) * LANES)
                l = _dot_nt(qs_ref[pair], k_tile(kt, sl))
                for half in range(2):
                    rs = slice(half * tq, (half + 1) * tq)
                    mx = mx_ref[pair, rs, :]
                    for sub in range(KTILE // LANES):
                        cs = slice(sub * LANES, (sub + 1) * LANES)
                        blk = l[rs, cs] + (masks[sub] + bias_sub(kt, sub, 2 * pair + half))
                        l_ref[pair, base + kt, rs, cs] = blk
                        mx = jnp.maximum(mx, blk)
                    mx_ref[pair, rs, :] = mx
            return carry

        over_tiles(count, p1)

    for pair in range(npairs):
        m = jnp.max(mx_ref[pair], axis=-1, keepdims=True)
        mx_ref[pair] = jnp.broadcast_to(m, mx_ref.shape[1:])

    for count, base, _, v_tile, _, _ in groups:
        def p2(kt, carry, base=base, v_tile=v_tile):
            for pair in range(npairs):
                sl = slice(pair * LANES, (pair + 1) * LANES)
                m = mx_ref[pair]
                p = jnp.concatenate(
                    [jnp.exp(l_ref[pair, base + kt, :, sub * LANES:(sub + 1) * LANES] - m)
                     for sub in range(KTILE // LANES)], axis=1)
                vaug = jnp.concatenate([v_tile(kt, sl), ones], axis=1)
                acc_ref[pair] += _dot(_bf(p), vaug)
            return carry

        over_tiles(count, p2)

    lane_full = lax.broadcasted_iota(I32, (tq, LANES), 1)
    for pair in range(npairs):
        a = acc_ref[pair]
        o = a[:, 0:LANES] / a[:, LANES:2 * LANES]
        o_ref[:, pair * LANES:(pair + 1) * LANES] = _bf(jnp.where(lane_full < A_HEAD_DIM, o[0:tq], o[tq:2 * tq]))


def _attn_scratch(tq, ntiles):
    npairs = A_HEADS // 2
    return [pltpu.VMEM((npairs, 2 * tq, LANES), BF16),
            pltpu.VMEM((npairs, ntiles, 2 * tq, KTILE), F32),
            pltpu.VMEM((npairs, 2 * tq, LANES), F32),
            pltpu.VMEM((npairs, 2 * tq, 2 * LANES), F32)]


def _attn_prompt_kernel(q_ref, k_ref, v_ref, mask_ref, bias_ref, o_ref, qs_ref, l_ref, mx_ref, acc_ref):
    i = pl.program_id(1)
    per = KTILE // TILE

    def rows(kt):
        return pl.ds(pl.multiple_of(kt * KTILE, KTILE), KTILE)

    def bias_sub(kt, sub, hd):
        s = kt * per + sub
        sel = jnp.where(s == i, 2, jnp.where(s == i - 1, 1, 0))
        return bias_ref[sel, hd]

    group = (i // per + 1, 0,
             lambda kt, sl: k_ref[rows(kt), sl],
             lambda kt, sl: v_ref[rows(kt), sl],
             lambda kt, sub: mask_ref[0, 0, kt * per + sub],
             bias_sub)
    _attn_core(q_ref, o_ref, qs_ref, l_ref, mx_ref, acc_ref, [group], TILE)


def attn_prompt(q, kb, vb, mask, bias, nseq, seqlen):
    n, hq = q.shape
    nqb = seqlen // TILE
    nt = mask.shape[2]
    return pl.pallas_call(
        _attn_prompt_kernel,
        grid=(nseq, nqb),
        in_specs=[pl.BlockSpec((TILE, hq), lambda b, i: (b * nqb + i, 0)),
                  pl.BlockSpec((seqlen, hq), lambda b, i: (b, 0)),
                  pl.BlockSpec((seqlen, hq), lambda b, i: (b, 0)),
                  pl.BlockSpec((1, 1, nt, TILE, TILE), lambda b, i: (b, i, 0, 0, 0)),
                  _full(bias.shape)],
        out_specs=pl.BlockSpec((TILE, hq), lambda b, i: (b * nqb + i, 0)),
        out_shape=jax.ShapeDtypeStruct((n, hq), BF16),
        scratch_shapes=_attn_scratch(TILE, seqlen // KTILE),
        compiler_params=_params("parallel", "arbitrary"),
        name="attn_prompt",
    )(q, kb, vb, mask, bias)


def _attn_sample_kernel(q_ref, kc_ref, vc_ref, kn_ref, vn_ref, mask_ref, bias_ref, o_ref,
                        qs_ref, l_ref, mx_ref, acc_ref, *, ncache):
    tq = q_ref.shape[0]
    per = KTILE // TILE
    last_sub = ncache * per - 1

    def rows(kt):
        return pl.ds(pl.multiple_of(kt * KTILE, KTILE), KTILE)

    def cache_bias(kt, sub, hd):
        return bias_ref[jnp.where(kt * per + sub == last_sub, 1, 0), hd]

    cache = (ncache, 0,
             lambda kt, sl: _bf(kc_ref[0, rows(kt), sl]),
             lambda kt, sl: _bf(vc_ref[0, rows(kt), sl]),
             lambda kt, sub: mask_ref[0, 0, kt * per + sub, 0:tq, :],
             cache_bias)
    fresh = (1, ncache,
             lambda kt, sl: kn_ref[0, :, sl],
             lambda kt, sl: vn_ref[0, :, sl],
             lambda kt, sub: mask_ref[0, 0, ncache * per + sub, 0:tq, :],
             lambda kt, sub, hd: bias_ref[2 if sub == 0 else 0, hd])
    _attn_core(q_ref, o_ref, qs_ref, l_ref, mx_ref, acc_ref, [cache, fresh], tq)


def attn_sample(q, kc, vc, kn, vn, mask, bias, nseq, tq):
    n, hq = q.shape
    past = kc.shape[1]
    ncache = past // KTILE
    nt = mask.shape[2]
    return pl.pallas_call(
        functools.partial(_attn_sample_kernel, ncache=ncache),
        grid=(nseq,),
        in_specs=[pl.BlockSpec((tq, hq), lambda s: (s, 0)),
                  pl.BlockSpec((1, past, hq), lambda s: (s, 0, 0)),
                  pl.BlockSpec((1, past, hq), lambda s: (s, 0, 0)),
                  pl.BlockSpec((1, KTILE, hq), lambda s: (s, 0, 0)),
                  pl.BlockSpec((1, KTILE, hq), lambda s: (s, 0, 0)),
                  pl.BlockSpec((1, 1, nt, TILE, TILE), lambda s: (s, 0, 0, 0, 0)),
                  _full(bias.shape)],
        out_specs=pl.BlockSpec((tq, hq), lambda s: (s, 0)),
        out_shape=jax.ShapeDtypeStruct((n, hq), BF16),
        scratch_shapes=_attn_scratch(tq, ncache + 1),
        compiler_params=_params("parallel"),
        name="attn_sample",
    )(q, kc, vc, kn, vn, mask, bias)


def _t5_bucket(rel):
    half = N_BUCKETS // 2
    max_exact = half // 2
    n = jnp.abs(rel)
    nf = jnp.maximum(n, 1).astype(F32)
    large = max_exact + (jnp.log(nf / max_exact) / math.log(MAX_DISTANCE / max_exact)
                         * (half - max_exact)).astype(I32)
    large = jnp.minimum(large, half - 1)
    return jnp.where(rel > 0, half, 0) + jnp.where(n < max_exact, n, large)


def bias_tiles(rel_bias):
    r = jnp.arange(TILE, dtype=I32)[:, None]
    c = jnp.arange(TILE, dtype=I32)[None, :]
    rels = jnp.stack([jnp.full((TILE, TILE), -(TILE + 1), I32), c - r - TILE, c - r])
    tab = rel_bias.astype(F32)[_t5_bucket(rels)]
    return jnp.moveaxis(tab, -1, 1)


def dsa_core_prompt(pr, nseq, seqlen, bias):
    q, _, kb, _, vb, qi, kiwi = pr
    nqb = seqlen // TILE
    wit = kiwi[:, IDX_DIM:IDX_DIM + IDX_HEADS].reshape(nseq * nqb, TILE, IDX_HEADS).swapaxes(1, 2)
    mask = index_mask(qi, wit, [kiwi], [pl.BlockSpec((seqlen, LANES), lambda b, i: (b, 0))],
                      [seqlen], [LANES], nseq, nqb, causal=True)
    return attn_prompt(q, kb, vb, mask, bias, nseq, seqlen)


def dsa_core_sample(pr, k_cache, v_cache, ki_cache, nseq, tq, bias):
    q, _, kb, _, vb, qi, kiwi = pr
    past = k_cache.shape[1]
    hq = q.shape[1]
    pad = KTILE - tq
    qi2 = jnp.concatenate([qi.reshape(nseq, tq, -1)] * (TILE // tq), axis=1).reshape(nseq * TILE, -1)
    wit = kiwi[:, IDX_DIM:IDX_DIM + IDX_HEADS].reshape(nseq, tq, IDX_HEADS).swapaxes(1, 2)
    wit = jnp.concatenate([wit] * (TILE // tq), axis=2)
    mask = index_mask(
        qi2, wit, [ki_cache, kiwi],
        [pl.BlockSpec((1, past, IDX_DIM), lambda b, i: (b, 0, 0)),
         pl.BlockSpec((tq, LANES), lambda b, i: (b, 0))],
        [past, tq], [IDX_DIM, LANES], nseq, 1, causal=False)
    kn = jnp.pad(kb.reshape(nseq, tq, hq), ((0, 0), (0, pad), (0, 0)))
    vn = jnp.pad(vb.reshape(nseq, tq, hq), ((0, 0), (0, pad), (0, 0)))
    return attn_sample(q, k_cache.reshape(nseq, past, hq), v_cache.reshape(nseq, past, hq),
                       kn, vn, mask, bias[:, :, :tq, :], nseq, tq)


def _mlstm_kernel(p_ref, gt_ref, gb_ref, gbt_ref, hg_ref, c0_ref, n0_ref, m0_ref,
                  hs_ref, c_ref, n_ref, m_ref, cs, ns, ms):
    c_id = pl.program_id(1)
    nh = B_HEADS
    hw = LANES

    @pl.when(c_id == 0)
    def _():
        cs[...] = jnp.zeros(cs.shape, F32)
        cs[:, 0:B_QK_DIM, :] = c0_ref[0]
        ns[...] = jnp.zeros(ns.shape, F32)
        ns[:, 0:B_QK_DIM] = n0_ref[0]
        ms[...] = m0_ref[0]

    L = p_ref.shape[0]
    r_id = lax.broadcasted_iota(I32, (L, L), 0)
    c_id2 = lax.broadcasted_iota(I32, (L, L), 1)
    tril = c_id2 <= r_id
    tril_b = _bf01(tril)
    triu_b = _bf01(r_id <= c_id2)
    g = p_ref[:, 4 * nh * hw:4 * nh * hw + 2 * nh] + gb_ref[...]
    gt = gt_ref[0] + gbt_ref[...]
    logf = jnp.minimum(g[:, nh:], 0.0) - jnp.log1p(jnp.exp(-jnp.abs(g[:, nh:])))
    logft = jnp.minimum(gt[nh:, :], 0.0) - jnp.log1p(jnp.exp(-jnp.abs(gt[nh:, :])))
    bcol = _dot_exact_lhs01(tril_b, logf)
    brow = _dot_exact_rhs01(logft, triu_b)
    icol = g[:, :nh]
    irow = gt[:nh, :]
    hg = hg_ref[...]
    mall = ms[...]
    for hd in range(nh):
        sl = slice(hd * hw, (hd + 1) * hw)
        qf = p_ref[:, sl]
        qh = _bf(qf)
        kh = p_ref[:, nh * hw + hd * hw:nh * hw + (hd + 1) * hw] * (B_QK_DIM ** -0.5)
        vh = _bf(p_ref[:, 2 * nh * hw + hd * hw:2 * nh * hw + (hd + 1) * hw])
        oh = p_ref[:, 3 * nh * hw + hd * hw:3 * nh * hw + (hd + 1) * hw]
        bc = bcol[:, hd:hd + 1]
        br = brow[hd:hd + 1, :]
        ic = icol[:, hd:hd + 1]
        ir = irow[hd:hd + 1, :]
        m = mall[:, hd:hd + 1]
        d = jnp.where(tril, bc - br + ir, NEG)
        inter = bc + m
        mt = jnp.maximum(inter, jnp.max(d, axis=-1, keepdims=True))
        s = _dot_nt(qh, _bf(kh)) * jnp.exp(d - mt)
        wst = jnp.exp(inter - mt)
        cmat = cs[hd]
        nrow = ns[hd:hd + 1, :]
        num = _dot(_bf(s), vh) + wst * _dot(qh, _bf(cmat))
        den = jnp.sum(s, axis=-1, keepdims=True) + wst * jnp.sum(qf * nrow, axis=-1, keepdims=True)
        hs = num / jnp.maximum(jnp.abs(den), jnp.exp(-mt))
        bl = bc[L - 1:L, :]
        dec = bl - bc + ic
        mnew = jnp.maximum(bl + m, jnp.max(dec, axis=0, keepdims=True))
        wk = jnp.exp(dec - mnew)
        ws = jnp.exp(bl + m - mnew)
        kw = kh * wk
        cs[hd] = ws * cmat + _dot_tn(_bf(kw), vh)
        ns[hd:hd + 1, :] = ws * nrow + jnp.sum(kw, axis=0, keepdims=True)
        ms[:, hd:hd + 1] = mnew
        hs_ref[:, sl] = _bf(_rms(hs, hg) * _sigmoid(oh))

    @pl.when(c_id == pl.num_programs(1) - 1)
    def _():
        c_ref[0] = cs[:, 0:B_QK_DIM, :]
        n_ref[0] = ns[:, 0:B_QK_DIM]
        m_ref[0] = ms[...]


def mlstm_weight(w_in):
    d = w_in.shape[0]
    nq = B_HEADS * B_QK_DIM
    nv = B_HEADS * B_V_DIM

    def padh(w):
        w = w.reshape(d, B_HEADS, B_QK_DIM)
        return jnp.pad(w, ((0, 0), (0, 0), (0, LANES - B_QK_DIM))).reshape(d, B_HEADS * LANES)

    main = 2 * nq + 2 * nv
    tail = jnp.pad(w_in[:, main:], ((0, 0), (0, LANES - 2 * B_HEADS)))
    return _bf(jnp.concatenate([padh(w_in[:, :nq]), padh(w_in[:, nq:2 * nq]), w_in[:, 2 * nq:main], tail], axis=1))


def mlstm_core(p, gate_bias, h_gain, c0, n0, m0, nseq, nchunk):
    n, m = p.shape
    nh = B_HEADS
    gcol = 4 * nh * LANES
    gt = p[:, gcol:gcol + 2 * nh].reshape(nseq * nchunk, CHUNK, 2 * nh).swapaxes(1, 2)
    return pl.pallas_call(
        _mlstm_kernel,
        grid=(nseq, nchunk),
        in_specs=[pl.BlockSpec((CHUNK, m), lambda s, c: (s * nchunk + c, 0)),
                  pl.BlockSpec((1, 2 * nh, CHUNK), lambda s, c: (s * nchunk + c, 0, 0)),
                  _full((1, 2 * nh)), _full((2 * nh, 1)), _full((1, B_V_DIM)),
                  pl.BlockSpec((1, nh, B_QK_DIM, B_V_DIM), lambda s, c: (s, 0, 0, 0)),
                  pl.BlockSpec((1, nh, B_QK_DIM), lambda s, c: (s, 0, 0)),
                  pl.BlockSpec((1, 1, nh), lambda s, c: (s, 0, 0))],
        out_specs=[pl.BlockSpec((CHUNK, nh * B_V_DIM), lambda s, c: (s * nchunk + c, 0)),
                   pl.BlockSpec((1, nh, B_QK_DIM, B_V_DIM), lambda s, c: (s, 0, 0, 0)),
                   pl.BlockSpec((1, nh, B_QK_DIM), lambda s, c: (s, 0, 0)),
                   pl.BlockSpec((1, 1, nh), lambda s, c: (s, 0, 0))],
        out_shape=[jax.ShapeDtypeStruct((n, nh * B_V_DIM), BF16),
                   jax.ShapeDtypeStruct((nseq, nh, B_QK_DIM, B_V_DIM), F32),
                   jax.ShapeDtypeStruct((nseq, nh, B_QK_DIM), F32),
                   jax.ShapeDtypeStruct((nseq, 1, nh), F32)],
        scratch_shapes=[pltpu.VMEM((nh, LANES, B_V_DIM), F32), pltpu.VMEM((nh, LANES), F32),
                        pltpu.VMEM((1, nh), F32)],
        compiler_params=_params("parallel", "arbitrary"),
        name="mlstm",
    )(p, gt, gate_bias.reshape(1, 2 * nh), gate_bias.reshape(2 * nh, 1), h_gain.reshape(1, B_V_DIM),
      c0, n0, m0.reshape(nseq, 1, nh))


def _gdn_kernel(p_ref, gt_ref, cw_ref, alog_ref, alogt_ref, dtb_ref, dtbt_ref, og_ref, s0_ref, cb0_ref,
                o_ref, s_ref, cb_ref, ss, tail):
    c_id = pl.program_id(1)
    nh = C_HEADS
    cdim = nh * (2 * C_DK + C_DV)
    L = p_ref.shape[0]
    nprev = CONV_W - 1

    @pl.when(c_id == 0)
    def _():
        ss[...] = s0_ref[0]
        tail[...] = jnp.zeros(tail.shape, F32)
        tail[8 - nprev:8, :] = cb0_ref[0]

    x = p_ref[:, 0:cdim]
    ext = jnp.concatenate([tail[...], x], axis=0)
    conv = ext[8:8 + L] * cw_ref[CONV_W - 1:CONV_W, :]
    for j in range(CONV_W - 1):
        conv = conv + ext[8 - nprev + j:8 - nprev + j + L] * cw_ref[j:j + 1, :]
    tail[...] = x[L - 8:L, :]
    cf = _silu(conv)

    gcolumn = cdim + nh * C_DV
    pre = p_ref[:, gcolumn:gcolumn + 2 * nh]
    pret = gt_ref[0]
    beta = _sigmoid(pre[:, :nh])
    g = -jnp.exp(alog_ref[...]) * _softplus(pre[:, nh:] + dtb_ref[...])
    gtr = -jnp.exp(alogt_ref[...]) * _softplus(pret[nh:, :] + dtbt_ref[...])
    r = lax.broadcasted_iota(I32, (L, L), 0)
    c = lax.broadcasted_iota(I32, (L, L), 1)
    incl = c <= r
    strict = c < r
    eye = (c == r).astype(F32)
    gc_col = _dot_exact_lhs01(_bf01(incl), g)
    gc_row = _dot_exact_rhs01(gtr, _bf01(r <= c))
    og = og_ref[...]
    for hd in range(nh):
        qraw = cf[:, hd * C_DK:(hd + 1) * C_DK]
        kraw = cf[:, nh * C_DK + hd * C_DK:nh * C_DK + (hd + 1) * C_DK]
        vc = cf[:, 2 * nh * C_DK + hd * C_DV:2 * nh * C_DK + (hd + 1) * C_DV]
        z = p_ref[:, cdim + hd * C_DV:cdim + (hd + 1) * C_DV]
        qc = qraw * lax.rsqrt(jnp.sum(qraw * qraw, axis=-1, keepdims=True) + EPS) * (C_DK ** -0.5)
        kc = kraw * lax.rsqrt(jnp.sum(kraw * kraw, axis=-1, keepdims=True) + EPS)
        bc = beta[:, hd:hd + 1]
        gcol = gc_col[:, hd:hd + 1]
        grow = gc_row[hd:hd + 1, :]
        dm = jnp.where(incl, jnp.exp(jnp.where(incl, gcol - grow, 0.0)), 0.0)
        kb = kc * bc
        kcb = _bf(kc)
        amat = jnp.where(strict, _dot_nt(_bf(kb), kcb) * dm, 0.0)
        egc = jnp.exp(gcol)
        rhs = jnp.concatenate([vc * bc, kb * egc], axis=-1)
        nm = -amat
        tinv = eye + nm
        pw = nm
        for _ in range(int(math.log2(L)) - 1):
            pw = _dot_f32(pw, pw)
            tinv = tinv + _dot_f32(tinv, pw)
        sol = _dot_f32(tinv, rhs)
        u = sol[:, :C_DV]
        w = sol[:, C_DV:]
        smat = ss[hd]
        sb = _bf(smat)
        vnew = u - _dot(_bf(w), sb)
        attn = _dot_nt(_bf(qc), kcb) * dm
        o = _dot(_bf(qc * egc), sb) + _dot(_bf(attn), _bf(vnew))
        glast = gcol[L - 1:L, :]
        ss[hd] = smat * jnp.exp(glast) + _dot_tn(_bf(kc * jnp.exp(glast - gcol)), _bf(vnew))
        o_ref[:, hd * C_DV:(hd + 1) * C_DV] = _bf(_rms(o, og) * _silu(z))

    @pl.when(c_id == pl.num_programs(1) - 1)
    def _():
        s_ref[0] = ss[...]
        cb_ref[0] = tail[8 - nprev:8, :]


def gdn_weight(w_in):
    main = C_HEADS * (2 * C_DK + C_DV) + C_HEADS * C_DV
    tail = jnp.pad(w_in[:, main:], ((0, 0), (0, LANES - 2 * C_HEADS)))
    return _bf(jnp.concatenate([w_in[:, :main], tail], axis=1))


def gdn_core(p, conv_w, a_log, dt_bias, o_gain, s0, cb0, nseq, nchunk):
    n, m = p.shape
    nh = C_HEADS
    cdim = nh * (2 * C_DK + C_DV)
    gcolumn = cdim + nh * C_DV
    gt = p[:, gcolumn:gcolumn + 2 * nh].reshape(nseq * nchunk, CHUNK, 2 * nh).swapaxes(1, 2)
    return pl.pallas_call(
        _gdn_kernel,
        grid=(nseq, nchunk),
        in_specs=[pl.BlockSpec((CHUNK, m), lambda s, c: (s * nchunk + c, 0)),
                  pl.BlockSpec((1, 2 * nh, CHUNK), lambda s, c: (s * nchunk + c, 0, 0)),
                  _full((CONV_W, cdim)), _full((1, nh)), _full((nh, 1)), _full((1, nh)), _full((nh, 1)),
                  _full((1, C_DV)),
                  pl.BlockSpec((1, nh, C_DK, C_DV), lambda s, c: (s, 0, 0, 0)),
                  pl.BlockSpec((1, CONV_W - 1, cdim), lambda s, c: (s, 0, 0))],
        out_specs=[pl.BlockSpec((CHUNK, nh * C_DV), lambda s, c: (s * nchunk + c, 0)),
                   pl.BlockSpec((1, nh, C_DK, C_DV), lambda s, c: (s, 0, 0, 0)),
                   pl.BlockSpec((1, CONV_W - 1, cdim), lambda s, c: (s, 0, 0))],
        out_shape=[jax.ShapeDtypeStruct((n, nh * C_DV), BF16),
                   jax.ShapeDtypeStruct((nseq, nh, C_DK, C_DV), F32),
                   jax.ShapeDtypeStruct((nseq, CONV_W - 1, cdim), F32)],
        scratch_shapes=[pltpu.VMEM((nh, C_DK, C_DV), F32), pltpu.VMEM((8, cdim), F32)],
        compiler_params=_params("parallel", "arbitrary"),
        name="gdn",
    )(p, gt, conv_w, a_log.reshape(1, nh), a_log.reshape(nh, 1), dt_bias.reshape(1, nh),
      dt_bias.reshape(nh, 1), o_gain.reshape(1, C_DV), s0, cb0)


def _trunk(x, nseq, seqlen, mem_k, mem_v, st, W, bias, is_prompt):
    d = x.shape[-1]
    n = nseq * seqlen
    x = x.reshape(n, d)
    tm = 256
    tm_mem = min(256, seqlen)
    new = {}
    for i in range(4):
        kind = i % 3
        mx = W["mixer"][i]
        if kind == 0:
            pr = proj_dsa(x, W["norm_mix"][i], mx["w_in"], mx["q_gain"], mx["k_gain"], mx["ki_gain"], tm)
            if is_prompt:
                o = dsa_core_prompt(pr, nseq, seqlen, bias)
            else:
                o = dsa_core_sample(pr, *st[i], nseq, seqlen, bias)
            new[i] = (pr[1].reshape(nseq, seqlen, A_HEADS, A_HEAD_DIM),
                      pr[3].reshape(nseq, seqlen, A_HEADS, A_HEAD_DIM),
                      pr[6][:, :IDX_DIM].reshape(nseq, seqlen, IDX_DIM))
        elif kind == 1:
            p = proj(x, W["norm_mix"][i], mx["w_in"], tm)
            o, c_new, n_new, m_new = mlstm_core(p, mx["gate_bias"], mx["h_gain"], *st[i], nseq, seqlen // CHUNK)
            new[i] = (c_new, n_new, m_new.reshape(nseq, B_HEADS))
        else:
            p = proj(x, W["norm_mix"][i], mx["w_in"], tm)
            o, s_new, cb_new = gdn_core(p, mx["conv_w"], mx["a_log"], mx["dt_bias"], mx["o_gain"], *st[i],
                                        nseq, seqlen // CHUNK)
            new[i] = (s_new, cb_new)
        x = mem_attend(x, o, mx["w_out"], W["norm_mem"][i], W["w_mq"][i], W["mq_gain"][i],
                       mem_k[i], mem_v[i], W["w_mo"][i], tm_mem, seqlen // tm_mem)
        x = ffn(x, W["norm_ffn"][i], W["w_ffn1"][i], W["w_ffn3"][i], W["w_ffn2"][i], tm)
    return x.reshape(nseq, seqlen, d), new


def kernel(x_prompt, x_sample, mem_prompt, cache_l0_k, cache_l0_v, cache_l0_kidx, state_l1_C, state_l1_n, state_l1_m, state_l2_S, state_l2_conv, cache_l3_k, cache_l3_v, cache_l3_kidx, cache_mem_k, cache_mem_v, rel_bias, norm_mix, norm_mem, norm_ffn, mem_norm, w_mq, w_mk, w_mv, w_mo, mq_gain, mk_gain, w_ffn1, w_ffn3, w_ffn2, a0_w_in, a0_w_out, a0_q_gain, a0_k_gain, a0_kidx_gain, b1_w_in, b1_gate_bias, b1_h_gain, b1_w_out, c2_w_in, c2_conv_w, c2_a_log, c2_dt_bias, c2_o_gain, c2_w_out, a3_w_in, a3_w_out, a3_q_gain, a3_k_gain, a3_kidx_gain):
    B, T, D = x_prompt.shape
    S, Ts, _ = x_sample.shape
    depth = w_mq.shape[0]
    mlen = mem_prompt.shape[1]
    mw = MEM_HEADS * MEM_HEAD_DIM

    def dsa_w(w_in, w_out, qg, kg, kig):
        return dict(w_in=dsa_weight(w_in), w_out=_bf(w_out), q_gain=qg, k_gain=kg, ki_gain=kig)

    W = dict(
        norm_mix=norm_mix, norm_mem=norm_mem, norm_ffn=norm_ffn,
        w_mq=_bf(w_mq), w_mo=_bf(w_mo), mq_gain=mq_gain,
        w_ffn1=_bf(w_ffn1), w_ffn3=_bf(w_ffn3), w_ffn2=_bf(w_ffn2),
        mixer={
            0: dsa_w(a0_w_in, a0_w_out, a0_q_gain, a0_k_gain, a0_kidx_gain),
            1: dict(w_in=mlstm_weight(b1_w_in), gate_bias=b1_gate_bias, h_gain=b1_h_gain, w_out=_bf(b1_w_out)),
            2: dict(w_in=gdn_weight(c2_w_in), conv_w=c2_conv_w, a_log=c2_a_log, dt_bias=c2_dt_bias,
                    o_gain=c2_o_gain, w_out=_bf(c2_w_out)),
            3: dsa_w(a3_w_in, a3_w_out, a3_q_gain, a3_k_gain, a3_kidx_gain),
        },
    )
    bias = bias_tiles(rel_bias)

    mk_p, mv_p = mem_kv(mem_prompt.reshape(B * mlen, D), mem_norm, w_mk, w_mv, mk_gain)
    mk_p = mk_p.reshape(depth, B, mlen, mw)
    mv_p = mv_p.reshape(depth, B, mlen, mw)
    st_p = {
        0: None,
        1: (jnp.zeros((B, B_HEADS, B_QK_DIM, B_V_DIM), F32), jnp.zeros((B, B_HEADS, B_QK_DIM), F32),
            jnp.full((B, B_HEADS), NEG, F32)),
        2: (jnp.zeros((B, C_HEADS, C_DK, C_DV), F32), jnp.zeros((B, CONV_W - 1, state_l2_conv.shape[-1]), F32)),
        3: None,
    }
    y_p, np_ = _trunk(x_prompt, B, T, mk_p, mv_p, st_p, W, bias, True)

    st_s = {
        0: (cache_l0_k, cache_l0_v, cache_l0_kidx),
        1: (state_l1_C, state_l1_n, state_l1_m),
        2: (state_l2_S, state_l2_conv),
        3: (cache_l3_k, cache_l3_v, cache_l3_kidx),
    }
    mk_s = cache_mem_k.reshape(depth, S, mlen, mw)
    mv_s = cache_mem_v.reshape(depth, S, mlen, mw)
    y_s, ns_ = _trunk(x_sample, S, Ts, mk_s, mv_s, st_s, W, bias, False)

    shp = (depth, B, mlen, MEM_HEADS, MEM_HEAD_DIM)
    return (y_p, y_s,
            *np_[0], *np_[1], *np_[2], *np_[3], mk_p.reshape(shp), mv_p.reshape(shp),
            *ns_[0], *ns_[1], *ns_[2], *ns_[3])
```

```python
import functools
import math

import jax
import jax.numpy as jnp
from jax import lax
from jax.experimental import pallas as pl
from jax.experimental.pallas import tpu as pltpu

F32 = jnp.float32
BF16 = jnp.bfloat16
I32 = jnp.int32

EPS = 1e-6
NEG = -1e30
CHUNK = 64
CHUNK_SHIFT = 6
LANES = 128
TILE = 128
KTILE = 256
COUNT_CHAINS = 8
VMEM_LIMIT = 56 * 1024 * 1024

A_HEADS, A_HEAD_DIM = 16, 64
IDX_HEADS, IDX_DIM = 8, 64
TOPK_MAX = 256
N_BUCKETS, MAX_DISTANCE = 32, 128
B_HEADS, B_QK_DIM, B_V_DIM = 8, 64, 128
C_HEADS, C_DK, C_DV = 8, 128, 128
CONV_W = 4
MEM_HEADS, MEM_HEAD_DIM = 4, 128


def _bf(x):
    return x.astype(BF16)


def _bf01(mask):
    return mask.astype(F32).astype(BF16)


def _dot(a, b):
    return jnp.dot(a, b, preferred_element_type=F32)


def _dot_nt(a, b):
    return lax.dot_general(a, b, (((1,), (1,)), ((), ())), preferred_element_type=F32)


def _dot_tn(a, b):
    return lax.dot_general(a, b, (((0,), (0,)), ((), ())), preferred_element_type=F32)


def _split3(x):
    hi = _bf(x)
    r1 = x - hi.astype(F32)
    mid = _bf(r1)
    lo = _bf(r1 - mid.astype(F32))
    return hi, mid, lo


def _dot_exact_rhs01(x, m01):
    hi, mid, lo = _split3(x)
    return _dot(hi, m01) + _dot(mid, m01) + _dot(lo, m01)


def _dot_exact_lhs01(m01, x):
    hi, mid, lo = _split3(x)
    return _dot(m01, hi) + _dot(m01, mid) + _dot(m01, lo)


def _dot_f32(a, b):
    ah, am, al = _split3(a)
    bh, bm, bl = _split3(b)
    return (_dot(ah, bh) + (_dot(ah, bm) + _dot(am, bh))
            + (_dot(am, bm) + _dot(ah, bl) + _dot(al, bh)))


def _rms(x, g):
    ms = jnp.mean(x * x, axis=-1, keepdims=True)
    return x * lax.rsqrt(ms + EPS) * g


def _sigmoid(x):
    return 1.0 / (1.0 + jnp.exp(-x))


def _silu(x):
    return x * _sigmoid(x)


def _softplus(x):
    return jnp.maximum(x, 0.0) + jnp.log1p(jnp.exp(-jnp.abs(x)))


def _params(*sem):
    return pltpu.CompilerParams(dimension_semantics=sem, vmem_limit_bytes=VMEM_LIMIT)


def _full(shape):
    n = len(shape)
    return pl.BlockSpec(shape, lambda *_: (0,) * n)


def _proj_kernel(x_ref, g_ref, w_ref, o_ref, *, col_chunk):
    h = _bf(_rms(x_ref[...], g_ref[...]))
    m = w_ref.shape[1]
    for c in range(0, m, col_chunk):
        e = min(c + col_chunk, m)
        o_ref[:, c:e] = _dot(h, w_ref[:, c:e])


def proj(x, g, w, tm):
    n, d = x.shape
    m = w.shape[1]
    return pl.pallas_call(
        functools.partial(_proj_kernel, col_chunk=512),
        grid=(n // tm,),
        in_specs=[pl.BlockSpec((tm, d), lambda i: (i, 0)), _full((1, d)), _full((d, m))],
        out_specs=pl.BlockSpec((tm, m), lambda i: (i, 0)),
        out_shape=jax.ShapeDtypeStruct((n, m), F32),
        compiler_params=_params("parallel"),
        name="proj",
    )(x, g.reshape(1, d), w)


def _memkv_kernel(x_ref, g_ref, wk_ref, wv_ref, kg_ref, k_ref, v_ref):
    h = _bf(_rms(x_ref[...], g_ref[0]))
    k = _dot(h, wk_ref[0])
    v_ref[0] = _dot(h, wv_ref[0])
    kg = kg_ref[0]
    for hd in range(MEM_HEADS):
        sl = slice(hd * MEM_HEAD_DIM, (hd + 1) * MEM_HEAD_DIM)
        k_ref[0, :, sl] = _rms(k[:, sl], kg)


def mem_kv(mem2d, mem_norm, w_mk, w_mv, mk_gain, tm=256):
    n, d = mem2d.shape
    depth = w_mk.shape[0]
    mw = w_mk.shape[2]
    return pl.pallas_call(
        _memkv_kernel,
        grid=(depth, n // tm),
        in_specs=[pl.BlockSpec((tm, d), lambda l, i: (i, 0)),
                  pl.BlockSpec((1, 1, d), lambda l, i: (l, 0, 0)),
                  pl.BlockSpec((1, d, mw), lambda l, i: (l, 0, 0)),
                  pl.BlockSpec((1, d, mw), lambda l, i: (l, 0, 0)),
                  pl.BlockSpec((1, 1, MEM_HEAD_DIM), lambda l, i: (l, 0, 0))],
        out_specs=[pl.BlockSpec((1, tm, mw), lambda l, i: (l, i, 0)),
                   pl.BlockSpec((1, tm, mw), lambda l, i: (l, i, 0))],
        out_shape=[jax.ShapeDtypeStruct((depth, n, mw), F32)] * 2,
        compiler_params=_params("parallel", "parallel"),
        name="mem_kv",
    )(mem2d, mem_norm.reshape(depth, 1, d), _bf(w_mk), _bf(w_mv), mk_gain.reshape(depth, 1, MEM_HEAD_DIM))


def _memattn_kernel(x_ref, o_ref, wo_ref, g_ref, wq_ref, qg_ref, mk_ref, mv_ref, wmo_ref, y_ref):
    x1 = x_ref[...] + _dot(o_ref[...], wo_ref[...])
    h = _bf(_rms(x1, g_ref[...]))
    q = _dot(h, wq_ref[...])
    qg = qg_ref[...]
    scale = MEM_HEAD_DIM ** -0.5
    outs = []
    for hd in range(MEM_HEADS):
        sl = slice(hd * MEM_HEAD_DIM, (hd + 1) * MEM_HEAD_DIM)
        qh = _bf(_rms(q[:, sl], qg))
        kh = _bf(mk_ref[0, :, sl])
        logits = _dot_nt(qh, kh) * scale
        m = jnp.max(logits, axis=-1, keepdims=True)
        p = jnp.exp(logits - m)
        p = p / jnp.sum(p, axis=-1, keepdims=True)
        outs.append(_bf(_dot(_bf(p), _bf(mv_ref[0, :, sl]))))
    att = jnp.concatenate(outs, axis=-1)
    y_ref[...] = x1 + _dot(att, wmo_ref[...])


def mem_attend(x, o, w_out, g, w_mq, mq_gain, mk, mv, w_mo, tm, tiles_per_seq):
    n, d = x.shape
    mlen, mw = mk.shape[1], mk.shape[2]
    return pl.pallas_call(
        _memattn_kernel,
        grid=(n // tm,),
        in_specs=[pl.BlockSpec((tm, d), lambda i: (i, 0)),
                  pl.BlockSpec((tm, o.shape[1]), lambda i: (i, 0)),
                  _full(w_out.shape), _full((1, d)), _full(w_mq.shape), _full((1, MEM_HEAD_DIM)),
                  pl.BlockSpec((1, mlen, mw), lambda i: (i // tiles_per_seq, 0, 0)),
                  pl.BlockSpec((1, mlen, mw), lambda i: (i // tiles_per_seq, 0, 0)),
                  _full(w_mo.shape)],
        out_specs=pl.BlockSpec((tm, d), lambda i: (i, 0)),
        out_shape=jax.ShapeDtypeStruct((n, d), F32),
        compiler_params=_params("parallel"),
        name="mem_attend",
    )(x, o, w_out, g.reshape(1, d), w_mq, mq_gain.reshape(1, MEM_HEAD_DIM), mk, mv, w_mo)


def _ffn_kernel(x_ref, g_ref, w1_ref, w3_ref, w2_ref, y_ref, *, hid_chunk):
    x = x_ref[...]
    h = _bf(_rms(x, g_ref[...]))
    hidden = w1_ref.shape[1]
    y_ref[...] = x
    for c in range(0, hidden, hid_chunk):
        a = _dot(h, w1_ref[:, c:c + hid_chunk])
        b = _dot(h, w3_ref[:, c:c + hid_chunk])
        y_ref[...] += _dot(_bf(_silu(a) * b), w2_ref[c:c + hid_chunk, :])


def ffn(x, g, w1, w3, w2, tm):
    n, d = x.shape
    hidden = w1.shape[1]
    return pl.pallas_call(
        functools.partial(_ffn_kernel, hid_chunk=256),
        grid=(n // tm,),
        in_specs=[pl.BlockSpec((tm, d), lambda i: (i, 0)), _full((1, d)),
                  _full((d, hidden)), _full((d, hidden)), _full((hidden, d))],
        out_specs=pl.BlockSpec((tm, d), lambda i: (i, 0)),
        out_shape=jax.ShapeDtypeStruct((n, d), F32),
        compiler_params=_params("parallel"),
        name="ffn",
    )(x, g.reshape(1, d), w1, w3, w2)


def _proj_dsa_kernel(x_ref, g_ref, w_ref, qg_ref, kg_ref, kig_ref, bd_ref,
                     q_o, k_o, kb_o, v_o, vb_o, qi_o, kiwi_o):
    h = _bf(_rms(x_ref[...], g_ref[...]))
    hq = A_HEADS * A_HEAD_DIM
    bd = bd_ref[...]
    inv_hd = 1.0 / A_HEAD_DIM

    def head_norm(p, gain):
        ss = _dot_exact_rhs01(p * p, bd)
        return p * lax.rsqrt(ss * inv_hd + EPS) * gain

    step = 512
    for c in range(0, hq, step):
        pq = _dot(h, w_ref[:, c:c + step])
        pk = _dot(h, w_ref[:, hq + c:hq + c + step])
        pv = _dot(h, w_ref[:, 2 * hq + c:2 * hq + c + step])
        for j in range(0, step, LANES):
            sl = slice(c + j, c + j + LANES)
            qn = head_norm(pq[:, j:j + LANES], qg_ref[:, sl])
            q_o[:, sl] = _bf(qn * (A_HEAD_DIM ** -0.5))
            kn = head_norm(pk[:, j:j + LANES], kg_ref[:, sl])
            k_o[:, sl] = kn
            kb_o[:, sl] = _bf(kn)
        v_o[:, c:c + step] = pv
        vb_o[:, c:c + step] = _bf(pv)
    qiw = IDX_HEADS * LANES
    for c in range(0, qiw, step):
        qi_o[:, c:c + step] = _bf(_dot(h, w_ref[:, 3 * hq + c:3 * hq + c + step]))
    p = _dot(h, w_ref[:, 3 * hq + qiw:3 * hq + qiw + LANES])
    lane = lax.broadcasted_iota(I32, p.shape, 1)
    is_ki = lane < IDX_DIM
    ss = jnp.sum(jnp.where(is_ki, p * p, 0.0), axis=-1, keepdims=True)
    kin = p * lax.rsqrt(ss * (1.0 / IDX_DIM) + EPS) * kig_ref[...]
    kiwi_o[...] = jnp.where(is_ki, kin, p)


def proj_dsa(x, g, wa, q_gain, k_gain, ki_gain, tm):
    n, d = x.shape
    hq = A_HEADS * A_HEAD_DIM
    m = wa.shape[1]
    qg = jnp.tile(q_gain, A_HEADS).reshape(1, hq)
    kg = jnp.tile(k_gain, A_HEADS).reshape(1, hq)
    kig = jnp.concatenate([ki_gain, jnp.ones((LANES - IDX_DIM,), F32)]).reshape(1, LANES)
    r = jnp.arange(LANES)
    bd = _bf((r[:, None] // A_HEAD_DIM) == (r[None, :] // A_HEAD_DIM))
    row = lambda w: pl.BlockSpec((tm, w), lambda i: (i, 0))
    return pl.pallas_call(
        _proj_dsa_kernel,
        grid=(n // tm,),
        in_specs=[row(d), _full((1, d)), _full((d, m)), _full((1, hq)), _full((1, hq)),
                  _full((1, LANES)), _full((LANES, LANES))],
        out_specs=[row(hq), row(hq), row(hq), row(hq), row(hq), row(IDX_HEADS * LANES), row(LANES)],
        out_shape=[jax.ShapeDtypeStruct((n, hq), BF16), jax.ShapeDtypeStruct((n, hq), F32),
                   jax.ShapeDtypeStruct((n, hq), BF16), jax.ShapeDtypeStruct((n, hq), F32),
                   jax.ShapeDtypeStruct((n, hq), BF16),
                   jax.ShapeDtypeStruct((n, IDX_HEADS * LANES), BF16),
                   jax.ShapeDtypeStruct((n, LANES), F32)],
        compiler_params=_params("parallel"),
        name="proj_dsa",
    )(x, g.reshape(1, d), wa, qg, kg, kig, bd)


def dsa_weight(w_in):
    hq = A_HEADS * A_HEAD_DIM
    o3 = 3 * hq
    o4 = o3 + IDX_HEADS * IDX_DIM
    d = w_in.shape[0]
    wqi = w_in[:, o3:o4].reshape(d, IDX_HEADS, IDX_DIM)
    wqi = jnp.pad(wqi, ((0, 0), (0, 0), (0, LANES - IDX_DIM))).reshape(d, IDX_HEADS * LANES)
    tail = jnp.pad(w_in[:, o4:], ((0, 0), (0, LANES - (w_in.shape[1] - o4))))
    return _bf(jnp.concatenate([w_in[:, :o3], wqi, tail], axis=1))


def _sortable(s):
    b = pltpu.bitcast(s, I32)
    b = jnp.where(b == jnp.int32(-2 ** 31), 0, b)
    return jnp.where(b < 0, b ^ jnp.int32(0x7FFFFFFF), b)


def _index_mask_kernel(qi_ref, wit_ref, *rest, seg_rows, seg_width, topk, causal, idx_bits):
    nseg = len(seg_rows)
    ki_refs = rest[:nseg]
    o_ref = rest[nseg]
    key_ref, sel_ref, jv_ref = rest[nseg + 1:]
    tq = qi_ref.shape[0]
    ltot = sum(seg_rows)
    i = pl.program_id(1)

    wit = wit_ref[0]
    row0 = 0
    for ki_ref, rows, width in zip(ki_refs, seg_rows, seg_width):
        ki = _bf(ki_ref[0] if len(ki_ref.shape) == 3 else ki_ref[...])
        acc = jnp.zeros((rows, tq), F32)
        for hd in range(IDX_HEADS):
            qh = qi_ref[:, hd * LANES:hd * LANES + width]
            rel = _dot_nt(ki, qh)
            acc = acc + wit[hd:hd + 1, :] * jnp.maximum(rel, 0.0)
        s = acc * ((IDX_DIM ** -0.5) * (IDX_HEADS ** -0.5))
        if causal:
            kpos = row0 + lax.broadcasted_iota(I32, (rows, tq), 0)
            qpos = i * tq + lax.broadcasted_iota(I32, (rows, tq), 1)
            s = jnp.where((kpos >> CHUNK_SHIFT) <= (qpos >> CHUNK_SHIFT), s, NEG)
        key_ref[row0:row0 + rows, :] = _sortable(s)
        row0 += rows

    def count(pred):
        c = pred.astype(I32).reshape(COUNT_CHAINS, ltot // COUNT_CHAINS, tq)
        return jnp.sum(jnp.sum(c, axis=1), axis=0, keepdims=True)

    def ge_count(cand):
        return count(key_ref[...] >= cand)

    t0 = jnp.full((1, tq), -2 ** 31, I32)
    t = jnp.where(ge_count(jnp.zeros((1, tq), I32)) >= topk, 0, t0)

    def vbody(it, t):
        cand = t + (jnp.int32(1) << (30 - it))
        return jnp.where(ge_count(cand) >= topk, cand, t)

    t = lax.fori_loop(0, 31, vbody, t)
    keys = key_ref[...]
    gt = keys > t
    eq = keys == t
    need = topk - count(gt)
    rowi = lax.broadcasted_iota(I32, (ltot, tq), 0)

    def jbody(it, jv):
        cand = jv + (jnp.int32(1) << (idx_bits - 1 - it))
        below = count(eq & (rowi < cand))
        return jnp.where(below < need, cand, jv)

    jv_ref[...] = jnp.full((1, tq), ltot, I32)
    has_tie = jnp.max(count(eq) - need) > 0

    @pl.when(has_tie)
    def _():
        jv_ref[...] = lax.fori_loop(0, idx_bits, jbody, jnp.zeros((1, tq), I32))

    sel = gt | (eq & (rowi <= jv_ref[...]))
    if causal:
        qpos = i * tq + lax.broadcasted_iota(I32, (ltot, tq), 1)
        sel = sel & ((rowi >> CHUNK_SHIFT) <= (qpos >> CHUNK_SHIFT))
    sel_ref[0:ltot, :] = jnp.where(sel, 0.0, NEG)
    pad = sel_ref.shape[0] - ltot
    if pad:
        sel_ref[ltot:, :] = jnp.full((pad, tq), NEG, F32)
    for kt in range(sel_ref.shape[0] // TILE):
        o_ref[0, 0, kt] = sel_ref[kt * TILE:(kt + 1) * TILE, :].T


def index_mask(qi, wit, ki_segs, seg_specs, seg_rows, seg_width, nseq, nqb, causal):
    ltot = sum(seg_rows)
    lpad = -(-ltot // KTILE) * KTILE
    nt = lpad // TILE
    topk = min(TOPK_MAX, ltot // 4)
    idx_bits = max(1, (ltot - 1).bit_length())
    kern = functools.partial(_index_mask_kernel, seg_rows=tuple(seg_rows), seg_width=tuple(seg_width),
                             topk=topk, causal=causal, idx_bits=idx_bits)
    return pl.pallas_call(
        kern,
        grid=(nseq, nqb),
        in_specs=[pl.BlockSpec((TILE, qi.shape[1]), lambda b, i: (b * nqb + i, 0)),
                  pl.BlockSpec((1, IDX_HEADS, TILE), lambda b, i: (b * nqb + i, 0, 0))] + list(seg_specs),
        out_specs=pl.BlockSpec((1, 1, nt, TILE, TILE), lambda b, i: (b, i, 0, 0, 0)),
        out_shape=jax.ShapeDtypeStruct((nseq, nqb, nt, TILE, TILE), F32),
        scratch_shapes=[pltpu.VMEM((ltot, TILE), I32), pltpu.VMEM((lpad, TILE), F32),
                        pltpu.VMEM((1, TILE), I32)],
        compiler_params=_params("parallel", "parallel"),
        name="index_mask",
    )(qi, wit, *ki_segs)


def _attn_core(q_ref, o_ref, qs_ref, l_ref, mx_ref, acc_ref, groups, tq):
    npairs = A_HEADS // 2
    lane = lax.broadcasted_iota(I32, (1, LANES), 1)
    keep_lo = _bf((lane < A_HEAD_DIM).astype(F32))
    keep_hi = _bf((lane >= A_HEAD_DIM).astype(F32))
    for pair in range(npairs):
        qp = q_ref[:, pair * LANES:(pair + 1) * LANES]
        qs_ref[pair, 0:tq, :] = qp * keep_lo
        qs_ref[pair, tq:2 * tq, :] = qp * keep_hi
    mx_ref[...] = jnp.full(mx_ref.shape, NEG, F32)
    acc_ref[...] = jnp.zeros(acc_ref.shape, F32)
    ones = jnp.ones((KTILE, LANES), BF16)

    def over_tiles(count, body):
        if isinstance(count, int) and count == 1:
            body(0, 0)
        else:
            lax.fori_loop(0, count, body, 0)

    for count, base, k_tile, _, mask_sub, bias_sub in groups:
        def p1(kt, carry, base=base, k_tile=k_tile, mask_sub=mask_sub, bias_sub=bias_sub):
            masks = [mask_sub(kt, sub) for sub in range(KTILE // LANES)]
            for pair in range(npairs):
                sl = slice(pair * LANES, (pair + 1) * LANES)
                l = _dot_nt(qs_ref[pair], k_tile(kt, sl))
                for half in range(2):
                    rs = slice(half * tq, (half + 1) * tq)
                    mx = mx_ref[pair, rs, :]
                    for sub in range(KTILE // LANES):
                        cs = slice(sub * LANES, (sub + 1) * LANES)
                        blk = l[rs, cs] + (masks[sub] + bias_sub(kt, sub, 2 * pair + half))
                        l_ref[pair, base + kt, rs, cs] = blk
                        mx = jnp.maximum(mx, blk)
                    mx_ref[pair, rs, :] = mx
            return carry

        over_tiles(count, p1)

    for pair in range(npairs):
        m = jnp.max(mx_ref[pair], axis=-1, keepdims=True)
        mx_ref[pair] = jnp.broadcast_to(m, mx_ref.shape[1:])

    for count, base, _, v_tile, _, _ in groups:
        def p2(kt, carry, base=base, v_tile=v_tile):
            for pair in range(npairs):
                sl = slice(pair * LANES, (pair + 1) * LANES)
                m = mx_ref[pair]
                p = jnp.concatenate(
                    [jnp.exp(l_ref[pair, base + kt, :, sub * LANES:(sub + 1) * LANES] - m)
                     for sub in range(KTILE // LANES)], axis=1)
                vaug = jnp.concatenate([v_tile(kt, sl), ones], axis=1)
                acc_ref[pair] += _dot(_bf(p), vaug)
            return carry

        over_tiles(count, p2)

    lane_full = lax.broadcasted_iota(I32, (tq, LANES), 1)
    for pair in range(npairs):
        a = acc_ref[pair]
        o = a[:, 0:LANES] / a[:, LANES:2 * LANES]
        o_ref[:, pair * LANES:(pair + 1) * LANES] = _bf(jnp.where(lane_full < A_HEAD_DIM, o[0:tq], o[tq:2 * tq]))


def _attn_scratch(tq, ntiles):
    npairs = A_HEADS // 2
    return [pltpu.VMEM((npairs, 2 * tq, LANES), BF16),
            pltpu.VMEM((npairs, ntiles, 2 * tq, KTILE), F32),
            pltpu.VMEM((npairs, 2 * tq, LANES), F32),
            pltpu.VMEM((npairs, 2 * tq, 2 * LANES), F32)]


def _attn_prompt_kernel(q_ref, k_ref, v_ref, mask_ref, bias_ref, o_ref, qs_ref, l_ref, mx_ref, acc_ref):
    i = pl.program_id(1)
    per = KTILE // TILE

    def rows(kt):
        return pl.ds(pl.multiple_of(kt * KTILE, KTILE), KTILE)

    def bias_sub(kt, sub, hd):
        s = kt * per + sub
        sel = jnp.where(s == i, 2, jnp.where(s == i - 1, 1, 0))
        return bias_ref[sel, hd]

    group = (i // per + 1, 0,
             lambda kt, sl: k_ref[rows(kt), sl],
             lambda kt, sl: v_ref[rows(kt), sl],
             lambda kt, sub: mask_ref[0, 0, kt * per + sub],
             bias_sub)
    _attn_core(q_ref, o_ref, qs_ref, l_ref, mx_ref, acc_ref, [group], TILE)


def attn_prompt(q, kb, vb, mask, bias, nseq, seqlen):
    n, hq = q.shape
    nqb = seqlen // TILE
    nt = mask.shape[2]
    return pl.pallas_call(
        _attn_prompt_kernel,
        grid=(nseq, nqb),
        in_specs=[pl.BlockSpec((TILE, hq), lambda b, i: (b * nqb + i, 0)),
                  pl.BlockSpec((seqlen, hq), lambda b, i: (b, 0)),
                  pl.BlockSpec((seqlen, hq), lambda b, i: (b, 0)),
                  pl.BlockSpec((1, 1, nt, TILE, TILE), lambda b, i: (b, i, 0, 0, 0)),
                  _full(bias.shape)],
        out_specs=pl.BlockSpec((TILE, hq), lambda b, i: (b * nqb + i, 0)),
        out_shape=jax.ShapeDtypeStruct((n, hq), BF16),
        scratch_shapes=_attn_scratch(TILE, seqlen // KTILE),
        compiler_params=_params("parallel", "arbitrary"),
        name="attn_prompt",
    )(q, kb, vb, mask, bias)


def _attn_sample_kernel(q_ref, kc_ref, vc_ref, kn_ref, vn_ref, mask_ref, bias_ref, o_ref,
                        qs_ref, l_ref, mx_ref, acc_ref, *, ncache):
    tq = q_ref.shape[0]
    per = KTILE // TILE
    last_sub = ncache * per - 1

    def rows(kt):
        return pl.ds(pl.multiple_of(kt * KTILE, KTILE), KTILE)

    def cache_bias(kt, sub, hd):
        return bias_ref[jnp.where(kt * per + sub == last_sub, 1, 0), hd]

    cache = (ncache, 0,
             lambda kt, sl: _bf(kc_ref[0, rows(kt), sl]),
             lambda kt, sl: _bf(vc_ref[0, rows(kt), sl]),
             lambda kt, sub: mask_ref[0, 0, kt * per + sub, 0:tq, :],
             cache_bias)
    fresh = (1, ncache,
             lambda kt, sl: kn_ref[0, :, sl],
             lambda kt, sl: vn_ref[0, :, sl],
             lambda kt, sub: mask_ref[0, 0, ncache * per + sub, 0:tq, :],
             lambda kt, sub, hd: bias_ref[2 if sub == 0 else 0, hd])
    _attn_core(q_ref, o_ref, qs_ref, l_ref, mx_ref, acc_ref, [cache, fresh], tq)


def attn_sample(q, kc, vc, kn, vn, mask, bias, nseq, tq):
    n, hq = q.shape
    past = kc.shape[1]
    ncache = past // KTILE
    nt = mask.shape[2]
    return pl.pallas_call(
        functools.partial(_attn_sample_kernel, ncache=ncache),
        grid=(nseq,),
        in_specs=[pl.BlockSpec((tq, hq), lambda s: (s, 0)),
                  pl.BlockSpec((1, past, hq), lambda s: (s, 0, 0)),
                  pl.BlockSpec((1, past, hq), lambda s: (s, 0, 0)),
                  pl.BlockSpec((1, KTILE, hq), lambda s: (s, 0, 0)),
                  pl.BlockSpec((1, KTILE, hq), lambda s: (s, 0, 0)),
                  pl.BlockSpec((1, 1, nt, TILE, TILE), lambda s: (s, 0, 0, 0, 0)),
                  _full(bias.shape)],
        out_specs=pl.BlockSpec((tq, hq), lambda s: (s, 0)),
        out_shape=jax.ShapeDtypeStruct((n, hq), BF16),
        scratch_shapes=_attn_scratch(tq, ncache + 1),
        compiler_params=_params("parallel"),
        name="attn_sample",
    )(q, kc, vc, kn, vn, mask, bias)


def _t5_bucket(rel):
    half = N_BUCKETS // 2
    max_exact = half // 2
    n = jnp.abs(rel)
    nf = jnp.maximum(n, 1).astype(F32)
    large = max_exact + (jnp.log(nf / max_exact) / math.log(MAX_DISTANCE / max_exact)
                         * (half - max_exact)).astype(I32)
    large = jnp.minimum(large, half - 1)
    return jnp.where(rel > 0, half, 0) + jnp.where(n < max_exact, n, large)


def bias_tiles(rel_bias):
    r = jnp.arange(TILE, dtype=I32)[:, None]
    c = jnp.arange(TILE, dtype=I32)[None, :]
    rels = jnp.stack([jnp.full((TILE, TILE), -(TILE + 1), I32), c - r - TILE, c - r])
    tab = rel_bias.astype(F32)[_t5_bucket(rels)]
    return jnp.moveaxis(tab, -1, 1)


def dsa_core_prompt(pr, nseq, seqlen, bias):
    q, _, kb, _, vb, qi, kiwi = pr
    nqb = seqlen // TILE
    wit = kiwi[:, IDX_DIM:IDX_DIM + IDX_HEADS].reshape(nseq * nqb, TILE, IDX_HEADS).swapaxes(1, 2)
    mask = index_mask(qi, wit, [kiwi], [pl.BlockSpec((seqlen, LANES), lambda b, i: (b, 0))],
                      [seqlen], [LANES], nseq, nqb, causal=True)
    return attn_prompt(q, kb, vb, mask, bias, nseq, seqlen)


def dsa_core_sample(pr, k_cache, v_cache, ki_cache, nseq, tq, bias):
    q, _, kb, _, vb, qi, kiwi = pr
    past = k_cache.shape[1]
    hq = q.shape[1]
    pad = KTILE - tq
    qi2 = jnp.concatenate([qi.reshape(nseq, tq, -1)] * (TILE // tq), axis=1).reshape(nseq * TILE, -1)
    wit = kiwi[:, IDX_DIM:IDX_DIM + IDX_HEADS].reshape(nseq, tq, IDX_HEADS).swapaxes(1, 2)
    wit = jnp.concatenate([wit] * (TILE // tq), axis=2)
    mask = index_mask(
        qi2, wit, [ki_cache, kiwi],
        [pl.BlockSpec((1, past, IDX_DIM), lambda b, i: (b, 0, 0)),
         pl.BlockSpec((tq, LANES), lambda b, i: (b, 0))],
        [past, tq], [IDX_DIM, LANES], nseq, 1, causal=False)
    kn = jnp.pad(kb.reshape(nseq, tq, hq), ((0, 0), (0, pad), (0, 0)))
    vn = jnp.pad(vb.reshape(nseq, tq, hq), ((0, 0), (0, pad), (0, 0)))
    return attn_sample(q, k_cache.reshape(nseq, past, hq), v_cache.reshape(nseq, past, hq),
                       kn, vn, mask, bias[:, :, :tq, :], nseq, tq)


def _log_sigmoid(x):
    return jnp.minimum(x, 0.0) - jnp.log1p(jnp.exp(-jnp.abs(x)))


def _mlstm_kernel(p_ref, gs_ref, gr_ref, gbc_ref, gbr_ref, hg_ref, c0_ref, n0_ref, m0_ref,
                  hs_ref, c_ref, n_ref, m_ref, cs, ns, ms):
    c_id = pl.program_id(1)
    nh = B_HEADS
    npair = nh // 2
    hw = LANES
    L = p_ref.shape[0]
    L2 = 2 * L

    @pl.when(c_id == 0)
    def _():
        cs[...] = jnp.zeros(cs.shape, F32)
        ns[...] = jnp.zeros(ns.shape, F32)
        ns[:, 0:B_QK_DIM] = n0_ref[0]
        for hd in range(nh):
            cs[hd // 2, (hd % 2) * hw:(hd % 2) * hw + B_QK_DIM, :] = c0_ref[0, hd]
        ms[...] = m0_ref[0]

    r = lax.broadcasted_iota(I32, (L2, L2), 0)
    c = lax.broadcasted_iota(I32, (L2, L2), 1)
    same = (r >> CHUNK_SHIFT) == (c >> CHUNK_SHIFT)
    incl = same & (c <= r)
    gs = gs_ref[0] + gbc_ref[...]
    gr = gr_ref[0] + gbr_ref[...]
    bcol = _dot_exact_lhs01(_bf01(incl), _log_sigmoid(gs[:, npair:]))
    brow = _dot_exact_rhs01(_log_sigmoid(gr[npair:, :]), _bf01(same & (r <= c)))
    icol = gs[:, :npair]
    irow = gr[:npair, :]
    rowc = lax.broadcasted_iota(I32, (L2, 1), 0)
    is_top = rowc < L
    top = is_top.astype(F32)
    bot = 1.0 - top
    row2 = lax.broadcasted_iota(I32, (2 * hw, 1), 0)
    hg = hg_ref[...]
    mall = ms[...]

    def stack(base, pr):
        a = base + 2 * pr * hw
        return jnp.concatenate([p_ref[:, a:a + hw], p_ref[:, a + hw:a + 2 * hw]], axis=0)

    def bd(x):
        return _bf(jnp.concatenate([x * top, x * bot], axis=1))

    pairs = range(npair)
    qf = [stack(0, pr) for pr in pairs]
    kf = [stack(nh * hw, pr) * (B_QK_DIM ** -0.5) for pr in pairs]
    vb = [_bf(stack(2 * nh * hw, pr)) for pr in pairs]
    bc = [bcol[:, pr:pr + 1] for pr in pairs]
    ic = [icol[:, pr:pr + 1] for pr in pairs]
    mc = [mall[:, pr:pr + 1] for pr in pairs]
    d = [jnp.where(incl, bc[pr] - brow[pr:pr + 1, :] + irow[pr:pr + 1, :], NEG) for pr in pairs]
    inter = [bc[pr] + mc[pr] for pr in pairs]
    mt = [jnp.maximum(inter[pr], jnp.max(d[pr], axis=-1, keepdims=True)) for pr in pairs]
    s = [_dot_nt(_bf(qf[pr]), _bf(kf[pr])) * jnp.exp(d[pr] - mt[pr]) for pr in pairs]
    wst = [jnp.exp(inter[pr] - mt[pr]) for pr in pairs]
    cmat = [cs[pr] for pr in pairs]
    num = [_dot(_bf(s[pr]), vb[pr]) + wst[pr] * _dot(bd(qf[pr]), _bf(cmat[pr])) for pr in pairs]
    for pr in pairs:
        n0 = ns[2 * pr:2 * pr + 1, :]
        n1 = ns[2 * pr + 1:2 * pr + 2, :]
        qn = jnp.sum(qf[pr] * jnp.where(is_top, n0, n1), axis=-1, keepdims=True)
        den = jnp.sum(s[pr], axis=-1, keepdims=True) + wst[pr] * qn
        hs = num[pr] / jnp.maximum(jnp.abs(den), jnp.exp(-mt[pr]))
        bl0 = bc[pr][L - 1:L, :]
        bl1 = bc[pr][L2 - 1:L2, :]
        bl = jnp.where(is_top, bl0, bl1)
        dec = bl - bc[pr] + ic[pr]
        blm = bl + mc[pr]
        mnew0 = jnp.maximum(blm[0:1, :], jnp.max(dec[0:L], axis=0, keepdims=True))
        mnew1 = jnp.maximum(blm[L:L + 1, :], jnp.max(dec[L:L2], axis=0, keepdims=True))
        mnew = jnp.where(is_top, mnew0, mnew1)
        wk = jnp.exp(dec - mnew)
        ws = jnp.exp(blm - mnew)
        kw = kf[pr] * wk
        ws2 = jnp.where(row2 < hw, ws[0:1, :], ws[L:L + 1, :])
        cs[pr] = ws2 * cmat[pr] + _dot_tn(bd(kw), vb[pr])
        ns[2 * pr:2 * pr + 1, :] = ws[0:1, :] * n0 + jnp.sum(kw[0:L], axis=0, keepdims=True)
        ns[2 * pr + 1:2 * pr + 2, :] = ws[L:L + 1, :] * n1 + jnp.sum(kw[L:L2], axis=0, keepdims=True)
        ms[:, pr:pr + 1] = mnew
        on = _bf(_rms(hs, hg) * _sigmoid(stack(3 * nh * hw, pr)))
        oa = 2 * pr * B_V_DIM
        hs_ref[:, oa:oa + B_V_DIM] = on[0:L]
        hs_ref[:, oa + B_V_DIM:oa + 2 * B_V_DIM] = on[L:L2]

    @pl.when(c_id == pl.num_programs(1) - 1)
    def _():
        for hd in range(nh):
            c_ref[0, hd] = cs[hd // 2, (hd % 2) * hw:(hd % 2) * hw + B_QK_DIM, :]
        n_ref[0] = ns[:, 0:B_QK_DIM]
        m_ref[0] = ms[...]


def mlstm_weight(w_in):
    d = w_in.shape[0]
    nq = B_HEADS * B_QK_DIM
    nv = B_HEADS * B_V_DIM

    def padh(w):
        w = w.reshape(d, B_HEADS, B_QK_DIM)
        return jnp.pad(w, ((0, 0), (0, 0), (0, LANES - B_QK_DIM))).reshape(d, B_HEADS * LANES)

    main = 2 * nq + 2 * nv
    tail = jnp.pad(w_in[:, main:], ((0, 0), (0, LANES - 2 * B_HEADS)))
    return _bf(jnp.concatenate([padh(w_in[:, :nq]), padh(w_in[:, nq:2 * nq]), w_in[:, 2 * nq:main], tail], axis=1))


def mlstm_core(p, gate_bias, h_gain, c0, n0, m0, nseq, nchunk):
    n, m = p.shape
    nh = B_HEADS
    npair = nh // 2
    gcol = 4 * nh * LANES
    pre = p[:, gcol:gcol + 2 * nh].reshape(nseq * nchunk, CHUNK, 2, npair, 2)
    gs = pre.transpose(0, 4, 1, 2, 3).reshape(nseq * nchunk, 2 * CHUNK, 2 * npair)
    gr = pre.transpose(0, 2, 3, 4, 1).reshape(nseq * nchunk, 2 * npair, 2 * CHUNK)
    gb = gate_bias.reshape(2, npair, 2)
    gbc = jnp.repeat(gb.transpose(2, 0, 1).reshape(2, 2 * npair), CHUNK, axis=0)
    m0s = jnp.repeat(m0.reshape(nseq, npair, 2).transpose(0, 2, 1), CHUNK, axis=1)
    o, c_new, n_new, m_new = pl.pallas_call(
        _mlstm_kernel,
        grid=(nseq, nchunk),
        in_specs=[pl.BlockSpec((CHUNK, m), lambda s, c: (s * nchunk + c, 0)),
                  pl.BlockSpec((1, 2 * CHUNK, 2 * npair), lambda s, c: (s * nchunk + c, 0, 0)),
                  pl.BlockSpec((1, 2 * npair, 2 * CHUNK), lambda s, c: (s * nchunk + c, 0, 0)),
                  _full((2 * CHUNK, 2 * npair)), _full((2 * npair, 2 * CHUNK)), _full((1, B_V_DIM)),
                  pl.BlockSpec((1, nh, B_QK_DIM, B_V_DIM), lambda s, c: (s, 0, 0, 0)),
                  pl.BlockSpec((1, nh, B_QK_DIM), lambda s, c: (s, 0, 0)),
                  pl.BlockSpec((1, 2 * CHUNK, npair), lambda s, c: (s, 0, 0))],
        out_specs=[pl.BlockSpec((CHUNK, nh * B_V_DIM), lambda s, c: (s * nchunk + c, 0)),
                   pl.BlockSpec((1, nh, B_QK_DIM, B_V_DIM), lambda s, c: (s, 0, 0, 0)),
                   pl.BlockSpec((1, nh, B_QK_DIM), lambda s, c: (s, 0, 0)),
                   pl.BlockSpec((1, 2 * CHUNK, npair), lambda s, c: (s, 0, 0))],
        out_shape=[jax.ShapeDtypeStruct((n, nh * B_V_DIM), BF16),
                   jax.ShapeDtypeStruct((nseq, nh, B_QK_DIM, B_V_DIM), F32),
                   jax.ShapeDtypeStruct((nseq, nh, B_QK_DIM), F32),
                   jax.ShapeDtypeStruct((nseq, 2 * CHUNK, npair), F32)],
        scratch_shapes=[pltpu.VMEM((npair, 2 * LANES, B_V_DIM), F32), pltpu.VMEM((nh, LANES), F32),
                        pltpu.VMEM((2 * CHUNK, npair), F32)],
        compiler_params=_params("parallel", "arbitrary"),
        name="mlstm",
    )(p, gs, gr, gbc, gbc.T, h_gain.reshape(1, B_V_DIM), c0, n0, m0s)
    m_heads = m_new[:, ::CHUNK, :].transpose(0, 2, 1).reshape(nseq, nh)
    return o, c_new, n_new, m_heads


def _split2(x):
    hi = _bf(x)
    return hi, _bf(x - hi.astype(F32))


def _cat3_lhs(x):
    hi, mid = _split2(x)
    return jnp.concatenate([hi, hi, mid], axis=1)


def _cat3_rhs(x):
    hi, mid = _split2(x)
    return jnp.concatenate([hi, mid, hi], axis=0)


def _gdn_kernel(p_ref, gs_ref, gr_ref, cw_ref, alc_ref, alr_ref, dtc_ref, dtr_ref, og_ref, s0_ref, cb0_ref,
                o_ref, s_ref, cb_ref, ss, tail):
    c_id = pl.program_id(1)
    nh = C_HEADS
    cdim = nh * (2 * C_DK + C_DV)
    L = p_ref.shape[0]
    L2 = 2 * L
    nprev = CONV_W - 1

    @pl.when(c_id == 0)
    def _():
        ss[...] = s0_ref[0]
        tail[...] = jnp.zeros(tail.shape, F32)
        tail[8 - nprev:8, :] = cb0_ref[0]

    x = p_ref[:, 0:cdim]
    ext = jnp.concatenate([tail[...], x], axis=0)
    conv = ext[8:8 + L] * cw_ref[CONV_W - 1:CONV_W, :]
    for j in range(CONV_W - 1):
        conv = conv + ext[8 - nprev + j:8 - nprev + j + L] * cw_ref[j:j + 1, :]
    tail[...] = x[L - 8:L, :]
    cf = _silu(conv)

    npair = nh // 2
    gs = gs_ref[0]
    gr = gr_ref[0]
    beta = _sigmoid(gs[:, 0:npair])
    g_col = -jnp.exp(alc_ref[...]) * _softplus(gs[:, npair:] + dtc_ref[...])
    g_row = -jnp.exp(alr_ref[...]) * _softplus(gr[npair:, :] + dtr_ref[...])
    r = lax.broadcasted_iota(I32, (L2, L2), 0)
    c = lax.broadcasted_iota(I32, (L2, L2), 1)
    same = (r >> CHUNK_SHIFT) == (c >> CHUNK_SHIFT)
    incl = same & (c <= r)
    strict = same & (c < r)
    eye = (c == r).astype(F32)
    gc_col = _dot_exact_lhs01(_bf01(incl), g_col)
    gc_row = _dot_exact_rhs01(g_row, _bf01(same & (r <= c)))
    rowc = lax.broadcasted_iota(I32, (L2, 1), 0)
    top = (rowc < L).astype(F32)
    bot = 1.0 - top
    row2 = lax.broadcasted_iota(I32, (2 * C_DK, 1), 0)
    og = og_ref[...]

    def stack(base, pr):
        a = base + 2 * pr * C_DK
        return jnp.concatenate([cf[:, a:a + C_DK], cf[:, a + C_DK:a + 2 * C_DK]], axis=0)

    def bd(x):
        return _bf(jnp.concatenate([x * top, x * bot], axis=1))

    pairs = range(npair)
    qc, kc, kcb, dm, amat, rhs, gcols, egcs = [], [], [], [], [], [], [], []
    for pr in pairs:
        qraw = stack(0, pr)
        kraw = stack(nh * C_DK, pr)
        vc = stack(2 * nh * C_DK, pr)
        qc.append(qraw * lax.rsqrt(jnp.sum(qraw * qraw, axis=-1, keepdims=True) + EPS) * (C_DK ** -0.5))
        kc.append(kraw * lax.rsqrt(jnp.sum(kraw * kraw, axis=-1, keepdims=True) + EPS))
        bc = beta[:, pr:pr + 1]
        gcol = gc_col[:, pr:pr + 1]
        grow = gc_row[pr:pr + 1, :]
        dm.append(jnp.where(incl, jnp.exp(jnp.where(incl, gcol - grow, 0.0)), 0.0))
        kb = kc[pr] * bc
        kcb.append(_bf(kc[pr]))
        amat.append(jnp.where(strict, _dot_nt(_bf(kb), kcb[pr]) * dm[pr], 0.0))
        egc = jnp.exp(gcol)
        rhs.append(jnp.concatenate([vc * bc, kb * egc], axis=-1))
        gcols.append(gcol)
        egcs.append(egc)
    tinv = [eye - amat[pr] for pr in pairs]
    pw_l = [_cat3_lhs(-amat[pr]) for pr in pairs]
    pw_r = [_cat3_rhs(-amat[pr]) for pr in pairs]
    for _ in range(CHUNK_SHIFT - 1):
        pw = [_dot(pw_l[pr], pw_r[pr]) for pr in pairs]
        pw_l = [_cat3_lhs(pw[pr]) for pr in pairs]
        pw_r = [_cat3_rhs(pw[pr]) for pr in pairs]
        tinv = [tinv[pr] + _dot(_cat3_lhs(tinv[pr]), pw_r[pr]) for pr in pairs]
    sol = [_dot(_cat3_lhs(tinv[pr]), _cat3_rhs(rhs[pr])) for pr in pairs]
    attn = [_dot_nt(_bf(qc[pr]), kcb[pr]) * dm[pr] for pr in pairs]
    smat = [ss[pr] for pr in pairs]
    sb = [_bf(smat[pr]) for pr in pairs]
    vnew = [sol[pr][:, :C_DV] - _dot(bd(sol[pr][:, C_DV:]), sb[pr]) for pr in pairs]
    o = [_dot(bd(qc[pr] * egcs[pr]), sb[pr]) + _dot(_bf(attn[pr]), _bf(vnew[pr])) for pr in pairs]
    for pr in pairs:
        gl0 = gcols[pr][L - 1:L, :]
        gl1 = gcols[pr][L2 - 1:L2, :]
        ke = kc[pr] * jnp.exp(jnp.where(rowc < L, gl0, gl1) - gcols[pr])
        decay = jnp.exp(jnp.where(row2 < C_DK, gl0, gl1))
        ss[pr] = smat[pr] * decay + _dot_tn(bd(ke), _bf(vnew[pr]))
    for pr in pairs:
        za = cdim + 2 * pr * C_DV
        z = jnp.concatenate([p_ref[:, za:za + C_DV], p_ref[:, za + C_DV:za + 2 * C_DV]], axis=0)
        on = _bf(_rms(o[pr], og) * _silu(z))
        oa = 2 * pr * C_DV
        o_ref[:, oa:oa + C_DV] = on[0:L]
        o_ref[:, oa + C_DV:oa + 2 * C_DV] = on[L:L2]

    @pl.when(c_id == pl.num_programs(1) - 1)
    def _():
        s_ref[0] = ss[...]
        cb_ref[0] = tail[8 - nprev:8, :]


def gdn_weight(w_in):
    main = C_HEADS * (2 * C_DK + C_DV) + C_HEADS * C_DV
    tail = jnp.pad(w_in[:, main:], ((0, 0), (0, LANES - 2 * C_HEADS)))
    return _bf(jnp.concatenate([w_in[:, :main], tail], axis=1))


def gdn_core(p, conv_w, a_log, dt_bias, o_gain, s0, cb0, nseq, nchunk):
    n, m = p.shape
    nh = C_HEADS
    npair = nh // 2
    cdim = nh * (2 * C_DK + C_DV)
    gcolumn = cdim + nh * C_DV
    pre = p[:, gcolumn:gcolumn + 2 * nh].reshape(nseq * nchunk, CHUNK, 2, npair, 2)
    gs = pre.transpose(0, 4, 1, 2, 3).reshape(nseq * nchunk, 2 * CHUNK, 2 * npair)
    gr = pre.transpose(0, 2, 3, 4, 1).reshape(nseq * nchunk, 2 * npair, 2 * CHUNK)

    def col(v):
        return jnp.repeat(v.reshape(npair, 2).T, CHUNK, axis=0)

    o, s_new, cb_new = pl.pallas_call(
        _gdn_kernel,
        grid=(nseq, nchunk),
        in_specs=[pl.BlockSpec((CHUNK, m), lambda s, c: (s * nchunk + c, 0)),
                  pl.BlockSpec((1, 2 * CHUNK, 2 * npair), lambda s, c: (s * nchunk + c, 0, 0)),
                  pl.BlockSpec((1, 2 * npair, 2 * CHUNK), lambda s, c: (s * nchunk + c, 0, 0)),
                  _full((CONV_W, cdim)),
                  _full((2 * CHUNK, npair)), _full((npair, 2 * CHUNK)),
                  _full((2 * CHUNK, npair)), _full((npair, 2 * CHUNK)),
                  _full((1, C_DV)),
                  pl.BlockSpec((1, npair, 2 * C_DK, C_DV), lambda s, c: (s, 0, 0, 0)),
                  pl.BlockSpec((1, CONV_W - 1, cdim), lambda s, c: (s, 0, 0))],
        out_specs=[pl.BlockSpec((CHUNK, nh * C_DV), lambda s, c: (s * nchunk + c, 0)),
                   pl.BlockSpec((1, npair, 2 * C_DK, C_DV), lambda s, c: (s, 0, 0, 0)),
                   pl.BlockSpec((1, CONV_W - 1, cdim), lambda s, c: (s, 0, 0))],
        out_shape=[jax.ShapeDtypeStruct((n, nh * C_DV), BF16),
                   jax.ShapeDtypeStruct((nseq, npair, 2 * C_DK, C_DV), F32),
                   jax.ShapeDtypeStruct((nseq, CONV_W - 1, cdim), F32)],
        scratch_shapes=[pltpu.VMEM((npair, 2 * C_DK, C_DV), F32), pltpu.VMEM((8, cdim), F32)],
        compiler_params=_params("parallel", "arbitrary"),
        name="gdn",
    )(p, gs, gr, conv_w, col(a_log), col(a_log).T, col(dt_bias), col(dt_bias).T, o_gain.reshape(1, C_DV),
      s0.reshape(nseq, npair, 2 * C_DK, C_DV), cb0)
    return o, s_new.reshape(nseq, nh, C_DK, C_DV), cb_new


def _trunk(x, nseq, seqlen, mem_k, mem_v, st, W, bias, is_prompt):
    d = x.shape[-1]
    n = nseq * seqlen
    x = x.reshape(n, d)
    tm = min(256, n)
    tm_mem = min(256, seqlen)
    new = {}
    for i in range(4):
        kind = i % 3
        mx = W["mixer"][i]
        if kind == 0:
            pr = proj_dsa(x, W["norm_mix"][i], mx["w_in"], mx["q_gain"], mx["k_gain"], mx["ki_gain"], tm)
            if is_prompt:
                o = dsa_core_prompt(pr, nseq, seqlen, bias)
            else:
                o = dsa_core_sample(pr, *st[i], nseq, seqlen, bias)
            new[i] = (pr[1].reshape(nseq, seqlen, A_HEADS, A_HEAD_DIM),
                      pr[3].reshape(nseq, seqlen, A_HEADS, A_HEAD_DIM),
                      pr[6][:, :IDX_DIM].reshape(nseq, seqlen, IDX_DIM))
        elif kind == 1:
            p = proj(x, W["norm_mix"][i], mx["w_in"], tm)
            o, c_new, n_new, m_new = mlstm_core(p, mx["gate_bias"], mx["h_gain"], *st[i], nseq, seqlen // CHUNK)
            new[i] = (c_new, n_new, m_new.reshape(nseq, B_HEADS))
        else:
            p = proj(x, W["norm_mix"][i], mx["w_in"], tm)
            o, s_new, cb_new = gdn_core(p, mx["conv_w"], mx["a_log"], mx["dt_bias"], mx["o_gain"], *st[i],
                                        nseq, seqlen // CHUNK)
            new[i] = (s_new, cb_new)
        x = mem_attend(x, o, mx["w_out"], W["norm_mem"][i], W["w_mq"][i], W["mq_gain"][i],
                       mem_k[i], mem_v[i], W["w_mo"][i], tm_mem, seqlen // tm_mem)
        x = ffn(x, W["norm_ffn"][i], W["w_ffn1"][i], W["w_ffn3"][i], W["w_ffn2"][i], tm)
    return x.reshape(nseq, seqlen, d), new


def kernel(x_prompt, x_sample, mem_prompt, cache_l0_k, cache_l0_v, cache_l0_kidx, state_l1_C, state_l1_n, state_l1_m, state_l2_S, state_l2_conv, cache_l3_k, cache_l3_v, cache_l3_kidx, cache_mem_k, cache_mem_v, rel_bias, norm_mix, norm_mem, norm_ffn, mem_norm, w_mq, w_mk, w_mv, w_mo, mq_gain, mk_gain, w_ffn1, w_ffn3, w_ffn2, a0_w_in, a0_w_out, a0_q_gain, a0_k_gain, a0_kidx_gain, b1_w_in, b1_gate_bias, b1_h_gain, b1_w_out, c2_w_in, c2_conv_w, c2_a_log, c2_dt_bias, c2_o_gain, c2_w_out, a3_w_in, a3_w_out, a3_q_gain, a3_k_gain, a3_kidx_gain):
    B, T, D = x_prompt.shape
    S, Ts, _ = x_sample.shape
    depth = w_mq.shape[0]
    mlen = mem_prompt.shape[1]
    mw = MEM_HEADS * MEM_HEAD_DIM

    def dsa_w(w_in, w_out, qg, kg, kig):
        return dict(w_in=dsa_weight(w_in), w_out=_bf(w_out), q_gain=qg, k_gain=kg, ki_gain=kig)

    W = dict(
        norm_mix=norm_mix, norm_mem=norm_mem, norm_ffn=norm_ffn,
        w_mq=_bf(w_mq), w_mo=_bf(w_mo), mq_gain=mq_gain,
        w_ffn1=_bf(w_ffn1), w_ffn3=_bf(w_ffn3), w_ffn2=_bf(w_ffn2),
        mixer={
            0: dsa_w(a0_w_in, a0_w_out, a0_q_gain, a0_k_gain, a0_kidx_gain),
            1: dict(w_in=mlstm_weight(b1_w_in), gate_bias=b1_gate_bias, h_gain=b1_h_gain, w_out=_bf(b1_w_out)),
            2: dict(w_in=gdn_weight(c2_w_in), conv_w=c2_conv_w, a_log=c2_a_log, dt_bias=c2_dt_bias,
                    o_gain=c2_o_gain, w_out=_bf(c2_w_out)),
            3: dsa_w(a3_w_in, a3_w_out, a3_q_gain, a3_k_gain, a3_kidx_gain),
        },
    )
    bias = bias_tiles(rel_bias)

    mk_p, mv_p = mem_kv(mem_prompt.reshape(B * mlen, D), mem_norm, w_mk, w_mv, mk_gain)
    mk_p = mk_p.reshape(depth, B, mlen, mw)
    mv_p = mv_p.reshape(depth, B, mlen, mw)
    st_p = {
        0: None,
        1: (jnp.zeros((B, B_HEADS, B_QK_DIM, B_V_DIM), F32), jnp.zeros((B, B_HEADS, B_QK_DIM), F32),
            jnp.full((B, B_HEADS), NEG, F32)),
        2: (jnp.zeros((B, C_HEADS, C_DK, C_DV), F32), jnp.zeros((B, CONV_W - 1, state_l2_conv.shape[-1]), F32)),
        3: None,
    }
    y_p, np_ = _trunk(x_prompt, B, T, mk_p, mv_p, st_p, W, bias, True)

    st_s = {
        0: (cache_l0_k, cache_l0_v, cache_l0_kidx),
        1: (state_l1_C, state_l1_n, state_l1_m),
        2: (state_l2_S, state_l2_conv),
        3: (cache_l3_k, cache_l3_v, cache_l3_kidx),
    }
    mk_s = cache_mem_k.reshape(depth, S, mlen, mw)
    mv_s = cache_mem_v.reshape(depth, S, mlen, mw)
    y_s, ns_ = _trunk(x_sample, S, Ts, mk_s, mv_s, st_s, W, bias, False)

    shp = (depth, B, mlen, MEM_HEADS, MEM_HEAD_DIM)
    return (y_p, y_s,
            *np_[0], *np_[1], *np_[2], *np_[3], mk_p.reshape(shp), mv_p.reshape(shp),
            *ns_[0], *ns_[1], *ns_[2], *ns_[3])
```

```python
import functools
import math

import jax
import jax.numpy as jnp
from jax import lax
from jax.experimental import pallas as pl
from jax.experimental.pallas import tpu as pltpu

F32 = jnp.float32
BF16 = jnp.bfloat16
I32 = jnp.int32

EPS = 1e-6
NEG = -1e30
CHUNK = 64
CHUNK_SHIFT = 6
LANES = 128
TILE = 128
KTILE = 256
COUNT_CHAINS = 8
SCORE_ROWS = 256
MASK_CASE_ROWS = 512
VMEM_LIMIT = 56 * 1024 * 1024

A_HEADS, A_HEAD_DIM = 16, 64
IDX_HEADS, IDX_DIM = 8, 64
TOPK_MAX = 256
N_BUCKETS, MAX_DISTANCE = 32, 128
B_HEADS, B_QK_DIM, B_V_DIM = 8, 64, 128
C_HEADS, C_DK, C_DV = 8, 128, 128
CONV_W = 4
MEM_HEADS, MEM_HEAD_DIM = 4, 128


def _bf(x):
    return x.astype(BF16)


def _bf01(mask):
    return mask.astype(F32).astype(BF16)


def _dot(a, b):
    return jnp.dot(a, b, preferred_element_type=F32)


def _dot_nt(a, b):
    return lax.dot_general(a, b, (((1,), (1,)), ((), ())), preferred_element_type=F32)


def _dot_tn(a, b):
    return lax.dot_general(a, b, (((0,), (0,)), ((), ())), preferred_element_type=F32)


def _split3(x):
    hi = _bf(x)
    r1 = x - hi.astype(F32)
    mid = _bf(r1)
    lo = _bf(r1 - mid.astype(F32))
    return hi, mid, lo


def _dot_exact_rhs01(x, m01):
    hi, mid, lo = _split3(x)
    return _dot(hi, m01) + _dot(mid, m01) + _dot(lo, m01)


def _dot_exact_lhs01(m01, x):
    hi, mid, lo = _split3(x)
    return _dot(m01, hi) + _dot(m01, mid) + _dot(m01, lo)


def _dot_f32(a, b):
    ah, am, al = _split3(a)
    bh, bm, bl = _split3(b)
    return (_dot(ah, bh) + (_dot(ah, bm) + _dot(am, bh))
            + (_dot(am, bm) + _dot(ah, bl) + _dot(al, bh)))


def _rms(x, g):
    ms = jnp.mean(x * x, axis=-1, keepdims=True)
    return x * lax.rsqrt(ms + EPS) * g


def _sigmoid(x):
    return 1.0 / (1.0 + jnp.exp(-x))


def _silu(x):
    return x * _sigmoid(x)


def _softplus(x):
    return jnp.maximum(x, 0.0) + jnp.log1p(jnp.exp(-jnp.abs(x)))


def _params(*sem):
    return pltpu.CompilerParams(dimension_semantics=sem, vmem_limit_bytes=VMEM_LIMIT)


def _full(shape):
    n = len(shape)
    return pl.BlockSpec(shape, lambda *_: (0,) * n)


def _proj_kernel(x_ref, g_ref, w_ref, o_ref, *, col_chunk):
    h = _bf(_rms(x_ref[...], g_ref[...]))
    m = w_ref.shape[1]
    for c in range(0, m, col_chunk):
        e = min(c + col_chunk, m)
        o_ref[:, c:e] = _dot(h, w_ref[:, c:e])


def proj(x, g, w, tm):
    n, d = x.shape
    m = w.shape[1]
    return pl.pallas_call(
        functools.partial(_proj_kernel, col_chunk=512),
        grid=(n // tm,),
        in_specs=[pl.BlockSpec((tm, d), lambda i: (i, 0)), _full((1, d)), _full((d, m))],
        out_specs=pl.BlockSpec((tm, m), lambda i: (i, 0)),
        out_shape=jax.ShapeDtypeStruct((n, m), F32),
        compiler_params=_params("parallel"),
        name="proj",
    )(x, g.reshape(1, d), w)


def _memkv_kernel(x_ref, g_ref, wk_ref, wv_ref, kg_ref, k_ref, v_ref):
    h = _bf(_rms(x_ref[...], g_ref[0]))
    k = _dot(h, wk_ref[0])
    v_ref[0] = _dot(h, wv_ref[0])
    kg = kg_ref[0]
    for hd in range(MEM_HEADS):
        sl = slice(hd * MEM_HEAD_DIM, (hd + 1) * MEM_HEAD_DIM)
        k_ref[0, :, sl] = _rms(k[:, sl], kg)


def mem_kv(mem2d, mem_norm, w_mk, w_mv, mk_gain, tm=256):
    n, d = mem2d.shape
    depth = w_mk.shape[0]
    mw = w_mk.shape[2]
    return pl.pallas_call(
        _memkv_kernel,
        grid=(depth, n // tm),
        in_specs=[pl.BlockSpec((tm, d), lambda l, i: (i, 0)),
                  pl.BlockSpec((1, 1, d), lambda l, i: (l, 0, 0)),
                  pl.BlockSpec((1, d, mw), lambda l, i: (l, 0, 0)),
                  pl.BlockSpec((1, d, mw), lambda l, i: (l, 0, 0)),
                  pl.BlockSpec((1, 1, MEM_HEAD_DIM), lambda l, i: (l, 0, 0))],
        out_specs=[pl.BlockSpec((1, tm, mw), lambda l, i: (l, i, 0)),
                   pl.BlockSpec((1, tm, mw), lambda l, i: (l, i, 0))],
        out_shape=[jax.ShapeDtypeStruct((depth, n, mw), F32)] * 2,
        compiler_params=_params("parallel", "parallel"),
        name="mem_kv",
    )(mem2d, mem_norm.reshape(depth, 1, d), _bf(w_mk), _bf(w_mv), mk_gain.reshape(depth, 1, MEM_HEAD_DIM))


def _memattn_kernel(x_ref, o_ref, wo_ref, g_ref, wq_ref, qg_ref, mk_ref, mv_ref, wmo_ref, y_ref):
    x1 = x_ref[...] + _dot(o_ref[...], wo_ref[...])
    h = _bf(_rms(x1, g_ref[...]))
    q = _dot(h, wq_ref[...])
    qg = qg_ref[...]
    scale = MEM_HEAD_DIM ** -0.5
    outs = []
    for hd in range(MEM_HEADS):
        sl = slice(hd * MEM_HEAD_DIM, (hd + 1) * MEM_HEAD_DIM)
        qh = _bf(_rms(q[:, sl], qg))
        kh = _bf(mk_ref[0, :, sl])
        logits = _dot_nt(qh, kh) * scale
        m = jnp.max(logits, axis=-1, keepdims=True)
        p = jnp.exp(logits - m)
        p = p / jnp.sum(p, axis=-1, keepdims=True)
        outs.append(_bf(_dot(_bf(p), _bf(mv_ref[0, :, sl]))))
    att = jnp.concatenate(outs, axis=-1)
    y_ref[...] = x1 + _dot(att, wmo_ref[...])


def mem_attend(x, o, w_out, g, w_mq, mq_gain, mk, mv, w_mo, tm, tiles_per_seq):
    n, d = x.shape
    mlen, mw = mk.shape[1], mk.shape[2]
    return pl.pallas_call(
        _memattn_kernel,
        grid=(n // tm,),
        in_specs=[pl.BlockSpec((tm, d), lambda i: (i, 0)),
                  pl.BlockSpec((tm, o.shape[1]), lambda i: (i, 0)),
                  _full(w_out.shape), _full((1, d)), _full(w_mq.shape), _full((1, MEM_HEAD_DIM)),
                  pl.BlockSpec((1, mlen, mw), lambda i: (i // tiles_per_seq, 0, 0)),
                  pl.BlockSpec((1, mlen, mw), lambda i: (i // tiles_per_seq, 0, 0)),
                  _full(w_mo.shape)],
        out_specs=pl.BlockSpec((tm, d), lambda i: (i, 0)),
        out_shape=jax.ShapeDtypeStruct((n, d), F32),
        compiler_params=_params("parallel"),
        name="mem_attend",
    )(x, o, w_out, g.reshape(1, d), w_mq, mq_gain.reshape(1, MEM_HEAD_DIM), mk, mv, w_mo)


def _ffn_kernel(x_ref, g_ref, w1_ref, w3_ref, w2_ref, y_ref, *, hid_chunk):
    x = x_ref[...]
    h = _bf(_rms(x, g_ref[...]))
    hidden = w1_ref.shape[1]
    y_ref[...] = x
    for c in range(0, hidden, hid_chunk):
        a = _dot(h, w1_ref[:, c:c + hid_chunk])
        b = _dot(h, w3_ref[:, c:c + hid_chunk])
        y_ref[...] += _dot(_bf(_silu(a) * b), w2_ref[c:c + hid_chunk, :])


def ffn(x, g, w1, w3, w2, tm):
    n, d = x.shape
    hidden = w1.shape[1]
    return pl.pallas_call(
        functools.partial(_ffn_kernel, hid_chunk=256),
        grid=(n // tm,),
        in_specs=[pl.BlockSpec((tm, d), lambda i: (i, 0)), _full((1, d)),
                  _full((d, hidden)), _full((d, hidden)), _full((hidden, d))],
        out_specs=pl.BlockSpec((tm, d), lambda i: (i, 0)),
        out_shape=jax.ShapeDtypeStruct((n, d), F32),
        compiler_params=_params("parallel"),
        name="ffn",
    )(x, g.reshape(1, d), w1, w3, w2)


def _proj_dsa_kernel(x_ref, g_ref, w_ref, qg_ref, kg_ref, kig_ref, bd_ref,
                     q_o, k_o, kb_o, v_o, vb_o, qi_o, kiwi_o):
    h = _bf(_rms(x_ref[...], g_ref[...]))
    hq = A_HEADS * A_HEAD_DIM
    bd = bd_ref[...]
    inv_hd = 1.0 / A_HEAD_DIM

    def head_norm(p, gain):
        ss = _dot_exact_rhs01(p * p, bd)
        return p * lax.rsqrt(ss * inv_hd + EPS) * gain

    step = 512
    for c in range(0, hq, step):
        pq = _dot(h, w_ref[:, c:c + step])
        pk = _dot(h, w_ref[:, hq + c:hq + c + step])
        pv = _dot(h, w_ref[:, 2 * hq + c:2 * hq + c + step])
        for j in range(0, step, LANES):
            sl = slice(c + j, c + j + LANES)
            qn = head_norm(pq[:, j:j + LANES], qg_ref[:, sl])
            q_o[:, sl] = _bf(qn * (A_HEAD_DIM ** -0.5))
            kn = head_norm(pk[:, j:j + LANES], kg_ref[:, sl])
            k_o[:, sl] = kn
            kb_o[:, sl] = _bf(kn)
        v_o[:, c:c + step] = pv
        vb_o[:, c:c + step] = _bf(pv)
    qiw = IDX_HEADS * LANES
    for c in range(0, qiw, step):
        qi_o[:, c:c + step] = _bf(_dot(h, w_ref[:, 3 * hq + c:3 * hq + c + step]))
    p = _dot(h, w_ref[:, 3 * hq + qiw:3 * hq + qiw + LANES])
    lane = lax.broadcasted_iota(I32, p.shape, 1)
    is_ki = lane < IDX_DIM
    ss = jnp.sum(jnp.where(is_ki, p * p, 0.0), axis=-1, keepdims=True)
    kin = p * lax.rsqrt(ss * (1.0 / IDX_DIM) + EPS) * kig_ref[...]
    kiwi_o[...] = jnp.where(is_ki, kin, p)


def proj_dsa(x, g, wa, q_gain, k_gain, ki_gain, tm):
    n, d = x.shape
    hq = A_HEADS * A_HEAD_DIM
    m = wa.shape[1]
    qg = jnp.tile(q_gain, A_HEADS).reshape(1, hq)
    kg = jnp.tile(k_gain, A_HEADS).reshape(1, hq)
    kig = jnp.concatenate([ki_gain, jnp.ones((LANES - IDX_DIM,), F32)]).reshape(1, LANES)
    r = jnp.arange(LANES)
    bd = _bf((r[:, None] // A_HEAD_DIM) == (r[None, :] // A_HEAD_DIM))
    row = lambda w: pl.BlockSpec((tm, w), lambda i: (i, 0))
    return pl.pallas_call(
        _proj_dsa_kernel,
        grid=(n // tm,),
        in_specs=[row(d), _full((1, d)), _full((d, m)), _full((1, hq)), _full((1, hq)),
                  _full((1, LANES)), _full((LANES, LANES))],
        out_specs=[row(hq), row(hq), row(hq), row(hq), row(hq), row(IDX_HEADS * LANES), row(LANES)],
        out_shape=[jax.ShapeDtypeStruct((n, hq), BF16), jax.ShapeDtypeStruct((n, hq), F32),
                   jax.ShapeDtypeStruct((n, hq), BF16), jax.ShapeDtypeStruct((n, hq), F32),
                   jax.ShapeDtypeStruct((n, hq), BF16),
                   jax.ShapeDtypeStruct((n, IDX_HEADS * LANES), BF16),
                   jax.ShapeDtypeStruct((n, LANES), F32)],
        compiler_params=_params("parallel"),
        name="proj_dsa",
    )(x, g.reshape(1, d), wa, qg, kg, kig, bd)


def dsa_weight(w_in):
    hq = A_HEADS * A_HEAD_DIM
    o3 = 3 * hq
    o4 = o3 + IDX_HEADS * IDX_DIM
    d = w_in.shape[0]
    wqi = w_in[:, o3:o4].reshape(d, IDX_HEADS, IDX_DIM)
    wqi = jnp.pad(wqi, ((0, 0), (0, 0), (0, LANES - IDX_DIM))).reshape(d, IDX_HEADS * LANES)
    tail = jnp.pad(w_in[:, o4:], ((0, 0), (0, LANES - (w_in.shape[1] - o4))))
    return _bf(jnp.concatenate([w_in[:, :o3], wqi, tail], axis=1))


def _sortable(s):
    b = pltpu.bitcast(s, I32)
    b = jnp.where(b == jnp.int32(-2 ** 31), 0, b)
    return jnp.where(b < 0, b ^ jnp.int32(0x7FFFFFFF), b)


def _index_mask_kernel(qi_ref, wit_ref, *rest, nref, groups, ltot, topk, causal, case_rows):
    ki_refs = rest[:nref]
    o_ref = rest[nref]
    key_ref, sel_ref, jv_ref = rest[nref + 1:]
    tq = qi_ref.shape[0]
    i = pl.program_id(1)
    lpad = sel_ref.shape[0]
    wit = wit_ref[0]
    if case_rows:
        ncase = (jnp.maximum((i + 1) * tq, topk) + case_rows - 1) // case_rows
        used_rows = ncase * case_rows
    else:
        used_rows = None

    def score_group(dst0, rows, sources):
        accs = []
        for rp, lead, src0, width, _ in sources:
            ref = ki_refs[rp]
            ki = _bf(ref[src0:src0 + rows, :] if lead is None else ref[lead, src0:src0 + rows, :])
            acc = jnp.zeros((rows, tq), F32)
            for hd in range(IDX_HEADS):
                rel = _dot_nt(ki, qi_ref[:, hd * LANES:hd * LANES + width])
                acc = acc + wit[hd:hd + 1, :] * jnp.maximum(rel, 0.0)
            accs.append(acc)
        if len(accs) == 1:
            acc = accs[0]
        else:
            lane = lax.broadcasted_iota(I32, (rows, tq), 1)
            acc = jnp.where(lane < tq // 2, accs[0], accs[1])
        s = acc * ((IDX_DIM ** -0.5) * (IDX_HEADS ** -0.5))
        if causal:
            kpos = dst0 + lax.broadcasted_iota(I32, (rows, tq), 0)
            qpos = i * tq + lax.broadcasted_iota(I32, (rows, tq), 1)
            s = jnp.where((kpos >> CHUNK_SHIFT) <= (qpos >> CHUNK_SHIFT), s, NEG)
        key_ref[dst0:dst0 + rows, :] = _sortable(s)

    for dst0, rows, sources in groups:
        if used_rows is None:
            score_group(dst0, rows, sources)
        else:
            pl.when(dst0 < used_rows)(functools.partial(score_group, dst0, rows, sources))

    def select(nrows):
        idx_bits = max(1, (nrows - 1).bit_length())

        def count(pred):
            c = pred.astype(I32).reshape(COUNT_CHAINS, nrows // COUNT_CHAINS, tq)
            return jnp.sum(jnp.sum(c, axis=1), axis=0, keepdims=True)

        def ge_count(cand):
            return count(key_ref[0:nrows, :] >= cand)

        t0 = jnp.full((1, tq), -2 ** 31, I32)
        t = jnp.where(ge_count(jnp.zeros((1, tq), I32)) >= topk, 0, t0)

        def vbody(it, t):
            cand = t + (jnp.int32(1) << (30 - it))
            return jnp.where(ge_count(cand) >= topk, cand, t)

        t = lax.fori_loop(0, 31, vbody, t)
        keys = key_ref[0:nrows, :]
        gt = keys > t
        eq = keys == t
        need = topk - count(gt)
        rowi = lax.broadcasted_iota(I32, (nrows, tq), 0)

        def jbody(it, jv):
            cand = jv + (jnp.int32(1) << (idx_bits - 1 - it))
            below = count(eq & (rowi < cand))
            return jnp.where(below < need, cand, jv)

        jv_ref[...] = jnp.full((1, tq), nrows, I32)
        has_tie = jnp.max(count(eq) - need) > 0

        @pl.when(has_tie)
        def _():
            jv_ref[...] = lax.fori_loop(0, idx_bits, jbody, jnp.zeros((1, tq), I32))

        sel = gt | (eq & (rowi <= jv_ref[...]))
        if causal:
            qpos = i * tq + lax.broadcasted_iota(I32, (nrows, tq), 1)
            sel = sel & ((rowi >> CHUNK_SHIFT) <= (qpos >> CHUNK_SHIFT))
        sel_ref[0:nrows, :] = jnp.where(sel, 0.0, NEG)
        nreal = -(-nrows // TILE)
        if nreal * TILE > nrows:
            sel_ref[nrows:nreal * TILE, :] = jnp.full((nreal * TILE - nrows, tq), NEG, F32)
        for kt in range(lpad // TILE):
            if kt < nreal:
                o_ref[0, 0, kt] = sel_ref[kt * TILE:(kt + 1) * TILE, :].T
            else:
                o_ref[0, 0, kt] = jnp.full((tq, TILE), NEG, F32)

    if case_rows:
        for k in range(ltot // case_rows):
            pl.when(ncase == k + 1)(functools.partial(select, (k + 1) * case_rows))
    else:
        select(ltot)


def index_mask(qi, wit, ki_arrays, ki_specs, groups, ltot, nstep, nqb, causal):
    lpad = -(-ltot // KTILE) * KTILE
    nt = lpad // TILE
    topk = min(TOPK_MAX, ltot // 4)
    case_rows = MASK_CASE_ROWS if (causal and ltot % MASK_CASE_ROWS == 0 and ltot > MASK_CASE_ROWS) else 0
    kern = functools.partial(_index_mask_kernel, nref=len(ki_arrays), groups=tuple(groups), ltot=ltot,
                             topk=topk, causal=causal, case_rows=case_rows)
    return pl.pallas_call(
        kern,
        grid=(nstep, nqb),
        in_specs=[pl.BlockSpec((TILE, qi.shape[1]), lambda b, i: (b * nqb + i, 0)),
                  pl.BlockSpec((1, IDX_HEADS, TILE), lambda b, i: (b * nqb + i, 0, 0))] + list(ki_specs),
        out_specs=pl.BlockSpec((1, 1, nt, TILE, TILE), lambda b, i: (b, i, 0, 0, 0)),
        out_shape=jax.ShapeDtypeStruct((nstep, nqb, nt, TILE, TILE), F32),
        scratch_shapes=[pltpu.VMEM((ltot, TILE), I32), pltpu.VMEM((lpad, TILE), F32),
                        pltpu.VMEM((1, TILE), I32)],
        compiler_params=_params("parallel", "parallel"),
        name="index_mask",
    )(qi, wit, *ki_arrays)


def _row_groups(row0, rows, make_sources):
    out = []
    for off in range(0, rows, SCORE_ROWS):
        out.append((row0 + off, min(SCORE_ROWS, rows - off), tuple(make_sources(off))))
    return out


def _attn_core(q_ref, o_ref, qs_ref, l_ref, mx_ref, acc_ref, groups, tq):
    npairs = A_HEADS // 2
    lane = lax.broadcasted_iota(I32, (1, LANES), 1)
    keep_lo = _bf((lane < A_HEAD_DIM).astype(F32))
    keep_hi = _bf((lane >= A_HEAD_DIM).astype(F32))
    for pair in range(npairs):
        qp = q_ref[:, pair * LANES:(pair + 1) * LANES]
        qs_ref[pair, 0:tq, :] = qp * keep_lo
        qs_ref[pair, tq:2 * tq, :] = qp * keep_hi
    mx_ref[...] = jnp.full(mx_ref.shape, NEG, F32)
    acc_ref[...] = jnp.zeros(acc_ref.shape, F32)
    ones = jnp.ones((KTILE, LANES), BF16)

    def over_tiles(count, body):
        if isinstance(count, int) and count == 1:
            body(0, 0)
        else:
            lax.fori_loop(0, count, body, 0)

    for count, base, k_tile, _, mask_sub, bias_sub in groups:
        def p1(kt, carry, base=base, k_tile=k_tile, mask_sub=mask_sub, bias_sub=bias_sub):
            masks = [mask_sub(kt, sub) for sub in range(KTILE // LANES)]
            for pair in range(npairs):
                sl = slice(pair * LANES, (pair + 1) * LANES)
                l = _dot_nt(qs_ref[pair], k_tile(kt, sl))
                for half in range(2):
                    rs = slice(half * tq, (half + 1) * tq)
                    mx = mx_ref[pair, rs, :]
                    for sub in range(KTILE // LANES):
                        cs = slice(sub * LANES, (sub + 1) * LANES)
                        blk = l[rs, cs] + (masks[sub] + bias_sub(kt, sub, 2 * pair + half))
                        l_ref[pair, base + kt, rs, cs] = blk
                        mx = jnp.maximum(mx, blk)
                    mx_ref[pair, rs, :] = mx
            return carry

        over_tiles(count, p1)

    for pair in range(npairs):
        m = jnp.max(mx_ref[pair], axis=-1, keepdims=True)
        mx_ref[pair] = jnp.broadcast_to(m, mx_ref.shape[1:])

    for count, base, _, v_tile, _, _ in groups:
        def p2(kt, carry, base=base, v_tile=v_tile):
            for pair in range(npairs):
                sl = slice(pair * LANES, (pair + 1) * LANES)
                m = mx_ref[pair]
                p = jnp.concatenate(
                    [jnp.exp(l_ref[pair, base + kt, :, sub * LANES:(sub + 1) * LANES] - m)
                     for sub in range(KTILE // LANES)], axis=1)
                vaug = jnp.concatenate([v_tile(kt, sl), ones], axis=1)
                acc_ref[pair] += _dot(_bf(p), vaug)
            return carry

        over_tiles(count, p2)

    lane_full = lax.broadcasted_iota(I32, (tq, LANES), 1)
    for pair in range(npairs):
        a = acc_ref[pair]
        o = a[:, 0:LANES] / a[:, LANES:2 * LANES]
        o_ref[:, pair * LANES:(pair + 1) * LANES] = _bf(jnp.where(lane_full < A_HEAD_DIM, o[0:tq], o[tq:2 * tq]))


def _attn_scratch(tq, ntiles):
    npairs = A_HEADS // 2
    return [pltpu.VMEM((npairs, 2 * tq, LANES), BF16),
            pltpu.VMEM((npairs, ntiles, 2 * tq, KTILE), F32),
            pltpu.VMEM((npairs, 2 * tq, LANES), F32),
            pltpu.VMEM((npairs, 2 * tq, 2 * LANES), F32)]


def _attn_prompt_kernel(q_ref, k_ref, v_ref, mask_ref, bias_ref, o_ref, qs_ref, l_ref, mx_ref, acc_ref):
    i = pl.program_id(1)
    per = KTILE // TILE

    def rows(kt):
        return pl.ds(pl.multiple_of(kt * KTILE, KTILE), KTILE)

    def bias_sub(kt, sub, hd):
        s = kt * per + sub
        sel = jnp.where(s == i, 2, jnp.where(s == i - 1, 1, 0))
        return bias_ref[sel, hd]

    group = (i // per + 1, 0,
             lambda kt, sl: k_ref[rows(kt), sl],
             lambda kt, sl: v_ref[rows(kt), sl],
             lambda kt, sub: mask_ref[0, 0, kt * per + sub],
             bias_sub)
    _attn_core(q_ref, o_ref, qs_ref, l_ref, mx_ref, acc_ref, [group], TILE)


def attn_prompt(q, kb, vb, mask, bias, nseq, seqlen):
    n, hq = q.shape
    nqb = seqlen // TILE
    nt = mask.shape[2]
    return pl.pallas_call(
        _attn_prompt_kernel,
        grid=(nseq, nqb),
        in_specs=[pl.BlockSpec((TILE, hq), lambda b, i: (b * nqb + i, 0)),
                  pl.BlockSpec((seqlen, hq), lambda b, i: (b, 0)),
                  pl.BlockSpec((seqlen, hq), lambda b, i: (b, 0)),
                  pl.BlockSpec((1, 1, nt, TILE, TILE), lambda b, i: (b, i, 0, 0, 0)),
                  _full(bias.shape)],
        out_specs=pl.BlockSpec((TILE, hq), lambda b, i: (b * nqb + i, 0)),
        out_shape=jax.ShapeDtypeStruct((n, hq), BF16),
        scratch_shapes=_attn_scratch(TILE, seqlen // KTILE),
        compiler_params=_params("parallel", "arbitrary"),
        name="attn_prompt",
    )(q, kb, vb, mask, bias)


def _attn_sample_kernel(q_ref, kc_ref, vc_ref, kn_ref, vn_ref, mask_ref, bias_ref, o_ref,
                        qs_ref, l_ref, mx_ref, acc_ref, *, ncache):
    tq = q_ref.shape[0]
    per = KTILE // TILE
    last_sub = ncache * per - 1

    def rows(kt):
        return pl.ds(pl.multiple_of(kt * KTILE, KTILE), KTILE)

    def cache_bias(kt, sub, hd):
        return bias_ref[jnp.where(kt * per + sub == last_sub, 1, 0), hd]

    cache = (ncache, 0,
             lambda kt, sl: _bf(kc_ref[0, rows(kt), sl]),
             lambda kt, sl: _bf(vc_ref[0, rows(kt), sl]),
             lambda kt, sub: mask_ref[0, 0, kt * per + sub],
             cache_bias)
    fresh = (1, ncache,
             lambda kt, sl: kn_ref[0, :, sl],
             lambda kt, sl: vn_ref[0, :, sl],
             lambda kt, sub: mask_ref[0, 0, ncache * per + sub],
             lambda kt, sub, hd: bias_ref[2 if sub == 0 else 0, hd])
    _attn_core(q_ref, o_ref, qs_ref, l_ref, mx_ref, acc_ref, [cache, fresh], tq)


def attn_sample(q, kc, vc, kn, vn, mask, bias, nseq, tq):
    n, hq = q.shape
    past = kc.shape[1]
    ncache = past // KTILE
    nt = mask.shape[2]
    return pl.pallas_call(
        functools.partial(_attn_sample_kernel, ncache=ncache),
        grid=(nseq,),
        in_specs=[pl.BlockSpec((tq, hq), lambda s: (s, 0)),
                  pl.BlockSpec((1, past, hq), lambda s: (s, 0, 0)),
                  pl.BlockSpec((1, past, hq), lambda s: (s, 0, 0)),
                  pl.BlockSpec((1, KTILE, hq), lambda s: (s, 0, 0)),
                  pl.BlockSpec((1, KTILE, hq), lambda s: (s, 0, 0)),
                  pl.BlockSpec((1, 1, nt, tq, TILE), lambda s: (s // (TILE // tq), 0, 0, s % (TILE // tq), 0)),
                  _full(bias.shape)],
        out_specs=pl.BlockSpec((tq, hq), lambda s: (s, 0)),
        out_shape=jax.ShapeDtypeStruct((n, hq), BF16),
        scratch_shapes=_attn_scratch(tq, ncache + 1),
        compiler_params=_params("parallel"),
        name="attn_sample",
    )(q, kc, vc, kn, vn, mask, bias)


def _t5_bucket(rel):
    half = N_BUCKETS // 2
    max_exact = half // 2
    n = jnp.abs(rel)
    nf = jnp.maximum(n, 1).astype(F32)
    large = max_exact + (jnp.log(nf / max_exact) / math.log(MAX_DISTANCE / max_exact)
                         * (half - max_exact)).astype(I32)
    large = jnp.minimum(large, half - 1)
    return jnp.where(rel > 0, half, 0) + jnp.where(n < max_exact, n, large)


def bias_tiles(rel_bias):
    heads = rel_bias.shape[1]
    span = 2 * TILE - 1

    def toeplitz(shift):
        rel = jnp.arange(span, dtype=I32) - (TILE - 1) + shift
        tab = rel_bias.astype(F32)[_t5_bucket(rel)].T
        strip = jnp.tile(jnp.pad(tab, ((0, 0), (0, 1))), (1, TILE))[:, :TILE * span]
        return strip.reshape(heads, TILE, span)[:, :, TILE - 1:]

    far = rel_bias.astype(F32)[_t5_bucket(jnp.full((1,), -(TILE + 1), I32))]
    far = jnp.broadcast_to(far.T[:, :, None], (heads, TILE, TILE))
    return jnp.stack([far, toeplitz(-TILE), toeplitz(0)])


def dsa_core_prompt(pr, nseq, seqlen, bias):
    q, _, kb, _, vb, qi, kiwi = pr
    nqb = seqlen // TILE
    wit = kiwi[:, IDX_DIM:IDX_DIM + IDX_HEADS].reshape(nseq * nqb, TILE, IDX_HEADS).swapaxes(1, 2)
    groups = _row_groups(0, seqlen, lambda off: [(0, None, off, LANES, None)])
    mask = index_mask(qi, wit, [kiwi], [pl.BlockSpec((seqlen, LANES), lambda b, i: (b, 0))],
                      groups, seqlen, nseq, nqb, causal=True)
    return attn_prompt(q, kb, vb, mask, bias, nseq, seqlen)


def dsa_core_sample(pr, k_cache, v_cache, ki_cache, nseq, tq, bias):
    q, _, kb, _, vb, qi, kiwi = pr
    past = k_cache.shape[1]
    hq = q.shape[1]
    pad = KTILE - tq
    per = TILE // tq
    wit = kiwi[:, IDX_DIM:IDX_DIM + IDX_HEADS].reshape(nseq // per, TILE, IDX_HEADS).swapaxes(1, 2)
    groups = (_row_groups(0, past, lambda off: [(0, j, off, IDX_DIM, j) for j in range(per)])
              + _row_groups(past, tq, lambda off: [(1, None, j * tq + off, LANES, j) for j in range(per)]))
    mask = index_mask(
        qi, wit, [ki_cache, kiwi],
        [pl.BlockSpec((per, past, IDX_DIM), lambda b, i: (b, 0, 0)),
         pl.BlockSpec((TILE, LANES), lambda b, i: (b, 0))],
        groups, past + tq, nseq // per, 1, causal=False)
    kn = jnp.pad(kb.reshape(nseq, tq, hq), ((0, 0), (0, pad), (0, 0)))
    vn = jnp.pad(vb.reshape(nseq, tq, hq), ((0, 0), (0, pad), (0, 0)))
    return attn_sample(q, k_cache.reshape(nseq, past, hq), v_cache.reshape(nseq, past, hq),
                       kn, vn, mask, bias[:, :, :tq, :], nseq, tq)


def _log_sigmoid(x):
    return jnp.minimum(x, 0.0) - jnp.log1p(jnp.exp(-jnp.abs(x)))


def _mlstm_kernel(p_ref, gs_ref, gr_ref, gbc_ref, gbr_ref, hg_ref, c0_ref, n0_ref, m0_ref,
                  hs_ref, c_ref, n_ref, m_ref, cs, ns, ms):
    c_id = pl.program_id(1)
    nh = B_HEADS
    npair = nh // 2
    hw = LANES
    L = p_ref.shape[0]
    L2 = 2 * L

    @pl.when(c_id == 0)
    def _():
        cs[...] = jnp.zeros(cs.shape, F32)
        ns[...] = jnp.zeros(ns.shape, F32)
        ns[:, 0:B_QK_DIM] = n0_ref[0]
        for hd in range(nh):
            cs[hd // 2, (hd % 2) * hw:(hd % 2) * hw + B_QK_DIM, :] = c0_ref[0, hd]
        ms[...] = m0_ref[0]

    r = lax.broadcasted_iota(I32, (L2, L2), 0)
    c = lax.broadcasted_iota(I32, (L2, L2), 1)
    same = (r >> CHUNK_SHIFT) == (c >> CHUNK_SHIFT)
    incl = same & (c <= r)
    gs = gs_ref[0] + gbc_ref[...]
    gr = gr_ref[0] + gbr_ref[...]
    bcol = _dot_exact_lhs01(_bf01(incl), _log_sigmoid(gs[:, npair:]))
    brow = _dot_exact_rhs01(_log_sigmoid(gr[npair:, :]), _bf01(same & (r <= c)))
    icol = gs[:, :npair]
    irow = gr[:npair, :]
    rowc = lax.broadcasted_iota(I32, (L2, 1), 0)
    is_top = rowc < L
    top = is_top.astype(F32)
    bot = 1.0 - top
    row2 = lax.broadcasted_iota(I32, (2 * hw, 1), 0)
    hg = hg_ref[...]
    mall = ms[...]

    def stack(base, pr):
        a = base + 2 * pr * hw
        return jnp.concatenate([p_ref[:, a:a + hw], p_ref[:, a + hw:a + 2 * hw]], axis=0)

    def bd(x):
        return _bf(jnp.concatenate([x * top, x * bot], axis=1))

    pairs = range(npair)
    qf = [stack(0, pr) for pr in pairs]
    kf = [stack(nh * hw, pr) * (B_QK_DIM ** -0.5) for pr in pairs]
    vb = [_bf(stack(2 * nh * hw, pr)) for pr in pairs]
    bc = [bcol[:, pr:pr + 1] for pr in pairs]
    ic = [icol[:, pr:pr + 1] for pr in pairs]
    mc = [mall[:, pr:pr + 1] for pr in pairs]
    d = [jnp.where(incl, bc[pr] - brow[pr:pr + 1, :] + irow[pr:pr + 1, :], NEG) for pr in pairs]
    inter = [bc[pr] + mc[pr] for pr in pairs]
    mt = [jnp.maximum(inter[pr], jnp.max(d[pr], axis=-1, keepdims=True)) for pr in pairs]
    s = [_dot_nt(_bf(qf[pr]), _bf(kf[pr])) * jnp.exp(d[pr] - mt[pr]) for pr in pairs]
    wst = [jnp.exp(inter[pr] - mt[pr]) for pr in pairs]
    cmat = [cs[pr] for pr in pairs]
    num = [_dot(_bf(s[pr]), vb[pr]) + wst[pr] * _dot(bd(qf[pr]), _bf(cmat[pr])) for pr in pairs]
    for pr in pairs:
        n0 = ns[2 * pr:2 * pr + 1, :]
        n1 = ns[2 * pr + 1:2 * pr + 2, :]
        qn = jnp.sum(qf[pr] * jnp.where(is_top, n0, n1), axis=-1, keepdims=True)
        den = jnp.sum(s[pr], axis=-1, keepdims=True) + wst[pr] * qn
        hs = num[pr] / jnp.maximum(jnp.abs(den), jnp.exp(-mt[pr]))
        bl0 = bc[pr][L - 1:L, :]
        bl1 = bc[pr][L2 - 1:L2, :]
        bl = jnp.where(is_top, bl0, bl1)
        dec = bl - bc[pr] + ic[pr]
        blm = bl + mc[pr]
        mnew0 = jnp.maximum(blm[0:1, :], jnp.max(dec[0:L], axis=0, keepdims=True))
        mnew1 = jnp.maximum(blm[L:L + 1, :], jnp.max(dec[L:L2], axis=0, keepdims=True))
        mnew = jnp.where(is_top, mnew0, mnew1)
        wk = jnp.exp(dec - mnew)
        ws = jnp.exp(blm - mnew)
        kw = kf[pr] * wk
        ws2 = jnp.where(row2 < hw, ws[0:1, :], ws[L:L + 1, :])
        cs[pr] = ws2 * cmat[pr] + _dot_tn(bd(kw), vb[pr])
        ns[2 * pr:2 * pr + 1, :] = ws[0:1, :] * n0 + jnp.sum(kw[0:L], axis=0, keepdims=True)
        ns[2 * pr + 1:2 * pr + 2, :] = ws[L:L + 1, :] * n1 + jnp.sum(kw[L:L2], axis=0, keepdims=True)
        ms[:, pr:pr + 1] = mnew
        on = _bf(_rms(hs, hg) * _sigmoid(stack(3 * nh * hw, pr)))
        oa = 2 * pr * B_V_DIM
        hs_ref[:, oa:oa + B_V_DIM] = on[0:L]
        hs_ref[:, oa + B_V_DIM:oa + 2 * B_V_DIM] = on[L:L2]

    @pl.when(c_id == pl.num_programs(1) - 1)
    def _():
        for hd in range(nh):
            c_ref[0, hd] = cs[hd // 2, (hd % 2) * hw:(hd % 2) * hw + B_QK_DIM, :]
        n_ref[0] = ns[:, 0:B_QK_DIM]
        m_ref[0] = ms[...]


def mlstm_weight(w_in):
    d = w_in.shape[0]
    nq = B_HEADS * B_QK_DIM
    nv = B_HEADS * B_V_DIM

    def padh(w):
        w = w.reshape(d, B_HEADS, B_QK_DIM)
        return jnp.pad(w, ((0, 0), (0, 0), (0, LANES - B_QK_DIM))).reshape(d, B_HEADS * LANES)

    main = 2 * nq + 2 * nv
    tail = jnp.pad(w_in[:, main:], ((0, 0), (0, LANES - 2 * B_HEADS)))
    return _bf(jnp.concatenate([padh(w_in[:, :nq]), padh(w_in[:, nq:2 * nq]), w_in[:, 2 * nq:main], tail], axis=1))


def mlstm_core(p, gate_bias, h_gain, c0, n0, m0, nseq, nchunk):
    n, m = p.shape
    nh = B_HEADS
    npair = nh // 2
    gcol = 4 * nh * LANES
    pre = p[:, gcol:gcol + 2 * nh].reshape(nseq * nchunk, CHUNK, 2, npair, 2)
    gs = pre.transpose(0, 4, 1, 2, 3).reshape(nseq * nchunk, 2 * CHUNK, 2 * npair)
    gr = pre.transpose(0, 2, 3, 4, 1).reshape(nseq * nchunk, 2 * npair, 2 * CHUNK)
    gb = gate_bias.reshape(2, npair, 2)
    gbc = jnp.repeat(gb.transpose(2, 0, 1).reshape(2, 2 * npair), CHUNK, axis=0)
    m0s = jnp.repeat(m0.reshape(nseq, npair, 2).transpose(0, 2, 1), CHUNK, axis=1)
    o, c_new, n_new, m_new = pl.pallas_call(
        _mlstm_kernel,
        grid=(nseq, nchunk),
        in_specs=[pl.BlockSpec((CHUNK, m), lambda s, c: (s * nchunk + c, 0)),
                  pl.BlockSpec((1, 2 * CHUNK, 2 * npair), lambda s, c: (s * nchunk + c, 0, 0)),
                  pl.BlockSpec((1, 2 * npair, 2 * CHUNK), lambda s, c: (s * nchunk + c, 0, 0)),
                  _full((2 * CHUNK, 2 * npair)), _full((2 * npair, 2 * CHUNK)), _full((1, B_V_DIM)),
                  pl.BlockSpec((1, nh, B_QK_DIM, B_V_DIM), lambda s, c: (s, 0, 0, 0)),
                  pl.BlockSpec((1, nh, B_QK_DIM), lambda s, c: (s, 0, 0)),
                  pl.BlockSpec((1, 2 * CHUNK, npair), lambda s, c: (s, 0, 0))],
        out_specs=[pl.BlockSpec((CHUNK, nh * B_V_DIM), lambda s, c: (s * nchunk + c, 0)),
                   pl.BlockSpec((1, nh, B_QK_DIM, B_V_DIM), lambda s, c: (s, 0, 0, 0)),
                   pl.BlockSpec((1, nh, B_QK_DIM), lambda s, c: (s, 0, 0)),
                   pl.BlockSpec((1, 2 * CHUNK, npair), lambda s, c: (s, 0, 0))],
        out_shape=[jax.ShapeDtypeStruct((n, nh * B_V_DIM), BF16),
                   jax.ShapeDtypeStruct((nseq, nh, B_QK_DIM, B_V_DIM), F32),
                   jax.ShapeDtypeStruct((nseq, nh, B_QK_DIM), F32),
                   jax.ShapeDtypeStruct((nseq, 2 * CHUNK, npair), F32)],
        scratch_shapes=[pltpu.VMEM((npair, 2 * LANES, B_V_DIM), F32), pltpu.VMEM((nh, LANES), F32),
                        pltpu.VMEM((2 * CHUNK, npair), F32)],
        compiler_params=_params("parallel", "arbitrary"),
        name="mlstm",
    )(p, gs, gr, gbc, gbc.T, h_gain.reshape(1, B_V_DIM), c0, n0, m0s)
    m_heads = m_new[:, ::CHUNK, :].transpose(0, 2, 1).reshape(nseq, nh)
    return o, c_new, n_new, m_heads


def _split2(x):
    hi = _bf(x)
    return hi, _bf(x - hi.astype(F32))


def _cat3_lhs(x):
    hi, mid = _split2(x)
    return jnp.concatenate([hi, hi, mid], axis=1)


def _cat3_rhs(x):
    hi, mid = _split2(x)
    return jnp.concatenate([hi, mid, hi], axis=0)


def _gdn_kernel(p_ref, gs_ref, gr_ref, cw_ref, alc_ref, alr_ref, dtc_ref, dtr_ref, og_ref, s0_ref, cb0_ref,
                o_ref, s_ref, cb_ref, ss, tail):
    c_id = pl.program_id(1)
    nh = C_HEADS
    cdim = nh * (2 * C_DK + C_DV)
    L = p_ref.shape[0]
    L2 = 2 * L
    nprev = CONV_W - 1

    @pl.when(c_id == 0)
    def _():
        ss[...] = s0_ref[0]
        tail[...] = jnp.zeros(tail.shape, F32)
        tail[8 - nprev:8, :] = cb0_ref[0]

    x = p_ref[:, 0:cdim]
    ext = jnp.concatenate([tail[...], x], axis=0)
    conv = ext[8:8 + L] * cw_ref[CONV_W - 1:CONV_W, :]
    for j in range(CONV_W - 1):
        conv = conv + ext[8 - nprev + j:8 - nprev + j + L] * cw_ref[j:j + 1, :]
    tail[...] = x[L - 8:L, :]
    cf = _silu(conv)

    npair = nh // 2
    gs = gs_ref[0]
    gr = gr_ref[0]
    beta = _sigmoid(gs[:, 0:npair])
    g_col = -jnp.exp(alc_ref[...]) * _softplus(gs[:, npair:] + dtc_ref[...])
    g_row = -jnp.exp(alr_ref[...]) * _softplus(gr[npair:, :] + dtr_ref[...])
    r = lax.broadcasted_iota(I32, (L2, L2), 0)
    c = lax.broadcasted_iota(I32, (L2, L2), 1)
    same = (r >> CHUNK_SHIFT) == (c >> CHUNK_SHIFT)
    incl = same & (c <= r)
    strict = same & (c < r)
    eye = (c == r).astype(F32)
    gc_col = _dot_exact_lhs01(_bf01(incl), g_col)
    gc_row = _dot_exact_rhs01(g_row, _bf01(same & (r <= c)))
    rowc = lax.broadcasted_iota(I32, (L2, 1), 0)
    top = (rowc < L).astype(F32)
    bot = 1.0 - top
    row2 = lax.broadcasted_iota(I32, (2 * C_DK, 1), 0)
    og = og_ref[...]

    def stack(base, pr):
        a = base + 2 * pr * C_DK
        return jnp.concatenate([cf[:, a:a + C_DK], cf[:, a + C_DK:a + 2 * C_DK]], axis=0)

    def bd(x):
        return _bf(jnp.concatenate([x * top, x * bot], axis=1))

    pairs = range(npair)
    qc, kc, kcb, dm, amat, rhs, gcols, egcs = [], [], [], [], [], [], [], []
    for pr in pairs:
        qraw = stack(0, pr)
        kraw = stack(nh * C_DK, pr)
        vc = stack(2 * nh * C_DK, pr)
        qc.append(qraw * lax.rsqrt(jnp.sum(qraw * qraw, axis=-1, keepdims=True) + EPS) * (C_DK ** -0.5))
        kc.append(kraw * lax.rsqrt(jnp.sum(kraw * kraw, axis=-1, keepdims=True) + EPS))
        bc = beta[:, pr:pr + 1]
        gcol = gc_col[:, pr:pr + 1]
        grow = gc_row[pr:pr + 1, :]
        dm.append(jnp.where(incl, jnp.exp(jnp.where(incl, gcol - grow, 0.0)), 0.0))
        kb = kc[pr] * bc
        kcb.append(_bf(kc[pr]))
        amat.append(jnp.where(strict, _dot_nt(_bf(kb), kcb[pr]) * dm[pr], 0.0))
        egc = jnp.exp(gcol)
        rhs.append(jnp.concatenate([vc * bc, kb * egc], axis=-1))
        gcols.append(gcol)
        egcs.append(egc)
    tinv = [eye - amat[pr] for pr in pairs]
    pw_l = [_cat3_lhs(-amat[pr]) for pr in pairs]
    pw_r = [_cat3_rhs(-amat[pr]) for pr in pairs]
    for _ in range(CHUNK_SHIFT - 1):
        pw = [_dot(pw_l[pr], pw_r[pr]) for pr in pairs]
        pw_l = [_cat3_lhs(pw[pr]) for pr in pairs]
        pw_r = [_cat3_rhs(pw[pr]) for pr in pairs]
        tinv = [tinv[pr] + _dot(_cat3_lhs(tinv[pr]), pw_r[pr]) for pr in pairs]
    sol = [_dot(_cat3_lhs(tinv[pr]), _cat3_rhs(rhs[pr])) for pr in pairs]
    attn = [_dot_nt(_bf(qc[pr]), kcb[pr]) * dm[pr] for pr in pairs]
    smat = [ss[pr] for pr in pairs]
    sb = [_bf(smat[pr]) for pr in pairs]
    vnew = [sol[pr][:, :C_DV] - _dot(bd(sol[pr][:, C_DV:]), sb[pr]) for pr in pairs]
    o = [_dot(bd(qc[pr] * egcs[pr]), sb[pr]) + _dot(_bf(attn[pr]), _bf(vnew[pr])) for pr in pairs]
    for pr in pairs:
        gl0 = gcols[pr][L - 1:L, :]
        gl1 = gcols[pr][L2 - 1:L2, :]
        ke = kc[pr] * jnp.exp(jnp.where(rowc < L, gl0, gl1) - gcols[pr])
        decay = jnp.exp(jnp.where(row2 < C_DK, gl0, gl1))
        ss[pr] = smat[pr] * decay + _dot_tn(bd(ke), _bf(vnew[pr]))
    for pr in pairs:
        za = cdim + 2 * pr * C_DV
        z = jnp.concatenate([p_ref[:, za:za + C_DV], p_ref[:, za + C_DV:za + 2 * C_DV]], axis=0)
        on = _bf(_rms(o[pr], og) * _silu(z))
        oa = 2 * pr * C_DV
        o_ref[:, oa:oa + C_DV] = on[0:L]
        o_ref[:, oa + C_DV:oa + 2 * C_DV] = on[L:L2]

    @pl.when(c_id == pl.num_programs(1) - 1)
    def _():
        s_ref[0] = ss[...]
        cb_ref[0] = tail[8 - nprev:8, :]


def gdn_weight(w_in):
    main = C_HEADS * (2 * C_DK + C_DV) + C_HEADS * C_DV
    tail = jnp.pad(w_in[:, main:], ((0, 0), (0, LANES - 2 * C_HEADS)))
    return _bf(jnp.concatenate([w_in[:, :main], tail], axis=1))


def gdn_core(p, conv_w, a_log, dt_bias, o_gain, s0, cb0, nseq, nchunk):
    n, m = p.shape
    nh = C_HEADS
    npair = nh // 2
    cdim = nh * (2 * C_DK + C_DV)
    gcolumn = cdim + nh * C_DV
    pre = p[:, gcolumn:gcolumn + 2 * nh].reshape(nseq * nchunk, CHUNK, 2, npair, 2)
    gs = pre.transpose(0, 4, 1, 2, 3).reshape(nseq * nchunk, 2 * CHUNK, 2 * npair)
    gr = pre.transpose(0, 2, 3, 4, 1).reshape(nseq * nchunk, 2 * npair, 2 * CHUNK)

    def col(v):
        return jnp.repeat(v.reshape(npair, 2).T, CHUNK, axis=0)

    o, s_new, cb_new = pl.pallas_call(
        _gdn_kernel,
        grid=(nseq, nchunk),
        in_specs=[pl.BlockSpec((CHUNK, m), lambda s, c: (s * nchunk + c, 0)),
                  pl.BlockSpec((1, 2 * CHUNK, 2 * npair), lambda s, c: (s * nchunk + c, 0, 0)),
                  pl.BlockSpec((1, 2 * npair, 2 * CHUNK), lambda s, c: (s * nchunk + c, 0, 0)),
                  _full((CONV_W, cdim)),
                  _full((2 * CHUNK, npair)), _full((npair, 2 * CHUNK)),
                  _full((2 * CHUNK, npair)), _full((npair, 2 * CHUNK)),
                  _full((1, C_DV)),
                  pl.BlockSpec((1, npair, 2 * C_DK, C_DV), lambda s, c: (s, 0, 0, 0)),
                  pl.BlockSpec((1, CONV_W - 1, cdim), lambda s, c: (s, 0, 0))],
        out_specs=[pl.BlockSpec((CHUNK, nh * C_DV), lambda s, c: (s * nchunk + c, 0)),
                   pl.BlockSpec((1, npair, 2 * C_DK, C_DV), lambda s, c: (s, 0, 0, 0)),
                   pl.BlockSpec((1, CONV_W - 1, cdim), lambda s, c: (s, 0, 0))],
        out_shape=[jax.ShapeDtypeStruct((n, nh * C_DV), BF16),
                   jax.ShapeDtypeStruct((nseq, npair, 2 * C_DK, C_DV), F32),
                   jax.ShapeDtypeStruct((nseq, CONV_W - 1, cdim), F32)],
        scratch_shapes=[pltpu.VMEM((npair, 2 * C_DK, C_DV), F32), pltpu.VMEM((8, cdim), F32)],
        compiler_params=_params("parallel", "arbitrary"),
        name="gdn",
    )(p, gs, gr, conv_w, col(a_log), col(a_log).T, col(dt_bias), col(dt_bias).T, o_gain.reshape(1, C_DV),
      s0.reshape(nseq, npair, 2 * C_DK, C_DV), cb0)
    return o, s_new.reshape(nseq, nh, C_DK, C_DV), cb_new


def _trunk(x, nseq, seqlen, mem_k, mem_v, st, W, bias, is_prompt):
    d = x.shape[-1]
    n = nseq * seqlen
    x = x.reshape(n, d)
    tm = min(256, n)
    tm_mem = min(256, seqlen)
    new = {}
    for i in range(4):
        kind = i % 3
        mx = W["mixer"][i]
        if kind == 0:
            pr = proj_dsa(x, W["norm_mix"][i], mx["w_in"], mx["q_gain"], mx["k_gain"], mx["ki_gain"], tm)
            if is_prompt:
                o = dsa_core_prompt(pr, nseq, seqlen, bias)
            else:
                o = dsa_core_sample(pr, *st[i], nseq, seqlen, bias)
            new[i] = (pr[1].reshape(nseq, seqlen, A_HEADS, A_HEAD_DIM),
                      pr[3].reshape(nseq, seqlen, A_HEADS, A_HEAD_DIM),
                      pr[6][:, :IDX_DIM].reshape(nseq, seqlen, IDX_DIM))
        elif kind == 1:
            p = proj(x, W["norm_mix"][i], mx["w_in"], tm)
            o, c_new, n_new, m_new = mlstm_core(p, mx["gate_bias"], mx["h_gain"], *st[i], nseq, seqlen // CHUNK)
            new[i] = (c_new, n_new, m_new.reshape(nseq, B_HEADS))
        else:
            p = proj(x, W["norm_mix"][i], mx["w_in"], tm)
            o, s_new, cb_new = gdn_core(p, mx["conv_w"], mx["a_log"], mx["dt_bias"], mx["o_gain"], *st[i],
                                        nseq, seqlen // CHUNK)
            new[i] = (s_new, cb_new)
        x = mem_attend(x, o, mx["w_out"], W["norm_mem"][i], W["w_mq"][i], W["mq_gain"][i],
                       mem_k[i], mem_v[i], W["w_mo"][i], tm_mem, seqlen // tm_mem)
        x = ffn(x, W["norm_ffn"][i], W["w_ffn1"][i], W["w_ffn3"][i], W["w_ffn2"][i], tm)
    return x.reshape(nseq, seqlen, d), new


def kernel(x_prompt, x_sample, mem_prompt, cache_l0_k, cache_l0_v, cache_l0_kidx, state_l1_C, state_l1_n, state_l1_m, state_l2_S, state_l2_conv, cache_l3_k, cache_l3_v, cache_l3_kidx, cache_mem_k, cache_mem_v, rel_bias, norm_mix, norm_mem, norm_ffn, mem_norm, w_mq, w_mk, w_mv, w_mo, mq_gain, mk_gain, w_ffn1, w_ffn3, w_ffn2, a0_w_in, a0_w_out, a0_q_gain, a0_k_gain, a0_kidx_gain, b1_w_in, b1_gate_bias, b1_h_gain, b1_w_out, c2_w_in, c2_conv_w, c2_a_log, c2_dt_bias, c2_o_gain, c2_w_out, a3_w_in, a3_w_out, a3_q_gain, a3_k_gain, a3_kidx_gain):
    B, T, D = x_prompt.shape
    S, Ts, _ = x_sample.shape
    depth = w_mq.shape[0]
    mlen = mem_prompt.shape[1]
    mw = MEM_HEADS * MEM_HEAD_DIM

    def dsa_w(w_in, w_out, qg, kg, kig):
        return dict(w_in=dsa_weight(w_in), w_out=_bf(w_out), q_gain=qg, k_gain=kg, ki_gain=kig)

    W = dict(
        norm_mix=norm_mix, norm_mem=norm_mem, norm_ffn=norm_ffn,
        w_mq=_bf(w_mq), w_mo=_bf(w_mo), mq_gain=mq_gain,
        w_ffn1=_bf(w_ffn1), w_ffn3=_bf(w_ffn3), w_ffn2=_bf(w_ffn2),
        mixer={
            0: dsa_w(a0_w_in, a0_w_out, a0_q_gain, a0_k_gain, a0_kidx_gain),
            1: dict(w_in=mlstm_weight(b1_w_in), gate_bias=b1_gate_bias, h_gain=b1_h_gain, w_out=_bf(b1_w_out)),
            2: dict(w_in=gdn_weight(c2_w_in), conv_w=c2_conv_w, a_log=c2_a_log, dt_bias=c2_dt_bias,
                    o_gain=c2_o_gain, w_out=_bf(c2_w_out)),
            3: dsa_w(a3_w_in, a3_w_out, a3_q_gain, a3_k_gain, a3_kidx_gain),
        },
    )
    bias = bias_tiles(rel_bias)

    mk_p, mv_p = mem_kv(mem_prompt.reshape(B * mlen, D), mem_norm, w_mk, w_mv, mk_gain)
    mk_p = mk_p.reshape(depth, B, mlen, mw)
    mv_p = mv_p.reshape(depth, B, mlen, mw)
    st_p = {
        0: None,
        1: (jnp.zeros((B, B_HEADS, B_QK_DIM, B_V_DIM), F32), jnp.zeros((B, B_HEADS, B_QK_DIM), F32),
            jnp.full((B, B_HEADS), NEG, F32)),
        2: (jnp.zeros((B, C_HEADS, C_DK, C_DV), F32), jnp.zeros((B, CONV_W - 1, state_l2_conv.shape[-1]), F32)),
        3: None,
    }
    y_p, np_ = _trunk(x_prompt, B, T, mk_p, mv_p, st_p, W, bias, True)

    st_s = {
        0: (cache_l0_k, cache_l0_v, cache_l0_kidx),
        1: (state_l1_C, state_l1_n, state_l1_m),
        2: (state_l2_S, state_l2_conv),
        3: (cache_l3_k, cache_l3_v, cache_l3_kidx),
    }
    mk_s = cache_mem_k.reshape(depth, S, mlen, mw)
    mv_s = cache_mem_v.reshape(depth, S, mlen, mw)
    y_s, ns_ = _trunk(x_sample, S, Ts, mk_s, mv_s, st_s, W, bias, False)

    shp = (depth, B, mlen, MEM_HEADS, MEM_HEAD_DIM)
    return (y_p, y_s,
            *np_[0], *np_[1], *np_[2], *np_[3], mk_p.reshape(shp), mv_p.reshape(shp),
            *ns_[0], *ns_[1], *ns_[2], *ns_[3])
```

```python
import functools
import math

import jax
import jax.numpy as jnp
from jax import lax
from jax.experimental import pallas as pl
from jax.experimental.pallas import tpu as pltpu

F32 = jnp.float32
BF16 = jnp.bfloat16
I32 = jnp.int32

EPS = 1e-6
NEG = -1e30
CHUNK = 64
CHUNK_SHIFT = 6
LANES = 128
TILE = 128
KTILE = 256
COUNT_CHAINS = 8
SCORE_ROWS = 256
FFN_ROWS = 512
MEM_ROWS = 256
MASK_CASE_ROWS = 512
VMEM_LIMIT = 56 * 1024 * 1024

A_HEADS, A_HEAD_DIM = 16, 64
IDX_HEADS, IDX_DIM = 8, 64
TOPK_MAX = 256
N_BUCKETS, MAX_DISTANCE = 32, 128
B_HEADS, B_QK_DIM, B_V_DIM = 8, 64, 128
C_HEADS, C_DK, C_DV = 8, 128, 128
CONV_W = 4
MEM_HEADS, MEM_HEAD_DIM = 4, 128


def _bf(x):
    return x.astype(BF16)


def _bf01(mask):
    return mask.astype(F32).astype(BF16)


def _dot(a, b):
    return jnp.dot(a, b, preferred_element_type=F32)


def _dot_nt(a, b):
    return lax.dot_general(a, b, (((1,), (1,)), ((), ())), preferred_element_type=F32)


def _dot_tn(a, b):
    return lax.dot_general(a, b, (((0,), (0,)), ((), ())), preferred_element_type=F32)


def _split3(x):
    hi = _bf(x)
    r1 = x - hi.astype(F32)
    mid = _bf(r1)
    lo = _bf(r1 - mid.astype(F32))
    return hi, mid, lo


def _dot_exact_rhs01(x, m01):
    hi, mid, lo = _split3(x)
    return _dot(hi, m01) + _dot(mid, m01) + _dot(lo, m01)


def _dot_exact_lhs01(m01, x):
    hi, mid, lo = _split3(x)
    return _dot(m01, hi) + _dot(m01, mid) + _dot(m01, lo)


def _dot_f32(a, b):
    ah, am, al = _split3(a)
    bh, bm, bl = _split3(b)
    return (_dot(ah, bh) + (_dot(ah, bm) + _dot(am, bh))
            + (_dot(am, bm) + _dot(ah, bl) + _dot(al, bh)))


def _rms(x, g):
    ms = jnp.mean(x * x, axis=-1, keepdims=True)
    return x * lax.rsqrt(ms + EPS) * g


def _sigmoid(x):
    return 1.0 / (1.0 + jnp.exp(-x))


def _silu(x):
    return x * _sigmoid(x)


def _softplus(x):
    return jnp.maximum(x, 0.0) + jnp.log1p(jnp.exp(-jnp.abs(x)))


def _params(*sem):
    return pltpu.CompilerParams(dimension_semantics=sem, vmem_limit_bytes=VMEM_LIMIT)


def _full(shape):
    n = len(shape)
    return pl.BlockSpec(shape, lambda *_: (0,) * n)


def _proj_kernel(x_ref, g_ref, w_ref, o_ref, *, col_chunk):
    h = _bf(_rms(x_ref[...], g_ref[...]))
    m = w_ref.shape[1]
    for c in range(0, m, col_chunk):
        e = min(c + col_chunk, m)
        o_ref[:, c:e] = _dot(h, w_ref[:, c:e])


def proj(x, g, w, tm):
    n, d = x.shape
    m = w.shape[1]
    return pl.pallas_call(
        functools.partial(_proj_kernel, col_chunk=512),
        grid=(n // tm,),
        in_specs=[pl.BlockSpec((tm, d), lambda i: (i, 0)), _full((1, d)), _full((d, m))],
        out_specs=pl.BlockSpec((tm, m), lambda i: (i, 0)),
        out_shape=jax.ShapeDtypeStruct((n, m), F32),
        compiler_params=_params("parallel"),
        name="proj",
    )(x, g.reshape(1, d), w)


def _memkv_kernel(x_ref, g_ref, wk_ref, wv_ref, kg_ref, k_ref, v_ref):
    h = _bf(_rms(x_ref[...], g_ref[0]))
    k = _dot(h, wk_ref[0])
    v_ref[0] = _dot(h, wv_ref[0])
    kg = kg_ref[0]
    for hd in range(MEM_HEADS):
        sl = slice(hd * MEM_HEAD_DIM, (hd + 1) * MEM_HEAD_DIM)
        k_ref[0, :, sl] = _rms(k[:, sl], kg)


def mem_kv(mem2d, mem_norm, w_mk, w_mv, mk_gain, tm=256):
    n, d = mem2d.shape
    depth = w_mk.shape[0]
    mw = w_mk.shape[2]
    return pl.pallas_call(
        _memkv_kernel,
        grid=(depth, n // tm),
        in_specs=[pl.BlockSpec((tm, d), lambda l, i: (i, 0)),
                  pl.BlockSpec((1, 1, d), lambda l, i: (l, 0, 0)),
                  pl.BlockSpec((1, d, mw), lambda l, i: (l, 0, 0)),
                  pl.BlockSpec((1, d, mw), lambda l, i: (l, 0, 0)),
                  pl.BlockSpec((1, 1, MEM_HEAD_DIM), lambda l, i: (l, 0, 0))],
        out_specs=[pl.BlockSpec((1, tm, mw), lambda l, i: (l, i, 0)),
                   pl.BlockSpec((1, tm, mw), lambda l, i: (l, i, 0))],
        out_shape=[jax.ShapeDtypeStruct((depth, n, mw), F32)] * 2,
        compiler_params=_params("parallel", "parallel"),
        name="mem_kv",
    )(mem2d, mem_norm.reshape(depth, 1, d), _bf(w_mk), _bf(w_mv), mk_gain.reshape(depth, 1, MEM_HEAD_DIM))


def _memattn_kernel(x_ref, o_ref, wo_ref, g_ref, wq_ref, qg_ref, mk_ref, mv_ref, wmo_ref, y_ref):
    x1 = x_ref[...] + _dot(o_ref[...], wo_ref[...])
    h = _bf(_rms(x1, g_ref[...]))
    q = _dot(h, wq_ref[...])
    qg = qg_ref[...]
    scale = MEM_HEAD_DIM ** -0.5
    nsub = mk_ref.shape[0]
    rows = x1.shape[0] // nsub
    cells = [(hd, s) for hd in range(MEM_HEADS) for s in range(nsub)]
    hsl = lambda hd: slice(hd * MEM_HEAD_DIM, (hd + 1) * MEM_HEAD_DIM)
    qh = [_bf(_rms(q[:, hsl(hd)], qg)) for hd in range(MEM_HEADS)]
    logits = [_dot_nt(qh[hd][s * rows:(s + 1) * rows], _bf(mk_ref[s, :, hsl(hd)])) * scale for hd, s in cells]
    ps = [jnp.exp(l - jnp.max(l, axis=-1, keepdims=True)) for l in logits]
    ps = [_bf(p / jnp.sum(p, axis=-1, keepdims=True)) for p in ps]
    pv = [_bf(_dot(p, _bf(mv_ref[s, :, hsl(hd)]))) for p, (hd, s) in zip(ps, cells)]
    outs = []
    for hd in range(MEM_HEADS):
        subs = pv[hd * nsub:(hd + 1) * nsub]
        outs.append(subs[0] if nsub == 1 else jnp.concatenate(subs, axis=0))
    att = jnp.concatenate(outs, axis=-1)
    y_ref[...] = x1 + _dot(att, wmo_ref[...])


def mem_attend(x, o, w_out, g, w_mq, mq_gain, mk, mv, w_mo, tm, seqlen):
    n, d = x.shape
    mlen, mw = mk.shape[1], mk.shape[2]
    tiles_per_seq = max(1, seqlen // tm)
    seqs_per_tile = max(1, tm // seqlen)
    return pl.pallas_call(
        _memattn_kernel,
        grid=(n // tm,),
        in_specs=[pl.BlockSpec((tm, d), lambda i: (i, 0)),
                  pl.BlockSpec((tm, o.shape[1]), lambda i: (i, 0)),
                  _full(w_out.shape), _full((1, d)), _full(w_mq.shape), _full((1, MEM_HEAD_DIM)),
                  pl.BlockSpec((seqs_per_tile, mlen, mw), lambda i: (i // tiles_per_seq, 0, 0)),
                  pl.BlockSpec((seqs_per_tile, mlen, mw), lambda i: (i // tiles_per_seq, 0, 0)),
                  _full(w_mo.shape)],
        out_specs=pl.BlockSpec((tm, d), lambda i: (i, 0)),
        out_shape=jax.ShapeDtypeStruct((n, d), F32),
        compiler_params=_params("parallel"),
        name="mem_attend",
    )(x, o, w_out, g.reshape(1, d), w_mq, mq_gain.reshape(1, MEM_HEAD_DIM), mk, mv, w_mo)


def _ffn_kernel(x_ref, g_ref, w1_ref, w3_ref, w2_ref, y_ref, *, hid_chunk):
    x = x_ref[...]
    h = _bf(_rms(x, g_ref[...]))
    hidden = w1_ref.shape[1]
    y_ref[...] = x
    for c in range(0, hidden, hid_chunk):
        a = _dot(h, w1_ref[:, c:c + hid_chunk])
        b = _dot(h, w3_ref[:, c:c + hid_chunk])
        y_ref[...] += _dot(_bf(_silu(a) * b), w2_ref[c:c + hid_chunk, :])


def ffn(x, g, w1, w3, w2, tm):
    n, d = x.shape
    hidden = w1.shape[1]
    return pl.pallas_call(
        functools.partial(_ffn_kernel, hid_chunk=256),
        grid=(n // tm,),
        in_specs=[pl.BlockSpec((tm, d), lambda i: (i, 0)), _full((1, d)),
                  _full((d, hidden)), _full((d, hidden)), _full((hidden, d))],
        out_specs=pl.BlockSpec((tm, d), lambda i: (i, 0)),
        out_shape=jax.ShapeDtypeStruct((n, d), F32),
        compiler_params=_params("parallel"),
        name="ffn",
    )(x, g.reshape(1, d), w1, w3, w2)


def _proj_dsa_kernel(x_ref, g_ref, w_ref, qg_ref, kg_ref, kig_ref, bd_ref,
                     q_o, k_o, kb_o, v_o, vb_o, qi_o, kiwi_o):
    h = _bf(_rms(x_ref[...], g_ref[...]))
    hq = A_HEADS * A_HEAD_DIM
    bd = bd_ref[...]
    inv_hd = 1.0 / A_HEAD_DIM

    def head_norm(p, gain):
        ss = _dot_exact_rhs01(p * p, bd)
        return p * lax.rsqrt(ss * inv_hd + EPS) * gain

    step = 512
    for c in range(0, hq, step):
        pq = _dot(h, w_ref[:, c:c + step])
        pk = _dot(h, w_ref[:, hq + c:hq + c + step])
        pv = _dot(h, w_ref[:, 2 * hq + c:2 * hq + c + step])
        for j in range(0, step, LANES):
            sl = slice(c + j, c + j + LANES)
            qn = head_norm(pq[:, j:j + LANES], qg_ref[:, sl])
            q_o[:, sl] = _bf(qn * (A_HEAD_DIM ** -0.5))
            kn = head_norm(pk[:, j:j + LANES], kg_ref[:, sl])
            k_o[:, sl] = kn
            kb_o[:, sl] = _bf(kn)
        v_o[:, c:c + step] = pv
        vb_o[:, c:c + step] = _bf(pv)
    qiw = IDX_HEADS * LANES
    for c in range(0, qiw, step):
        qi_o[:, c:c + step] = _bf(_dot(h, w_ref[:, 3 * hq + c:3 * hq + c + step]))
    p = _dot(h, w_ref[:, 3 * hq + qiw:3 * hq + qiw + LANES])
    lane = lax.broadcasted_iota(I32, p.shape, 1)
    is_ki = lane < IDX_DIM
    ss = jnp.sum(jnp.where(is_ki, p * p, 0.0), axis=-1, keepdims=True)
    kin = p * lax.rsqrt(ss * (1.0 / IDX_DIM) + EPS) * kig_ref[...]
    kiwi_o[...] = jnp.where(is_ki, kin, p)


def proj_dsa(x, g, wa, q_gain, k_gain, ki_gain, tm):
    n, d = x.shape
    hq = A_HEADS * A_HEAD_DIM
    m = wa.shape[1]
    qg = jnp.tile(q_gain, A_HEADS).reshape(1, hq)
    kg = jnp.tile(k_gain, A_HEADS).reshape(1, hq)
    kig = jnp.concatenate([ki_gain, jnp.ones((LANES - IDX_DIM,), F32)]).reshape(1, LANES)
    r = jnp.arange(LANES)
    bd = _bf((r[:, None] // A_HEAD_DIM) == (r[None, :] // A_HEAD_DIM))
    row = lambda w: pl.BlockSpec((tm, w), lambda i: (i, 0))
    return pl.pallas_call(
        _proj_dsa_kernel,
        grid=(n // tm,),
        in_specs=[row(d), _full((1, d)), _full((d, m)), _full((1, hq)), _full((1, hq)),
                  _full((1, LANES)), _full((LANES, LANES))],
        out_specs=[row(hq), row(hq), row(hq), row(hq), row(hq), row(IDX_HEADS * LANES), row(LANES)],
        out_shape=[jax.ShapeDtypeStruct((n, hq), BF16), jax.ShapeDtypeStruct((n, hq), F32),
                   jax.ShapeDtypeStruct((n, hq), BF16), jax.ShapeDtypeStruct((n, hq), F32),
                   jax.ShapeDtypeStruct((n, hq), BF16),
                   jax.ShapeDtypeStruct((n, IDX_HEADS * LANES), BF16),
                   jax.ShapeDtypeStruct((n, LANES), F32)],
        compiler_params=_params("parallel"),
        name="proj_dsa",
    )(x, g.reshape(1, d), wa, qg, kg, kig, bd)


def dsa_weight(w_in):
    hq = A_HEADS * A_HEAD_DIM
    o3 = 3 * hq
    o4 = o3 + IDX_HEADS * IDX_DIM
    d = w_in.shape[0]
    wqi = w_in[:, o3:o4].reshape(d, IDX_HEADS, IDX_DIM)
    wqi = jnp.pad(wqi, ((0, 0), (0, 0), (0, LANES - IDX_DIM))).reshape(d, IDX_HEADS * LANES)
    tail = jnp.pad(w_in[:, o4:], ((0, 0), (0, LANES - (w_in.shape[1] - o4))))
    return _bf(jnp.concatenate([w_in[:, :o3], wqi, tail], axis=1))


def _sortable(s):
    b = pltpu.bitcast(s, I32)
    b = jnp.where(b == jnp.int32(-2 ** 31), 0, b)
    return jnp.where(b < 0, b ^ jnp.int32(0x7FFFFFFF), b)


def _index_mask_kernel(qi_ref, wit_ref, *rest, nref, groups, ltot, topk, causal, case_rows):
    ki_refs = rest[:nref]
    o_ref = rest[nref]
    key_ref, sel_ref, jv_ref = rest[nref + 1:]
    tq = qi_ref.shape[0]
    i = pl.program_id(1)
    lpad = sel_ref.shape[0]
    wit = wit_ref[0]
    if case_rows:
        ncase = (jnp.maximum((i + 1) * tq, topk) + case_rows - 1) // case_rows
        used_rows = ncase * case_rows
    else:
        used_rows = None

    def score_group(dst0, rows, sources):
        accs = []
        for rp, lead, src0, width, _ in sources:
            ref = ki_refs[rp]
            ki = _bf(ref[src0:src0 + rows, :] if lead is None else ref[lead, src0:src0 + rows, :])
            acc = jnp.zeros((rows, tq), F32)
            for hd in range(IDX_HEADS):
                rel = _dot_nt(ki, qi_ref[:, hd * LANES:hd * LANES + width])
                acc = acc + wit[hd:hd + 1, :] * jnp.maximum(rel, 0.0)
            accs.append(acc)
        if len(accs) == 1:
            acc = accs[0]
        else:
            lane = lax.broadcasted_iota(I32, (rows, tq), 1)
            acc = jnp.where(lane < tq // 2, accs[0], accs[1])
        s = acc * ((IDX_DIM ** -0.5) * (IDX_HEADS ** -0.5))
        if causal:
            kpos = dst0 + lax.broadcasted_iota(I32, (rows, tq), 0)
            qpos = i * tq + lax.broadcasted_iota(I32, (rows, tq), 1)
            s = jnp.where((kpos >> CHUNK_SHIFT) <= (qpos >> CHUNK_SHIFT), s, NEG)
        key_ref[dst0:dst0 + rows, :] = _sortable(s)

    for dst0, rows, sources in groups:
        if used_rows is None:
            score_group(dst0, rows, sources)
        else:
            pl.when(dst0 < used_rows)(functools.partial(score_group, dst0, rows, sources))

    def select(nrows):
        idx_bits = max(1, (nrows - 1).bit_length())

        def count(pred):
            c = pred.astype(I32).reshape(COUNT_CHAINS, nrows // COUNT_CHAINS, tq)
            return jnp.sum(jnp.sum(c, axis=1), axis=0, keepdims=True)

        def ge_count(cand):
            return count(key_ref[0:nrows, :] >= cand)

        t0 = jnp.full((1, tq), -2 ** 31, I32)
        t = jnp.where(ge_count(jnp.zeros((1, tq), I32)) >= topk, 0, t0)

        def vbody(it, t):
            cand = t + (jnp.int32(1) << (30 - it))
            return jnp.where(ge_count(cand) >= topk, cand, t)

        t = lax.fori_loop(0, 31, vbody, t)
        keys = key_ref[0:nrows, :]
        gt = keys > t
        eq = keys == t
        need = topk - count(gt)
        rowi = lax.broadcasted_iota(I32, (nrows, tq), 0)

        def jbody(it, jv):
            cand = jv + (jnp.int32(1) << (idx_bits - 1 - it))
            below = count(eq & (rowi < cand))
            return jnp.where(below < need, cand, jv)

        jv_ref[...] = jnp.full((1, tq), nrows, I32)
        has_tie = jnp.max(count(eq) - need) > 0

        @pl.when(has_tie)
        def _():
            jv_ref[...] = lax.fori_loop(0, idx_bits, jbody, jnp.zeros((1, tq), I32))

        sel = gt | (eq & (rowi <= jv_ref[...]))
        if causal:
            qpos = i * tq + lax.broadcasted_iota(I32, (nrows, tq), 1)
            sel = sel & ((rowi >> CHUNK_SHIFT) <= (qpos >> CHUNK_SHIFT))
        sel_ref[0:nrows, :] = jnp.where(sel, 0.0, NEG)
        nreal = -(-nrows // TILE)
        if nreal * TILE > nrows:
            sel_ref[nrows:nreal * TILE, :] = jnp.full((nreal * TILE - nrows, tq), NEG, F32)
        for kt in range(lpad // TILE):
            if kt < nreal:
                o_ref[0, 0, kt] = sel_ref[kt * TILE:(kt + 1) * TILE, :].T
            else:
                o_ref[0, 0, kt] = jnp.full((tq, TILE), NEG, F32)

    if case_rows:
        for k in range(ltot // case_rows):
            pl.when(ncase == k + 1)(functools.partial(select, (k + 1) * case_rows))
    else:
        select(ltot)


def index_mask(qi, wit, ki_arrays, ki_specs, groups, ltot, nstep, nqb, causal):
    lpad = -(-ltot // KTILE) * KTILE
    nt = lpad // TILE
    topk = min(TOPK_MAX, ltot // 4)
    case_rows = MASK_CASE_ROWS if (causal and ltot % MASK_CASE_ROWS == 0 and ltot > MASK_CASE_ROWS) else 0
    kern = functools.partial(_index_mask_kernel, nref=len(ki_arrays), groups=tuple(groups), ltot=ltot,
                             topk=topk, causal=causal, case_rows=case_rows)
    return pl.pallas_call(
        kern,
        grid=(nstep, nqb),
        in_specs=[pl.BlockSpec((TILE, qi.shape[1]), lambda b, i: (b * nqb + i, 0)),
                  pl.BlockSpec((1, IDX_HEADS, TILE), lambda b, i: (b * nqb + i, 0, 0))] + list(ki_specs),
        out_specs=pl.BlockSpec((1, 1, nt, TILE, TILE), lambda b, i: (b, i, 0, 0, 0)),
        out_shape=jax.ShapeDtypeStruct((nstep, nqb, nt, TILE, TILE), F32),
        scratch_shapes=[pltpu.VMEM((ltot, TILE), I32), pltpu.VMEM((lpad, TILE), F32),
                        pltpu.VMEM((1, TILE), I32)],
        compiler_params=_params("parallel", "parallel"),
        name="index_mask",
    )(qi, wit, *ki_arrays)


def _row_groups(row0, rows, make_sources):
    out = []
    for off in range(0, rows, SCORE_ROWS):
        out.append((row0 + off, min(SCORE_ROWS, rows - off), tuple(make_sources(off))))
    return out


def _attn_core(q_ref, o_ref, qs_ref, l_ref, mx_ref, acc_ref, groups, tq):
    npairs = A_HEADS // 2
    lane = lax.broadcasted_iota(I32, (1, LANES), 1)
    keep_lo = _bf((lane < A_HEAD_DIM).astype(F32))
    keep_hi = _bf((lane >= A_HEAD_DIM).astype(F32))
    for pair in range(npairs):
        qp = q_ref[:, pair * LANES:(pair + 1) * LANES]
        qs_ref[pair, 0:tq, :] = qp * keep_lo
        qs_ref[pair, tq:2 * tq, :] = qp * keep_hi
    mx_ref[...] = jnp.full(mx_ref.shape, NEG, F32)
    acc_ref[...] = jnp.zeros(acc_ref.shape, F32)
    ones = jnp.ones((KTILE, LANES), BF16)

    def over_tiles(count, body):
        if isinstance(count, int) and count == 1:
            body(0, 0)
        else:
            lax.fori_loop(0, count, body, 0)

    for count, base, k_tile, _, mask_sub, bias_sub in groups:
        def p1(kt, carry, base=base, k_tile=k_tile, mask_sub=mask_sub, bias_sub=bias_sub):
            masks = [mask_sub(kt, sub) for sub in range(KTILE // LANES)]
            for pair in range(npairs):
                sl = slice(pair * LANES, (pair + 1) * LANES)
                l = _dot_nt(qs_ref[pair], k_tile(kt, sl))
                for half in range(2):
                    rs = slice(half * tq, (half + 1) * tq)
                    mx = mx_ref[pair, rs, :]
                    for sub in range(KTILE // LANES):
                        cs = slice(sub * LANES, (sub + 1) * LANES)
                        blk = l[rs, cs] + (masks[sub] + bias_sub(kt, sub, 2 * pair + half))
                        l_ref[pair, base + kt, rs, cs] = blk
                        mx = jnp.maximum(mx, blk)
                    mx_ref[pair, rs, :] = mx
            return carry

        over_tiles(count, p1)

    for pair in range(npairs):
        m = jnp.max(mx_ref[pair], axis=-1, keepdims=True)
        mx_ref[pair] = jnp.broadcast_to(m, mx_ref.shape[1:])

    for count, base, _, v_tile, _, _ in groups:
        def p2(kt, carry, base=base, v_tile=v_tile):
            for pair in range(npairs):
                sl = slice(pair * LANES, (pair + 1) * LANES)
                m = mx_ref[pair]
                p = jnp.concatenate(
                    [jnp.exp(l_ref[pair, base + kt, :, sub * LANES:(sub + 1) * LANES] - m)
                     for sub in range(KTILE // LANES)], axis=1)
                vaug = jnp.concatenate([v_tile(kt, sl), ones], axis=1)
                acc_ref[pair] += _dot(_bf(p), vaug)
            return carry

        over_tiles(count, p2)

    lane_full = lax.broadcasted_iota(I32, (tq, LANES), 1)
    for pair in range(npairs):
        a = acc_ref[pair]
        o = a[:, 0:LANES] / a[:, LANES:2 * LANES]
        o_ref[:, pair * LANES:(pair + 1) * LANES] = _bf(jnp.where(lane_full < A_HEAD_DIM, o[0:tq], o[tq:2 * tq]))


def _attn_scratch(tq, ntiles):
    npairs = A_HEADS // 2
    return [pltpu.VMEM((npairs, 2 * tq, LANES), BF16),
            pltpu.VMEM((npairs, ntiles, 2 * tq, KTILE), F32),
            pltpu.VMEM((npairs, 2 * tq, LANES), F32),
            pltpu.VMEM((npairs, 2 * tq, 2 * LANES), F32)]


def _attn_prompt_kernel(q_ref, k_ref, v_ref, mask_ref, bias_ref, o_ref, qs_ref, l_ref, mx_ref, acc_ref):
    i = pl.program_id(1)
    per = KTILE // TILE

    def rows(kt):
        return pl.ds(pl.multiple_of(kt * KTILE, KTILE), KTILE)

    def bias_sub(kt, sub, hd):
        s = kt * per + sub
        sel = jnp.where(s == i, 2, jnp.where(s == i - 1, 1, 0))
        return bias_ref[sel, hd]

    group = (i // per + 1, 0,
             lambda kt, sl: k_ref[rows(kt), sl],
             lambda kt, sl: v_ref[rows(kt), sl],
             lambda kt, sub: mask_ref[0, 0, kt * per + sub],
             bias_sub)
    _attn_core(q_ref, o_ref, qs_ref, l_ref, mx_ref, acc_ref, [group], TILE)


def attn_prompt(q, kb, vb, mask, bias, nseq, seqlen):
    n, hq = q.shape
    nqb = seqlen // TILE
    nt = mask.shape[2]
    return pl.pallas_call(
        _attn_prompt_kernel,
        grid=(nseq, nqb),
        in_specs=[pl.BlockSpec((TILE, hq), lambda b, i: (b * nqb + i, 0)),
                  pl.BlockSpec((seqlen, hq), lambda b, i: (b, 0)),
                  pl.BlockSpec((seqlen, hq), lambda b, i: (b, 0)),
                  pl.BlockSpec((1, 1, nt, TILE, TILE), lambda b, i: (b, i, 0, 0, 0)),
                  _full(bias.shape)],
        out_specs=pl.BlockSpec((TILE, hq), lambda b, i: (b * nqb + i, 0)),
        out_shape=jax.ShapeDtypeStruct((n, hq), BF16),
        scratch_shapes=_attn_scratch(TILE, seqlen // KTILE),
        compiler_params=_params("parallel", "arbitrary"),
        name="attn_prompt",
    )(q, kb, vb, mask, bias)


def _attn_sample_kernel(q_ref, kc_ref, vc_ref, kn_ref, vn_ref, mask_ref, bias_ref, o_ref,
                        qs_ref, l_ref, mx_ref, acc_ref, *, ncache):
    tq = q_ref.shape[0]
    per = KTILE // TILE
    last_sub = ncache * per - 1

    def rows(kt):
        return pl.ds(pl.multiple_of(kt * KTILE, KTILE), KTILE)

    def cache_bias(kt, sub, hd):
        return bias_ref[jnp.where(kt * per + sub == last_sub, 1, 0), hd]

    cache = (ncache, 0,
             lambda kt, sl: _bf(kc_ref[0, rows(kt), sl]),
             lambda kt, sl: _bf(vc_ref[0, rows(kt), sl]),
             lambda kt, sub: mask_ref[0, 0, kt * per + sub],
             cache_bias)
    fresh = (1, ncache,
             lambda kt, sl: kn_ref[0, :, sl],
             lambda kt, sl: vn_ref[0, :, sl],
             lambda kt, sub: mask_ref[0, 0, ncache * per + sub],
             lambda kt, sub, hd: bias_ref[2 if sub == 0 else 0, hd])
    _attn_core(q_ref, o_ref, qs_ref, l_ref, mx_ref, acc_ref, [cache, fresh], tq)


def attn_sample(q, kc, vc, kn, vn, mask, bias, nseq, tq):
    n, hq = q.shape
    past = kc.shape[1]
    ncache = past // KTILE
    nt = mask.shape[2]
    return pl.pallas_call(
        functools.partial(_attn_sample_kernel, ncache=ncache),
        grid=(nseq,),
        in_specs=[pl.BlockSpec((tq, hq), lambda s: (s, 0)),
                  pl.BlockSpec((1, past, hq), lambda s: (s, 0, 0)),
                  pl.BlockSpec((1, past, hq), lambda s: (s, 0, 0)),
                  pl.BlockSpec((1, KTILE, hq), lambda s: (s, 0, 0)),
                  pl.BlockSpec((1, KTILE, hq), lambda s: (s, 0, 0)),
                  pl.BlockSpec((1, 1, nt, tq, TILE), lambda s: (s // (TILE // tq), 0, 0, s % (TILE // tq), 0)),
                  _full(bias.shape)],
        out_specs=pl.BlockSpec((tq, hq), lambda s: (s, 0)),
        out_shape=jax.ShapeDtypeStruct((n, hq), BF16),
        scratch_shapes=_attn_scratch(tq, ncache + 1),
        compiler_params=_params("parallel"),
        name="attn_sample",
    )(q, kc, vc, kn, vn, mask, bias)


def _t5_bucket(rel):
    half = N_BUCKETS // 2
    max_exact = half // 2
    n = jnp.abs(rel)
    nf = jnp.maximum(n, 1).astype(F32)
    large = max_exact + (jnp.log(nf / max_exact) / math.log(MAX_DISTANCE / max_exact)
                         * (half - max_exact)).astype(I32)
    large = jnp.minimum(large, half - 1)
    return jnp.where(rel > 0, half, 0) + jnp.where(n < max_exact, n, large)


def bias_tiles(rel_bias):
    heads = rel_bias.shape[1]
    span = 2 * TILE - 1

    def toeplitz(shift):
        rel = jnp.arange(span, dtype=I32) - (TILE - 1) + shift
        tab = rel_bias.astype(F32)[_t5_bucket(rel)].T
        strip = jnp.tile(jnp.pad(tab, ((0, 0), (0, 1))), (1, TILE))[:, :TILE * span]
        return strip.reshape(heads, TILE, span)[:, :, TILE - 1:]

    far = rel_bias.astype(F32)[_t5_bucket(jnp.full((1,), -(TILE + 1), I32))]
    far = jnp.broadcast_to(far.T[:, :, None], (heads, TILE, TILE))
    return jnp.stack([far, toeplitz(-TILE), toeplitz(0)])


def dsa_core_prompt(pr, nseq, seqlen, bias):
    q, _, kb, _, vb, qi, kiwi = pr
    nqb = seqlen // TILE
    wit = kiwi[:, IDX_DIM:IDX_DIM + IDX_HEADS].reshape(nseq * nqb, TILE, IDX_HEADS).swapaxes(1, 2)
    groups = _row_groups(0, seqlen, lambda off: [(0, None, off, LANES, None)])
    mask = index_mask(qi, wit, [kiwi], [pl.BlockSpec((seqlen, LANES), lambda b, i: (b, 0))],
                      groups, seqlen, nseq, nqb, causal=True)
    return attn_prompt(q, kb, vb, mask, bias, nseq, seqlen)


def dsa_core_sample(pr, k_cache, v_cache, ki_cache, nseq, tq, bias):
    q, _, kb, _, vb, qi, kiwi = pr
    past = k_cache.shape[1]
    hq = q.shape[1]
    pad = KTILE - tq
    per = TILE // tq
    wit = kiwi[:, IDX_DIM:IDX_DIM + IDX_HEADS].reshape(nseq // per, TILE, IDX_HEADS).swapaxes(1, 2)
    groups = (_row_groups(0, past, lambda off: [(0, j, off, IDX_DIM, j) for j in range(per)])
              + _row_groups(past, tq, lambda off: [(1, None, j * tq + off, LANES, j) for j in range(per)]))
    mask = index_mask(
        qi, wit, [ki_cache, kiwi],
        [pl.BlockSpec((per, past, IDX_DIM), lambda b, i: (b, 0, 0)),
         pl.BlockSpec((TILE, LANES), lambda b, i: (b, 0))],
        groups, past + tq, nseq // per, 1, causal=False)
    kn = jnp.pad(kb.reshape(nseq, tq, hq), ((0, 0), (0, pad), (0, 0)))
    vn = jnp.pad(vb.reshape(nseq, tq, hq), ((0, 0), (0, pad), (0, 0)))
    return attn_sample(q, _bf(k_cache.reshape(nseq, past, hq)), _bf(v_cache.reshape(nseq, past, hq)),
                       kn, vn, mask, bias[:, :, :tq, :], nseq, tq)


def _log_sigmoid(x):
    return jnp.minimum(x, 0.0) - jnp.log1p(jnp.exp(-jnp.abs(x)))


def _mlstm_kernel(p_ref, gs_ref, gr_ref, gbc_ref, gbr_ref, hg_ref, c0_ref, n0_ref, m0_ref,
                  hs_ref, c_ref, n_ref, m_ref, cs, ns, ms):
    c_id = pl.program_id(1)
    nh = B_HEADS
    npair = nh // 2
    hw = LANES
    L = p_ref.shape[0]
    L2 = 2 * L

    @pl.when(c_id == 0)
    def _():
        cs[...] = jnp.zeros(cs.shape, F32)
        ns[...] = jnp.zeros(ns.shape, F32)
        ns[:, 0:B_QK_DIM] = n0_ref[0]
        for hd in range(nh):
            cs[hd // 2, (hd % 2) * hw:(hd % 2) * hw + B_QK_DIM, :] = c0_ref[0, hd]
        ms[...] = m0_ref[0]

    r = lax.broadcasted_iota(I32, (L2, L2), 0)
    c = lax.broadcasted_iota(I32, (L2, L2), 1)
    same = (r >> CHUNK_SHIFT) == (c >> CHUNK_SHIFT)
    incl = same & (c <= r)
    gs = gs_ref[0] + gbc_ref[...]
    gr = gr_ref[0] + gbr_ref[...]
    bcol = _dot_exact_lhs01(_bf01(incl), _log_sigmoid(gs[:, npair:]))
    brow = _dot_exact_rhs01(_log_sigmoid(gr[npair:, :]), _bf01(same & (r <= c)))
    icol = gs[:, :npair]
    irow = gr[:npair, :]
    rowc = lax.broadcasted_iota(I32, (L2, 1), 0)
    is_top = rowc < L
    top = is_top.astype(F32)
    bot = 1.0 - top
    row2 = lax.broadcasted_iota(I32, (2 * hw, 1), 0)
    hg = hg_ref[...]
    mall = ms[...]

    def stack(base, pr):
        a = base + 2 * pr * hw
        return jnp.concatenate([p_ref[:, a:a + hw], p_ref[:, a + hw:a + 2 * hw]], axis=0)

    def bd(x):
        return _bf(jnp.concatenate([x * top, x * bot], axis=1))

    pairs = range(npair)
    qf = [stack(0, pr) for pr in pairs]
    kf = [stack(nh * hw, pr) * (B_QK_DIM ** -0.5) for pr in pairs]
    vb = [_bf(stack(2 * nh * hw, pr)) for pr in pairs]
    bc = [bcol[:, pr:pr + 1] for pr in pairs]
    ic = [icol[:, pr:pr + 1] for pr in pairs]
    mc = [mall[:, pr:pr + 1] for pr in pairs]
    d = [jnp.where(incl, bc[pr] - brow[pr:pr + 1, :] + irow[pr:pr + 1, :], NEG) for pr in pairs]
    inter = [bc[pr] + mc[pr] for pr in pairs]
    mt = [jnp.maximum(inter[pr], jnp.max(d[pr], axis=-1, keepdims=True)) for pr in pairs]
    s = [_dot_nt(_bf(qf[pr]), _bf(kf[pr])) * jnp.exp(d[pr] - mt[pr]) for pr in pairs]
    wst = [jnp.exp(inter[pr] - mt[pr]) for pr in pairs]
    cmat = [cs[pr] for pr in pairs]
    num = [_dot(_bf(s[pr]), vb[pr]) + wst[pr] * _dot(bd(qf[pr]), _bf(cmat[pr])) for pr in pairs]
    for pr in pairs:
        n0 = ns[2 * pr:2 * pr + 1, :]
        n1 = ns[2 * pr + 1:2 * pr + 2, :]
        qn = jnp.sum(qf[pr] * jnp.where(is_top, n0, n1), axis=-1, keepdims=True)
        den = jnp.sum(s[pr], axis=-1, keepdims=True) + wst[pr] * qn
        hs = num[pr] / jnp.maximum(jnp.abs(den), jnp.exp(-mt[pr]))
        bl0 = bc[pr][L - 1:L, :]
        bl1 = bc[pr][L2 - 1:L2, :]
        bl = jnp.where(is_top, bl0, bl1)
        dec = bl - bc[pr] + ic[pr]
        blm = bl + mc[pr]
        mnew0 = jnp.maximum(blm[0:1, :], jnp.max(dec[0:L], axis=0, keepdims=True))
        mnew1 = jnp.maximum(blm[L:L + 1, :], jnp.max(dec[L:L2], axis=0, keepdims=True))
        mnew = jnp.where(is_top, mnew0, mnew1)
        wk = jnp.exp(dec - mnew)
        ws = jnp.exp(blm - mnew)
        kw = kf[pr] * wk
        ws2 = jnp.where(row2 < hw, ws[0:1, :], ws[L:L + 1, :])
        cs[pr] = ws2 * cmat[pr] + _dot_tn(bd(kw), vb[pr])
        ns[2 * pr:2 * pr + 1, :] = ws[0:1, :] * n0 + jnp.sum(kw[0:L], axis=0, keepdims=True)
        ns[2 * pr + 1:2 * pr + 2, :] = ws[L:L + 1, :] * n1 + jnp.sum(kw[L:L2], axis=0, keepdims=True)
        ms[:, pr:pr + 1] = mnew
        on = _bf(_rms(hs, hg) * _sigmoid(stack(3 * nh * hw, pr)))
        oa = 2 * pr * B_V_DIM
        hs_ref[:, oa:oa + B_V_DIM] = on[0:L]
        hs_ref[:, oa + B_V_DIM:oa + 2 * B_V_DIM] = on[L:L2]

    @pl.when(c_id == pl.num_programs(1) - 1)
    def _():
        for hd in range(nh):
            c_ref[0, hd] = cs[hd // 2, (hd % 2) * hw:(hd % 2) * hw + B_QK_DIM, :]
        n_ref[0] = ns[:, 0:B_QK_DIM]
        m_ref[0] = ms[...]


def mlstm_weight(w_in):
    d = w_in.shape[0]
    nq = B_HEADS * B_QK_DIM
    nv = B_HEADS * B_V_DIM

    def padh(w):
        w = w.reshape(d, B_HEADS, B_QK_DIM)
        return jnp.pad(w, ((0, 0), (0, 0), (0, LANES - B_QK_DIM))).reshape(d, B_HEADS * LANES)

    main = 2 * nq + 2 * nv
    tail = jnp.pad(w_in[:, main:], ((0, 0), (0, LANES - 2 * B_HEADS)))
    return _bf(jnp.concatenate([padh(w_in[:, :nq]), padh(w_in[:, nq:2 * nq]), w_in[:, 2 * nq:main], tail], axis=1))


def mlstm_core(p, gate_bias, h_gain, c0, n0, m0, nseq, nchunk):
    n, m = p.shape
    nh = B_HEADS
    npair = nh // 2
    gcol = 4 * nh * LANES
    pre = p[:, gcol:gcol + 2 * nh].reshape(nseq * nchunk, CHUNK, 2, npair, 2)
    gs = pre.transpose(0, 4, 1, 2, 3).reshape(nseq * nchunk, 2 * CHUNK, 2 * npair)
    gr = pre.transpose(0, 2, 3, 4, 1).reshape(nseq * nchunk, 2 * npair, 2 * CHUNK)
    gb = gate_bias.reshape(2, npair, 2)
    gbc = jnp.repeat(gb.transpose(2, 0, 1).reshape(2, 2 * npair), CHUNK, axis=0)
    m0s = jnp.repeat(m0.reshape(nseq, npair, 2).transpose(0, 2, 1), CHUNK, axis=1)
    o, c_new, n_new, m_new = pl.pallas_call(
        _mlstm_kernel,
        grid=(nseq, nchunk),
        in_specs=[pl.BlockSpec((CHUNK, m), lambda s, c: (s * nchunk + c, 0)),
                  pl.BlockSpec((1, 2 * CHUNK, 2 * npair), lambda s, c: (s * nchunk + c, 0, 0)),
                  pl.BlockSpec((1, 2 * npair, 2 * CHUNK), lambda s, c: (s * nchunk + c, 0, 0)),
                  _full((2 * CHUNK, 2 * npair)), _full((2 * npair, 2 * CHUNK)), _full((1, B_V_DIM)),
                  pl.BlockSpec((1, nh, B_QK_DIM, B_V_DIM), lambda s, c: (s, 0, 0, 0)),
                  pl.BlockSpec((1, nh, B_QK_DIM), lambda s, c: (s, 0, 0)),
                  pl.BlockSpec((1, 2 * CHUNK, npair), lambda s, c: (s, 0, 0))],
        out_specs=[pl.BlockSpec((CHUNK, nh * B_V_DIM), lambda s, c: (s * nchunk + c, 0)),
                   pl.BlockSpec((1, nh, B_QK_DIM, B_V_DIM), lambda s, c: (s, 0, 0, 0)),
                   pl.BlockSpec((1, nh, B_QK_DIM), lambda s, c: (s, 0, 0)),
                   pl.BlockSpec((1, 2 * CHUNK, npair), lambda s, c: (s, 0, 0))],
        out_shape=[jax.ShapeDtypeStruct((n, nh * B_V_DIM), BF16),
                   jax.ShapeDtypeStruct((nseq, nh, B_QK_DIM, B_V_DIM), F32),
                   jax.ShapeDtypeStruct((nseq, nh, B_QK_DIM), F32),
                   jax.ShapeDtypeStruct((nseq, 2 * CHUNK, npair), F32)],
        scratch_shapes=[pltpu.VMEM((npair, 2 * LANES, B_V_DIM), F32), pltpu.VMEM((nh, LANES), F32),
                        pltpu.VMEM((2 * CHUNK, npair), F32)],
        compiler_params=_params("parallel", "arbitrary"),
        name="mlstm",
    )(p, gs, gr, gbc, gbc.T, h_gain.reshape(1, B_V_DIM), c0, n0, m0s)
    m_heads = m_new[:, ::CHUNK, :].transpose(0, 2, 1).reshape(nseq, nh)
    return o, c_new, n_new, m_heads


def _split2(x):
    hi = _bf(x)
    return hi, _bf(x - hi.astype(F32))


def _cat3_lhs(x):
    hi, mid = _split2(x)
    return jnp.concatenate([hi, hi, mid], axis=1)


def _cat3_rhs(x):
    hi, mid = _split2(x)
    return jnp.concatenate([hi, mid, hi], axis=0)


def _gdn_kernel(p_ref, gs_ref, gr_ref, cw_ref, alc_ref, alr_ref, dtc_ref, dtr_ref, og_ref, s0_ref, cb0_ref,
                o_ref, s_ref, cb_ref, ss, tail):
    c_id = pl.program_id(1)
    nh = C_HEADS
    cdim = nh * (2 * C_DK + C_DV)
    L = p_ref.shape[0]
    L2 = 2 * L
    nprev = CONV_W - 1

    @pl.when(c_id == 0)
    def _():
        ss[...] = s0_ref[0]
        tail[...] = jnp.zeros(tail.shape, F32)
        tail[8 - nprev:8, :] = cb0_ref[0]

    x = p_ref[:, 0:cdim]
    ext = jnp.concatenate([tail[...], x], axis=0)
    conv = ext[8:8 + L] * cw_ref[CONV_W - 1:CONV_W, :]
    for j in range(CONV_W - 1):
        conv = conv + ext[8 - nprev + j:8 - nprev + j + L] * cw_ref[j:j + 1, :]
    tail[...] = x[L - 8:L, :]
    cf = _silu(conv)

    npair = nh // 2
    gs = gs_ref[0]
    gr = gr_ref[0]
    beta = _sigmoid(gs[:, 0:npair])
    g_col = -jnp.exp(alc_ref[...]) * _softplus(gs[:, npair:] + dtc_ref[...])
    g_row = -jnp.exp(alr_ref[...]) * _softplus(gr[npair:, :] + dtr_ref[...])
    r = lax.broadcasted_iota(I32, (L2, L2), 0)
    c = lax.broadcasted_iota(I32, (L2, L2), 1)
    same = (r >> CHUNK_SHIFT) == (c >> CHUNK_SHIFT)
    incl = same & (c <= r)
    strict = same & (c < r)
    eye = (c == r).astype(F32)
    gc_col = _dot_exact_lhs01(_bf01(incl), g_col)
    gc_row = _dot_exact_rhs01(g_row, _bf01(same & (r <= c)))
    rowc = lax.broadcasted_iota(I32, (L2, 1), 0)
    top = (rowc < L).astype(F32)
    bot = 1.0 - top
    row2 = lax.broadcasted_iota(I32, (2 * C_DK, 1), 0)
    og = og_ref[...]

    def stack(base, pr):
        a = base + 2 * pr * C_DK
        return jnp.concatenate([cf[:, a:a + C_DK], cf[:, a + C_DK:a + 2 * C_DK]], axis=0)

    def bd(x):
        return _bf(jnp.concatenate([x * top, x * bot], axis=1))

    pairs = range(npair)
    qc, kc, kcb, dm, amat, rhs, gcols, egcs = [], [], [], [], [], [], [], []
    for pr in pairs:
        qraw = stack(0, pr)
        kraw = stack(nh * C_DK, pr)
        vc = stack(2 * nh * C_DK, pr)
        qc.append(qraw * lax.rsqrt(jnp.sum(qraw * qraw, axis=-1, keepdims=True) + EPS) * (C_DK ** -0.5))
        kc.append(kraw * lax.rsqrt(jnp.sum(kraw * kraw, axis=-1, keepdims=True) + EPS))
        bc = beta[:, pr:pr + 1]
        gcol = gc_col[:, pr:pr + 1]
        grow = gc_row[pr:pr + 1, :]
        dm.append(jnp.where(incl, jnp.exp(jnp.where(incl, gcol - grow, 0.0)), 0.0))
        kb = kc[pr] * bc
        kcb.append(_bf(kc[pr]))
        amat.append(jnp.where(strict, _dot_nt(_bf(kb), kcb[pr]) * dm[pr], 0.0))
        egc = jnp.exp(gcol)
        rhs.append(jnp.concatenate([vc * bc, kb * egc], axis=-1))
        gcols.append(gcol)
        egcs.append(egc)
    tinv = [eye - amat[pr] for pr in pairs]
    pw_l = [_cat3_lhs(-amat[pr]) for pr in pairs]
    pw_r = [_cat3_rhs(-amat[pr]) for pr in pairs]
    for _ in range(CHUNK_SHIFT - 1):
        pw = [_dot(pw_l[pr], pw_r[pr]) for pr in pairs]
        pw_l = [_cat3_lhs(pw[pr]) for pr in pairs]
        pw_r = [_cat3_rhs(pw[pr]) for pr in pairs]
        tinv = [tinv[pr] + _dot(_cat3_lhs(tinv[pr]), pw_r[pr]) for pr in pairs]
    sol = [_dot(_cat3_lhs(tinv[pr]), _cat3_rhs(rhs[pr])) for pr in pairs]
    attn = [_dot_nt(_bf(qc[pr]), kcb[pr]) * dm[pr] for pr in pairs]
    smat = [ss[pr] for pr in pairs]
    sb = [_bf(smat[pr]) for pr in pairs]
    vnew = [sol[pr][:, :C_DV] - _dot(bd(sol[pr][:, C_DV:]), sb[pr]) for pr in pairs]
    o = [_dot(bd(qc[pr] * egcs[pr]), sb[pr]) + _dot(_bf(attn[pr]), _bf(vnew[pr])) for pr in pairs]
    for pr in pairs:
        gl0 = gcols[pr][L - 1:L, :]
        gl1 = gcols[pr][L2 - 1:L2, :]
        ke = kc[pr] * jnp.exp(jnp.where(rowc < L, gl0, gl1) - gcols[pr])
        decay = jnp.exp(jnp.where(row2 < C_DK, gl0, gl1))
        ss[pr] = smat[pr] * decay + _dot_tn(bd(ke), _bf(vnew[pr]))
    for pr in pairs:
        za = cdim + 2 * pr * C_DV
        z = jnp.concatenate([p_ref[:, za:za + C_DV], p_ref[:, za + C_DV:za + 2 * C_DV]], axis=0)
        on = _bf(_rms(o[pr], og) * _silu(z))
        oa = 2 * pr * C_DV
        o_ref[:, oa:oa + C_DV] = on[0:L]
        o_ref[:, oa + C_DV:oa + 2 * C_DV] = on[L:L2]

    @pl.when(c_id == pl.num_programs(1) - 1)
    def _():
        s_ref[0] = ss[...]
        cb_ref[0] = tail[8 - nprev:8, :]


def gdn_weight(w_in):
    main = C_HEADS * (2 * C_DK + C_DV) + C_HEADS * C_DV
    tail = jnp.pad(w_in[:, main:], ((0, 0), (0, LANES - 2 * C_HEADS)))
    return _bf(jnp.concatenate([w_in[:, :main], tail], axis=1))


def gdn_core(p, conv_w, a_log, dt_bias, o_gain, s0, cb0, nseq, nchunk):
    n, m = p.shape
    nh = C_HEADS
    npair = nh // 2
    cdim = nh * (2 * C_DK + C_DV)
    gcolumn = cdim + nh * C_DV
    pre = p[:, gcolumn:gcolumn + 2 * nh].reshape(nseq * nchunk, CHUNK, 2, npair, 2)
    gs = pre.transpose(0, 4, 1, 2, 3).reshape(nseq * nchunk, 2 * CHUNK, 2 * npair)
    gr = pre.transpose(0, 2, 3, 4, 1).reshape(nseq * nchunk, 2 * npair, 2 * CHUNK)

    def col(v):
        return jnp.repeat(v.reshape(npair, 2).T, CHUNK, axis=0)

    o, s_new, cb_new = pl.pallas_call(
        _gdn_kernel,
        grid=(nseq, nchunk),
        in_specs=[pl.BlockSpec((CHUNK, m), lambda s, c: (s * nchunk + c, 0)),
                  pl.BlockSpec((1, 2 * CHUNK, 2 * npair), lambda s, c: (s * nchunk + c, 0, 0)),
                  pl.BlockSpec((1, 2 * npair, 2 * CHUNK), lambda s, c: (s * nchunk + c, 0, 0)),
                  _full((CONV_W, cdim)),
                  _full((2 * CHUNK, npair)), _full((npair, 2 * CHUNK)),
                  _full((2 * CHUNK, npair)), _full((npair, 2 * CHUNK)),
                  _full((1, C_DV)),
                  pl.BlockSpec((1, npair, 2 * C_DK, C_DV), lambda s, c: (s, 0, 0, 0)),
                  pl.BlockSpec((1, CONV_W - 1, cdim), lambda s, c: (s, 0, 0))],
        out_specs=[pl.BlockSpec((CHUNK, nh * C_DV), lambda s, c: (s * nchunk + c, 0)),
                   pl.BlockSpec((1, npair, 2 * C_DK, C_DV), lambda s, c: (s, 0, 0, 0)),
                   pl.BlockSpec((1, CONV_W - 1, cdim), lambda s, c: (s, 0, 0))],
        out_shape=[jax.ShapeDtypeStruct((n, nh * C_DV), BF16),
                   jax.ShapeDtypeStruct((nseq, npair, 2 * C_DK, C_DV), F32),
                   jax.ShapeDtypeStruct((nseq, CONV_W - 1, cdim), F32)],
        scratch_shapes=[pltpu.VMEM((npair, 2 * C_DK, C_DV), F32), pltpu.VMEM((8, cdim), F32)],
        compiler_params=_params("parallel", "arbitrary"),
        name="gdn",
    )(p, gs, gr, conv_w, col(a_log), col(a_log).T, col(dt_bias), col(dt_bias).T, o_gain.reshape(1, C_DV),
      s0.reshape(nseq, npair, 2 * C_DK, C_DV), cb0)
    return o, s_new.reshape(nseq, nh, C_DK, C_DV), cb_new


def _trunk(x, nseq, seqlen, mem_k, mem_v, st, W, bias, is_prompt):
    d = x.shape[-1]
    n = nseq * seqlen
    x = x.reshape(n, d)
    tm = min(256, n)
    tm_mem = min(MEM_ROWS, n)
    new = {}
    for i in range(4):
        kind = i % 3
        mx = W["mixer"][i]
        if kind == 0:
            pr = proj_dsa(x, W["norm_mix"][i], mx["w_in"], mx["q_gain"], mx["k_gain"], mx["ki_gain"], tm)
            if is_prompt:
                o = dsa_core_prompt(pr, nseq, seqlen, bias)
            else:
                o = dsa_core_sample(pr, *st[i], nseq, seqlen, bias)
            new[i] = (pr[1].reshape(nseq, seqlen, A_HEADS, A_HEAD_DIM),
                      pr[3].reshape(nseq, seqlen, A_HEADS, A_HEAD_DIM),
                      pr[6][:, :IDX_DIM].reshape(nseq, seqlen, IDX_DIM))
        elif kind == 1:
            p = proj(x, W["norm_mix"][i], mx["w_in"], tm)
            o, c_new, n_new, m_new = mlstm_core(p, mx["gate_bias"], mx["h_gain"], *st[i], nseq, seqlen // CHUNK)
            new[i] = (c_new, n_new, m_new.reshape(nseq, B_HEADS))
        else:
            p = proj(x, W["norm_mix"][i], mx["w_in"], tm)
            o, s_new, cb_new = gdn_core(p, mx["conv_w"], mx["a_log"], mx["dt_bias"], mx["o_gain"], *st[i],
                                        nseq, seqlen // CHUNK)
            new[i] = (s_new, cb_new)
        x = mem_attend(x, o, mx["w_out"], W["norm_mem"][i], W["w_mq"][i], W["mq_gain"][i],
                       mem_k[i], mem_v[i], W["w_mo"][i], tm_mem, seqlen)
        x = ffn(x, W["norm_ffn"][i], W["w_ffn1"][i], W["w_ffn3"][i], W["w_ffn2"][i], min(FFN_ROWS, n))
    return x.reshape(nseq, seqlen, d), new


def kernel(x_prompt, x_sample, mem_prompt, cache_l0_k, cache_l0_v, cache_l0_kidx, state_l1_C, state_l1_n, state_l1_m, state_l2_S, state_l2_conv, cache_l3_k, cache_l3_v, cache_l3_kidx, cache_mem_k, cache_mem_v, rel_bias, norm_mix, norm_mem, norm_ffn, mem_norm, w_mq, w_mk, w_mv, w_mo, mq_gain, mk_gain, w_ffn1, w_ffn3, w_ffn2, a0_w_in, a0_w_out, a0_q_gain, a0_k_gain, a0_kidx_gain, b1_w_in, b1_gate_bias, b1_h_gain, b1_w_out, c2_w_in, c2_conv_w, c2_a_log, c2_dt_bias, c2_o_gain, c2_w_out, a3_w_in, a3_w_out, a3_q_gain, a3_k_gain, a3_kidx_gain):
    B, T, D = x_prompt.shape
    S, Ts, _ = x_sample.shape
    depth = w_mq.shape[0]
    mlen = mem_prompt.shape[1]
    mw = MEM_HEADS * MEM_HEAD_DIM

    def dsa_w(w_in, w_out, qg, kg, kig):
        return dict(w_in=dsa_weight(w_in), w_out=_bf(w_out), q_gain=qg, k_gain=kg, ki_gain=kig)

    W = dict(
        norm_mix=norm_mix, norm_mem=norm_mem, norm_ffn=norm_ffn,
        w_mq=_bf(w_mq), w_mo=_bf(w_mo), mq_gain=mq_gain,
        w_ffn1=_bf(w_ffn1), w_ffn3=_bf(w_ffn3), w_ffn2=_bf(w_ffn2),
        mixer={
            0: dsa_w(a0_w_in, a0_w_out, a0_q_gain, a0_k_gain, a0_kidx_gain),
            1: dict(w_in=mlstm_weight(b1_w_in), gate_bias=b1_gate_bias, h_gain=b1_h_gain, w_out=_bf(b1_w_out)),
            2: dict(w_in=gdn_weight(c2_w_in), conv_w=c2_conv_w, a_log=c2_a_log, dt_bias=c2_dt_bias,
                    o_gain=c2_o_gain, w_out=_bf(c2_w_out)),
            3: dsa_w(a3_w_in, a3_w_out, a3_q_gain, a3_k_gain, a3_kidx_gain),
        },
    )
    bias = bias_tiles(rel_bias)

    mk_p, mv_p = mem_kv(mem_prompt.reshape(B * mlen, D), mem_norm, w_mk, w_mv, mk_gain)
    mk_p = mk_p.reshape(depth, B, mlen, mw)
    mv_p = mv_p.reshape(depth, B, mlen, mw)
    st_p = {
        0: None,
        1: (jnp.zeros((B, B_HEADS, B_QK_DIM, B_V_DIM), F32), jnp.zeros((B, B_HEADS, B_QK_DIM), F32),
            jnp.full((B, B_HEADS), NEG, F32)),
        2: (jnp.zeros((B, C_HEADS, C_DK, C_DV), F32), jnp.zeros((B, CONV_W - 1, state_l2_conv.shape[-1]), F32)),
        3: None,
    }
    y_p, np_ = _trunk(x_prompt, B, T, mk_p, mv_p, st_p, W, bias, True)

    st_s = {
        0: (cache_l0_k, cache_l0_v, cache_l0_kidx),
        1: (state_l1_C, state_l1_n, state_l1_m),
        2: (state_l2_S, state_l2_conv),
        3: (cache_l3_k, cache_l3_v, cache_l3_kidx),
    }
    mk_s = cache_mem_k.reshape(depth, S, mlen, mw)
    mv_s = cache_mem_v.reshape(depth, S, mlen, mw)
    y_s, ns_ = _trunk(x_sample, S, Ts, mk_s, mv_s, st_s, W, bias, False)

    shp = (depth, B, mlen, MEM_HEADS, MEM_HEAD_DIM)
    return (y_p, y_s,
            *np_[0], *np_[1], *np_[2], *np_[3], mk_p.reshape(shp), mv_p.reshape(shp),
            *ns_[0], *ns_[1], *ns_[2], *ns_[3])
```

```python
import functools
import math

import jax
import jax.numpy as jnp
from jax import lax
from jax.experimental import pallas as pl
from jax.experimental.pallas import tpu as pltpu

F32 = jnp.float32
BF16 = jnp.bfloat16
I32 = jnp.int32

EPS = 1e-6
NEG = -1e30
CHUNK = 64
CHUNK_SHIFT = 6
LANES = 128
TILE = 128
KTILE = 256
COUNT_CHAINS = 8
SCORE_ROWS = 256
FFN_ROWS = 512
MEM_ROWS = 256
CACHE_BLOCK = 512
MASK_CASE_ROWS = 512
VMEM_LIMIT = 56 * 1024 * 1024

A_HEADS, A_HEAD_DIM = 16, 64
IDX_HEADS, IDX_DIM = 8, 64
TOPK_MAX = 256
N_BUCKETS, MAX_DISTANCE = 32, 128
B_HEADS, B_QK_DIM, B_V_DIM = 8, 64, 128
C_HEADS, C_DK, C_DV = 8, 128, 128
CONV_W = 4
MEM_HEADS, MEM_HEAD_DIM = 4, 128


def _bf(x):
    return x.astype(BF16)


def _bf01(mask):
    return mask.astype(F32).astype(BF16)


def _dot(a, b):
    return jnp.dot(a, b, preferred_element_type=F32)


def _dot_nt(a, b):
    return lax.dot_general(a, b, (((1,), (1,)), ((), ())), preferred_element_type=F32)


def _dot_tn(a, b):
    return lax.dot_general(a, b, (((0,), (0,)), ((), ())), preferred_element_type=F32)


def _split3(x):
    hi = _bf(x)
    r1 = x - hi.astype(F32)
    mid = _bf(r1)
    lo = _bf(r1 - mid.astype(F32))
    return hi, mid, lo


def _dot_exact_rhs01(x, m01):
    hi, mid, lo = _split3(x)
    return _dot(hi, m01) + _dot(mid, m01) + _dot(lo, m01)


def _dot_exact_lhs01(m01, x):
    hi, mid, lo = _split3(x)
    return _dot(m01, hi) + _dot(m01, mid) + _dot(m01, lo)


def _dot_f32(a, b):
    ah, am, al = _split3(a)
    bh, bm, bl = _split3(b)
    return (_dot(ah, bh) + (_dot(ah, bm) + _dot(am, bh))
            + (_dot(am, bm) + _dot(ah, bl) + _dot(al, bh)))


def _rms(x, g):
    ms = jnp.mean(x * x, axis=-1, keepdims=True)
    return x * lax.rsqrt(ms + EPS) * g


def _sigmoid(x):
    return 1.0 / (1.0 + jnp.exp(-x))


def _silu(x):
    return x * _sigmoid(x)


def _softplus(x):
    return jnp.maximum(x, 0.0) + jnp.log1p(jnp.exp(-jnp.abs(x)))


def _params(*sem):
    return pltpu.CompilerParams(dimension_semantics=sem, vmem_limit_bytes=VMEM_LIMIT)


def _full(shape):
    n = len(shape)
    return pl.BlockSpec(shape, lambda *_: (0,) * n)


def _proj_kernel(x_ref, g_ref, w_ref, o_ref, *, col_chunk):
    h = _bf(_rms(x_ref[...], g_ref[...]))
    m = w_ref.shape[1]
    for c in range(0, m, col_chunk):
        e = min(c + col_chunk, m)
        o_ref[:, c:e] = _dot(h, w_ref[:, c:e])


def proj(x, g, w, tm):
    n, d = x.shape
    m = w.shape[1]
    return pl.pallas_call(
        functools.partial(_proj_kernel, col_chunk=512),
        grid=(n // tm,),
        in_specs=[pl.BlockSpec((tm, d), lambda i: (i, 0)), _full((1, d)), _full((d, m))],
        out_specs=pl.BlockSpec((tm, m), lambda i: (i, 0)),
        out_shape=jax.ShapeDtypeStruct((n, m), F32),
        compiler_params=_params("parallel"),
        name="proj",
    )(x, g.reshape(1, d), w)


def _memkv_kernel(x_ref, g_ref, wk_ref, wv_ref, kg_ref, k_ref, v_ref):
    h = _bf(_rms(x_ref[...], g_ref[0]))
    k = _dot(h, wk_ref[0])
    v_ref[0] = _dot(h, wv_ref[0])
    kg = kg_ref[0]
    for hd in range(MEM_HEADS):
        sl = slice(hd * MEM_HEAD_DIM, (hd + 1) * MEM_HEAD_DIM)
        k_ref[0, :, sl] = _rms(k[:, sl], kg)


def mem_kv(mem2d, mem_norm, w_mk, w_mv, mk_gain, tm=256):
    n, d = mem2d.shape
    depth = w_mk.shape[0]
    mw = w_mk.shape[2]
    return pl.pallas_call(
        _memkv_kernel,
        grid=(depth, n // tm),
        in_specs=[pl.BlockSpec((tm, d), lambda l, i: (i, 0)),
                  pl.BlockSpec((1, 1, d), lambda l, i: (l, 0, 0)),
                  pl.BlockSpec((1, d, mw), lambda l, i: (l, 0, 0)),
                  pl.BlockSpec((1, d, mw), lambda l, i: (l, 0, 0)),
                  pl.BlockSpec((1, 1, MEM_HEAD_DIM), lambda l, i: (l, 0, 0))],
        out_specs=[pl.BlockSpec((1, tm, mw), lambda l, i: (l, i, 0)),
                   pl.BlockSpec((1, tm, mw), lambda l, i: (l, i, 0))],
        out_shape=[jax.ShapeDtypeStruct((depth, n, mw), F32)] * 2,
        compiler_params=_params("parallel", "parallel"),
        name="mem_kv",
    )(mem2d, mem_norm.reshape(depth, 1, d), _bf(w_mk), _bf(w_mv), mk_gain.reshape(depth, 1, MEM_HEAD_DIM))


def _memattn_kernel(x_ref, o_ref, wo_ref, g_ref, wq_ref, qg_ref, mk_ref, mv_ref, wmo_ref, y_ref):
    x1 = x_ref[...] + _dot(o_ref[...], wo_ref[...])
    h = _bf(_rms(x1, g_ref[...]))
    q = _dot(h, wq_ref[...])
    qg = qg_ref[...]
    scale = MEM_HEAD_DIM ** -0.5
    nsub = mk_ref.shape[0]
    rows = x1.shape[0] // nsub
    cells = [(hd, s) for hd in range(MEM_HEADS) for s in range(nsub)]
    hsl = lambda hd: slice(hd * MEM_HEAD_DIM, (hd + 1) * MEM_HEAD_DIM)
    qh = [_bf(_rms(q[:, hsl(hd)], qg)) for hd in range(MEM_HEADS)]
    logits = [_dot_nt(qh[hd][s * rows:(s + 1) * rows], _bf(mk_ref[s, :, hsl(hd)])) * scale for hd, s in cells]
    ps = [jnp.exp(l - jnp.max(l, axis=-1, keepdims=True)) for l in logits]
    ps = [_bf(p / jnp.sum(p, axis=-1, keepdims=True)) for p in ps]
    pv = [_bf(_dot(p, _bf(mv_ref[s, :, hsl(hd)]))) for p, (hd, s) in zip(ps, cells)]
    outs = []
    for hd in range(MEM_HEADS):
        subs = pv[hd * nsub:(hd + 1) * nsub]
        outs.append(subs[0] if nsub == 1 else jnp.concatenate(subs, axis=0))
    att = jnp.concatenate(outs, axis=-1)
    y_ref[...] = x1 + _dot(att, wmo_ref[...])


def mem_attend(x, o, w_out, g, w_mq, mq_gain, mk, mv, w_mo, tm, seqlen):
    n, d = x.shape
    mlen, mw = mk.shape[1], mk.shape[2]
    tiles_per_seq = max(1, seqlen // tm)
    seqs_per_tile = max(1, tm // seqlen)
    return pl.pallas_call(
        _memattn_kernel,
        grid=(n // tm,),
        in_specs=[pl.BlockSpec((tm, d), lambda i: (i, 0)),
                  pl.BlockSpec((tm, o.shape[1]), lambda i: (i, 0)),
                  _full(w_out.shape), _full((1, d)), _full(w_mq.shape), _full((1, MEM_HEAD_DIM)),
                  pl.BlockSpec((seqs_per_tile, mlen, mw), lambda i: (i // tiles_per_seq, 0, 0)),
                  pl.BlockSpec((seqs_per_tile, mlen, mw), lambda i: (i // tiles_per_seq, 0, 0)),
                  _full(w_mo.shape)],
        out_specs=pl.BlockSpec((tm, d), lambda i: (i, 0)),
        out_shape=jax.ShapeDtypeStruct((n, d), F32),
        compiler_params=_params("parallel"),
        name="mem_attend",
    )(x, o, w_out, g.reshape(1, d), w_mq, mq_gain.reshape(1, MEM_HEAD_DIM), mk, mv, w_mo)


def _ffn_kernel(x_ref, g_ref, w1_ref, w3_ref, w2_ref, y_ref, *, hid_chunk):
    x = x_ref[...]
    h = _bf(_rms(x, g_ref[...]))
    hidden = w1_ref.shape[1]
    y_ref[...] = x
    for c in range(0, hidden, hid_chunk):
        a = _dot(h, w1_ref[:, c:c + hid_chunk])
        b = _dot(h, w3_ref[:, c:c + hid_chunk])
        y_ref[...] += _dot(_bf(_silu(a) * b), w2_ref[c:c + hid_chunk, :])


def ffn(x, g, w1, w3, w2, tm):
    n, d = x.shape
    hidden = w1.shape[1]
    return pl.pallas_call(
        functools.partial(_ffn_kernel, hid_chunk=256),
        grid=(n // tm,),
        in_specs=[pl.BlockSpec((tm, d), lambda i: (i, 0)), _full((1, d)),
                  _full((d, hidden)), _full((d, hidden)), _full((hidden, d))],
        out_specs=pl.BlockSpec((tm, d), lambda i: (i, 0)),
        out_shape=jax.ShapeDtypeStruct((n, d), F32),
        compiler_params=_params("parallel"),
        name="ffn",
    )(x, g.reshape(1, d), w1, w3, w2)


def _proj_dsa_kernel(x_ref, g_ref, w_ref, qg_ref, kg_ref, kig_ref, bd_ref,
                     q_o, k_o, kb_o, v_o, vb_o, qi_o, kiwi_o):
    h = _bf(_rms(x_ref[...], g_ref[...]))
    hq = A_HEADS * A_HEAD_DIM
    bd = bd_ref[...]
    inv_hd = 1.0 / A_HEAD_DIM

    def head_norm(p, gain):
        ss = _dot_exact_rhs01(p * p, bd)
        return p * lax.rsqrt(ss * inv_hd + EPS) * gain

    step = 512
    for c in range(0, hq, step):
        pq = _dot(h, w_ref[:, c:c + step])
        pk = _dot(h, w_ref[:, hq + c:hq + c + step])
        pv = _dot(h, w_ref[:, 2 * hq + c:2 * hq + c + step])
        for j in range(0, step, LANES):
            sl = slice(c + j, c + j + LANES)
            qn = head_norm(pq[:, j:j + LANES], qg_ref[:, sl])
            q_o[:, sl] = _bf(qn * (A_HEAD_DIM ** -0.5))
            kn = head_norm(pk[:, j:j + LANES], kg_ref[:, sl])
            k_o[:, sl] = kn
            kb_o[:, sl] = _bf(kn)
        v_o[:, c:c + step] = pv
        vb_o[:, c:c + step] = _bf(pv)
    qiw = IDX_HEADS * LANES
    for c in range(0, qiw, step):
        qi_o[:, c:c + step] = _bf(_dot(h, w_ref[:, 3 * hq + c:3 * hq + c + step]))
    p = _dot(h, w_ref[:, 3 * hq + qiw:3 * hq + qiw + LANES])
    lane = lax.broadcasted_iota(I32, p.shape, 1)
    is_ki = lane < IDX_DIM
    ss = jnp.sum(jnp.where(is_ki, p * p, 0.0), axis=-1, keepdims=True)
    kin = p * lax.rsqrt(ss * (1.0 / IDX_DIM) + EPS) * kig_ref[...]
    kiwi_o[...] = jnp.where(is_ki, kin, p)


def proj_dsa(x, g, wa, q_gain, k_gain, ki_gain, tm):
    n, d = x.shape
    hq = A_HEADS * A_HEAD_DIM
    m = wa.shape[1]
    qg = jnp.tile(q_gain, A_HEADS).reshape(1, hq)
    kg = jnp.tile(k_gain, A_HEADS).reshape(1, hq)
    kig = jnp.concatenate([ki_gain, jnp.ones((LANES - IDX_DIM,), F32)]).reshape(1, LANES)
    r = jnp.arange(LANES)
    bd = _bf((r[:, None] // A_HEAD_DIM) == (r[None, :] // A_HEAD_DIM))
    row = lambda w: pl.BlockSpec((tm, w), lambda i: (i, 0))
    return pl.pallas_call(
        _proj_dsa_kernel,
        grid=(n // tm,),
        in_specs=[row(d), _full((1, d)), _full((d, m)), _full((1, hq)), _full((1, hq)),
                  _full((1, LANES)), _full((LANES, LANES))],
        out_specs=[row(hq), row(hq), row(hq), row(hq), row(hq), row(IDX_HEADS * LANES), row(LANES)],
        out_shape=[jax.ShapeDtypeStruct((n, hq), BF16), jax.ShapeDtypeStruct((n, hq), F32),
                   jax.ShapeDtypeStruct((n, hq), BF16), jax.ShapeDtypeStruct((n, hq), F32),
                   jax.ShapeDtypeStruct((n, hq), BF16),
                   jax.ShapeDtypeStruct((n, IDX_HEADS * LANES), BF16),
                   jax.ShapeDtypeStruct((n, LANES), F32)],
        compiler_params=_params("parallel"),
        name="proj_dsa",
    )(x, g.reshape(1, d), wa, qg, kg, kig, bd)


def dsa_weight(w_in):
    hq = A_HEADS * A_HEAD_DIM
    o3 = 3 * hq
    o4 = o3 + IDX_HEADS * IDX_DIM
    d = w_in.shape[0]
    wqi = w_in[:, o3:o4].reshape(d, IDX_HEADS, IDX_DIM)
    wqi = jnp.pad(wqi, ((0, 0), (0, 0), (0, LANES - IDX_DIM))).reshape(d, IDX_HEADS * LANES)
    tail = jnp.pad(w_in[:, o4:], ((0, 0), (0, LANES - (w_in.shape[1] - o4))))
    return _bf(jnp.concatenate([w_in[:, :o3], wqi, tail], axis=1))


def _sortable(s):
    b = pltpu.bitcast(s, I32)
    b = jnp.where(b == jnp.int32(-2 ** 31), 0, b)
    return jnp.where(b < 0, b ^ jnp.int32(0x7FFFFFFF), b)


def _index_mask_kernel(qi_ref, wit_ref, *rest, nref, groups, ltot, topk, causal, case_rows):
    ki_refs = rest[:nref]
    o_ref = rest[nref]
    key_ref, sel_ref, jv_ref = rest[nref + 1:]
    tq = qi_ref.shape[0]
    i = pl.program_id(1)
    lpad = sel_ref.shape[0]
    wit = wit_ref[0]
    if case_rows:
        ncase = (jnp.maximum((i + 1) * tq, topk) + case_rows - 1) // case_rows
        used_rows = ncase * case_rows
    else:
        used_rows = None

    def score_group(dst0, rows, sources):
        accs = []
        for rp, lead, src0, width, _ in sources:
            ref = ki_refs[rp]
            ki = _bf(ref[src0:src0 + rows, :] if lead is None else ref[lead, src0:src0 + rows, :])
            acc = jnp.zeros((rows, tq), F32)
            for hd in range(IDX_HEADS):
                rel = _dot_nt(ki, qi_ref[:, hd * LANES:hd * LANES + width])
                acc = acc + wit[hd:hd + 1, :] * jnp.maximum(rel, 0.0)
            accs.append(acc)
        if len(accs) == 1:
            acc = accs[0]
        else:
            lane = lax.broadcasted_iota(I32, (rows, tq), 1)
            acc = jnp.where(lane < tq // 2, accs[0], accs[1])
        s = acc * ((IDX_DIM ** -0.5) * (IDX_HEADS ** -0.5))
        if causal:
            kpos = dst0 + lax.broadcasted_iota(I32, (rows, tq), 0)
            qpos = i * tq + lax.broadcasted_iota(I32, (rows, tq), 1)
            s = jnp.where((kpos >> CHUNK_SHIFT) <= (qpos >> CHUNK_SHIFT), s, NEG)
        key_ref[dst0:dst0 + rows, :] = _sortable(s)

    for dst0, rows, sources in groups:
        if used_rows is None:
            score_group(dst0, rows, sources)
        else:
            pl.when(dst0 < used_rows)(functools.partial(score_group, dst0, rows, sources))

    def select(nrows):
        idx_bits = max(1, (nrows - 1).bit_length())

        def count(pred):
            c = pred.astype(I32).reshape(COUNT_CHAINS, nrows // COUNT_CHAINS, tq)
            return jnp.sum(jnp.sum(c, axis=1), axis=0, keepdims=True)

        def ge_count(cand):
            return count(key_ref[0:nrows, :] >= cand)

        t0 = jnp.full((1, tq), -2 ** 31, I32)
        t = jnp.where(ge_count(jnp.zeros((1, tq), I32)) >= topk, 0, t0)

        def vbody(it, t):
            cand = t + (jnp.int32(1) << (30 - it))
            return jnp.where(ge_count(cand) >= topk, cand, t)

        t = lax.fori_loop(0, 31, vbody, t)
        keys = key_ref[0:nrows, :]
        gt = keys > t
        eq = keys == t
        need = topk - count(gt)
        rowi = lax.broadcasted_iota(I32, (nrows, tq), 0)

        def jbody(it, jv):
            cand = jv + (jnp.int32(1) << (idx_bits - 1 - it))
            below = count(eq & (rowi < cand))
            return jnp.where(below < need, cand, jv)

        jv_ref[...] = jnp.full((1, tq), nrows, I32)
        has_tie = jnp.max(count(eq) - need) > 0

        @pl.when(has_tie)
        def _():
            jv_ref[...] = lax.fori_loop(0, idx_bits, jbody, jnp.zeros((1, tq), I32))

        sel = gt | (eq & (rowi <= jv_ref[...]))
        if causal:
            qpos = i * tq + lax.broadcasted_iota(I32, (nrows, tq), 1)
            sel = sel & ((rowi >> CHUNK_SHIFT) <= (qpos >> CHUNK_SHIFT))
        sel_ref[0:nrows, :] = jnp.where(sel, 0.0, NEG)
        nreal = -(-nrows // TILE)
        if nreal * TILE > nrows:
            sel_ref[nrows:nreal * TILE, :] = jnp.full((nreal * TILE - nrows, tq), NEG, F32)
        for kt in range(lpad // TILE):
            if kt < nreal:
                o_ref[0, 0, kt] = sel_ref[kt * TILE:(kt + 1) * TILE, :].T
            else:
                o_ref[0, 0, kt] = jnp.full((tq, TILE), NEG, F32)

    if case_rows:
        for k in range(ltot // case_rows):
            pl.when(ncase == k + 1)(functools.partial(select, (k + 1) * case_rows))
    else:
        select(ltot)


def index_mask(qi, wit, ki_arrays, ki_specs, groups, ltot, nstep, nqb, causal):
    lpad = -(-ltot // KTILE) * KTILE
    nt = lpad // TILE
    topk = min(TOPK_MAX, ltot // 4)
    case_rows = MASK_CASE_ROWS if (causal and ltot % MASK_CASE_ROWS == 0 and ltot > MASK_CASE_ROWS) else 0
    kern = functools.partial(_index_mask_kernel, nref=len(ki_arrays), groups=tuple(groups), ltot=ltot,
                             topk=topk, causal=causal, case_rows=case_rows)
    return pl.pallas_call(
        kern,
        grid=(nstep, nqb),
        in_specs=[pl.BlockSpec((TILE, qi.shape[1]), lambda b, i: (b * nqb + i, 0)),
                  pl.BlockSpec((1, IDX_HEADS, TILE), lambda b, i: (b * nqb + i, 0, 0))] + list(ki_specs),
        out_specs=pl.BlockSpec((1, 1, nt, TILE, TILE), lambda b, i: (b, i, 0, 0, 0)),
        out_shape=jax.ShapeDtypeStruct((nstep, nqb, nt, TILE, TILE), F32),
        scratch_shapes=[pltpu.VMEM((ltot, TILE), I32), pltpu.VMEM((lpad, TILE), F32),
                        pltpu.VMEM((1, TILE), I32)],
        compiler_params=_params("parallel", "parallel"),
        name="index_mask",
    )(qi, wit, *ki_arrays)


def _row_groups(row0, rows, make_sources):
    out = []
    for off in range(0, rows, SCORE_ROWS):
        out.append((row0 + off, min(SCORE_ROWS, rows - off), tuple(make_sources(off))))
    return out


def _attn_core(q_ref, o_ref, qs_ref, l_ref, mx_ref, acc_ref, groups, tq):
    npairs = A_HEADS // 2
    lane = lax.broadcasted_iota(I32, (1, LANES), 1)
    keep_lo = _bf((lane < A_HEAD_DIM).astype(F32))
    keep_hi = _bf((lane >= A_HEAD_DIM).astype(F32))
    for pair in range(npairs):
        qp = q_ref[:, pair * LANES:(pair + 1) * LANES]
        qs_ref[pair, 0:tq, :] = qp * keep_lo
        qs_ref[pair, tq:2 * tq, :] = qp * keep_hi
    mx_ref[...] = jnp.full(mx_ref.shape, NEG, F32)
    acc_ref[...] = jnp.zeros(acc_ref.shape, F32)
    ones = jnp.ones((KTILE, LANES), BF16)

    def over_tiles(count, body):
        if isinstance(count, int) and count == 1:
            body(0, 0)
        else:
            lax.fori_loop(0, count, body, 0)

    for count, base, k_tile, _, mask_sub, bias_sub in groups:
        def p1(kt, carry, base=base, k_tile=k_tile, mask_sub=mask_sub, bias_sub=bias_sub):
            masks = [mask_sub(kt, sub) for sub in range(KTILE // LANES)]
            for pair in range(npairs):
                sl = slice(pair * LANES, (pair + 1) * LANES)
                l = _dot_nt(qs_ref[pair], k_tile(kt, sl))
                for half in range(2):
                    rs = slice(half * tq, (half + 1) * tq)
                    mx = mx_ref[pair, rs, :]
                    for sub in range(KTILE // LANES):
                        cs = slice(sub * LANES, (sub + 1) * LANES)
                        blk = l[rs, cs] + (masks[sub] + bias_sub(kt, sub, 2 * pair + half))
                        l_ref[pair, base + kt, rs, cs] = blk
                        mx = jnp.maximum(mx, blk)
                    mx_ref[pair, rs, :] = mx
            return carry

        over_tiles(count, p1)

    for pair in range(npairs):
        m = jnp.max(mx_ref[pair], axis=-1, keepdims=True)
        mx_ref[pair] = jnp.broadcast_to(m, mx_ref.shape[1:])

    for count, base, _, v_tile, _, _ in groups:
        def p2(kt, carry, base=base, v_tile=v_tile):
            for pair in range(npairs):
                sl = slice(pair * LANES, (pair + 1) * LANES)
                m = mx_ref[pair]
                p = jnp.concatenate(
                    [jnp.exp(l_ref[pair, base + kt, :, sub * LANES:(sub + 1) * LANES] - m)
                     for sub in range(KTILE // LANES)], axis=1)
                vaug = jnp.concatenate([v_tile(kt, sl), ones], axis=1)
                acc_ref[pair] += _dot(_bf(p), vaug)
            return carry

        over_tiles(count, p2)

    lane_full = lax.broadcasted_iota(I32, (tq, LANES), 1)
    for pair in range(npairs):
        a = acc_ref[pair]
        o = a[:, 0:LANES] / a[:, LANES:2 * LANES]
        o_ref[:, pair * LANES:(pair + 1) * LANES] = _bf(jnp.where(lane_full < A_HEAD_DIM, o[0:tq], o[tq:2 * tq]))


def _attn_scratch(tq, ntiles):
    npairs = A_HEADS // 2
    return [pltpu.VMEM((npairs, 2 * tq, LANES), BF16),
            pltpu.VMEM((npairs, ntiles, 2 * tq, KTILE), F32),
            pltpu.VMEM((npairs, 2 * tq, LANES), F32),
            pltpu.VMEM((npairs, 2 * tq, 2 * LANES), F32)]


def _attn_prompt_kernel(q_ref, k_ref, v_ref, mask_ref, bias_ref, o_ref, qs_ref, l_ref, mx_ref, acc_ref):
    i = pl.program_id(1)
    per = KTILE // TILE

    def rows(kt):
        return pl.ds(pl.multiple_of(kt * KTILE, KTILE), KTILE)

    def bias_sub(kt, sub, hd):
        s = kt * per + sub
        sel = jnp.where(s == i, 2, jnp.where(s == i - 1, 1, 0))
        return bias_ref[sel, hd]

    group = (i // per + 1, 0,
             lambda kt, sl: k_ref[rows(kt), sl],
             lambda kt, sl: v_ref[rows(kt), sl],
             lambda kt, sub: mask_ref[0, 0, kt * per + sub],
             bias_sub)
    _attn_core(q_ref, o_ref, qs_ref, l_ref, mx_ref, acc_ref, [group], TILE)


def attn_prompt(q, kb, vb, mask, bias, nseq, seqlen):
    n, hq = q.shape
    nqb = seqlen // TILE
    nt = mask.shape[2]
    return pl.pallas_call(
        _attn_prompt_kernel,
        grid=(nseq, nqb),
        in_specs=[pl.BlockSpec((TILE, hq), lambda b, i: (b * nqb + i, 0)),
                  pl.BlockSpec((seqlen, hq), lambda b, i: (b, 0)),
                  pl.BlockSpec((seqlen, hq), lambda b, i: (b, 0)),
                  pl.BlockSpec((1, 1, nt, TILE, TILE), lambda b, i: (b, i, 0, 0, 0)),
                  _full(bias.shape)],
        out_specs=pl.BlockSpec((TILE, hq), lambda b, i: (b * nqb + i, 0)),
        out_shape=jax.ShapeDtypeStruct((n, hq), BF16),
        scratch_shapes=_attn_scratch(TILE, seqlen // KTILE),
        compiler_params=_params("parallel", "arbitrary"),
        name="attn_prompt",
    )(q, kb, vb, mask, bias)


def _attn_sample_kernel(q_ref, kc_ref, vc_ref, kn_ref, vn_ref, mask_ref, bias_ref, o_ref,
                        l_ref, mx_ref, acc_ref, *, nblk):
    ph = pl.program_id(1)
    b = pl.program_id(2)
    tq = q_ref.shape[0]
    per = KTILE // TILE
    tiles_per_blk = kc_ref.shape[1] // (A_HEADS * KTILE)
    ncache = nblk * tiles_per_blk
    last_sub = ncache * per - 1
    new_rows = kn_ref.shape[1] // A_HEADS
    heads = range(A_HEADS)

    def head_tile(ref, r0, rows, hd):
        x = _bf(ref[0, pl.ds(r0 * A_HEADS + hd, rows, stride=A_HEADS), :])
        if rows < KTILE:
            x = jnp.concatenate([x, jnp.zeros((KTILE - rows, A_HEAD_DIM), BF16)], axis=0)
        return x

    def pass1(slot, ref, r0, rows, mask_sub, bias_sub):
        ls = [_dot_nt(q_ref[:, hd * A_HEAD_DIM:(hd + 1) * A_HEAD_DIM], head_tile(ref, r0, rows, hd))
              for hd in heads]
        masks = [mask_sub(sub) for sub in range(per)]
        for hd in heads:
            mx = mx_ref[hd]
            for sub in range(per):
                cs = slice(sub * LANES, (sub + 1) * LANES)
                blk = ls[hd][:, cs] + (masks[sub] + bias_sub(sub, hd))
                l_ref[hd, slot, :, cs] = blk
                mx = jnp.maximum(mx, blk)
            mx_ref[hd] = mx

    def pass2(slot, ref, r0, rows):
        ones = jnp.ones((KTILE, LANES - A_HEAD_DIM), BF16)
        for hd in heads:
            m = mx_ref[hd]
            p = jnp.concatenate([jnp.exp(l_ref[hd, slot, :, sub * LANES:(sub + 1) * LANES] - m)
                                 for sub in range(per)], axis=1)
            vaug = jnp.concatenate([head_tile(ref, r0, rows, hd), ones], axis=1)
            acc_ref[hd] += _dot(_bf(p), vaug)

    @pl.when((ph == 0) & (b == 0))
    def _():
        mx_ref[...] = jnp.full(mx_ref.shape, NEG, F32)
        acc_ref[...] = jnp.zeros(acc_ref.shape, F32)

    @pl.when(ph == 0)
    def _():
        for t in range(tiles_per_blk):
            g = b * tiles_per_blk + t
            pass1(g, kc_ref, t * KTILE, KTILE,
                  lambda sub, g=g: mask_ref[0, 0, g * per + sub],
                  lambda sub, hd, g=g: bias_ref[jnp.where(g * per + sub == last_sub, 1, 0), hd])

    @pl.when((ph == 0) & (b == nblk - 1))
    def _():
        pass1(ncache, kn_ref, 0, new_rows,
              lambda sub: mask_ref[0, 0, ncache * per + sub],
              lambda sub, hd: bias_ref[2 if sub == 0 else 0, hd])
        for hd in heads:
            m = jnp.max(mx_ref[hd], axis=-1, keepdims=True)
            mx_ref[hd] = jnp.broadcast_to(m, mx_ref.shape[1:])

    @pl.when(ph == 1)
    def _():
        for t in range(tiles_per_blk):
            pass2(b * tiles_per_blk + t, vc_ref, t * KTILE, KTILE)

    @pl.when((ph == 1) & (b == nblk - 1))
    def _():
        pass2(ncache, vn_ref, 0, new_rows)
        for pair in range(A_HEADS // 2):
            halves = []
            for hd in (2 * pair, 2 * pair + 1):
                a = acc_ref[hd]
                halves.append(a[:, 0:A_HEAD_DIM] / a[:, A_HEAD_DIM:2 * A_HEAD_DIM])
            o_ref[:, pair * LANES:(pair + 1) * LANES] = _bf(jnp.concatenate(halves, axis=1))


def attn_sample(q, kc, vc, kn, vn, mask, bias, nseq, tq):
    n, hq = q.shape
    past = kc.shape[1]
    cblk = min(CACHE_BLOCK, past)
    nblk = past // cblk
    ncache = past // KTILE
    nt = mask.shape[2]
    per = TILE // tq
    flat = lambda a: a.reshape(nseq, a.shape[1] * A_HEADS, A_HEAD_DIM)
    kc, vc, kn, vn = flat(kc), flat(vc), flat(kn), flat(vn)
    cache_k = pl.BlockSpec((1, cblk * A_HEADS, A_HEAD_DIM),
                           lambda s, ph, b: (s, b * (1 - ph) + (nblk - 1) * ph, 0))
    cache_v = pl.BlockSpec((1, cblk * A_HEADS, A_HEAD_DIM), lambda s, ph, b: (s, b * ph, 0))
    fresh = pl.BlockSpec((1, tq * A_HEADS, A_HEAD_DIM), lambda s, ph, b: (s, 0, 0))
    return pl.pallas_call(
        functools.partial(_attn_sample_kernel, nblk=nblk),
        grid=(nseq, 2, nblk),
        in_specs=[pl.BlockSpec((tq, hq), lambda s, ph, b: (s, 0)),
                  cache_k, cache_v, fresh, fresh,
                  pl.BlockSpec((1, 1, nt, tq, TILE), lambda s, ph, b: (s // per, 0, 0, s % per, 0)),
                  pl.BlockSpec(bias.shape, lambda s, ph, b: (0, 0, 0, 0))],
        out_specs=pl.BlockSpec((tq, hq), lambda s, ph, b: (s, 0)),
        out_shape=jax.ShapeDtypeStruct((n, hq), BF16),
        scratch_shapes=[pltpu.VMEM((A_HEADS, ncache + 1, tq, KTILE), F32),
                        pltpu.VMEM((A_HEADS, tq, LANES), F32),
                        pltpu.VMEM((A_HEADS, tq, LANES), F32)],
        compiler_params=_params("parallel", "arbitrary", "arbitrary"),
        name="attn_sample",
    )(q, kc, vc, kn, vn, mask, bias)


def _t5_bucket(rel):
    half = N_BUCKETS // 2
    max_exact = half // 2
    n = jnp.abs(rel)
    nf = jnp.maximum(n, 1).astype(F32)
    large = max_exact + (jnp.log(nf / max_exact) / math.log(MAX_DISTANCE / max_exact)
                         * (half - max_exact)).astype(I32)
    large = jnp.minimum(large, half - 1)
    return jnp.where(rel > 0, half, 0) + jnp.where(n < max_exact, n, large)


def bias_tiles(rel_bias):
    heads = rel_bias.shape[1]
    span = 2 * TILE - 1

    def toeplitz(shift):
        rel = jnp.arange(span, dtype=I32) - (TILE - 1) + shift
        tab = rel_bias.astype(F32)[_t5_bucket(rel)].T
        strip = jnp.tile(jnp.pad(tab, ((0, 0), (0, 1))), (1, TILE))[:, :TILE * span]
        return strip.reshape(heads, TILE, span)[:, :, TILE - 1:]

    far = rel_bias.astype(F32)[_t5_bucket(jnp.full((1,), -(TILE + 1), I32))]
    far = jnp.broadcast_to(far.T[:, :, None], (heads, TILE, TILE))
    return jnp.stack([far, toeplitz(-TILE), toeplitz(0)])


def dsa_core_prompt(pr, nseq, seqlen, bias):
    q, _, kb, _, vb, qi, kiwi = pr
    nqb = seqlen // TILE
    wit = kiwi[:, IDX_DIM:IDX_DIM + IDX_HEADS].reshape(nseq * nqb, TILE, IDX_HEADS).swapaxes(1, 2)
    groups = _row_groups(0, seqlen, lambda off: [(0, None, off, LANES, None)])
    mask = index_mask(qi, wit, [kiwi], [pl.BlockSpec((seqlen, LANES), lambda b, i: (b, 0))],
                      groups, seqlen, nseq, nqb, causal=True)
    return attn_prompt(q, kb, vb, mask, bias, nseq, seqlen)


def dsa_core_sample(pr, k_cache, v_cache, ki_cache, nseq, tq, bias):
    q, kf, _, vf, _, qi, kiwi = pr
    past = k_cache.shape[1]
    per = TILE // tq
    wit = kiwi[:, IDX_DIM:IDX_DIM + IDX_HEADS].reshape(nseq // per, TILE, IDX_HEADS).swapaxes(1, 2)
    groups = (_row_groups(0, past, lambda off: [(0, j, off, IDX_DIM, j) for j in range(per)])
              + _row_groups(past, tq, lambda off: [(1, None, j * tq + off, LANES, j) for j in range(per)]))
    mask = index_mask(
        qi, wit, [ki_cache, kiwi],
        [pl.BlockSpec((per, past, IDX_DIM), lambda b, i: (b, 0, 0)),
         pl.BlockSpec((TILE, LANES), lambda b, i: (b, 0))],
        groups, past + tq, nseq // per, 1, causal=False)
    kn = kf.reshape(nseq, tq, A_HEADS, A_HEAD_DIM)
    vn = vf.reshape(nseq, tq, A_HEADS, A_HEAD_DIM)
    return attn_sample(q, k_cache, v_cache, kn, vn, mask, bias[:, :, :tq, :], nseq, tq)


def _log_sigmoid(x):
    return jnp.minimum(x, 0.0) - jnp.log1p(jnp.exp(-jnp.abs(x)))


def _mlstm_kernel(p_ref, gs_ref, gr_ref, gbc_ref, gbr_ref, hg_ref, c0_ref, n0_ref, m0_ref,
                  hs_ref, c_ref, n_ref, m_ref, cs, ns, ms):
    c_id = pl.program_id(1)
    nh = B_HEADS
    npair = nh // 2
    hw = LANES
    L = p_ref.shape[0]
    L2 = 2 * L

    @pl.when(c_id == 0)
    def _():
        cs[...] = jnp.zeros(cs.shape, F32)
        ns[...] = jnp.zeros(ns.shape, F32)
        ns[:, 0:B_QK_DIM] = n0_ref[0]
        for hd in range(nh):
            cs[hd // 2, (hd % 2) * hw:(hd % 2) * hw + B_QK_DIM, :] = c0_ref[0, hd]
        ms[...] = m0_ref[0]

    r = lax.broadcasted_iota(I32, (L2, L2), 0)
    c = lax.broadcasted_iota(I32, (L2, L2), 1)
    same = (r >> CHUNK_SHIFT) == (c >> CHUNK_SHIFT)
    incl = same & (c <= r)
    gs = gs_ref[0] + gbc_ref[...]
    gr = gr_ref[0] + gbr_ref[...]
    bcol = _dot_exact_lhs01(_bf01(incl), _log_sigmoid(gs[:, npair:]))
    brow = _dot_exact_rhs01(_log_sigmoid(gr[npair:, :]), _bf01(same & (r <= c)))
    icol = gs[:, :npair]
    irow = gr[:npair, :]
    rowc = lax.broadcasted_iota(I32, (L2, 1), 0)
    is_top = rowc < L
    top = is_top.astype(F32)
    bot = 1.0 - top
    row2 = lax.broadcasted_iota(I32, (2 * hw, 1), 0)
    hg = hg_ref[...]
    mall = ms[...]

    def stack(base, pr):
        a = base + 2 * pr * hw
        return jnp.concatenate([p_ref[:, a:a + hw], p_ref[:, a + hw:a + 2 * hw]], axis=0)

    def bd(x):
        return _bf(jnp.concatenate([x * top, x * bot], axis=1))

    pairs = range(npair)
    qf = [stack(0, pr) for pr in pairs]
    kf = [stack(nh * hw, pr) * (B_QK_DIM ** -0.5) for pr in pairs]
    vb = [_bf(stack(2 * nh * hw, pr)) for pr in pairs]
    bc = [bcol[:, pr:pr + 1] for pr in pairs]
    ic = [icol[:, pr:pr + 1] for pr in pairs]
    mc = [mall[:, pr:pr + 1] for pr in pairs]
    d = [jnp.where(incl, bc[pr] - brow[pr:pr + 1, :] + irow[pr:pr + 1, :], NEG) for pr in pairs]
    inter = [bc[pr] + mc[pr] for pr in pairs]
    mt = [jnp.maximum(inter[pr], jnp.max(d[pr], axis=-1, keepdims=True)) for pr in pairs]
    s = [_dot_nt(_bf(qf[pr]), _bf(kf[pr])) * jnp.exp(d[pr] - mt[pr]) for pr in pairs]
    wst = [jnp.exp(inter[pr] - mt[pr]) for pr in pairs]
    cmat = [cs[pr] for pr in pairs]
    num = [_dot(_bf(s[pr]), vb[pr]) + wst[pr] * _dot(bd(qf[pr]), _bf(cmat[pr])) for pr in pairs]
    for pr in pairs:
        n0 = ns[2 * pr:2 * pr + 1, :]
        n1 = ns[2 * pr + 1:2 * pr + 2, :]
        qn = jnp.sum(qf[pr] * jnp.where(is_top, n0, n1), axis=-1, keepdims=True)
        den = jnp.sum(s[pr], axis=-1, keepdims=True) + wst[pr] * qn
        hs = num[pr] / jnp.maximum(jnp.abs(den), jnp.exp(-mt[pr]))
        bl0 = bc[pr][L - 1:L, :]
        bl1 = bc[pr][L2 - 1:L2, :]
        bl = jnp.where(is_top, bl0, bl1)
        dec = bl - bc[pr] + ic[pr]
        blm = bl + mc[pr]
        mnew0 = jnp.maximum(blm[0:1, :], jnp.max(dec[0:L], axis=0, keepdims=True))
        mnew1 = jnp.maximum(blm[L:L + 1, :], jnp.max(dec[L:L2], axis=0, keepdims=True))
        mnew = jnp.where(is_top, mnew0, mnew1)
        wk = jnp.exp(dec - mnew)
        ws = jnp.exp(blm - mnew)
        kw = kf[pr] * wk
        ws2 = jnp.where(row2 < hw, ws[0:1, :], ws[L:L + 1, :])
        cs[pr] = ws2 * cmat[pr] + _dot_tn(bd(kw), vb[pr])
        ns[2 * pr:2 * pr + 1, :] = ws[0:1, :] * n0 + jnp.sum(kw[0:L], axis=0, keepdims=True)
        ns[2 * pr + 1:2 * pr + 2, :] = ws[L:L + 1, :] * n1 + jnp.sum(kw[L:L2], axis=0, keepdims=True)
        ms[:, pr:pr + 1] = mnew
        on = _bf(_rms(hs, hg) * _sigmoid(stack(3 * nh * hw, pr)))
        oa = 2 * pr * B_V_DIM
        hs_ref[:, oa:oa + B_V_DIM] = on[0:L]
        hs_ref[:, oa + B_V_DIM:oa + 2 * B_V_DIM] = on[L:L2]

    @pl.when(c_id == pl.num_programs(1) - 1)
    def _():
        for hd in range(nh):
            c_ref[0, hd] = cs[hd // 2, (hd % 2) * hw:(hd % 2) * hw + B_QK_DIM, :]
        n_ref[0] = ns[:, 0:B_QK_DIM]
        m_ref[0] = ms[...]


def mlstm_weight(w_in):
    d = w_in.shape[0]
    nq = B_HEADS * B_QK_DIM
    nv = B_HEADS * B_V_DIM

    def padh(w):
        w = w.reshape(d, B_HEADS, B_QK_DIM)
        return jnp.pad(w, ((0, 0), (0, 0), (0, LANES - B_QK_DIM))).reshape(d, B_HEADS * LANES)

    main = 2 * nq + 2 * nv
    tail = jnp.pad(w_in[:, main:], ((0, 0), (0, LANES - 2 * B_HEADS)))
    return _bf(jnp.concatenate([padh(w_in[:, :nq]), padh(w_in[:, nq:2 * nq]), w_in[:, 2 * nq:main], tail], axis=1))


def mlstm_core(p, gate_bias, h_gain, c0, n0, m0, nseq, nchunk):
    n, m = p.shape
    nh = B_HEADS
    npair = nh // 2
    gcol = 4 * nh * LANES
    pre = p[:, gcol:gcol + 2 * nh].reshape(nseq * nchunk, CHUNK, 2, npair, 2)
    gs = pre.transpose(0, 4, 1, 2, 3).reshape(nseq * nchunk, 2 * CHUNK, 2 * npair)
    gr = pre.transpose(0, 2, 3, 4, 1).reshape(nseq * nchunk, 2 * npair, 2 * CHUNK)
    gb = gate_bias.reshape(2, npair, 2)
    gbc = jnp.repeat(gb.transpose(2, 0, 1).reshape(2, 2 * npair), CHUNK, axis=0)
    m0s = jnp.repeat(m0.reshape(nseq, npair, 2).transpose(0, 2, 1), CHUNK, axis=1)
    o, c_new, n_new, m_new = pl.pallas_call(
        _mlstm_kernel,
        grid=(nseq, nchunk),
        in_specs=[pl.BlockSpec((CHUNK, m), lambda s, c: (s * nchunk + c, 0)),
                  pl.BlockSpec((1, 2 * CHUNK, 2 * npair), lambda s, c: (s * nchunk + c, 0, 0)),
                  pl.BlockSpec((1, 2 * npair, 2 * CHUNK), lambda s, c: (s * nchunk + c, 0, 0)),
                  _full((2 * CHUNK, 2 * npair)), _full((2 * npair, 2 * CHUNK)), _full((1, B_V_DIM)),
                  pl.BlockSpec((1, nh, B_QK_DIM, B_V_DIM), lambda s, c: (s, 0, 0, 0)),
                  pl.BlockSpec((1, nh, B_QK_DIM), lambda s, c: (s, 0, 0)),
                  pl.BlockSpec((1, 2 * CHUNK, npair), lambda s, c: (s, 0, 0))],
        out_specs=[pl.BlockSpec((CHUNK, nh * B_V_DIM), lambda s, c: (s * nchunk + c, 0)),
                   pl.BlockSpec((1, nh, B_QK_DIM, B_V_DIM), lambda s, c: (s, 0, 0, 0)),
                   pl.BlockSpec((1, nh, B_QK_DIM), lambda s, c: (s, 0, 0)),
                   pl.BlockSpec((1, 2 * CHUNK, npair), lambda s, c: (s, 0, 0))],
        out_shape=[jax.ShapeDtypeStruct((n, nh * B_V_DIM), BF16),
                   jax.ShapeDtypeStruct((nseq, nh, B_QK_DIM, B_V_DIM), F32),
                   jax.ShapeDtypeStruct((nseq, nh, B_QK_DIM), F32),
                   jax.ShapeDtypeStruct((nseq, 2 * CHUNK, npair), F32)],
        scratch_shapes=[pltpu.VMEM((npair, 2 * LANES, B_V_DIM), F32), pltpu.VMEM((nh, LANES), F32),
                        pltpu.VMEM((2 * CHUNK, npair), F32)],
        compiler_params=_params("parallel", "arbitrary"),
        name="mlstm",
    )(p, gs, gr, gbc, gbc.T, h_gain.reshape(1, B_V_DIM), c0, n0, m0s)
    m_heads = m_new[:, ::CHUNK, :].transpose(0, 2, 1).reshape(nseq, nh)
    return o, c_new, n_new, m_heads


def _split2(x):
    hi = _bf(x)
    return hi, _bf(x - hi.astype(F32))


def _cat3_lhs(x):
    hi, mid = _split2(x)
    return jnp.concatenate([hi, hi, mid], axis=1)


def _cat3_rhs(x):
    hi, mid = _split2(x)
    return jnp.concatenate([hi, mid, hi], axis=0)


def _gdn_kernel(p_ref, gs_ref, gr_ref, cw_ref, alc_ref, alr_ref, dtc_ref, dtr_ref, og_ref, s0_ref, cb0_ref,
                o_ref, s_ref, cb_ref, ss, tail):
    c_id = pl.program_id(1)
    nh = C_HEADS
    cdim = nh * (2 * C_DK + C_DV)
    L = p_ref.shape[0]
    L2 = 2 * L
    nprev = CONV_W - 1

    @pl.when(c_id == 0)
    def _():
        ss[...] = s0_ref[0]
        tail[...] = jnp.zeros(tail.shape, F32)
        tail[8 - nprev:8, :] = cb0_ref[0]

    x = p_ref[:, 0:cdim]
    ext = jnp.concatenate([tail[...], x], axis=0)
    conv = ext[8:8 + L] * cw_ref[CONV_W - 1:CONV_W, :]
    for j in range(CONV_W - 1):
        conv = conv + ext[8 - nprev + j:8 - nprev + j + L] * cw_ref[j:j + 1, :]
    tail[...] = x[L - 8:L, :]
    cf = _silu(conv)

    npair = nh // 2
    gs = gs_ref[0]
    gr = gr_ref[0]
    beta = _sigmoid(gs[:, 0:npair])
    g_col = -jnp.exp(alc_ref[...]) * _softplus(gs[:, npair:] + dtc_ref[...])
    g_row = -jnp.exp(alr_ref[...]) * _softplus(gr[npair:, :] + dtr_ref[...])
    r = lax.broadcasted_iota(I32, (L2, L2), 0)
    c = lax.broadcasted_iota(I32, (L2, L2), 1)
    same = (r >> CHUNK_SHIFT) == (c >> CHUNK_SHIFT)
    incl = same & (c <= r)
    strict = same & (c < r)
    eye = (c == r).astype(F32)
    gc_col = _dot_exact_lhs01(_bf01(incl), g_col)
    gc_row = _dot_exact_rhs01(g_row, _bf01(same & (r <= c)))
    rowc = lax.broadcasted_iota(I32, (L2, 1), 0)
    top = (rowc < L).astype(F32)
    bot = 1.0 - top
    row2 = lax.broadcasted_iota(I32, (2 * C_DK, 1), 0)
    og = og_ref[...]

    def stack(base, pr):
        a = base + 2 * pr * C_DK
        return jnp.concatenate([cf[:, a:a + C_DK], cf[:, a + C_DK:a + 2 * C_DK]], axis=0)

    def bd(x):
        return _bf(jnp.concatenate([x * top, x * bot], axis=1))

    pairs = range(npair)
    qc, kc, kcb, dm, amat, rhs, gcols, egcs = [], [], [], [], [], [], [], []
    for pr in pairs:
        qraw = stack(0, pr)
        kraw = stack(nh * C_DK, pr)
        vc = stack(2 * nh * C_DK, pr)
        qc.append(qraw * lax.rsqrt(jnp.sum(qraw * qraw, axis=-1, keepdims=True) + EPS) * (C_DK ** -0.5))
        kc.append(kraw * lax.rsqrt(jnp.sum(kraw * kraw, axis=-1, keepdims=True) + EPS))
        bc = beta[:, pr:pr + 1]
        gcol = gc_col[:, pr:pr + 1]
        grow = gc_row[pr:pr + 1, :]
        dm.append(jnp.where(incl, jnp.exp(jnp.where(incl, gcol - grow, 0.0)), 0.0))
        kb = kc[pr] * bc
        kcb.append(_bf(kc[pr]))
        amat.append(jnp.where(strict, _dot_nt(_bf(kb), kcb[pr]) * dm[pr], 0.0))
        egc = jnp.exp(gcol)
        rhs.append(jnp.concatenate([vc * bc, kb * egc], axis=-1))
        gcols.append(gcol)
        egcs.append(egc)
    tinv = [eye - amat[pr] for pr in pairs]
    pw_l = [_cat3_lhs(-amat[pr]) for pr in pairs]
    pw_r = [_cat3_rhs(-amat[pr]) for pr in pairs]
    for _ in range(CHUNK_SHIFT - 1):
        pw = [_dot(pw_l[pr], pw_r[pr]) for pr in pairs]
        pw_l = [_cat3_lhs(pw[pr]) for pr in pairs]
        pw_r = [_cat3_rhs(pw[pr]) for pr in pairs]
        tinv = [tinv[pr] + _dot(_cat3_lhs(tinv[pr]), pw_r[pr]) for pr in pairs]
    sol = [_dot(_cat3_lhs(tinv[pr]), _cat3_rhs(rhs[pr])) for pr in pairs]
    attn = [_dot_nt(_bf(qc[pr]), kcb[pr]) * dm[pr] for pr in pairs]
    smat = [ss[pr] for pr in pairs]
    sb = [_bf(smat[pr]) for pr in pairs]
    vnew = [sol[pr][:, :C_DV] - _dot(bd(sol[pr][:, C_DV:]), sb[pr]) for pr in pairs]
    o = [_dot(bd(qc[pr] * egcs[pr]), sb[pr]) + _dot(_bf(attn[pr]), _bf(vnew[pr])) for pr in pairs]
    for pr in pairs:
        gl0 = gcols[pr][L - 1:L, :]
        gl1 = gcols[pr][L2 - 1:L2, :]
        ke = kc[pr] * jnp.exp(jnp.where(rowc < L, gl0, gl1) - gcols[pr])
        decay = jnp.exp(jnp.where(row2 < C_DK, gl0, gl1))
        ss[pr] = smat[pr] * decay + _dot_tn(bd(ke), _bf(vnew[pr]))
    for pr in pairs:
        za = cdim + 2 * pr * C_DV
        z = jnp.concatenate([p_ref[:, za:za + C_DV], p_ref[:, za + C_DV:za + 2 * C_DV]], axis=0)
        on = _bf(_rms(o[pr], og) * _silu(z))
        oa = 2 * pr * C_DV
        o_ref[:, oa:oa + C_DV] = on[0:L]
        o_ref[:, oa + C_DV:oa + 2 * C_DV] = on[L:L2]

    @pl.when(c_id == pl.num_programs(1) - 1)
    def _():
        s_ref[0] = ss[...]
        cb_ref[0] = tail[8 - nprev:8, :]


def gdn_weight(w_in):
    main = C_HEADS * (2 * C_DK + C_DV) + C_HEADS * C_DV
    tail = jnp.pad(w_in[:, main:], ((0, 0), (0, LANES - 2 * C_HEADS)))
    return _bf(jnp.concatenate([w_in[:, :main], tail], axis=1))


def gdn_core(p, conv_w, a_log, dt_bias, o_gain, s0, cb0, nseq, nchunk):
    n, m = p.shape
    nh = C_HEADS
    npair = nh // 2
    cdim = nh * (2 * C_DK + C_DV)
    gcolumn = cdim + nh * C_DV
    pre = p[:, gcolumn:gcolumn + 2 * nh].reshape(nseq * nchunk, CHUNK, 2, npair, 2)
    gs = pre.transpose(0, 4, 1, 2, 3).reshape(nseq * nchunk, 2 * CHUNK, 2 * npair)
    gr = pre.transpose(0, 2, 3, 4, 1).reshape(nseq * nchunk, 2 * npair, 2 * CHUNK)

    def col(v):
        return jnp.repeat(v.reshape(npair, 2).T, CHUNK, axis=0)

    o, s_new, cb_new = pl.pallas_call(
        _gdn_kernel,
        grid=(nseq, nchunk),
        in_specs=[pl.BlockSpec((CHUNK, m), lambda s, c: (s * nchunk + c, 0)),
                  pl.BlockSpec((1, 2 * CHUNK, 2 * npair), lambda s, c: (s * nchunk + c, 0, 0)),
                  pl.BlockSpec((1, 2 * npair, 2 * CHUNK), lambda s, c: (s * nchunk + c, 0, 0)),
                  _full((CONV_W, cdim)),
                  _full((2 * CHUNK, npair)), _full((npair, 2 * CHUNK)),
                  _full((2 * CHUNK, npair)), _full((npair, 2 * CHUNK)),
                  _full((1, C_DV)),
                  pl.BlockSpec((1, npair, 2 * C_DK, C_DV), lambda s, c: (s, 0, 0, 0)),
                  pl.BlockSpec((1, CONV_W - 1, cdim), lambda s, c: (s, 0, 0))],
        out_specs=[pl.BlockSpec((CHUNK, nh * C_DV), lambda s, c: (s * nchunk + c, 0)),
                   pl.BlockSpec((1, npair, 2 * C_DK, C_DV), lambda s, c: (s, 0, 0, 0)),
                   pl.BlockSpec((1, CONV_W - 1, cdim), lambda s, c: (s, 0, 0))],
        out_shape=[jax.ShapeDtypeStruct((n, nh * C_DV), BF16),
                   jax.ShapeDtypeStruct((nseq, npair, 2 * C_DK, C_DV), F32),
                   jax.ShapeDtypeStruct((nseq, CONV_W - 1, cdim), F32)],
        scratch_shapes=[pltpu.VMEM((npair, 2 * C_DK, C_DV), F32), pltpu.VMEM((8, cdim), F32)],
        compiler_params=_params("parallel", "arbitrary"),
        name="gdn",
    )(p, gs, gr, conv_w, col(a_log), col(a_log).T, col(dt_bias), col(dt_bias).T, o_gain.reshape(1, C_DV),
      s0.reshape(nseq, npair, 2 * C_DK, C_DV), cb0)
    return o, s_new.reshape(nseq, nh, C_DK, C_DV), cb_new


def _trunk(x, nseq, seqlen, mem_k, mem_v, st, W, bias, is_prompt):
    d = x.shape[-1]
    n = nseq * seqlen
    x = x.reshape(n, d)
    tm = min(256, n)
    tm_mem = min(MEM_ROWS, n)
    new = {}
    for i in range(4):
        kind = i % 3
        mx = W["mixer"][i]
        if kind == 0:
            pr = proj_dsa(x, W["norm_mix"][i], mx["w_in"], mx["q_gain"], mx["k_gain"], mx["ki_gain"], tm)
            if is_prompt:
                o = dsa_core_prompt(pr, nseq, seqlen, bias)
            else:
                o = dsa_core_sample(pr, *st[i], nseq, seqlen, bias)
            new[i] = (pr[1].reshape(nseq, seqlen, A_HEADS, A_HEAD_DIM),
                      pr[3].reshape(nseq, seqlen, A_HEADS, A_HEAD_DIM),
                      pr[6][:, :IDX_DIM].reshape(nseq, seqlen, IDX_DIM))
        elif kind == 1:
            p = proj(x, W["norm_mix"][i], mx["w_in"], tm)
            o, c_new, n_new, m_new = mlstm_core(p, mx["gate_bias"], mx["h_gain"], *st[i], nseq, seqlen // CHUNK)
            new[i] = (c_new, n_new, m_new.reshape(nseq, B_HEADS))
        else:
            p = proj(x, W["norm_mix"][i], mx["w_in"], tm)
            o, s_new, cb_new = gdn_core(p, mx["conv_w"], mx["a_log"], mx["dt_bias"], mx["o_gain"], *st[i],
                                        nseq, seqlen // CHUNK)
            new[i] = (s_new, cb_new)
        x = mem_attend(x, o, mx["w_out"], W["norm_mem"][i], W["w_mq"][i], W["mq_gain"][i],
                       mem_k[i], mem_v[i], W["w_mo"][i], tm_mem, seqlen)
        x = ffn(x, W["norm_ffn"][i], W["w_ffn1"][i], W["w_ffn3"][i], W["w_ffn2"][i], min(FFN_ROWS, n))
    return x.reshape(nseq, seqlen, d), new


def kernel(x_prompt, x_sample, mem_prompt, cache_l0_k, cache_l0_v, cache_l0_kidx, state_l1_C, state_l1_n, state_l1_m, state_l2_S, state_l2_conv, cache_l3_k, cache_l3_v, cache_l3_kidx, cache_mem_k, cache_mem_v, rel_bias, norm_mix, norm_mem, norm_ffn, mem_norm, w_mq, w_mk, w_mv, w_mo, mq_gain, mk_gain, w_ffn1, w_ffn3, w_ffn2, a0_w_in, a0_w_out, a0_q_gain, a0_k_gain, a0_kidx_gain, b1_w_in, b1_gate_bias, b1_h_gain, b1_w_out, c2_w_in, c2_conv_w, c2_a_log, c2_dt_bias, c2_o_gain, c2_w_out, a3_w_in, a3_w_out, a3_q_gain, a3_k_gain, a3_kidx_gain):
    B, T, D = x_prompt.shape
    S, Ts, _ = x_sample.shape
    depth = w_mq.shape[0]
    mlen = mem_prompt.shape[1]
    mw = MEM_HEADS * MEM_HEAD_DIM

    def dsa_w(w_in, w_out, qg, kg, kig):
        return dict(w_in=dsa_weight(w_in), w_out=_bf(w_out), q_gain=qg, k_gain=kg, ki_gain=kig)

    W = dict(
        norm_mix=norm_mix, norm_mem=norm_mem, norm_ffn=norm_ffn,
        w_mq=_bf(w_mq), w_mo=_bf(w_mo), mq_gain=mq_gain,
        w_ffn1=_bf(w_ffn1), w_ffn3=_bf(w_ffn3), w_ffn2=_bf(w_ffn2),
        mixer={
            0: dsa_w(a0_w_in, a0_w_out, a0_q_gain, a0_k_gain, a0_kidx_gain),
            1: dict(w_in=mlstm_weight(b1_w_in), gate_bias=b1_gate_bias, h_gain=b1_h_gain, w_out=_bf(b1_w_out)),
            2: dict(w_in=gdn_weight(c2_w_in), conv_w=c2_conv_w, a_log=c2_a_log, dt_bias=c2_dt_bias,
                    o_gain=c2_o_gain, w_out=_bf(c2_w_out)),
            3: dsa_w(a3_w_in, a3_w_out, a3_q_gain, a3_k_gain, a3_kidx_gain),
        },
    )
    bias = bias_tiles(rel_bias)

    mk_p, mv_p = mem_kv(mem_prompt.reshape(B * mlen, D), mem_norm, w_mk, w_mv, mk_gain)
    mk_p = mk_p.reshape(depth, B, mlen, mw)
    mv_p = mv_p.reshape(depth, B, mlen, mw)
    st_p = {
        0: None,
        1: (jnp.zeros((B, B_HEADS, B_QK_DIM, B_V_DIM), F32), jnp.zeros((B, B_HEADS, B_QK_DIM), F32),
            jnp.full((B, B_HEADS), NEG, F32)),
        2: (jnp.zeros((B, C_HEADS, C_DK, C_DV), F32), jnp.zeros((B, CONV_W - 1, state_l2_conv.shape[-1]), F32)),
        3: None,
    }
    y_p, np_ = _trunk(x_prompt, B, T, mk_p, mv_p, st_p, W, bias, True)

    st_s = {
        0: (cache_l0_k, cache_l0_v, cache_l0_kidx),
        1: (state_l1_C, state_l1_n, state_l1_m),
        2: (state_l2_S, state_l2_conv),
        3: (cache_l3_k, cache_l3_v, cache_l3_kidx),
    }
    mk_s = cache_mem_k.reshape(depth, S, mlen, mw)
    mv_s = cache_mem_v.reshape(depth, S, mlen, mw)
    y_s, ns_ = _trunk(x_sample, S, Ts, mk_s, mv_s, st_s, W, bias, False)

    shp = (depth, B, mlen, MEM_HEADS, MEM_HEAD_DIM)
    return (y_p, y_s,
            *np_[0], *np_[1], *np_[2], *np_[3], mk_p.reshape(shp), mv_p.reshape(shp),
            *ns_[0], *ns_[1], *ns_[2], *ns_[3])
```

```python
import functools
import math

import jax
import jax.numpy as jnp
from jax import lax
from jax.experimental import pallas as pl
from jax.experimental.pallas import tpu as pltpu

F32 = jnp.float32
BF16 = jnp.bfloat16
I32 = jnp.int32

EPS = 1e-6
NEG = -1e30
CHUNK = 64
CHUNK_SHIFT = 6
LANES = 128
TILE = 128
KTILE = 256
COUNT_CHAINS = 8
SCORE_ROWS = 256
FFN_ROWS = 512
MEM_ROWS = 256
CACHE_BLOCK = 512
MASK_CASE_ROWS = 512
VMEM_LIMIT = 56 * 1024 * 1024

A_HEADS, A_HEAD_DIM = 16, 64
IDX_HEADS, IDX_DIM = 8, 64
TOPK_MAX = 256
N_BUCKETS, MAX_DISTANCE = 32, 128
B_HEADS, B_QK_DIM, B_V_DIM = 8, 64, 128
C_HEADS, C_DK, C_DV = 8, 128, 128
CONV_W = 4
MEM_HEADS, MEM_HEAD_DIM = 4, 128


def _bf(x):
    return x.astype(BF16)


def _bf01(mask):
    return mask.astype(F32).astype(BF16)


def _dot(a, b):
    return jnp.dot(a, b, preferred_element_type=F32)


def _dot_nt(a, b):
    return lax.dot_general(a, b, (((1,), (1,)), ((), ())), preferred_element_type=F32)


def _dot_tn(a, b):
    return lax.dot_general(a, b, (((0,), (0,)), ((), ())), preferred_element_type=F32)


def _split3(x):
    hi = _bf(x)
    r1 = x - hi.astype(F32)
    mid = _bf(r1)
    lo = _bf(r1 - mid.astype(F32))
    return hi, mid, lo


def _dot_exact_rhs01(x, m01):
    hi, mid, lo = _split3(x)
    return _dot(hi, m01) + _dot(mid, m01) + _dot(lo, m01)


def _dot_exact_lhs01(m01, x):
    hi, mid, lo = _split3(x)
    return _dot(m01, hi) + _dot(m01, mid) + _dot(m01, lo)


def _dot_f32(a, b):
    ah, am, al = _split3(a)
    bh, bm, bl = _split3(b)
    return (_dot(ah, bh) + (_dot(ah, bm) + _dot(am, bh))
            + (_dot(am, bm) + _dot(ah, bl) + _dot(al, bh)))


def _rms(x, g):
    ms = jnp.mean(x * x, axis=-1, keepdims=True)
    return x * lax.rsqrt(ms + EPS) * g


def _sigmoid(x):
    return 1.0 / (1.0 + jnp.exp(-x))


def _silu(x):
    return x * _sigmoid(x)


def _softplus(x):
    return jnp.maximum(x, 0.0) + jnp.log1p(jnp.exp(-jnp.abs(x)))


def _params(*sem):
    return pltpu.CompilerParams(dimension_semantics=sem, vmem_limit_bytes=VMEM_LIMIT)


def _full(shape):
    n = len(shape)
    return pl.BlockSpec(shape, lambda *_: (0,) * n)


def _proj_kernel(x_ref, g_ref, w_ref, o_ref, *, col_chunk):
    h = _bf(_rms(x_ref[...], g_ref[...]))
    m = w_ref.shape[1]
    for c in range(0, m, col_chunk):
        e = min(c + col_chunk, m)
        o_ref[:, c:e] = _dot(h, w_ref[:, c:e])


def proj(x, g, w, tm):
    n, d = x.shape
    m = w.shape[1]
    return pl.pallas_call(
        functools.partial(_proj_kernel, col_chunk=512),
        grid=(n // tm,),
        in_specs=[pl.BlockSpec((tm, d), lambda i: (i, 0)), _full((1, d)), _full((d, m))],
        out_specs=pl.BlockSpec((tm, m), lambda i: (i, 0)),
        out_shape=jax.ShapeDtypeStruct((n, m), F32),
        compiler_params=_params("parallel"),
        name="proj",
    )(x, g.reshape(1, d), w)


def _memkv_kernel(x_ref, g_ref, wk_ref, wv_ref, kg_ref, k_ref, v_ref):
    h = _bf(_rms(x_ref[...], g_ref[0]))
    k = _dot(h, wk_ref[0])
    v_ref[0] = _dot(h, wv_ref[0])
    kg = kg_ref[0]
    for hd in range(MEM_HEADS):
        sl = slice(hd * MEM_HEAD_DIM, (hd + 1) * MEM_HEAD_DIM)
        k_ref[0, :, sl] = _rms(k[:, sl], kg)


def mem_kv(mem2d, mem_norm, w_mk, w_mv, mk_gain, tm=256):
    n, d = mem2d.shape
    depth = w_mk.shape[0]
    mw = w_mk.shape[2]
    return pl.pallas_call(
        _memkv_kernel,
        grid=(depth, n // tm),
        in_specs=[pl.BlockSpec((tm, d), lambda l, i: (i, 0)),
                  pl.BlockSpec((1, 1, d), lambda l, i: (l, 0, 0)),
                  pl.BlockSpec((1, d, mw), lambda l, i: (l, 0, 0)),
                  pl.BlockSpec((1, d, mw), lambda l, i: (l, 0, 0)),
                  pl.BlockSpec((1, 1, MEM_HEAD_DIM), lambda l, i: (l, 0, 0))],
        out_specs=[pl.BlockSpec((1, tm, mw), lambda l, i: (l, i, 0)),
                   pl.BlockSpec((1, tm, mw), lambda l, i: (l, i, 0))],
        out_shape=[jax.ShapeDtypeStruct((depth, n, mw), F32)] * 2,
        compiler_params=_params("parallel", "parallel"),
        name="mem_kv",
    )(mem2d, mem_norm.reshape(depth, 1, d), _bf(w_mk), _bf(w_mv), mk_gain.reshape(depth, 1, MEM_HEAD_DIM))


def _memattn_kernel(x_ref, o_ref, wo_ref, g_ref, wq_ref, qg_ref, mk_ref, mv_ref, wmo_ref, y_ref):
    x1 = x_ref[...] + _dot(o_ref[...], wo_ref[...])
    h = _bf(_rms(x1, g_ref[...]))
    q = _dot(h, wq_ref[...])
    qg = qg_ref[...]
    scale = MEM_HEAD_DIM ** -0.5
    nsub = mk_ref.shape[0]
    rows = x1.shape[0] // nsub
    cells = [(hd, s) for hd in range(MEM_HEADS) for s in range(nsub)]
    hsl = lambda hd: slice(hd * MEM_HEAD_DIM, (hd + 1) * MEM_HEAD_DIM)
    qh = [_bf(_rms(q[:, hsl(hd)], qg)) for hd in range(MEM_HEADS)]
    logits = [_dot_nt(qh[hd][s * rows:(s + 1) * rows], _bf(mk_ref[s, :, hsl(hd)])) * scale for hd, s in cells]
    ps = [jnp.exp(l - jnp.max(l, axis=-1, keepdims=True)) for l in logits]
    ps = [_bf(p / jnp.sum(p, axis=-1, keepdims=True)) for p in ps]
    pv = [_bf(_dot(p, _bf(mv_ref[s, :, hsl(hd)]))) for p, (hd, s) in zip(ps, cells)]
    outs = []
    for hd in range(MEM_HEADS):
        subs = pv[hd * nsub:(hd + 1) * nsub]
        outs.append(subs[0] if nsub == 1 else jnp.concatenate(subs, axis=0))
    att = jnp.concatenate(outs, axis=-1)
    y_ref[...] = x1 + _dot(att, wmo_ref[...])


def mem_attend(x, o, w_out, g, w_mq, mq_gain, mk, mv, w_mo, tm, seqlen):
    n, d = x.shape
    mlen, mw = mk.shape[1], mk.shape[2]
    tiles_per_seq = max(1, seqlen // tm)
    seqs_per_tile = max(1, tm // seqlen)
    return pl.pallas_call(
        _memattn_kernel,
        grid=(n // tm,),
        in_specs=[pl.BlockSpec((tm, d), lambda i: (i, 0)),
                  pl.BlockSpec((tm, o.shape[1]), lambda i: (i, 0)),
                  _full(w_out.shape), _full((1, d)), _full(w_mq.shape), _full((1, MEM_HEAD_DIM)),
                  pl.BlockSpec((seqs_per_tile, mlen, mw), lambda i: (i // tiles_per_seq, 0, 0)),
                  pl.BlockSpec((seqs_per_tile, mlen, mw), lambda i: (i // tiles_per_seq, 0, 0)),
                  _full(w_mo.shape)],
        out_specs=pl.BlockSpec((tm, d), lambda i: (i, 0)),
        out_shape=jax.ShapeDtypeStruct((n, d), F32),
        compiler_params=_params("parallel"),
        name="mem_attend",
    )(x, o, w_out, g.reshape(1, d), w_mq, mq_gain.reshape(1, MEM_HEAD_DIM), mk, mv, w_mo)


def _ffn_kernel(x_ref, g_ref, w1_ref, w3_ref, w2_ref, y_ref, *, hid_chunk):
    x = x_ref[...]
    h = _bf(_rms(x, g_ref[...]))
    hidden = w1_ref.shape[1]
    y_ref[...] = x
    for c in range(0, hidden, hid_chunk):
        a = _dot(h, w1_ref[:, c:c + hid_chunk])
        b = _dot(h, w3_ref[:, c:c + hid_chunk])
        y_ref[...] += _dot(_bf(_silu(a) * b), w2_ref[c:c + hid_chunk, :])


def ffn(x, g, w1, w3, w2, tm):
    n, d = x.shape
    hidden = w1.shape[1]
    return pl.pallas_call(
        functools.partial(_ffn_kernel, hid_chunk=256),
        grid=(n // tm,),
        in_specs=[pl.BlockSpec((tm, d), lambda i: (i, 0)), _full((1, d)),
                  _full((d, hidden)), _full((d, hidden)), _full((hidden, d))],
        out_specs=pl.BlockSpec((tm, d), lambda i: (i, 0)),
        out_shape=jax.ShapeDtypeStruct((n, d), F32),
        compiler_params=_params("parallel"),
        name="ffn",
    )(x, g.reshape(1, d), w1, w3, w2)


def _proj_dsa_kernel(x_ref, g_ref, w_ref, qg_ref, kg_ref, kig_ref, bd_ref,
                     q_o, k_o, kb_o, v_o, vb_o, qi_o, kiwi_o):
    h = _bf(_rms(x_ref[...], g_ref[...]))
    hq = A_HEADS * A_HEAD_DIM
    bd = bd_ref[...]
    inv_hd = 1.0 / A_HEAD_DIM

    def head_norm(p, gain):
        ss = _dot_exact_rhs01(p * p, bd)
        return p * lax.rsqrt(ss * inv_hd + EPS) * gain

    step = 512
    for c in range(0, hq, step):
        pq = _dot(h, w_ref[:, c:c + step])
        pk = _dot(h, w_ref[:, hq + c:hq + c + step])
        pv = _dot(h, w_ref[:, 2 * hq + c:2 * hq + c + step])
        for j in range(0, step, LANES):
            sl = slice(c + j, c + j + LANES)
            qn = head_norm(pq[:, j:j + LANES], qg_ref[:, sl])
            q_o[:, sl] = _bf(qn * (A_HEAD_DIM ** -0.5))
            kn = head_norm(pk[:, j:j + LANES], kg_ref[:, sl])
            k_o[:, sl] = kn
            kb_o[:, sl] = _bf(kn)
        v_o[:, c:c + step] = pv
        vb_o[:, c:c + step] = _bf(pv)
    qiw = IDX_HEADS * LANES
    for c in range(0, qiw, step):
        qi_o[:, c:c + step] = _bf(_dot(h, w_ref[:, 3 * hq + c:3 * hq + c + step]))
    p = _dot(h, w_ref[:, 3 * hq + qiw:3 * hq + qiw + LANES])
    lane = lax.broadcasted_iota(I32, p.shape, 1)
    is_ki = lane < IDX_DIM
    ss = jnp.sum(jnp.where(is_ki, p * p, 0.0), axis=-1, keepdims=True)
    kin = p * lax.rsqrt(ss * (1.0 / IDX_DIM) + EPS) * kig_ref[...]
    kiwi_o[...] = jnp.where(is_ki, kin, p)


def proj_dsa(x, g, wa, q_gain, k_gain, ki_gain, tm):
    n, d = x.shape
    hq = A_HEADS * A_HEAD_DIM
    m = wa.shape[1]
    qg = jnp.tile(q_gain, A_HEADS).reshape(1, hq)
    kg = jnp.tile(k_gain, A_HEADS).reshape(1, hq)
    kig = jnp.concatenate([ki_gain, jnp.ones((LANES - IDX_DIM,), F32)]).reshape(1, LANES)
    r = jnp.arange(LANES)
    bd = _bf((r[:, None] // A_HEAD_DIM) == (r[None, :] // A_HEAD_DIM))
    row = lambda w: pl.BlockSpec((tm, w), lambda i: (i, 0))
    return pl.pallas_call(
        _proj_dsa_kernel,
        grid=(n // tm,),
        in_specs=[row(d), _full((1, d)), _full((d, m)), _full((1, hq)), _full((1, hq)),
                  _full((1, LANES)), _full((LANES, LANES))],
        out_specs=[row(hq), row(hq), row(hq), row(hq), row(hq), row(IDX_HEADS * LANES), row(LANES)],
        out_shape=[jax.ShapeDtypeStruct((n, hq), BF16), jax.ShapeDtypeStruct((n, hq), F32),
                   jax.ShapeDtypeStruct((n, hq), BF16), jax.ShapeDtypeStruct((n, hq), F32),
                   jax.ShapeDtypeStruct((n, hq), BF16),
                   jax.ShapeDtypeStruct((n, IDX_HEADS * LANES), BF16),
                   jax.ShapeDtypeStruct((n, LANES), F32)],
        compiler_params=_params("parallel"),
        name="proj_dsa",
    )(x, g.reshape(1, d), wa, qg, kg, kig, bd)


def dsa_weight(w_in):
    hq = A_HEADS * A_HEAD_DIM
    o3 = 3 * hq
    o4 = o3 + IDX_HEADS * IDX_DIM
    d = w_in.shape[0]
    wqi = w_in[:, o3:o4].reshape(d, IDX_HEADS, IDX_DIM)
    wqi = jnp.pad(wqi, ((0, 0), (0, 0), (0, LANES - IDX_DIM))).reshape(d, IDX_HEADS * LANES)
    tail = jnp.pad(w_in[:, o4:], ((0, 0), (0, LANES - (w_in.shape[1] - o4))))
    return _bf(jnp.concatenate([w_in[:, :o3], wqi, tail], axis=1))


def _sortable(s):
    b = pltpu.bitcast(s, I32)
    b = jnp.where(b == jnp.int32(-2 ** 31), 0, b)
    return jnp.where(b < 0, b ^ jnp.int32(0x7FFFFFFF), b)


def _index_mask_kernel(qi_ref, wit_ref, *rest, nref, groups, ltot, topk, causal, case_rows):
    ki_refs = rest[:nref]
    o_ref = rest[nref]
    key_ref, sel_ref, jv_ref = rest[nref + 1:]
    tq = qi_ref.shape[0]
    i = pl.program_id(1)
    lpad = sel_ref.shape[0]
    wit = wit_ref[0]
    if case_rows:
        ncase = (jnp.maximum((i + 1) * tq, topk) + case_rows - 1) // case_rows
        used_rows = ncase * case_rows
    else:
        used_rows = None

    def score_group(dst0, rows, sources):
        accs = []
        for rp, lead, src0, width, _ in sources:
            ref = ki_refs[rp]
            ki = _bf(ref[src0:src0 + rows, :] if lead is None else ref[lead, src0:src0 + rows, :])
            acc = jnp.zeros((rows, tq), F32)
            for hd in range(IDX_HEADS):
                rel = _dot_nt(ki, qi_ref[:, hd * LANES:hd * LANES + width])
                acc = acc + wit[hd:hd + 1, :] * jnp.maximum(rel, 0.0)
            accs.append(acc)
        if len(accs) == 1:
            acc = accs[0]
        else:
            lane = lax.broadcasted_iota(I32, (rows, tq), 1)
            acc = jnp.where(lane < tq // 2, accs[0], accs[1])
        s = acc * ((IDX_DIM ** -0.5) * (IDX_HEADS ** -0.5))
        if causal:
            kpos = dst0 + lax.broadcasted_iota(I32, (rows, tq), 0)
            qpos = i * tq + lax.broadcasted_iota(I32, (rows, tq), 1)
            s = jnp.where((kpos >> CHUNK_SHIFT) <= (qpos >> CHUNK_SHIFT), s, NEG)
        key_ref[dst0:dst0 + rows, :] = _sortable(s)

    for dst0, rows, sources in groups:
        if used_rows is None:
            score_group(dst0, rows, sources)
        else:
            pl.when(dst0 < used_rows)(functools.partial(score_group, dst0, rows, sources))

    def select(nrows):
        idx_bits = max(1, (nrows - 1).bit_length())

        def count(pred):
            c = pred.astype(I32).reshape(COUNT_CHAINS, nrows // COUNT_CHAINS, tq)
            return jnp.sum(jnp.sum(c, axis=1), axis=0, keepdims=True)

        def ge_count(cand):
            return count(key_ref[0:nrows, :] >= cand)

        t0 = jnp.full((1, tq), -2 ** 31, I32)
        t = jnp.where(ge_count(jnp.zeros((1, tq), I32)) >= topk, 0, t0)

        def vbody(it, t):
            cand = t + (jnp.int32(1) << (30 - it))
            return jnp.where(ge_count(cand) >= topk, cand, t)

        t = lax.fori_loop(0, 31, vbody, t)
        keys = key_ref[0:nrows, :]
        gt = keys > t
        eq = keys == t
        need = topk - count(gt)
        rowi = lax.broadcasted_iota(I32, (nrows, tq), 0)

        def jbody(it, jv):
            cand = jv + (jnp.int32(1) << (idx_bits - 1 - it))
            below = count(eq & (rowi < cand))
            return jnp.where(below < need, cand, jv)

        jv_ref[...] = jnp.full((1, tq), nrows, I32)
        has_tie = jnp.max(count(eq) - need) > 0

        @pl.when(has_tie)
        def _():
            jv_ref[...] = lax.fori_loop(0, idx_bits, jbody, jnp.zeros((1, tq), I32))

        sel = gt | (eq & (rowi <= jv_ref[...]))
        if causal:
            qpos = i * tq + lax.broadcasted_iota(I32, (nrows, tq), 1)
            sel = sel & ((rowi >> CHUNK_SHIFT) <= (qpos >> CHUNK_SHIFT))
        sel_ref[0:nrows, :] = jnp.where(sel, 0.0, NEG)
        nreal = -(-nrows // TILE)
        if nreal * TILE > nrows:
            sel_ref[nrows:nreal * TILE, :] = jnp.full((nreal * TILE - nrows, tq), NEG, F32)
        for kt in range(lpad // TILE):
            if kt < nreal:
                o_ref[0, 0, kt] = sel_ref[kt * TILE:(kt + 1) * TILE, :].T
            else:
                o_ref[0, 0, kt] = jnp.full((tq, TILE), NEG, F32)

    if case_rows:
        for k in range(ltot // case_rows):
            pl.when(ncase == k + 1)(functools.partial(select, (k + 1) * case_rows))
    else:
        select(ltot)


def index_mask(qi, wit, ki_arrays, ki_specs, groups, ltot, nstep, nqb, causal):
    lpad = -(-ltot // KTILE) * KTILE
    nt = lpad // TILE
    topk = min(TOPK_MAX, ltot // 4)
    case_rows = MASK_CASE_ROWS if (causal and ltot % MASK_CASE_ROWS == 0 and ltot > MASK_CASE_ROWS) else 0
    kern = functools.partial(_index_mask_kernel, nref=len(ki_arrays), groups=tuple(groups), ltot=ltot,
                             topk=topk, causal=causal, case_rows=case_rows)
    return pl.pallas_call(
        kern,
        grid=(nstep, nqb),
        in_specs=[pl.BlockSpec((TILE, qi.shape[1]), lambda b, i: (b * nqb + i, 0)),
                  pl.BlockSpec((1, IDX_HEADS, TILE), lambda b, i: (b * nqb + i, 0, 0))] + list(ki_specs),
        out_specs=pl.BlockSpec((1, 1, nt, TILE, TILE), lambda b, i: (b, i, 0, 0, 0)),
        out_shape=jax.ShapeDtypeStruct((nstep, nqb, nt, TILE, TILE), F32),
        scratch_shapes=[pltpu.VMEM((ltot, TILE), I32), pltpu.VMEM((lpad, TILE), F32),
                        pltpu.VMEM((1, TILE), I32)],
        compiler_params=_params("parallel", "parallel"),
        name="index_mask",
    )(qi, wit, *ki_arrays)


def _row_groups(row0, rows, make_sources):
    out = []
    for off in range(0, rows, SCORE_ROWS):
        out.append((row0 + off, min(SCORE_ROWS, rows - off), tuple(make_sources(off))))
    return out


def _attn_core(q_ref, o_ref, qs_ref, l_ref, mx_ref, acc_ref, groups, tq):
    npairs = A_HEADS // 2
    lane = lax.broadcasted_iota(I32, (1, LANES), 1)
    keep_lo = _bf((lane < A_HEAD_DIM).astype(F32))
    keep_hi = _bf((lane >= A_HEAD_DIM).astype(F32))
    for pair in range(npairs):
        qp = q_ref[:, pair * LANES:(pair + 1) * LANES]
        qs_ref[pair, 0:tq, :] = qp * keep_lo
        qs_ref[pair, tq:2 * tq, :] = qp * keep_hi
    mx_ref[...] = jnp.full(mx_ref.shape, NEG, F32)
    acc_ref[...] = jnp.zeros(acc_ref.shape, F32)
    ones = jnp.ones((KTILE, LANES), BF16)

    def over_tiles(count, body):
        if isinstance(count, int) and count == 1:
            body(0, 0)
        else:
            lax.fori_loop(0, count, body, 0)

    for count, base, k_tile, _, mask_sub, bias_sub in groups:
        def p1(kt, carry, base=base, k_tile=k_tile, mask_sub=mask_sub, bias_sub=bias_sub):
            masks = [mask_sub(kt, sub) for sub in range(KTILE // LANES)]
            for pair in range(npairs):
                sl = slice(pair * LANES, (pair + 1) * LANES)
                l = _dot_nt(qs_ref[pair], k_tile(kt, sl))
                for half in range(2):
                    rs = slice(half * tq, (half + 1) * tq)
                    mx = mx_ref[pair, rs, :]
                    for sub in range(KTILE // LANES):
                        cs = slice(sub * LANES, (sub + 1) * LANES)
                        blk = l[rs, cs] + (masks[sub] + bias_sub(kt, sub, 2 * pair + half))
                        l_ref[pair, base + kt, rs, cs] = blk
                        mx = jnp.maximum(mx, blk)
                    mx_ref[pair, rs, :] = mx
            return carry

        over_tiles(count, p1)

    for pair in range(npairs):
        m = jnp.max(mx_ref[pair], axis=-1, keepdims=True)
        mx_ref[pair] = jnp.broadcast_to(m, mx_ref.shape[1:])

    for count, base, _, v_tile, _, _ in groups:
        def p2(kt, carry, base=base, v_tile=v_tile):
            for pair in range(npairs):
                sl = slice(pair * LANES, (pair + 1) * LANES)
                m = mx_ref[pair]
                p = jnp.concatenate(
                    [jnp.exp(l_ref[pair, base + kt, :, sub * LANES:(sub + 1) * LANES] - m)
                     for sub in range(KTILE // LANES)], axis=1)
                vaug = jnp.concatenate([v_tile(kt, sl), ones], axis=1)
                acc_ref[pair] += _dot(_bf(p), vaug)
            return carry

        over_tiles(count, p2)

    lane_full = lax.broadcasted_iota(I32, (tq, LANES), 1)
    for pair in range(npairs):
        a = acc_ref[pair]
        o = a[:, 0:LANES] / a[:, LANES:2 * LANES]
        o_ref[:, pair * LANES:(pair + 1) * LANES] = _bf(jnp.where(lane_full < A_HEAD_DIM, o[0:tq], o[tq:2 * tq]))


def _attn_scratch(tq, ntiles):
    npairs = A_HEADS // 2
    return [pltpu.VMEM((npairs, 2 * tq, LANES), BF16),
            pltpu.VMEM((npairs, ntiles, 2 * tq, KTILE), F32),
            pltpu.VMEM((npairs, 2 * tq, LANES), F32),
            pltpu.VMEM((npairs, 2 * tq, 2 * LANES), F32)]


def _attn_prompt_kernel(q_ref, k_ref, v_ref, mask_ref, bias_ref, o_ref, qs_ref, l_ref, mx_ref, acc_ref):
    i = pl.program_id(1)
    per = KTILE // TILE

    def rows(kt):
        return pl.ds(pl.multiple_of(kt * KTILE, KTILE), KTILE)

    def bias_sub(kt, sub, hd):
        s = kt * per + sub
        sel = jnp.where(s == i, 2, jnp.where(s == i - 1, 1, 0))
        return bias_ref[sel, hd]

    group = (i // per + 1, 0,
             lambda kt, sl: k_ref[rows(kt), sl],
             lambda kt, sl: v_ref[rows(kt), sl],
             lambda kt, sub: mask_ref[0, 0, kt * per + sub],
             bias_sub)
    _attn_core(q_ref, o_ref, qs_ref, l_ref, mx_ref, acc_ref, [group], TILE)


def attn_prompt(q, kb, vb, mask, bias, nseq, seqlen):
    n, hq = q.shape
    nqb = seqlen // TILE
    nt = mask.shape[2]
    return pl.pallas_call(
        _attn_prompt_kernel,
        grid=(nseq, nqb),
        in_specs=[pl.BlockSpec((TILE, hq), lambda b, i: (b * nqb + i, 0)),
                  pl.BlockSpec((seqlen, hq), lambda b, i: (b, 0)),
                  pl.BlockSpec((seqlen, hq), lambda b, i: (b, 0)),
                  pl.BlockSpec((1, 1, nt, TILE, TILE), lambda b, i: (b, i, 0, 0, 0)),
                  _full(bias.shape)],
        out_specs=pl.BlockSpec((TILE, hq), lambda b, i: (b * nqb + i, 0)),
        out_shape=jax.ShapeDtypeStruct((n, hq), BF16),
        scratch_shapes=_attn_scratch(TILE, seqlen // KTILE),
        compiler_params=_params("parallel", "arbitrary"),
        name="attn_prompt",
    )(q, kb, vb, mask, bias)


def _attn_sample_kernel(q_ref, kc_ref, vc_ref, kn_ref, vn_ref, mask_ref, bias_ref, o_ref,
                        l_ref, mx_ref, acc_ref, *, nblk):
    ph = pl.program_id(1)
    b = pl.program_id(2)
    tq = q_ref.shape[0]
    per = KTILE // TILE
    tiles_per_blk = kc_ref.shape[1] // KTILE
    ncache = nblk * tiles_per_blk
    last_sub = ncache * per - 1
    new_rows = kn_ref.shape[1]
    heads = range(A_HEADS)
    flat = lambda r: r.reshape(1, r.shape[1] * A_HEADS, A_HEAD_DIM)
    kc_ref, vc_ref, kn_ref, vn_ref = flat(kc_ref), flat(vc_ref), flat(kn_ref), flat(vn_ref)

    def head_tile(ref, r0, rows, hd):
        x = _bf(ref[0, pl.ds(r0 * A_HEADS + hd, rows, stride=A_HEADS), :])
        if rows < KTILE:
            x = jnp.concatenate([x, jnp.zeros((KTILE - rows, A_HEAD_DIM), BF16)], axis=0)
        return x

    def pass1(slot, ref, r0, rows, mask_sub, bias_sub):
        ls = [_dot_nt(q_ref[:, hd * A_HEAD_DIM:(hd + 1) * A_HEAD_DIM], head_tile(ref, r0, rows, hd))
              for hd in heads]
        masks = [mask_sub(sub) for sub in range(per)]
        for hd in heads:
            mx = mx_ref[hd]
            for sub in range(per):
                cs = slice(sub * LANES, (sub + 1) * LANES)
                blk = ls[hd][:, cs] + (masks[sub] + bias_sub(sub, hd))
                l_ref[hd, slot, :, cs] = blk
                mx = jnp.maximum(mx, blk)
            mx_ref[hd] = mx

    def pass2(slot, ref, r0, rows):
        ones = jnp.ones((KTILE, LANES - A_HEAD_DIM), BF16)
        for hd in heads:
            m = mx_ref[hd]
            p = jnp.concatenate([jnp.exp(l_ref[hd, slot, :, sub * LANES:(sub + 1) * LANES] - m)
                                 for sub in range(per)], axis=1)
            vaug = jnp.concatenate([head_tile(ref, r0, rows, hd), ones], axis=1)
            acc_ref[hd] += _dot(_bf(p), vaug)

    @pl.when((ph == 0) & (b == 0))
    def _():
        mx_ref[...] = jnp.full(mx_ref.shape, NEG, F32)
        acc_ref[...] = jnp.zeros(acc_ref.shape, F32)

    @pl.when(ph == 0)
    def _():
        for t in range(tiles_per_blk):
            g = b * tiles_per_blk + t
            pass1(g, kc_ref, t * KTILE, KTILE,
                  lambda sub, g=g: mask_ref[0, 0, g * per + sub],
                  lambda sub, hd, g=g: bias_ref[jnp.where(g * per + sub == last_sub, 1, 0), hd])

    @pl.when((ph == 0) & (b == nblk - 1))
    def _():
        pass1(ncache, kn_ref, 0, new_rows,
              lambda sub: mask_ref[0, 0, ncache * per + sub],
              lambda sub, hd: bias_ref[2 if sub == 0 else 0, hd])
        for hd in heads:
            m = jnp.max(mx_ref[hd], axis=-1, keepdims=True)
            mx_ref[hd] = jnp.broadcast_to(m, mx_ref.shape[1:])

    @pl.when(ph == 1)
    def _():
        for t in range(tiles_per_blk):
            pass2(b * tiles_per_blk + t, vc_ref, t * KTILE, KTILE)

    @pl.when((ph == 1) & (b == nblk - 1))
    def _():
        pass2(ncache, vn_ref, 0, new_rows)
        for pair in range(A_HEADS // 2):
            halves = []
            for hd in (2 * pair, 2 * pair + 1):
                a = acc_ref[hd]
                halves.append(a[:, 0:A_HEAD_DIM] / a[:, A_HEAD_DIM:2 * A_HEAD_DIM])
            o_ref[:, pair * LANES:(pair + 1) * LANES] = _bf(jnp.concatenate(halves, axis=1))


def attn_sample(q, kc, vc, kn, vn, mask, bias, nseq, tq):
    n, hq = q.shape
    past = kc.shape[1]
    cblk = min(CACHE_BLOCK, past)
    nblk = past // cblk
    ncache = past // KTILE
    nt = mask.shape[2]
    per = TILE // tq
    cache_k = pl.BlockSpec((1, cblk, A_HEADS, A_HEAD_DIM),
                           lambda s, ph, b: (s, b * (1 - ph) + (nblk - 1) * ph, 0, 0))
    cache_v = pl.BlockSpec((1, cblk, A_HEADS, A_HEAD_DIM), lambda s, ph, b: (s, b * ph, 0, 0))
    fresh = pl.BlockSpec((1, tq, A_HEADS, A_HEAD_DIM), lambda s, ph, b: (s, 0, 0, 0))
    return pl.pallas_call(
        functools.partial(_attn_sample_kernel, nblk=nblk),
        grid=(nseq, 2, nblk),
        in_specs=[pl.BlockSpec((tq, hq), lambda s, ph, b: (s, 0)),
                  cache_k, cache_v, fresh, fresh,
                  pl.BlockSpec((1, 1, nt, tq, TILE), lambda s, ph, b: (s // per, 0, 0, s % per, 0)),
                  pl.BlockSpec(bias.shape, lambda s, ph, b: (0, 0, 0, 0))],
        out_specs=pl.BlockSpec((tq, hq), lambda s, ph, b: (s, 0)),
        out_shape=jax.ShapeDtypeStruct((n, hq), BF16),
        scratch_shapes=[pltpu.VMEM((A_HEADS, ncache + 1, tq, KTILE), F32),
                        pltpu.VMEM((A_HEADS, tq, LANES), F32),
                        pltpu.VMEM((A_HEADS, tq, LANES), F32)],
        compiler_params=_params("parallel", "arbitrary", "arbitrary"),
        name="attn_sample",
    )(q, kc, vc, kn, vn, mask, bias)


def _t5_bucket(rel):
    half = N_BUCKETS // 2
    max_exact = half // 2
    n = jnp.abs(rel)
    nf = jnp.maximum(n, 1).astype(F32)
    large = max_exact + (jnp.log(nf / max_exact) / math.log(MAX_DISTANCE / max_exact)
                         * (half - max_exact)).astype(I32)
    large = jnp.minimum(large, half - 1)
    return jnp.where(rel > 0, half, 0) + jnp.where(n < max_exact, n, large)


def bias_tiles(rel_bias):
    heads = rel_bias.shape[1]
    span = 2 * TILE - 1

    def toeplitz(shift):
        rel = jnp.arange(span, dtype=I32) - (TILE - 1) + shift
        tab = rel_bias.astype(F32)[_t5_bucket(rel)].T
        strip = jnp.tile(jnp.pad(tab, ((0, 0), (0, 1))), (1, TILE))[:, :TILE * span]
        return strip.reshape(heads, TILE, span)[:, :, TILE - 1:]

    far = rel_bias.astype(F32)[_t5_bucket(jnp.full((1,), -(TILE + 1), I32))]
    far = jnp.broadcast_to(far.T[:, :, None], (heads, TILE, TILE))
    return jnp.stack([far, toeplitz(-TILE), toeplitz(0)])


def dsa_core_prompt(pr, nseq, seqlen, bias):
    q, _, kb, _, vb, qi, kiwi = pr
    nqb = seqlen // TILE
    wit = kiwi[:, IDX_DIM:IDX_DIM + IDX_HEADS].reshape(nseq * nqb, TILE, IDX_HEADS).swapaxes(1, 2)
    groups = _row_groups(0, seqlen, lambda off: [(0, None, off, LANES, None)])
    mask = index_mask(qi, wit, [kiwi], [pl.BlockSpec((seqlen, LANES), lambda b, i: (b, 0))],
                      groups, seqlen, nseq, nqb, causal=True)
    return attn_prompt(q, kb, vb, mask, bias, nseq, seqlen)


def dsa_core_sample(pr, k_cache, v_cache, ki_cache, nseq, tq, bias):
    q, kf, _, vf, _, qi, kiwi = pr
    past = k_cache.shape[1]
    per = TILE // tq
    wit = kiwi[:, IDX_DIM:IDX_DIM + IDX_HEADS].reshape(nseq // per, TILE, IDX_HEADS).swapaxes(1, 2)
    groups = (_row_groups(0, past, lambda off: [(0, j, off, IDX_DIM, j) for j in range(per)])
              + _row_groups(past, tq, lambda off: [(1, None, j * tq + off, LANES, j) for j in range(per)]))
    mask = index_mask(
        qi, wit, [ki_cache, kiwi],
        [pl.BlockSpec((per, past, IDX_DIM), lambda b, i: (b, 0, 0)),
         pl.BlockSpec((TILE, LANES), lambda b, i: (b, 0))],
        groups, past + tq, nseq // per, 1, causal=False)
    kn = kf.reshape(nseq, tq, A_HEADS, A_HEAD_DIM)
    vn = vf.reshape(nseq, tq, A_HEADS, A_HEAD_DIM)
    return attn_sample(q, k_cache, v_cache, kn, vn, mask, bias[:, :, :tq, :], nseq, tq)


def _log_sigmoid(x):
    return jnp.minimum(x, 0.0) - jnp.log1p(jnp.exp(-jnp.abs(x)))


def _mlstm_kernel(p_ref, gs_ref, gr_ref, gbc_ref, gbr_ref, hg_ref, c0_ref, n0_ref, m0_ref,
                  hs_ref, c_ref, n_ref, m_ref, cs, ns, ms):
    c_id = pl.program_id(1)
    nh = B_HEADS
    npair = nh // 2
    hw = LANES
    L = p_ref.shape[0]
    L2 = 2 * L

    @pl.when(c_id == 0)
    def _():
        cs[...] = jnp.zeros(cs.shape, F32)
        ns[...] = jnp.zeros(ns.shape, F32)
        ns[:, 0:B_QK_DIM] = n0_ref[0]
        for hd in range(nh):
            cs[hd // 2, (hd % 2) * hw:(hd % 2) * hw + B_QK_DIM, :] = c0_ref[0, hd]
        ms[...] = m0_ref[0]

    r = lax.broadcasted_iota(I32, (L2, L2), 0)
    c = lax.broadcasted_iota(I32, (L2, L2), 1)
    same = (r >> CHUNK_SHIFT) == (c >> CHUNK_SHIFT)
    incl = same & (c <= r)
    gs = gs_ref[0] + gbc_ref[...]
    gr = gr_ref[0] + gbr_ref[...]
    bcol = _dot_exact_lhs01(_bf01(incl), _log_sigmoid(gs[:, npair:]))
    brow = _dot_exact_rhs01(_log_sigmoid(gr[npair:, :]), _bf01(same & (r <= c)))
    icol = gs[:, :npair]
    irow = gr[:npair, :]
    rowc = lax.broadcasted_iota(I32, (L2, 1), 0)
    is_top = rowc < L
    top = is_top.astype(F32)
    bot = 1.0 - top
    row2 = lax.broadcasted_iota(I32, (2 * hw, 1), 0)
    hg = hg_ref[...]
    mall = ms[...]

    def stack(base, pr):
        a = base + 2 * pr * hw
        return jnp.concatenate([p_ref[:, a:a + hw], p_ref[:, a + hw:a + 2 * hw]], axis=0)

    def bd(x):
        return _bf(jnp.concatenate([x * top, x * bot], axis=1))

    pairs = range(npair)
    qf = [stack(0, pr) for pr in pairs]
    kf = [stack(nh * hw, pr) * (B_QK_DIM ** -0.5) for pr in pairs]
    vb = [_bf(stack(2 * nh * hw, pr)) for pr in pairs]
    bc = [bcol[:, pr:pr + 1] for pr in pairs]
    ic = [icol[:, pr:pr + 1] for pr in pairs]
    mc = [mall[:, pr:pr + 1] for pr in pairs]
    d = [jnp.where(incl, bc[pr] - brow[pr:pr + 1, :] + irow[pr:pr + 1, :], NEG) for pr in pairs]
    inter = [bc[pr] + mc[pr] for pr in pairs]
    mt = [jnp.maximum(inter[pr], jnp.max(d[pr], axis=-1, keepdims=True)) for pr in pairs]
    s = [_dot_nt(_bf(qf[pr]), _bf(kf[pr])) * jnp.exp(d[pr] - mt[pr]) for pr in pairs]
    wst = [jnp.exp(inter[pr] - mt[pr]) for pr in pairs]
    cmat = [cs[pr] for pr in pairs]
    num = [_dot(_bf(s[pr]), vb[pr]) + wst[pr] * _dot(bd(qf[pr]), _bf(cmat[pr])) for pr in pairs]
    for pr in pairs:
        n0 = ns[2 * pr:2 * pr + 1, :]
        n1 = ns[2 * pr + 1:2 * pr + 2, :]
        qn = jnp.sum(qf[pr] * jnp.where(is_top, n0, n1), axis=-1, keepdims=True)
        den = jnp.sum(s[pr], axis=-1, keepdims=True) + wst[pr] * qn
        hs = num[pr] / jnp.maximum(jnp.abs(den), jnp.exp(-mt[pr]))
        bl0 = bc[pr][L - 1:L, :]
        bl1 = bc[pr][L2 - 1:L2, :]
        bl = jnp.where(is_top, bl0, bl1)
        dec = bl - bc[pr] + ic[pr]
        blm = bl + mc[pr]
        mnew0 = jnp.maximum(blm[0:1, :], jnp.max(dec[0:L], axis=0, keepdims=True))
        mnew1 = jnp.maximum(blm[L:L + 1, :], jnp.max(dec[L:L2], axis=0, keepdims=True))
        mnew = jnp.where(is_top, mnew0, mnew1)
        wk = jnp.exp(dec - mnew)
        ws = jnp.exp(blm - mnew)
        kw = kf[pr] * wk
        ws2 = jnp.where(row2 < hw, ws[0:1, :], ws[L:L + 1, :])
        cs[pr] = ws2 * cmat[pr] + _dot_tn(bd(kw), vb[pr])
        ns[2 * pr:2 * pr + 1, :] = ws[0:1, :] * n0 + jnp.sum(kw[0:L], axis=0, keepdims=True)
        ns[2 * pr + 1:2 * pr + 2, :] = ws[L:L + 1, :] * n1 + jnp.sum(kw[L:L2], axis=0, keepdims=True)
        ms[:, pr:pr + 1] = mnew
        on = _bf(_rms(hs, hg) * _sigmoid(stack(3 * nh * hw, pr)))
        oa = 2 * pr * B_V_DIM
        hs_ref[:, oa:oa + B_V_DIM] = on[0:L]
        hs_ref[:, oa + B_V_DIM:oa + 2 * B_V_DIM] = on[L:L2]

    @pl.when(c_id == pl.num_programs(1) - 1)
    def _():
        for hd in range(nh):
            c_ref[0, hd] = cs[hd // 2, (hd % 2) * hw:(hd % 2) * hw + B_QK_DIM, :]
        n_ref[0] = ns[:, 0:B_QK_DIM]
        m_ref[0] = ms[...]


def mlstm_weight(w_in):
    d = w_in.shape[0]
    nq = B_HEADS * B_QK_DIM
    nv = B_HEADS * B_V_DIM

    def padh(w):
        w = w.reshape(d, B_HEADS, B_QK_DIM)
        return jnp.pad(w, ((0, 0), (0, 0), (0, LANES - B_QK_DIM))).reshape(d, B_HEADS * LANES)

    main = 2 * nq + 2 * nv
    tail = jnp.pad(w_in[:, main:], ((0, 0), (0, LANES - 2 * B_HEADS)))
    return _bf(jnp.concatenate([padh(w_in[:, :nq]), padh(w_in[:, nq:2 * nq]), w_in[:, 2 * nq:main], tail], axis=1))


def mlstm_core(p, gate_bias, h_gain, c0, n0, m0, nseq, nchunk):
    n, m = p.shape
    nh = B_HEADS
    npair = nh // 2
    gcol = 4 * nh * LANES
    pre = p[:, gcol:gcol + 2 * nh].reshape(nseq * nchunk, CHUNK, 2, npair, 2)
    gs = pre.transpose(0, 4, 1, 2, 3).reshape(nseq * nchunk, 2 * CHUNK, 2 * npair)
    gr = pre.transpose(0, 2, 3, 4, 1).reshape(nseq * nchunk, 2 * npair, 2 * CHUNK)
    gb = gate_bias.reshape(2, npair, 2)
    gbc = jnp.repeat(gb.transpose(2, 0, 1).reshape(2, 2 * npair), CHUNK, axis=0)
    m0s = jnp.repeat(m0.reshape(nseq, npair, 2).transpose(0, 2, 1), CHUNK, axis=1)
    o, c_new, n_new, m_new = pl.pallas_call(
        _mlstm_kernel,
        grid=(nseq, nchunk),
        in_specs=[pl.BlockSpec((CHUNK, m), lambda s, c: (s * nchunk + c, 0)),
                  pl.BlockSpec((1, 2 * CHUNK, 2 * npair), lambda s, c: (s * nchunk + c, 0, 0)),
                  pl.BlockSpec((1, 2 * npair, 2 * CHUNK), lambda s, c: (s * nchunk + c, 0, 0)),
                  _full((2 * CHUNK, 2 * npair)), _full((2 * npair, 2 * CHUNK)), _full((1, B_V_DIM)),
                  pl.BlockSpec((1, nh, B_QK_DIM, B_V_DIM), lambda s, c: (s, 0, 0, 0)),
                  pl.BlockSpec((1, nh, B_QK_DIM), lambda s, c: (s, 0, 0)),
                  pl.BlockSpec((1, 2 * CHUNK, npair), lambda s, c: (s, 0, 0))],
        out_specs=[pl.BlockSpec((CHUNK, nh * B_V_DIM), lambda s, c: (s * nchunk + c, 0)),
                   pl.BlockSpec((1, nh, B_QK_DIM, B_V_DIM), lambda s, c: (s, 0, 0, 0)),
                   pl.BlockSpec((1, nh, B_QK_DIM), lambda s, c: (s, 0, 0)),
                   pl.BlockSpec((1, 2 * CHUNK, npair), lambda s, c: (s, 0, 0))],
        out_shape=[jax.ShapeDtypeStruct((n, nh * B_V_DIM), BF16),
                   jax.ShapeDtypeStruct((nseq, nh, B_QK_DIM, B_V_DIM), F32),
                   jax.ShapeDtypeStruct((nseq, nh, B_QK_DIM), F32),
                   jax.ShapeDtypeStruct((nseq, 2 * CHUNK, npair), F32)],
        scratch_shapes=[pltpu.VMEM((npair, 2 * LANES, B_V_DIM), F32), pltpu.VMEM((nh, LANES), F32),
                        pltpu.VMEM((2 * CHUNK, npair), F32)],
        compiler_params=_params("parallel", "arbitrary"),
        name="mlstm",
    )(p, gs, gr, gbc, gbc.T, h_gain.reshape(1, B_V_DIM), c0, n0, m0s)
    m_heads = m_new[:, ::CHUNK, :].transpose(0, 2, 1).reshape(nseq, nh)
    return o, c_new, n_new, m_heads


def _split2(x):
    hi = _bf(x)
    return hi, _bf(x - hi.astype(F32))


def _cat3_lhs(x):
    hi, mid = _split2(x)
    return jnp.concatenate([hi, hi, mid], axis=1)


def _cat3_rhs(x):
    hi, mid = _split2(x)
    return jnp.concatenate([hi, mid, hi], axis=0)


def _gdn_kernel(p_ref, gs_ref, gr_ref, cw_ref, alc_ref, alr_ref, dtc_ref, dtr_ref, og_ref, s0_ref, cb0_ref,
                o_ref, s_ref, cb_ref, ss, tail):
    c_id = pl.program_id(1)
    nh = C_HEADS
    cdim = nh * (2 * C_DK + C_DV)
    L = p_ref.shape[0]
    L2 = 2 * L
    nprev = CONV_W - 1

    @pl.when(c_id == 0)
    def _():
        ss[...] = s0_ref[0]
        tail[...] = jnp.zeros(tail.shape, F32)
        tail[8 - nprev:8, :] = cb0_ref[0]

    x = p_ref[:, 0:cdim]
    ext = jnp.concatenate([tail[...], x], axis=0)
    conv = ext[8:8 + L] * cw_ref[CONV_W - 1:CONV_W, :]
    for j in range(CONV_W - 1):
        conv = conv + ext[8 - nprev + j:8 - nprev + j + L] * cw_ref[j:j + 1, :]
    tail[...] = x[L - 8:L, :]
    cf = _silu(conv)

    npair = nh // 2
    gs = gs_ref[0]
    gr = gr_ref[0]
    beta = _sigmoid(gs[:, 0:npair])
    g_col = -jnp.exp(alc_ref[...]) * _softplus(gs[:, npair:] + dtc_ref[...])
    g_row = -jnp.exp(alr_ref[...]) * _softplus(gr[npair:, :] + dtr_ref[...])
    r = lax.broadcasted_iota(I32, (L2, L2), 0)
    c = lax.broadcasted_iota(I32, (L2, L2), 1)
    same = (r >> CHUNK_SHIFT) == (c >> CHUNK_SHIFT)
    incl = same & (c <= r)
    strict = same & (c < r)
    eye = (c == r).astype(F32)
    gc_col = _dot_exact_lhs01(_bf01(incl), g_col)
    gc_row = _dot_exact_rhs01(g_row, _bf01(same & (r <= c)))
    rowc = lax.broadcasted_iota(I32, (L2, 1), 0)
    top = (rowc < L).astype(F32)
    bot = 1.0 - top
    row2 = lax.broadcasted_iota(I32, (2 * C_DK, 1), 0)
    og = og_ref[...]

    def stack(base, pr):
        a = base + 2 * pr * C_DK
        return jnp.concatenate([cf[:, a:a + C_DK], cf[:, a + C_DK:a + 2 * C_DK]], axis=0)

    def bd(x):
        return _bf(jnp.concatenate([x * top, x * bot], axis=1))

    pairs = range(npair)
    qc, kc, kcb, dm, amat, rhs, gcols, egcs = [], [], [], [], [], [], [], []
    for pr in pairs:
        qraw = stack(0, pr)
        kraw = stack(nh * C_DK, pr)
        vc = stack(2 * nh * C_DK, pr)
        qc.append(qraw * lax.rsqrt(jnp.sum(qraw * qraw, axis=-1, keepdims=True) + EPS) * (C_DK ** -0.5))
        kc.append(kraw * lax.rsqrt(jnp.sum(kraw * kraw, axis=-1, keepdims=True) + EPS))
        bc = beta[:, pr:pr + 1]
        gcol = gc_col[:, pr:pr + 1]
        grow = gc_row[pr:pr + 1, :]
        dm.append(jnp.where(incl, jnp.exp(jnp.where(incl, gcol - grow, 0.0)), 0.0))
        kb = kc[pr] * bc
        kcb.append(_bf(kc[pr]))
        amat.append(jnp.where(strict, _dot_nt(_bf(kb), kcb[pr]) * dm[pr], 0.0))
        egc = jnp.exp(gcol)
        rhs.append(jnp.concatenate([vc * bc, kb * egc], axis=-1))
        gcols.append(gcol)
        egcs.append(egc)
    tinv = [eye - amat[pr] for pr in pairs]
    pw_l = [_cat3_lhs(-amat[pr]) for pr in pairs]
    pw_r = [_cat3_rhs(-amat[pr]) for pr in pairs]
    for _ in range(CHUNK_SHIFT - 1):
        pw = [_dot(pw_l[pr], pw_r[pr]) for pr in pairs]
        pw_l = [_cat3_lhs(pw[pr]) for pr in pairs]
        pw_r = [_cat3_rhs(pw[pr]) for pr in pairs]
        tinv = [tinv[pr] + _dot(_cat3_lhs(tinv[pr]), pw_r[pr]) for pr in pairs]
    sol = [_dot(_cat3_lhs(tinv[pr]), _cat3_rhs(rhs[pr])) for pr in pairs]
    attn = [_dot_nt(_bf(qc[pr]), kcb[pr]) * dm[pr] for pr in pairs]
    smat = [ss[pr] for pr in pairs]
    sb = [_bf(smat[pr]) for pr in pairs]
    vnew = [sol[pr][:, :C_DV] - _dot(bd(sol[pr][:, C_DV:]), sb[pr]) for pr in pairs]
    o = [_dot(bd(qc[pr] * egcs[pr]), sb[pr]) + _dot(_bf(attn[pr]), _bf(vnew[pr])) for pr in pairs]
    for pr in pairs:
        gl0 = gcols[pr][L - 1:L, :]
        gl1 = gcols[pr][L2 - 1:L2, :]
        ke = kc[pr] * jnp.exp(jnp.where(rowc < L, gl0, gl1) - gcols[pr])
        decay = jnp.exp(jnp.where(row2 < C_DK, gl0, gl1))
        ss[pr] = smat[pr] * decay + _dot_tn(bd(ke), _bf(vnew[pr]))
    for pr in pairs:
        za = cdim + 2 * pr * C_DV
        z = jnp.concatenate([p_ref[:, za:za + C_DV], p_ref[:, za + C_DV:za + 2 * C_DV]], axis=0)
        on = _bf(_rms(o[pr], og) * _silu(z))
        oa = 2 * pr * C_DV
        o_ref[:, oa:oa + C_DV] = on[0:L]
        o_ref[:, oa + C_DV:oa + 2 * C_DV] = on[L:L2]

    @pl.when(c_id == pl.num_programs(1) - 1)
    def _():
        s_ref[0] = ss[...]
        cb_ref[0] = tail[8 - nprev:8, :]


def gdn_weight(w_in):
    main = C_HEADS * (2 * C_DK + C_DV) + C_HEADS * C_DV
    tail = jnp.pad(w_in[:, main:], ((0, 0), (0, LANES - 2 * C_HEADS)))
    return _bf(jnp.concatenate([w_in[:, :main], tail], axis=1))


def gdn_core(p, conv_w, a_log, dt_bias, o_gain, s0, cb0, nseq, nchunk):
    n, m = p.shape
    nh = C_HEADS
    npair = nh // 2
    cdim = nh * (2 * C_DK + C_DV)
    gcolumn = cdim + nh * C_DV
    pre = p[:, gcolumn:gcolumn + 2 * nh].reshape(nseq * nchunk, CHUNK, 2, npair, 2)
    gs = pre.transpose(0, 4, 1, 2, 3).reshape(nseq * nchunk, 2 * CHUNK, 2 * npair)
    gr = pre.transpose(0, 2, 3, 4, 1).reshape(nseq * nchunk, 2 * npair, 2 * CHUNK)

    def col(v):
        return jnp.repeat(v.reshape(npair, 2).T, CHUNK, axis=0)

    o, s_new, cb_new = pl.pallas_call(
        _gdn_kernel,
        grid=(nseq, nchunk),
        in_specs=[pl.BlockSpec((CHUNK, m), lambda s, c: (s * nchunk + c, 0)),
                  pl.BlockSpec((1, 2 * CHUNK, 2 * npair), lambda s, c: (s * nchunk + c, 0, 0)),
                  pl.BlockSpec((1, 2 * npair, 2 * CHUNK), lambda s, c: (s * nchunk + c, 0, 0)),
                  _full((CONV_W, cdim)),
                  _full((2 * CHUNK, npair)), _full((npair, 2 * CHUNK)),
                  _full((2 * CHUNK, npair)), _full((npair, 2 * CHUNK)),
                  _full((1, C_DV)),
                  pl.BlockSpec((1, npair, 2 * C_DK, C_DV), lambda s, c: (s, 0, 0, 0)),
                  pl.BlockSpec((1, CONV_W - 1, cdim), lambda s, c: (s, 0, 0))],
        out_specs=[pl.BlockSpec((CHUNK, nh * C_DV), lambda s, c: (s * nchunk + c, 0)),
                   pl.BlockSpec((1, npair, 2 * C_DK, C_DV), lambda s, c: (s, 0, 0, 0)),
                   pl.BlockSpec((1, CONV_W - 1, cdim), lambda s, c: (s, 0, 0))],
        out_shape=[jax.ShapeDtypeStruct((n, nh * C_DV), BF16),
                   jax.ShapeDtypeStruct((nseq, npair, 2 * C_DK, C_DV), F32),
                   jax.ShapeDtypeStruct((nseq, CONV_W - 1, cdim), F32)],
        scratch_shapes=[pltpu.VMEM((npair, 2 * C_DK, C_DV), F32), pltpu.VMEM((8, cdim), F32)],
        compiler_params=_params("parallel", "arbitrary"),
        name="gdn",
    )(p, gs, gr, conv_w, col(a_log), col(a_log).T, col(dt_bias), col(dt_bias).T, o_gain.reshape(1, C_DV),
      s0.reshape(nseq, npair, 2 * C_DK, C_DV), cb0)
    return o, s_new.reshape(nseq, nh, C_DK, C_DV), cb_new


def _trunk(x, nseq, seqlen, mem_k, mem_v, st, W, bias, is_prompt):
    d = x.shape[-1]
    n = nseq * seqlen
    x = x.reshape(n, d)
    tm = min(256, n)
    tm_mem = min(MEM_ROWS, n)
    new = {}
    for i in range(4):
        kind = i % 3
        mx = W["mixer"][i]
        if kind == 0:
            pr = proj_dsa(x, W["norm_mix"][i], mx["w_in"], mx["q_gain"], mx["k_gain"], mx["ki_gain"], tm)
            if is_prompt:
                o = dsa_core_prompt(pr, nseq, seqlen, bias)
            else:
                o = dsa_core_sample(pr, *st[i], nseq, seqlen, bias)
            new[i] = (pr[1].reshape(nseq, seqlen, A_HEADS, A_HEAD_DIM),
                      pr[3].reshape(nseq, seqlen, A_HEADS, A_HEAD_DIM),
                      pr[6][:, :IDX_DIM].reshape(nseq, seqlen, IDX_DIM))
        elif kind == 1:
            p = proj(x, W["norm_mix"][i], mx["w_in"], tm)
            o, c_new, n_new, m_new = mlstm_core(p, mx["gate_bias"], mx["h_gain"], *st[i], nseq, seqlen // CHUNK)
            new[i] = (c_new, n_new, m_new.reshape(nseq, B_HEADS))
        else:
            p = proj(x, W["norm_mix"][i], mx["w_in"], tm)
            o, s_new, cb_new = gdn_core(p, mx["conv_w"], mx["a_log"], mx["dt_bias"], mx["o_gain"], *st[i],
                                        nseq, seqlen // CHUNK)
            new[i] = (s_new, cb_new)
        x = mem_attend(x, o, mx["w_out"], W["norm_mem"][i], W["w_mq"][i], W["mq_gain"][i],
                       mem_k[i], mem_v[i], W["w_mo"][i], tm_mem, seqlen)
        x = ffn(x, W["norm_ffn"][i], W["w_ffn1"][i], W["w_ffn3"][i], W["w_ffn2"][i], min(FFN_ROWS, n))
    return x.reshape(nseq, seqlen, d), new


def kernel(x_prompt, x_sample, mem_prompt, cache_l0_k, cache_l0_v, cache_l0_kidx, state_l1_C, state_l1_n, state_l1_m, state_l2_S, state_l2_conv, cache_l3_k, cache_l3_v, cache_l3_kidx, cache_mem_k, cache_mem_v, rel_bias, norm_mix, norm_mem, norm_ffn, mem_norm, w_mq, w_mk, w_mv, w_mo, mq_gain, mk_gain, w_ffn1, w_ffn3, w_ffn2, a0_w_in, a0_w_out, a0_q_gain, a0_k_gain, a0_kidx_gain, b1_w_in, b1_gate_bias, b1_h_gain, b1_w_out, c2_w_in, c2_conv_w, c2_a_log, c2_dt_bias, c2_o_gain, c2_w_out, a3_w_in, a3_w_out, a3_q_gain, a3_k_gain, a3_kidx_gain):
    B, T, D = x_prompt.shape
    S, Ts, _ = x_sample.shape
    depth = w_mq.shape[0]
    mlen = mem_prompt.shape[1]
    mw = MEM_HEADS * MEM_HEAD_DIM

    def dsa_w(w_in, w_out, qg, kg, kig):
        return dict(w_in=dsa_weight(w_in), w_out=_bf(w_out), q_gain=qg, k_gain=kg, ki_gain=kig)

    W = dict(
        norm_mix=norm_mix, norm_mem=norm_mem, norm_ffn=norm_ffn,
        w_mq=_bf(w_mq), w_mo=_bf(w_mo), mq_gain=mq_gain,
        w_ffn1=_bf(w_ffn1), w_ffn3=_bf(w_ffn3), w_ffn2=_bf(w_ffn2),
        mixer={
            0: dsa_w(a0_w_in, a0_w_out, a0_q_gain, a0_k_gain, a0_kidx_gain),
            1: dict(w_in=mlstm_weight(b1_w_in), gate_bias=b1_gate_bias, h_gain=b1_h_gain, w_out=_bf(b1_w_out)),
            2: dict(w_in=gdn_weight(c2_w_in), conv_w=c2_conv_w, a_log=c2_a_log, dt_bias=c2_dt_bias,
                    o_gain=c2_o_gain, w_out=_bf(c2_w_out)),
            3: dsa_w(a3_w_in, a3_w_out, a3_q_gain, a3_k_gain, a3_kidx_gain),
        },
    )
    bias = bias_tiles(rel_bias)

    mk_p, mv_p = mem_kv(mem_prompt.reshape(B * mlen, D), mem_norm, w_mk, w_mv, mk_gain)
    mk_p = mk_p.reshape(depth, B, mlen, mw)
    mv_p = mv_p.reshape(depth, B, mlen, mw)
    st_p = {
        0: None,
        1: (jnp.zeros((B, B_HEADS, B_QK_DIM, B_V_DIM), F32), jnp.zeros((B, B_HEADS, B_QK_DIM), F32),
            jnp.full((B, B_HEADS), NEG, F32)),
        2: (jnp.zeros((B, C_HEADS, C_DK, C_DV), F32), jnp.zeros((B, CONV_W - 1, state_l2_conv.shape[-1]), F32)),
        3: None,
    }
    y_p, np_ = _trunk(x_prompt, B, T, mk_p, mv_p, st_p, W, bias, True)

    st_s = {
        0: (cache_l0_k, cache_l0_v, cache_l0_kidx),
        1: (state_l1_C, state_l1_n, state_l1_m),
        2: (state_l2_S, state_l2_conv),
        3: (cache_l3_k, cache_l3_v, cache_l3_kidx),
    }
    mk_s = cache_mem_k.reshape(depth, S, mlen, mw)
    mv_s = cache_mem_v.reshape(depth, S, mlen, mw)
    y_s, ns_ = _trunk(x_sample, S, Ts, mk_s, mv_s, st_s, W, bias, False)

    shp = (depth, B, mlen, MEM_HEADS, MEM_HEAD_DIM)
    return (y_p, y_s,
            *np_[0], *np_[1], *np_[2], *np_[3], mk_p.reshape(shp), mv_p.reshape(shp),
            *ns_[0], *ns_[1], *ns_[2], *ns_[3])
```

```python
import functools
import math

import jax
import jax.numpy as jnp
from jax import lax
from jax.experimental import pallas as pl
from jax.experimental.pallas import tpu as pltpu

F32 = jnp.float32
BF16 = jnp.bfloat16
I32 = jnp.int32

EPS = 1e-6
NEG = -1e30
CHUNK = 64
CHUNK_SHIFT = 6
LANES = 128
TILE = 128
KTILE = 256
COUNT_CHAINS = 8
SCORE_ROWS = 256
FFN_ROWS = 512
MEM_ROWS = 256
CACHE_BLOCK = 1024
MASK_CASE_ROWS = 512
VMEM_LIMIT = 56 * 1024 * 1024

A_HEADS, A_HEAD_DIM = 16, 64
IDX_HEADS, IDX_DIM = 8, 64
TOPK_MAX = 256
N_BUCKETS, MAX_DISTANCE = 32, 128
B_HEADS, B_QK_DIM, B_V_DIM = 8, 64, 128
C_HEADS, C_DK, C_DV = 8, 128, 128
CONV_W = 4
MEM_HEADS, MEM_HEAD_DIM = 4, 128


def _bf(x):
    return x.astype(BF16)


def _bf01(mask):
    return mask.astype(F32).astype(BF16)


def _dot(a, b):
    return jnp.dot(a, b, preferred_element_type=F32)


def _dot_nt(a, b):
    return lax.dot_general(a, b, (((1,), (1,)), ((), ())), preferred_element_type=F32)


def _dot_tn(a, b):
    return lax.dot_general(a, b, (((0,), (0,)), ((), ())), preferred_element_type=F32)


def _split3(x):
    hi = _bf(x)
    r1 = x - hi.astype(F32)
    mid = _bf(r1)
    lo = _bf(r1 - mid.astype(F32))
    return hi, mid, lo


def _dot_exact_rhs01(x, m01):
    hi, mid, lo = _split3(x)
    return _dot(hi, m01) + _dot(mid, m01) + _dot(lo, m01)


def _dot_exact_lhs01(m01, x):
    hi, mid, lo = _split3(x)
    return _dot(m01, hi) + _dot(m01, mid) + _dot(m01, lo)


def _dot_f32(a, b):
    ah, am, al = _split3(a)
    bh, bm, bl = _split3(b)
    return (_dot(ah, bh) + (_dot(ah, bm) + _dot(am, bh))
            + (_dot(am, bm) + _dot(ah, bl) + _dot(al, bh)))


def _rms(x, g):
    ms = jnp.mean(x * x, axis=-1, keepdims=True)
    return x * lax.rsqrt(ms + EPS) * g


def _sigmoid(x):
    return 1.0 / (1.0 + jnp.exp(-x))


def _silu(x):
    return x * _sigmoid(x)


def _softplus(x):
    return jnp.maximum(x, 0.0) + jnp.log1p(jnp.exp(-jnp.abs(x)))


def _params(*sem):
    return pltpu.CompilerParams(dimension_semantics=sem, vmem_limit_bytes=VMEM_LIMIT)


def _full(shape):
    n = len(shape)
    return pl.BlockSpec(shape, lambda *_: (0,) * n)


def _proj_kernel(x_ref, g_ref, w_ref, o_ref, *, col_chunk):
    h = _bf(_rms(x_ref[...], g_ref[...]))
    m = w_ref.shape[1]
    for c in range(0, m, col_chunk):
        e = min(c + col_chunk, m)
        o_ref[:, c:e] = _dot(h, w_ref[:, c:e])


def proj(x, g, w, tm):
    n, d = x.shape
    m = w.shape[1]
    return pl.pallas_call(
        functools.partial(_proj_kernel, col_chunk=512),
        grid=(n // tm,),
        in_specs=[pl.BlockSpec((tm, d), lambda i: (i, 0)), _full((1, d)), _full((d, m))],
        out_specs=pl.BlockSpec((tm, m), lambda i: (i, 0)),
        out_shape=jax.ShapeDtypeStruct((n, m), F32),
        compiler_params=_params("parallel"),
        name="proj",
    )(x, g.reshape(1, d), w)


def _memkv_kernel(x_ref, g_ref, wk_ref, wv_ref, kg_ref, k_ref, v_ref):
    h = _bf(_rms(x_ref[...], g_ref[0]))
    k = _dot(h, wk_ref[0])
    v_ref[0] = _dot(h, wv_ref[0])
    kg = kg_ref[0]
    for hd in range(MEM_HEADS):
        sl = slice(hd * MEM_HEAD_DIM, (hd + 1) * MEM_HEAD_DIM)
        k_ref[0, :, sl] = _rms(k[:, sl], kg)


def mem_kv(mem2d, mem_norm, w_mk, w_mv, mk_gain, tm=256):
    n, d = mem2d.shape
    depth = w_mk.shape[0]
    mw = w_mk.shape[2]
    return pl.pallas_call(
        _memkv_kernel,
        grid=(depth, n // tm),
        in_specs=[pl.BlockSpec((tm, d), lambda l, i: (i, 0)),
                  pl.BlockSpec((1, 1, d), lambda l, i: (l, 0, 0)),
                  pl.BlockSpec((1, d, mw), lambda l, i: (l, 0, 0)),
                  pl.BlockSpec((1, d, mw), lambda l, i: (l, 0, 0)),
                  pl.BlockSpec((1, 1, MEM_HEAD_DIM), lambda l, i: (l, 0, 0))],
        out_specs=[pl.BlockSpec((1, tm, mw), lambda l, i: (l, i, 0)),
                   pl.BlockSpec((1, tm, mw), lambda l, i: (l, i, 0))],
        out_shape=[jax.ShapeDtypeStruct((depth, n, mw), F32)] * 2,
        compiler_params=_params("parallel", "parallel"),
        name="mem_kv",
    )(mem2d, mem_norm.reshape(depth, 1, d), _bf(w_mk), _bf(w_mv), mk_gain.reshape(depth, 1, MEM_HEAD_DIM))


def _memattn_kernel(x_ref, o_ref, wo_ref, g_ref, wq_ref, qg_ref, mk_ref, mv_ref, wmo_ref, y_ref):
    x1 = x_ref[...] + _dot(o_ref[...], wo_ref[...])
    h = _bf(_rms(x1, g_ref[...]))
    q = _dot(h, wq_ref[...])
    qg = qg_ref[...]
    scale = MEM_HEAD_DIM ** -0.5
    nsub = mk_ref.shape[0]
    rows = x1.shape[0] // nsub
    cells = [(hd, s) for hd in range(MEM_HEADS) for s in range(nsub)]
    hsl = lambda hd: slice(hd * MEM_HEAD_DIM, (hd + 1) * MEM_HEAD_DIM)
    qh = [_bf(_rms(q[:, hsl(hd)], qg)) for hd in range(MEM_HEADS)]
    logits = [_dot_nt(qh[hd][s * rows:(s + 1) * rows], _bf(mk_ref[s, :, hsl(hd)])) * scale for hd, s in cells]
    ps = [jnp.exp(l - jnp.max(l, axis=-1, keepdims=True)) for l in logits]
    ps = [_bf(p / jnp.sum(p, axis=-1, keepdims=True)) for p in ps]
    pv = [_bf(_dot(p, _bf(mv_ref[s, :, hsl(hd)]))) for p, (hd, s) in zip(ps, cells)]
    outs = []
    for hd in range(MEM_HEADS):
        subs = pv[hd * nsub:(hd + 1) * nsub]
        outs.append(subs[0] if nsub == 1 else jnp.concatenate(subs, axis=0))
    att = jnp.concatenate(outs, axis=-1)
    y_ref[...] = x1 + _dot(att, wmo_ref[...])


def mem_attend(x, o, w_out, g, w_mq, mq_gain, mk, mv, w_mo, tm, seqlen):
    n, d = x.shape
    mlen, mw = mk.shape[1], mk.shape[2]
    tiles_per_seq = max(1, seqlen // tm)
    seqs_per_tile = max(1, tm // seqlen)
    return pl.pallas_call(
        _memattn_kernel,
        grid=(n // tm,),
        in_specs=[pl.BlockSpec((tm, d), lambda i: (i, 0)),
                  pl.BlockSpec((tm, o.shape[1]), lambda i: (i, 0)),
                  _full(w_out.shape), _full((1, d)), _full(w_mq.shape), _full((1, MEM_HEAD_DIM)),
                  pl.BlockSpec((seqs_per_tile, mlen, mw), lambda i: (i // tiles_per_seq, 0, 0)),
                  pl.BlockSpec((seqs_per_tile, mlen, mw), lambda i: (i // tiles_per_seq, 0, 0)),
                  _full(w_mo.shape)],
        out_specs=pl.BlockSpec((tm, d), lambda i: (i, 0)),
        out_shape=jax.ShapeDtypeStruct((n, d), F32),
        compiler_params=_params("parallel"),
        name="mem_attend",
    )(x, o, w_out, g.reshape(1, d), w_mq, mq_gain.reshape(1, MEM_HEAD_DIM), mk, mv, w_mo)


def _ffn_kernel(x_ref, g_ref, w1_ref, w3_ref, w2_ref, y_ref, *, hid_chunk):
    x = x_ref[...]
    h = _bf(_rms(x, g_ref[...]))
    hidden = w1_ref.shape[1]
    y_ref[...] = x
    for c in range(0, hidden, hid_chunk):
        a = _dot(h, w1_ref[:, c:c + hid_chunk])
        b = _dot(h, w3_ref[:, c:c + hid_chunk])
        y_ref[...] += _dot(_bf(_silu(a) * b), w2_ref[c:c + hid_chunk, :])


def ffn(x, g, w1, w3, w2, tm):
    n, d = x.shape
    hidden = w1.shape[1]
    return pl.pallas_call(
        functools.partial(_ffn_kernel, hid_chunk=256),
        grid=(n // tm,),
        in_specs=[pl.BlockSpec((tm, d), lambda i: (i, 0)), _full((1, d)),
                  _full((d, hidden)), _full((d, hidden)), _full((hidden, d))],
        out_specs=pl.BlockSpec((tm, d), lambda i: (i, 0)),
        out_shape=jax.ShapeDtypeStruct((n, d), F32),
        compiler_params=_params("parallel"),
        name="ffn",
    )(x, g.reshape(1, d), w1, w3, w2)


def _proj_dsa_kernel(x_ref, g_ref, w_ref, qg_ref, kg_ref, kig_ref, bd_ref,
                     q_o, k_o, kb_o, v_o, vb_o, qi_o, kiwi_o):
    h = _bf(_rms(x_ref[...], g_ref[...]))
    hq = A_HEADS * A_HEAD_DIM
    bd = bd_ref[...]
    inv_hd = 1.0 / A_HEAD_DIM

    def head_norm(p, gain):
        ss = _dot_exact_rhs01(p * p, bd)
        return p * lax.rsqrt(ss * inv_hd + EPS) * gain

    step = 512
    for c in range(0, hq, step):
        pq = _dot(h, w_ref[:, c:c + step])
        pk = _dot(h, w_ref[:, hq + c:hq + c + step])
        pv = _dot(h, w_ref[:, 2 * hq + c:2 * hq + c + step])
        for j in range(0, step, LANES):
            sl = slice(c + j, c + j + LANES)
            qn = head_norm(pq[:, j:j + LANES], qg_ref[:, sl])
            q_o[:, sl] = _bf(qn * (A_HEAD_DIM ** -0.5))
            kn = head_norm(pk[:, j:j + LANES], kg_ref[:, sl])
            k_o[:, sl] = kn
            kb_o[:, sl] = _bf(kn)
        v_o[:, c:c + step] = pv
        vb_o[:, c:c + step] = _bf(pv)
    qiw = IDX_HEADS * LANES
    for c in range(0, qiw, step):
        qi_o[:, c:c + step] = _bf(_dot(h, w_ref[:, 3 * hq + c:3 * hq + c + step]))
    p = _dot(h, w_ref[:, 3 * hq + qiw:3 * hq + qiw + LANES])
    lane = lax.broadcasted_iota(I32, p.shape, 1)
    is_ki = lane < IDX_DIM
    ss = jnp.sum(jnp.where(is_ki, p * p, 0.0), axis=-1, keepdims=True)
    kin = p * lax.rsqrt(ss * (1.0 / IDX_DIM) + EPS) * kig_ref[...]
    kiwi_o[...] = jnp.where(is_ki, kin, p)


def proj_dsa(x, g, wa, q_gain, k_gain, ki_gain, tm):
    n, d = x.shape
    hq = A_HEADS * A_HEAD_DIM
    m = wa.shape[1]
    qg = jnp.tile(q_gain, A_HEADS).reshape(1, hq)
    kg = jnp.tile(k_gain, A_HEADS).reshape(1, hq)
    kig = jnp.concatenate([ki_gain, jnp.ones((LANES - IDX_DIM,), F32)]).reshape(1, LANES)
    r = jnp.arange(LANES)
    bd = _bf((r[:, None] // A_HEAD_DIM) == (r[None, :] // A_HEAD_DIM))
    row = lambda w: pl.BlockSpec((tm, w), lambda i: (i, 0))
    return pl.pallas_call(
        _proj_dsa_kernel,
        grid=(n // tm,),
        in_specs=[row(d), _full((1, d)), _full((d, m)), _full((1, hq)), _full((1, hq)),
                  _full((1, LANES)), _full((LANES, LANES))],
        out_specs=[row(hq), row(hq), row(hq), row(hq), row(hq), row(IDX_HEADS * LANES), row(LANES)],
        out_shape=[jax.ShapeDtypeStruct((n, hq), BF16), jax.ShapeDtypeStruct((n, hq), F32),
                   jax.ShapeDtypeStruct((n, hq), BF16), jax.ShapeDtypeStruct((n, hq), F32),
                   jax.ShapeDtypeStruct((n, hq), BF16),
                   jax.ShapeDtypeStruct((n, IDX_HEADS * LANES), BF16),
                   jax.ShapeDtypeStruct((n, LANES), F32)],
        compiler_params=_params("parallel"),
        name="proj_dsa",
    )(x, g.reshape(1, d), wa, qg, kg, kig, bd)


def dsa_weight(w_in):
    hq = A_HEADS * A_HEAD_DIM
    o3 = 3 * hq
    o4 = o3 + IDX_HEADS * IDX_DIM
    d = w_in.shape[0]
    wqi = w_in[:, o3:o4].reshape(d, IDX_HEADS, IDX_DIM)
    wqi = jnp.pad(wqi, ((0, 0), (0, 0), (0, LANES - IDX_DIM))).reshape(d, IDX_HEADS * LANES)
    tail = jnp.pad(w_in[:, o4:], ((0, 0), (0, LANES - (w_in.shape[1] - o4))))
    return _bf(jnp.concatenate([w_in[:, :o3], wqi, tail], axis=1))


def _sortable(s):
    b = pltpu.bitcast(s, I32)
    b = jnp.where(b == jnp.int32(-2 ** 31), 0, b)
    return jnp.where(b < 0, b ^ jnp.int32(0x7FFFFFFF), b)


def _index_mask_kernel(qi_ref, wit_ref, *rest, nref, groups, ltot, topk, causal, case_rows):
    ki_refs = rest[:nref]
    o_ref = rest[nref]
    key_ref, sel_ref, jv_ref = rest[nref + 1:]
    tq = qi_ref.shape[0]
    i = pl.program_id(1)
    lpad = sel_ref.shape[0]
    wit = wit_ref[0]
    if case_rows:
        ncase = (jnp.maximum((i + 1) * tq, topk) + case_rows - 1) // case_rows
        used_rows = ncase * case_rows
    else:
        used_rows = None

    def score_group(dst0, rows, sources):
        accs = []
        for rp, lead, src0, width, _ in sources:
            ref = ki_refs[rp]
            ki = _bf(ref[src0:src0 + rows, :] if lead is None else ref[lead, src0:src0 + rows, :])
            acc = jnp.zeros((rows, tq), F32)
            for hd in range(IDX_HEADS):
                rel = _dot_nt(ki, qi_ref[:, hd * LANES:hd * LANES + width])
                acc = acc + wit[hd:hd + 1, :] * jnp.maximum(rel, 0.0)
            accs.append(acc)
        if len(accs) == 1:
            acc = accs[0]
        else:
            lane = lax.broadcasted_iota(I32, (rows, tq), 1)
            acc = jnp.where(lane < tq // 2, accs[0], accs[1])
        s = acc * ((IDX_DIM ** -0.5) * (IDX_HEADS ** -0.5))
        if causal:
            kpos = dst0 + lax.broadcasted_iota(I32, (rows, tq), 0)
            qpos = i * tq + lax.broadcasted_iota(I32, (rows, tq), 1)
            s = jnp.where((kpos >> CHUNK_SHIFT) <= (qpos >> CHUNK_SHIFT), s, NEG)
        key_ref[dst0:dst0 + rows, :] = _sortable(s)

    for dst0, rows, sources in groups:
        if used_rows is None:
            score_group(dst0, rows, sources)
        else:
            pl.when(dst0 < used_rows)(functools.partial(score_group, dst0, rows, sources))

    def select(nrows):
        idx_bits = max(1, (nrows - 1).bit_length())

        def count(pred):
            c = pred.astype(I32).reshape(COUNT_CHAINS, nrows // COUNT_CHAINS, tq)
            return jnp.sum(jnp.sum(c, axis=1), axis=0, keepdims=True)

        def ge_count(cand):
            return count(key_ref[0:nrows, :] >= cand)

        t0 = jnp.full((1, tq), -2 ** 31, I32)
        t = jnp.where(ge_count(jnp.zeros((1, tq), I32)) >= topk, 0, t0)

        def vbody(it, t):
            cand = t + (jnp.int32(1) << (30 - it))
            return jnp.where(ge_count(cand) >= topk, cand, t)

        t = lax.fori_loop(0, 31, vbody, t)
        keys = key_ref[0:nrows, :]
        gt = keys > t
        eq = keys == t
        need = topk - count(gt)
        rowi = lax.broadcasted_iota(I32, (nrows, tq), 0)

        def jbody(it, jv):
            cand = jv + (jnp.int32(1) << (idx_bits - 1 - it))
            below = count(eq & (rowi < cand))
            return jnp.where(below < need, cand, jv)

        jv_ref[...] = jnp.full((1, tq), nrows, I32)
        has_tie = jnp.max(count(eq) - need) > 0

        @pl.when(has_tie)
        def _():
            jv_ref[...] = lax.fori_loop(0, idx_bits, jbody, jnp.zeros((1, tq), I32))

        sel = gt | (eq & (rowi <= jv_ref[...]))
        if causal:
            qpos = i * tq + lax.broadcasted_iota(I32, (nrows, tq), 1)
            sel = sel & ((rowi >> CHUNK_SHIFT) <= (qpos >> CHUNK_SHIFT))
        sel_ref[0:nrows, :] = jnp.where(sel, 0.0, NEG)
        nreal = -(-nrows // TILE)
        if nreal * TILE > nrows:
            sel_ref[nrows:nreal * TILE, :] = jnp.full((nreal * TILE - nrows, tq), NEG, F32)
        for kt in range(lpad // TILE):
            if kt < nreal:
                o_ref[0, 0, kt] = sel_ref[kt * TILE:(kt + 1) * TILE, :].T
            else:
                o_ref[0, 0, kt] = jnp.full((tq, TILE), NEG, F32)

    if case_rows:
        for k in range(ltot // case_rows):
            pl.when(ncase == k + 1)(functools.partial(select, (k + 1) * case_rows))
    else:
        select(ltot)


def index_mask(qi, wit, ki_arrays, ki_specs, groups, ltot, nstep, nqb, causal):
    lpad = -(-ltot // KTILE) * KTILE
    nt = lpad // TILE
    topk = min(TOPK_MAX, ltot // 4)
    case_rows = MASK_CASE_ROWS if (causal and ltot % MASK_CASE_ROWS == 0 and ltot > MASK_CASE_ROWS) else 0
    kern = functools.partial(_index_mask_kernel, nref=len(ki_arrays), groups=tuple(groups), ltot=ltot,
                             topk=topk, causal=causal, case_rows=case_rows)
    return pl.pallas_call(
        kern,
        grid=(nstep, nqb),
        in_specs=[pl.BlockSpec((TILE, qi.shape[1]), lambda b, i: (b * nqb + i, 0)),
                  pl.BlockSpec((1, IDX_HEADS, TILE), lambda b, i: (b * nqb + i, 0, 0))] + list(ki_specs),
        out_specs=pl.BlockSpec((1, 1, nt, TILE, TILE), lambda b, i: (b, i, 0, 0, 0)),
        out_shape=jax.ShapeDtypeStruct((nstep, nqb, nt, TILE, TILE), F32),
        scratch_shapes=[pltpu.VMEM((ltot, TILE), I32), pltpu.VMEM((lpad, TILE), F32),
                        pltpu.VMEM((1, TILE), I32)],
        compiler_params=_params("parallel", "parallel"),
        name="index_mask",
    )(qi, wit, *ki_arrays)


def _row_groups(row0, rows, make_sources):
    out = []
    for off in range(0, rows, SCORE_ROWS):
        out.append((row0 + off, min(SCORE_ROWS, rows - off), tuple(make_sources(off))))
    return out


def _attn_core(q_ref, o_ref, qs_ref, l_ref, mx_ref, acc_ref, groups, tq):
    npairs = A_HEADS // 2
    lane = lax.broadcasted_iota(I32, (1, LANES), 1)
    keep_lo = _bf((lane < A_HEAD_DIM).astype(F32))
    keep_hi = _bf((lane >= A_HEAD_DIM).astype(F32))
    for pair in range(npairs):
        qp = q_ref[:, pair * LANES:(pair + 1) * LANES]
        qs_ref[pair, 0:tq, :] = qp * keep_lo
        qs_ref[pair, tq:2 * tq, :] = qp * keep_hi
    mx_ref[...] = jnp.full(mx_ref.shape, NEG, F32)
    acc_ref[...] = jnp.zeros(acc_ref.shape, F32)
    ones = jnp.ones((KTILE, LANES), BF16)

    def over_tiles(count, body):
        if isinstance(count, int) and count == 1:
            body(0, 0)
        else:
            lax.fori_loop(0, count, body, 0)

    for count, base, k_tile, _, mask_sub, bias_sub in groups:
        def p1(kt, carry, base=base, k_tile=k_tile, mask_sub=mask_sub, bias_sub=bias_sub):
            masks = [mask_sub(kt, sub) for sub in range(KTILE // LANES)]
            for pair in range(npairs):
                sl = slice(pair * LANES, (pair + 1) * LANES)
                l = _dot_nt(qs_ref[pair], k_tile(kt, sl))
                for half in range(2):
                    rs = slice(half * tq, (half + 1) * tq)
                    mx = mx_ref[pair, rs, :]
                    for sub in range(KTILE // LANES):
                        cs = slice(sub * LANES, (sub + 1) * LANES)
                        blk = l[rs, cs] + (masks[sub] + bias_sub(kt, sub, 2 * pair + half))
                        l_ref[pair, base + kt, rs, cs] = blk
                        mx = jnp.maximum(mx, blk)
                    mx_ref[pair, rs, :] = mx
            return carry

        over_tiles(count, p1)

    for pair in range(npairs):
        m = jnp.max(mx_ref[pair], axis=-1, keepdims=True)
        mx_ref[pair] = jnp.broadcast_to(m, mx_ref.shape[1:])

    for count, base, _, v_tile, _, _ in groups:
        def p2(kt, carry, base=base, v_tile=v_tile):
            for pair in range(npairs):
                sl = slice(pair * LANES, (pair + 1) * LANES)
                m = mx_ref[pair]
                p = jnp.concatenate(
                    [jnp.exp(l_ref[pair, base + kt, :, sub * LANES:(sub + 1) * LANES] - m)
                     for sub in range(KTILE // LANES)], axis=1)
                vaug = jnp.concatenate([v_tile(kt, sl), ones], axis=1)
                acc_ref[pair] += _dot(_bf(p), vaug)
            return carry

        over_tiles(count, p2)

    lane_full = lax.broadcasted_iota(I32, (tq, LANES), 1)
    for pair in range(npairs):
        a = acc_ref[pair]
        o = a[:, 0:LANES] / a[:, LANES:2 * LANES]
        o_ref[:, pair * LANES:(pair + 1) * LANES] = _bf(jnp.where(lane_full < A_HEAD_DIM, o[0:tq], o[tq:2 * tq]))


def _attn_scratch(tq, ntiles):
    npairs = A_HEADS // 2
    return [pltpu.VMEM((npairs, 2 * tq, LANES), BF16),
            pltpu.VMEM((npairs, ntiles, 2 * tq, KTILE), F32),
            pltpu.VMEM((npairs, 2 * tq, LANES), F32),
            pltpu.VMEM((npairs, 2 * tq, 2 * LANES), F32)]


def _attn_prompt_kernel(q_ref, k_ref, v_ref, mask_ref, bias_ref, o_ref, qs_ref, l_ref, mx_ref, acc_ref):
    i = pl.program_id(1)
    per = KTILE // TILE

    def rows(kt):
        return pl.ds(pl.multiple_of(kt * KTILE, KTILE), KTILE)

    def bias_sub(kt, sub, hd):
        s = kt * per + sub
        sel = jnp.where(s == i, 2, jnp.where(s == i - 1, 1, 0))
        return bias_ref[sel, hd]

    group = (i // per + 1, 0,
             lambda kt, sl: k_ref[rows(kt), sl],
             lambda kt, sl: v_ref[rows(kt), sl],
             lambda kt, sub: mask_ref[0, 0, kt * per + sub],
             bias_sub)
    _attn_core(q_ref, o_ref, qs_ref, l_ref, mx_ref, acc_ref, [group], TILE)


def attn_prompt(q, kb, vb, mask, bias, nseq, seqlen):
    n, hq = q.shape
    nqb = seqlen // TILE
    nt = mask.shape[2]
    return pl.pallas_call(
        _attn_prompt_kernel,
        grid=(nseq, nqb),
        in_specs=[pl.BlockSpec((TILE, hq), lambda b, i: (b * nqb + i, 0)),
                  pl.BlockSpec((seqlen, hq), lambda b, i: (b, 0)),
                  pl.BlockSpec((seqlen, hq), lambda b, i: (b, 0)),
                  pl.BlockSpec((1, 1, nt, TILE, TILE), lambda b, i: (b, i, 0, 0, 0)),
                  _full(bias.shape)],
        out_specs=pl.BlockSpec((TILE, hq), lambda b, i: (b * nqb + i, 0)),
        out_shape=jax.ShapeDtypeStruct((n, hq), BF16),
        scratch_shapes=_attn_scratch(TILE, seqlen // KTILE),
        compiler_params=_params("parallel", "arbitrary"),
        name="attn_prompt",
    )(q, kb, vb, mask, bias)


def _attn_sample_kernel(q_ref, kc_ref, vc_ref, kn_ref, vn_ref, mask_ref, bias_ref, o_ref,
                        l_ref, mx_ref, acc_ref, *, nblk):
    ph = pl.program_id(1)
    b = pl.program_id(2)
    tq = q_ref.shape[0]
    per = KTILE // TILE
    tiles_per_blk = kc_ref.shape[3] // KTILE
    ncache = nblk * tiles_per_blk
    last_sub = ncache * per - 1
    heads = range(A_HEADS)

    def head_tile(ref, c0, hd):
        return _bf(ref[0, hd, :, c0:c0 + KTILE])

    def pass1(slot, ref, c0, mask_sub, bias_sub):
        ls = [_dot(q_ref[:, hd * A_HEAD_DIM:(hd + 1) * A_HEAD_DIM], head_tile(ref, c0, hd))
              for hd in heads]
        masks = [mask_sub(sub) for sub in range(per)]
        for hd in heads:
            mx = mx_ref[hd]
            for sub in range(per):
                cs = slice(sub * LANES, (sub + 1) * LANES)
                blk = ls[hd][:, cs] + (masks[sub] + bias_sub(sub, hd))
                l_ref[hd, slot, :, cs] = blk
                mx = jnp.maximum(mx, blk)
            mx_ref[hd] = mx

    def pass2(slot, ref, c0):
        ones = jnp.ones((LANES - A_HEAD_DIM, KTILE), BF16)
        for hd in heads:
            m = mx_ref[hd]
            p = jnp.concatenate([jnp.exp(l_ref[hd, slot, :, sub * LANES:(sub + 1) * LANES] - m)
                                 for sub in range(per)], axis=1)
            vaug = jnp.concatenate([head_tile(ref, c0, hd), ones], axis=0)
            acc_ref[hd] += _dot_nt(_bf(p), vaug)

    @pl.when((ph == 0) & (b == 0))
    def _():
        mx_ref[...] = jnp.full(mx_ref.shape, NEG, F32)
        acc_ref[...] = jnp.zeros(acc_ref.shape, F32)

    @pl.when(ph == 0)
    def _():
        for t in range(tiles_per_blk):
            g = b * tiles_per_blk + t
            pass1(g, kc_ref, t * KTILE,
                  lambda sub, g=g: mask_ref[0, 0, g * per + sub],
                  lambda sub, hd, g=g: bias_ref[jnp.where(g * per + sub == last_sub, 1, 0), hd])

    @pl.when((ph == 0) & (b == nblk - 1))
    def _():
        pass1(ncache, kn_ref, 0,
              lambda sub: mask_ref[0, 0, ncache * per + sub],
              lambda sub, hd: bias_ref[2 if sub == 0 else 0, hd])
        for hd in heads:
            m = jnp.max(mx_ref[hd], axis=-1, keepdims=True)
            mx_ref[hd] = jnp.broadcast_to(m, mx_ref.shape[1:])

    @pl.when(ph == 1)
    def _():
        for t in range(tiles_per_blk):
            pass2(b * tiles_per_blk + t, vc_ref, t * KTILE)

    @pl.when((ph == 1) & (b == nblk - 1))
    def _():
        pass2(ncache, vn_ref, 0)
        for pair in range(A_HEADS // 2):
            halves = []
            for hd in (2 * pair, 2 * pair + 1):
                a = acc_ref[hd]
                halves.append(a[:, 0:A_HEAD_DIM] / a[:, A_HEAD_DIM:2 * A_HEAD_DIM])
            o_ref[:, pair * LANES:(pair + 1) * LANES] = _bf(jnp.concatenate(halves, axis=1))


def attn_sample(q, kc, vc, kn, vn, mask, bias, nseq, tq):
    n, hq = q.shape
    past = kc.shape[3]
    cblk = min(CACHE_BLOCK, past)
    nblk = past // cblk
    ncache = past // KTILE
    nt = mask.shape[2]
    per = TILE // tq
    cache_k = pl.BlockSpec((1, A_HEADS, A_HEAD_DIM, cblk),
                           lambda s, ph, b: (s, 0, 0, b * (1 - ph) + (nblk - 1) * ph))
    cache_v = pl.BlockSpec((1, A_HEADS, A_HEAD_DIM, cblk), lambda s, ph, b: (s, 0, 0, b * ph))
    fresh = pl.BlockSpec((1, A_HEADS, A_HEAD_DIM, KTILE), lambda s, ph, b: (s, 0, 0, 0))
    return pl.pallas_call(
        functools.partial(_attn_sample_kernel, nblk=nblk),
        grid=(nseq, 2, nblk),
        in_specs=[pl.BlockSpec((tq, hq), lambda s, ph, b: (s, 0)),
                  cache_k, cache_v, fresh, fresh,
                  pl.BlockSpec((1, 1, nt, tq, TILE), lambda s, ph, b: (s // per, 0, 0, s % per, 0)),
                  pl.BlockSpec(bias.shape, lambda s, ph, b: (0, 0, 0, 0))],
        out_specs=pl.BlockSpec((tq, hq), lambda s, ph, b: (s, 0)),
        out_shape=jax.ShapeDtypeStruct((n, hq), BF16),
        scratch_shapes=[pltpu.VMEM((A_HEADS, ncache + 1, tq, KTILE), F32),
                        pltpu.VMEM((A_HEADS, tq, LANES), F32),
                        pltpu.VMEM((A_HEADS, tq, LANES), F32)],
        compiler_params=_params("parallel", "arbitrary", "arbitrary"),
        name="attn_sample",
    )(q, kc, vc, kn, vn, mask, bias)


def _t5_bucket(rel):
    half = N_BUCKETS // 2
    max_exact = half // 2
    n = jnp.abs(rel)
    nf = jnp.maximum(n, 1).astype(F32)
    large = max_exact + (jnp.log(nf / max_exact) / math.log(MAX_DISTANCE / max_exact)
                         * (half - max_exact)).astype(I32)
    large = jnp.minimum(large, half - 1)
    return jnp.where(rel > 0, half, 0) + jnp.where(n < max_exact, n, large)


def bias_tiles(rel_bias):
    heads = rel_bias.shape[1]
    span = 2 * TILE - 1

    def toeplitz(shift):
        rel = jnp.arange(span, dtype=I32) - (TILE - 1) + shift
        tab = rel_bias.astype(F32)[_t5_bucket(rel)].T
        strip = jnp.tile(jnp.pad(tab, ((0, 0), (0, 1))), (1, TILE))[:, :TILE * span]
        return strip.reshape(heads, TILE, span)[:, :, TILE - 1:]

    far = rel_bias.astype(F32)[_t5_bucket(jnp.full((1,), -(TILE + 1), I32))]
    far = jnp.broadcast_to(far.T[:, :, None], (heads, TILE, TILE))
    return jnp.stack([far, toeplitz(-TILE), toeplitz(0)])


def dsa_core_prompt(pr, nseq, seqlen, bias):
    q, _, kb, _, vb, qi, kiwi = pr
    nqb = seqlen // TILE
    wit = kiwi[:, IDX_DIM:IDX_DIM + IDX_HEADS].reshape(nseq * nqb, TILE, IDX_HEADS).swapaxes(1, 2)
    groups = _row_groups(0, seqlen, lambda off: [(0, None, off, LANES, None)])
    mask = index_mask(qi, wit, [kiwi], [pl.BlockSpec((seqlen, LANES), lambda b, i: (b, 0))],
                      groups, seqlen, nseq, nqb, causal=True)
    return attn_prompt(q, kb, vb, mask, bias, nseq, seqlen)


def dsa_core_sample(pr, k_cache, v_cache, ki_cache, nseq, tq, bias):
    q, kf, _, vf, _, qi, kiwi = pr
    past = k_cache.shape[1]
    per = TILE // tq
    wit = kiwi[:, IDX_DIM:IDX_DIM + IDX_HEADS].reshape(nseq // per, TILE, IDX_HEADS).swapaxes(1, 2)
    groups = (_row_groups(0, past, lambda off: [(0, j, off, IDX_DIM, j) for j in range(per)])
              + _row_groups(past, tq, lambda off: [(1, None, j * tq + off, LANES, j) for j in range(per)]))
    mask = index_mask(
        qi, wit, [ki_cache, kiwi],
        [pl.BlockSpec((per, past, IDX_DIM), lambda b, i: (b, 0, 0)),
         pl.BlockSpec((TILE, LANES), lambda b, i: (b, 0))],
        groups, past + tq, nseq // per, 1, causal=False)
    to_hdk = lambda a: a.transpose(0, 2, 3, 1)
    pad_keys = ((0, 0), (0, 0), (0, 0), (0, KTILE - tq))
    kn = jnp.pad(to_hdk(kf.reshape(nseq, tq, A_HEADS, A_HEAD_DIM)), pad_keys)
    vn = jnp.pad(to_hdk(vf.reshape(nseq, tq, A_HEADS, A_HEAD_DIM)), pad_keys)
    return attn_sample(q, to_hdk(k_cache), to_hdk(v_cache), kn, vn, mask, bias[:, :, :tq, :], nseq, tq)


def _log_sigmoid(x):
    return jnp.minimum(x, 0.0) - jnp.log1p(jnp.exp(-jnp.abs(x)))


def _mlstm_kernel(p_ref, gs_ref, gr_ref, gbc_ref, gbr_ref, hg_ref, c0_ref, n0_ref, m0_ref,
                  hs_ref, c_ref, n_ref, m_ref, cs, ns, ms):
    c_id = pl.program_id(1)
    nh = B_HEADS
    npair = nh // 2
    hw = LANES
    L = p_ref.shape[0]
    L2 = 2 * L

    @pl.when(c_id == 0)
    def _():
        cs[...] = jnp.zeros(cs.shape, F32)
        ns[...] = jnp.zeros(ns.shape, F32)
        ns[:, 0:B_QK_DIM] = n0_ref[0]
        for hd in range(nh):
            cs[hd // 2, (hd % 2) * hw:(hd % 2) * hw + B_QK_DIM, :] = c0_ref[0, hd]
        ms[...] = m0_ref[0]

    r = lax.broadcasted_iota(I32, (L2, L2), 0)
    c = lax.broadcasted_iota(I32, (L2, L2), 1)
    same = (r >> CHUNK_SHIFT) == (c >> CHUNK_SHIFT)
    incl = same & (c <= r)
    gs = gs_ref[0] + gbc_ref[...]
    gr = gr_ref[0] + gbr_ref[...]
    bcol = _dot_exact_lhs01(_bf01(incl), _log_sigmoid(gs[:, npair:]))
    brow = _dot_exact_rhs01(_log_sigmoid(gr[npair:, :]), _bf01(same & (r <= c)))
    icol = gs[:, :npair]
    irow = gr[:npair, :]
    rowc = lax.broadcasted_iota(I32, (L2, 1), 0)
    is_top = rowc < L
    top = is_top.astype(F32)
    bot = 1.0 - top
    row2 = lax.broadcasted_iota(I32, (2 * hw, 1), 0)
    hg = hg_ref[...]
    mall = ms[...]

    def stack(base, pr):
        a = base + 2 * pr * hw
        return jnp.concatenate([p_ref[:, a:a + hw], p_ref[:, a + hw:a + 2 * hw]], axis=0)

    def bd(x):
        return _bf(jnp.concatenate([x * top, x * bot], axis=1))

    pairs = range(npair)
    qf = [stack(0, pr) for pr in pairs]
    kf = [stack(nh * hw, pr) * (B_QK_DIM ** -0.5) for pr in pairs]
    vb = [_bf(stack(2 * nh * hw, pr)) for pr in pairs]
    bc = [bcol[:, pr:pr + 1] for pr in pairs]
    ic = [icol[:, pr:pr + 1] for pr in pairs]
    mc = [mall[:, pr:pr + 1] for pr in pairs]
    d = [jnp.where(incl, bc[pr] - brow[pr:pr + 1, :] + irow[pr:pr + 1, :], NEG) for pr in pairs]
    inter = [bc[pr] + mc[pr] for pr in pairs]
    mt = [jnp.maximum(inter[pr], jnp.max(d[pr], axis=-1, keepdims=True)) for pr in pairs]
    s = [_dot_nt(_bf(qf[pr]), _bf(kf[pr])) * jnp.exp(d[pr] - mt[pr]) for pr in pairs]
    wst = [jnp.exp(inter[pr] - mt[pr]) for pr in pairs]
    cmat = [cs[pr] for pr in pairs]
    num = [_dot(_bf(s[pr]), vb[pr]) + wst[pr] * _dot(bd(qf[pr]), _bf(cmat[pr])) for pr in pairs]
    for pr in pairs:
        n0 = ns[2 * pr:2 * pr + 1, :]
        n1 = ns[2 * pr + 1:2 * pr + 2, :]
        qn = jnp.sum(qf[pr] * jnp.where(is_top, n0, n1), axis=-1, keepdims=True)
        den = jnp.sum(s[pr], axis=-1, keepdims=True) + wst[pr] * qn
        hs = num[pr] / jnp.maximum(jnp.abs(den), jnp.exp(-mt[pr]))
        bl0 = bc[pr][L - 1:L, :]
        bl1 = bc[pr][L2 - 1:L2, :]
        bl = jnp.where(is_top, bl0, bl1)
        dec = bl - bc[pr] + ic[pr]
        blm = bl + mc[pr]
        mnew0 = jnp.maximum(blm[0:1, :], jnp.max(dec[0:L], axis=0, keepdims=True))
        mnew1 = jnp.maximum(blm[L:L + 1, :], jnp.max(dec[L:L2], axis=0, keepdims=True))
        mnew = jnp.where(is_top, mnew0, mnew1)
        wk = jnp.exp(dec - mnew)
        ws = jnp.exp(blm - mnew)
        kw = kf[pr] * wk
        ws2 = jnp.where(row2 < hw, ws[0:1, :], ws[L:L + 1, :])
        cs[pr] = ws2 * cmat[pr] + _dot_tn(bd(kw), vb[pr])
        ns[2 * pr:2 * pr + 1, :] = ws[0:1, :] * n0 + jnp.sum(kw[0:L], axis=0, keepdims=True)
        ns[2 * pr + 1:2 * pr + 2, :] = ws[L:L + 1, :] * n1 + jnp.sum(kw[L:L2], axis=0, keepdims=True)
        ms[:, pr:pr + 1] = mnew
        on = _bf(_rms(hs, hg) * _sigmoid(stack(3 * nh * hw, pr)))
        oa = 2 * pr * B_V_DIM
        hs_ref[:, oa:oa + B_V_DIM] = on[0:L]
        hs_ref[:, oa + B_V_DIM:oa + 2 * B_V_DIM] = on[L:L2]

    @pl.when(c_id == pl.num_programs(1) - 1)
    def _():
        for hd in range(nh):
            c_ref[0, hd] = cs[hd // 2, (hd % 2) * hw:(hd % 2) * hw + B_QK_DIM, :]
        n_ref[0] = ns[:, 0:B_QK_DIM]
        m_ref[0] = ms[...]


def mlstm_weight(w_in):
    d = w_in.shape[0]
    nq = B_HEADS * B_QK_DIM
    nv = B_HEADS * B_V_DIM

    def padh(w):
        w = w.reshape(d, B_HEADS, B_QK_DIM)
        return jnp.pad(w, ((0, 0), (0, 0), (0, LANES - B_QK_DIM))).reshape(d, B_HEADS * LANES)

    main = 2 * nq + 2 * nv
    tail = jnp.pad(w_in[:, main:], ((0, 0), (0, LANES - 2 * B_HEADS)))
    return _bf(jnp.concatenate([padh(w_in[:, :nq]), padh(w_in[:, nq:2 * nq]), w_in[:, 2 * nq:main], tail], axis=1))


def mlstm_core(p, gate_bias, h_gain, c0, n0, m0, nseq, nchunk):
    n, m = p.shape
    nh = B_HEADS
    npair = nh // 2
    gcol = 4 * nh * LANES
    pre = p[:, gcol:gcol + 2 * nh].reshape(nseq * nchunk, CHUNK, 2, npair, 2)
    gs = pre.transpose(0, 4, 1, 2, 3).reshape(nseq * nchunk, 2 * CHUNK, 2 * npair)
    gr = pre.transpose(0, 2, 3, 4, 1).reshape(nseq * nchunk, 2 * npair, 2 * CHUNK)
    gb = gate_bias.reshape(2, npair, 2)
    gbc = jnp.repeat(gb.transpose(2, 0, 1).reshape(2, 2 * npair), CHUNK, axis=0)
    m0s = jnp.repeat(m0.reshape(nseq, npair, 2).transpose(0, 2, 1), CHUNK, axis=1)
    o, c_new, n_new, m_new = pl.pallas_call(
        _mlstm_kernel,
        grid=(nseq, nchunk),
        in_specs=[pl.BlockSpec((CHUNK, m), lambda s, c: (s * nchunk + c, 0)),
                  pl.BlockSpec((1, 2 * CHUNK, 2 * npair), lambda s, c: (s * nchunk + c, 0, 0)),
                  pl.BlockSpec((1, 2 * npair, 2 * CHUNK), lambda s, c: (s * nchunk + c, 0, 0)),
                  _full((2 * CHUNK, 2 * npair)), _full((2 * npair, 2 * CHUNK)), _full((1, B_V_DIM)),
                  pl.BlockSpec((1, nh, B_QK_DIM, B_V_DIM), lambda s, c: (s, 0, 0, 0)),
                  pl.BlockSpec((1, nh, B_QK_DIM), lambda s, c: (s, 0, 0)),
                  pl.BlockSpec((1, 2 * CHUNK, npair), lambda s, c: (s, 0, 0))],
        out_specs=[pl.BlockSpec((CHUNK, nh * B_V_DIM), lambda s, c: (s * nchunk + c, 0)),
                   pl.BlockSpec((1, nh, B_QK_DIM, B_V_DIM), lambda s, c: (s, 0, 0, 0)),
                   pl.BlockSpec((1, nh, B_QK_DIM), lambda s, c: (s, 0, 0)),
                   pl.BlockSpec((1, 2 * CHUNK, npair), lambda s, c: (s, 0, 0))],
        out_shape=[jax.ShapeDtypeStruct((n, nh * B_V_DIM), BF16),
                   jax.ShapeDtypeStruct((nseq, nh, B_QK_DIM, B_V_DIM), F32),
                   jax.ShapeDtypeStruct((nseq, nh, B_QK_DIM), F32),
                   jax.ShapeDtypeStruct((nseq, 2 * CHUNK, npair), F32)],
        scratch_shapes=[pltpu.VMEM((npair, 2 * LANES, B_V_DIM), F32), pltpu.VMEM((nh, LANES), F32),
                        pltpu.VMEM((2 * CHUNK, npair), F32)],
        compiler_params=_params("parallel", "arbitrary"),
        name="mlstm",
    )(p, gs, gr, gbc, gbc.T, h_gain.reshape(1, B_V_DIM), c0, n0, m0s)
    m_heads = m_new[:, ::CHUNK, :].transpose(0, 2, 1).reshape(nseq, nh)
    return o, c_new, n_new, m_heads


def _split2(x):
    hi = _bf(x)
    return hi, _bf(x - hi.astype(F32))


def _cat3_lhs(x):
    hi, mid = _split2(x)
    return jnp.concatenate([hi, hi, mid], axis=1)


def _cat3_rhs(x):
    hi, mid = _split2(x)
    return jnp.concatenate([hi, mid, hi], axis=0)


def _gdn_kernel(p_ref, gs_ref, gr_ref, cw_ref, alc_ref, alr_ref, dtc_ref, dtr_ref, og_ref, s0_ref, cb0_ref,
                o_ref, s_ref, cb_ref, ss, tail):
    c_id = pl.program_id(1)
    nh = C_HEADS
    cdim = nh * (2 * C_DK + C_DV)
    L = p_ref.shape[0]
    L2 = 2 * L
    nprev = CONV_W - 1

    @pl.when(c_id == 0)
    def _():
        ss[...] = s0_ref[0]
        tail[...] = jnp.zeros(tail.shape, F32)
        tail[8 - nprev:8, :] = cb0_ref[0]

    x = p_ref[:, 0:cdim]
    ext = jnp.concatenate([tail[...], x], axis=0)
    conv = ext[8:8 + L] * cw_ref[CONV_W - 1:CONV_W, :]
    for j in range(CONV_W - 1):
        conv = conv + ext[8 - nprev + j:8 - nprev + j + L] * cw_ref[j:j + 1, :]
    tail[...] = x[L - 8:L, :]
    cf = _silu(conv)

    npair = nh // 2
    gs = gs_ref[0]
    gr = gr_ref[0]
    beta = _sigmoid(gs[:, 0:npair])
    g_col = -jnp.exp(alc_ref[...]) * _softplus(gs[:, npair:] + dtc_ref[...])
    g_row = -jnp.exp(alr_ref[...]) * _softplus(gr[npair:, :] + dtr_ref[...])
    r = lax.broadcasted_iota(I32, (L2, L2), 0)
    c = lax.broadcasted_iota(I32, (L2, L2), 1)
    same = (r >> CHUNK_SHIFT) == (c >> CHUNK_SHIFT)
    incl = same & (c <= r)
    strict = same & (c < r)
    eye = (c == r).astype(F32)
    gc_col = _dot_exact_lhs01(_bf01(incl), g_col)
    gc_row = _dot_exact_rhs01(g_row, _bf01(same & (r <= c)))
    rowc = lax.broadcasted_iota(I32, (L2, 1), 0)
    top = (rowc < L).astype(F32)
    bot = 1.0 - top
    row2 = lax.broadcasted_iota(I32, (2 * C_DK, 1), 0)
    og = og_ref[...]

    def stack(base, pr):
        a = base + 2 * pr * C_DK
        return jnp.concatenate([cf[:, a:a + C_DK], cf[:, a + C_DK:a + 2 * C_DK]], axis=0)

    def bd(x):
        return _bf(jnp.concatenate([x * top, x * bot], axis=1))

    pairs = range(npair)
    qc, kc, kcb, dm, amat, rhs, gcols, egcs = [], [], [], [], [], [], [], []
    for pr in pairs:
        qraw = stack(0, pr)
        kraw = stack(nh * C_DK, pr)
        vc = stack(2 * nh * C_DK, pr)
        qc.append(qraw * lax.rsqrt(jnp.sum(qraw * qraw, axis=-1, keepdims=True) + EPS) * (C_DK ** -0.5))
        kc.append(kraw * lax.rsqrt(jnp.sum(kraw * kraw, axis=-1, keepdims=True) + EPS))
        bc = beta[:, pr:pr + 1]
        gcol = gc_col[:, pr:pr + 1]
        grow = gc_row[pr:pr + 1, :]
        dm.append(jnp.where(incl, jnp.exp(jnp.where(incl, gcol - grow, 0.0)), 0.0))
        kb = kc[pr] * bc
        kcb.append(_bf(kc[pr]))
        amat.append(jnp.where(strict, _dot_nt(_bf(kb), kcb[pr]) * dm[pr], 0.0))
        egc = jnp.exp(gcol)
        rhs.append(jnp.concatenate([vc * bc, kb * egc], axis=-1))
        gcols.append(gcol)
        egcs.append(egc)
    tinv = [eye - amat[pr] for pr in pairs]
    pw_l = [_cat3_lhs(-amat[pr]) for pr in pairs]
    pw_r = [_cat3_rhs(-amat[pr]) for pr in pairs]
    for _ in range(CHUNK_SHIFT - 1):
        pw = [_dot(pw_l[pr], pw_r[pr]) for pr in pairs]
        pw_l = [_cat3_lhs(pw[pr]) for pr in pairs]
        pw_r = [_cat3_rhs(pw[pr]) for pr in pairs]
        tinv = [tinv[pr] + _dot(_cat3_lhs(tinv[pr]), pw_r[pr]) for pr in pairs]
    sol = [_dot(_cat3_lhs(tinv[pr]), _cat3_rhs(rhs[pr])) for pr in pairs]
    attn = [_dot_nt(_bf(qc[pr]), kcb[pr]) * dm[pr] for pr in pairs]
    smat = [ss[pr] for pr in pairs]
    sb = [_bf(smat[pr]) for pr in pairs]
    vnew = [sol[pr][:, :C_DV] - _dot(bd(sol[pr][:, C_DV:]), sb[pr]) for pr in pairs]
    o = [_dot(bd(qc[pr] * egcs[pr]), sb[pr]) + _dot(_bf(attn[pr]), _bf(vnew[pr])) for pr in pairs]
    for pr in pairs:
        gl0 = gcols[pr][L - 1:L, :]
        gl1 = gcols[pr][L2 - 1:L2, :]
        ke = kc[pr] * jnp.exp(jnp.where(rowc < L, gl0, gl1) - gcols[pr])
        decay = jnp.exp(jnp.where(row2 < C_DK, gl0, gl1))
        ss[pr] = smat[pr] * decay + _dot_tn(bd(ke), _bf(vnew[pr]))
    for pr in pairs:
        za = cdim + 2 * pr * C_DV
        z = jnp.concatenate([p_ref[:, za:za + C_DV], p_ref[:, za + C_DV:za + 2 * C_DV]], axis=0)
        on = _bf(_rms(o[pr], og) * _silu(z))
        oa = 2 * pr * C_DV
        o_ref[:, oa:oa + C_DV] = on[0:L]
        o_ref[:, oa + C_DV:oa + 2 * C_DV] = on[L:L2]

    @pl.when(c_id == pl.num_programs(1) - 1)
    def _():
        s_ref[0] = ss[...]
        cb_ref[0] = tail[8 - nprev:8, :]


def gdn_weight(w_in):
    main = C_HEADS * (2 * C_DK + C_DV) + C_HEADS * C_DV
    tail = jnp.pad(w_in[:, main:], ((0, 0), (0, LANES - 2 * C_HEADS)))
    return _bf(jnp.concatenate([w_in[:, :main], tail], axis=1))


def gdn_core(p, conv_w, a_log, dt_bias, o_gain, s0, cb0, nseq, nchunk):
    n, m = p.shape
    nh = C_HEADS
    npair = nh // 2
    cdim = nh * (2 * C_DK + C_DV)
    gcolumn = cdim + nh * C_DV
    pre = p[:, gcolumn:gcolumn + 2 * nh].reshape(nseq * nchunk, CHUNK, 2, npair, 2)
    gs = pre.transpose(0, 4, 1, 2, 3).reshape(nseq * nchunk, 2 * CHUNK, 2 * npair)
    gr = pre.transpose(0, 2, 3, 4, 1).reshape(nseq * nchunk, 2 * npair, 2 * CHUNK)

    def col(v):
        return jnp.repeat(v.reshape(npair, 2).T, CHUNK, axis=0)

    o, s_new, cb_new = pl.pallas_call(
        _gdn_kernel,
        grid=(nseq, nchunk),
        in_specs=[pl.BlockSpec((CHUNK, m), lambda s, c: (s * nchunk + c, 0)),
                  pl.BlockSpec((1, 2 * CHUNK, 2 * npair), lambda s, c: (s * nchunk + c, 0, 0)),
                  pl.BlockSpec((1, 2 * npair, 2 * CHUNK), lambda s, c: (s * nchunk + c, 0, 0)),
                  _full((CONV_W, cdim)),
                  _full((2 * CHUNK, npair)), _full((npair, 2 * CHUNK)),
                  _full((2 * CHUNK, npair)), _full((npair, 2 * CHUNK)),
                  _full((1, C_DV)),
                  pl.BlockSpec((1, npair, 2 * C_DK, C_DV), lambda s, c: (s, 0, 0, 0)),
                  pl.BlockSpec((1, CONV_W - 1, cdim), lambda s, c: (s, 0, 0))],
        out_specs=[pl.BlockSpec((CHUNK, nh * C_DV), lambda s, c: (s * nchunk + c, 0)),
                   pl.BlockSpec((1, npair, 2 * C_DK, C_DV), lambda s, c: (s, 0, 0, 0)),
                   pl.BlockSpec((1, CONV_W - 1, cdim), lambda s, c: (s, 0, 0))],
        out_shape=[jax.ShapeDtypeStruct((n, nh * C_DV), BF16),
                   jax.ShapeDtypeStruct((nseq, npair, 2 * C_DK, C_DV), F32),
                   jax.ShapeDtypeStruct((nseq, CONV_W - 1, cdim), F32)],
        scratch_shapes=[pltpu.VMEM((npair, 2 * C_DK, C_DV), F32), pltpu.VMEM((8, cdim), F32)],
        compiler_params=_params("parallel", "arbitrary"),
        name="gdn",
    )(p, gs, gr, conv_w, col(a_log), col(a_log).T, col(dt_bias), col(dt_bias).T, o_gain.reshape(1, C_DV),
      s0.reshape(nseq, npair, 2 * C_DK, C_DV), cb0)
    return o, s_new.reshape(nseq, nh, C_DK, C_DV), cb_new


def _trunk(x, nseq, seqlen, mem_k, mem_v, st, W, bias, is_prompt):
    d = x.shape[-1]
    n = nseq * seqlen
    x = x.reshape(n, d)
    tm = min(256, n)
    tm_mem = min(MEM_ROWS, n)
    new = {}
    for i in range(4):
        kind = i % 3
        mx = W["mixer"][i]
        if kind == 0:
            pr = proj_dsa(x, W["norm_mix"][i], mx["w_in"], mx["q_gain"], mx["k_gain"], mx["ki_gain"], tm)
            if is_prompt:
                o = dsa_core_prompt(pr, nseq, seqlen, bias)
            else:
                o = dsa_core_sample(pr, *st[i], nseq, seqlen, bias)
            new[i] = (pr[1].reshape(nseq, seqlen, A_HEADS, A_HEAD_DIM),
                      pr[3].reshape(nseq, seqlen, A_HEADS, A_HEAD_DIM),
                      pr[6][:, :IDX_DIM].reshape(nseq, seqlen, IDX_DIM))
        elif kind == 1:
            p = proj(x, W["norm_mix"][i], mx["w_in"], tm)
            o, c_new, n_new, m_new = mlstm_core(p, mx["gate_bias"], mx["h_gain"], *st[i], nseq, seqlen // CHUNK)
            new[i] = (c_new, n_new, m_new.reshape(nseq, B_HEADS))
        else:
            p = proj(x, W["norm_mix"][i], mx["w_in"], tm)
            o, s_new, cb_new = gdn_core(p, mx["conv_w"], mx["a_log"], mx["dt_bias"], mx["o_gain"], *st[i],
                                        nseq, seqlen // CHUNK)
            new[i] = (s_new, cb_new)
        x = mem_attend(x, o, mx["w_out"], W["norm_mem"][i], W["w_mq"][i], W["mq_gain"][i],
                       mem_k[i], mem_v[i], W["w_mo"][i], tm_mem, seqlen)
        x = ffn(x, W["norm_ffn"][i], W["w_ffn1"][i], W["w_ffn3"][i], W["w_ffn2"][i], min(FFN_ROWS, n))
    return x.reshape(nseq, seqlen, d), new


def kernel(x_prompt, x_sample, mem_prompt, cache_l0_k, cache_l0_v, cache_l0_kidx, state_l1_C, state_l1_n, state_l1_m, state_l2_S, state_l2_conv, cache_l3_k, cache_l3_v, cache_l3_kidx, cache_mem_k, cache_mem_v, rel_bias, norm_mix, norm_mem, norm_ffn, mem_norm, w_mq, w_mk, w_mv, w_mo, mq_gain, mk_gain, w_ffn1, w_ffn3, w_ffn2, a0_w_in, a0_w_out, a0_q_gain, a0_k_gain, a0_kidx_gain, b1_w_in, b1_gate_bias, b1_h_gain, b1_w_out, c2_w_in, c2_conv_w, c2_a_log, c2_dt_bias, c2_o_gain, c2_w_out, a3_w_in, a3_w_out, a3_q_gain, a3_k_gain, a3_kidx_gain):
    B, T, D = x_prompt.shape
    S, Ts, _ = x_sample.shape
    depth = w_mq.shape[0]
    mlen = mem_prompt.shape[1]
    mw = MEM_HEADS * MEM_HEAD_DIM

    def dsa_w(w_in, w_out, qg, kg, kig):
        return dict(w_in=dsa_weight(w_in), w_out=_bf(w_out), q_gain=qg, k_gain=kg, ki_gain=kig)

    W = dict(
        norm_mix=norm_mix, norm_mem=norm_mem, norm_ffn=norm_ffn,
        w_mq=_bf(w_mq), w_mo=_bf(w_mo), mq_gain=mq_gain,
        w_ffn1=_bf(w_ffn1), w_ffn3=_bf(w_ffn3), w_ffn2=_bf(w_ffn2),
        mixer={
            0: dsa_w(a0_w_in, a0_w_out, a0_q_gain, a0_k_gain, a0_kidx_gain),
            1: dict(w_in=mlstm_weight(b1_w_in), gate_bias=b1_gate_bias, h_gain=b1_h_gain, w_out=_bf(b1_w_out)),
            2: dict(w_in=gdn_weight(c2_w_in), conv_w=c2_conv_w, a_log=c2_a_log, dt_bias=c2_dt_bias,
                    o_gain=c2_o_gain, w_out=_bf(c2_w_out)),
            3: dsa_w(a3_w_in, a3_w_out, a3_q_gain, a3_k_gain, a3_kidx_gain),
        },
    )
    bias = bias_tiles(rel_bias)

    mk_p, mv_p = mem_kv(mem_prompt.reshape(B * mlen, D), mem_norm, w_mk, w_mv, mk_gain)
    mk_p = mk_p.reshape(depth, B, mlen, mw)
    mv_p = mv_p.reshape(depth, B, mlen, mw)
    st_p = {
        0: None,
        1: (jnp.zeros((B, B_HEADS, B_QK_DIM, B_V_DIM), F32), jnp.zeros((B, B_HEADS, B_QK_DIM), F32),
            jnp.full((B, B_HEADS), NEG, F32)),
        2: (jnp.zeros((B, C_HEADS, C_DK, C_DV), F32), jnp.zeros((B, CONV_W - 1, state_l2_conv.shape[-1]), F32)),
        3: None,
    }
    y_p, np_ = _trunk(x_prompt, B, T, mk_p, mv_p, st_p, W, bias, True)

    st_s = {
        0: (cache_l0_k, cache_l0_v, cache_l0_kidx),
        1: (state_l1_C, state_l1_n, state_l1_m),
        2: (state_l2_S, state_l2_conv),
        3: (cache_l3_k, cache_l3_v, cache_l3_kidx),
    }
    mk_s = cache_mem_k.reshape(depth, S, mlen, mw)
    mv_s = cache_mem_v.reshape(depth, S, mlen, mw)
    y_s, ns_ = _trunk(x_sample, S, Ts, mk_s, mv_s, st_s, W, bias, False)

    shp = (depth, B, mlen, MEM_HEADS, MEM_HEAD_DIM)
    return (y_p, y_s,
            *np_[0], *np_[1], *np_[2], *np_[3], mk_p.reshape(shp), mv_p.reshape(shp),
            *ns_[0], *ns_[1], *ns_[2], *ns_[3])
```

```python
import functools
import math

import jax
import jax.numpy as jnp
from jax import lax
from jax.experimental import pallas as pl
from jax.experimental.pallas import tpu as pltpu

F32 = jnp.float32
BF16 = jnp.bfloat16
I32 = jnp.int32

EPS = 1e-6
NEG = -1e30
CHUNK = 64
CHUNK_SHIFT = 6
LANES = 128
TILE = 128
KTILE = 256
COUNT_CHAINS = 8
SCORE_ROWS = 256
FFN_ROWS = 512
MEM_ROWS = 256
CACHE_BLOCK = 1024
MASK_CASE_ROWS = 512
VMEM_LIMIT = 56 * 1024 * 1024

A_HEADS, A_HEAD_DIM = 16, 64
IDX_HEADS, IDX_DIM = 8, 64
TOPK_MAX = 256
N_BUCKETS, MAX_DISTANCE = 32, 128
B_HEADS, B_QK_DIM, B_V_DIM = 8, 64, 128
C_HEADS, C_DK, C_DV = 8, 128, 128
CONV_W = 4
MEM_HEADS, MEM_HEAD_DIM = 4, 128


def _bf(x):
    return x.astype(BF16)


def _bf01(mask):
    return mask.astype(F32).astype(BF16)


def _dot(a, b):
    return jnp.dot(a, b, preferred_element_type=F32)


def _dot_nt(a, b):
    return lax.dot_general(a, b, (((1,), (1,)), ((), ())), preferred_element_type=F32)


def _dot_tn(a, b):
    return lax.dot_general(a, b, (((0,), (0,)), ((), ())), preferred_element_type=F32)


def _split3(x):
    hi = _bf(x)
    r1 = x - hi.astype(F32)
    mid = _bf(r1)
    lo = _bf(r1 - mid.astype(F32))
    return hi, mid, lo


def _dot_exact_rhs01(x, m01):
    hi, mid, lo = _split3(x)
    return _dot(hi, m01) + _dot(mid, m01) + _dot(lo, m01)


def _dot_exact_lhs01(m01, x):
    hi, mid, lo = _split3(x)
    return _dot(m01, hi) + _dot(m01, mid) + _dot(m01, lo)


def _dot_f32(a, b):
    ah, am, al = _split3(a)
    bh, bm, bl = _split3(b)
    return (_dot(ah, bh) + (_dot(ah, bm) + _dot(am, bh))
            + (_dot(am, bm) + _dot(ah, bl) + _dot(al, bh)))


def _rms(x, g):
    ms = jnp.mean(x * x, axis=-1, keepdims=True)
    return x * lax.rsqrt(ms + EPS) * g


def _sigmoid(x):
    return 1.0 / (1.0 + jnp.exp(-x))


def _silu(x):
    return x * _sigmoid(x)


def _softplus(x):
    return jnp.maximum(x, 0.0) + jnp.log1p(jnp.exp(-jnp.abs(x)))


def _params(*sem):
    return pltpu.CompilerParams(dimension_semantics=sem, vmem_limit_bytes=VMEM_LIMIT)


def _full(shape):
    n = len(shape)
    return pl.BlockSpec(shape, lambda *_: (0,) * n)


def _proj_kernel(x_ref, g_ref, w_ref, o_ref, *, col_chunk):
    h = _bf(_rms(x_ref[...], g_ref[...]))
    m = w_ref.shape[1]
    for c in range(0, m, col_chunk):
        e = min(c + col_chunk, m)
        o_ref[:, c:e] = _dot(h, w_ref[:, c:e])


def proj(x, g, w, tm):
    n, d = x.shape
    m = w.shape[1]
    return pl.pallas_call(
        functools.partial(_proj_kernel, col_chunk=512),
        grid=(n // tm,),
        in_specs=[pl.BlockSpec((tm, d), lambda i: (i, 0)), _full((1, d)), _full((d, m))],
        out_specs=pl.BlockSpec((tm, m), lambda i: (i, 0)),
        out_shape=jax.ShapeDtypeStruct((n, m), F32),
        compiler_params=_params("parallel"),
        name="proj",
    )(x, g.reshape(1, d), w)


def _memkv_kernel(x_ref, g_ref, wk_ref, wv_ref, kg_ref, k_ref, v_ref):
    h = _bf(_rms(x_ref[...], g_ref[0]))
    k = _dot(h, wk_ref[0])
    v_ref[0] = _dot(h, wv_ref[0])
    kg = kg_ref[0]
    for hd in range(MEM_HEADS):
        sl = slice(hd * MEM_HEAD_DIM, (hd + 1) * MEM_HEAD_DIM)
        k_ref[0, :, sl] = _rms(k[:, sl], kg)


def mem_kv(mem2d, mem_norm, w_mk, w_mv, mk_gain, tm=256):
    n, d = mem2d.shape
    depth = w_mk.shape[0]
    mw = w_mk.shape[2]
    return pl.pallas_call(
        _memkv_kernel,
        grid=(depth, n // tm),
        in_specs=[pl.BlockSpec((tm, d), lambda l, i: (i, 0)),
                  pl.BlockSpec((1, 1, d), lambda l, i: (l, 0, 0)),
                  pl.BlockSpec((1, d, mw), lambda l, i: (l, 0, 0)),
                  pl.BlockSpec((1, d, mw), lambda l, i: (l, 0, 0)),
                  pl.BlockSpec((1, 1, MEM_HEAD_DIM), lambda l, i: (l, 0, 0))],
        out_specs=[pl.BlockSpec((1, tm, mw), lambda l, i: (l, i, 0)),
                   pl.BlockSpec((1, tm, mw), lambda l, i: (l, i, 0))],
        out_shape=[jax.ShapeDtypeStruct((depth, n, mw), F32)] * 2,
        compiler_params=_params("parallel", "parallel"),
        name="mem_kv",
    )(mem2d, mem_norm.reshape(depth, 1, d), _bf(w_mk), _bf(w_mv), mk_gain.reshape(depth, 1, MEM_HEAD_DIM))


def _memattn_kernel(x_ref, o_ref, wo_ref, g_ref, wq_ref, qg_ref, mk_ref, mv_ref, wmo_ref, y_ref):
    x1 = x_ref[...] + _dot(o_ref[...], wo_ref[...])
    h = _bf(_rms(x1, g_ref[...]))
    q = _dot(h, wq_ref[...])
    qg = qg_ref[...]
    scale = MEM_HEAD_DIM ** -0.5
    nsub = mk_ref.shape[0]
    rows = x1.shape[0] // nsub
    cells = [(hd, s) for hd in range(MEM_HEADS) for s in range(nsub)]
    hsl = lambda hd: slice(hd * MEM_HEAD_DIM, (hd + 1) * MEM_HEAD_DIM)
    qh = [_bf(_rms(q[:, hsl(hd)], qg)) for hd in range(MEM_HEADS)]
    logits = [_dot_nt(qh[hd][s * rows:(s + 1) * rows], _bf(mk_ref[s, :, hsl(hd)])) * scale for hd, s in cells]
    ps = [jnp.exp(l - jnp.max(l, axis=-1, keepdims=True)) for l in logits]
    ps = [_bf(p / jnp.sum(p, axis=-1, keepdims=True)) for p in ps]
    pv = [_bf(_dot(p, _bf(mv_ref[s, :, hsl(hd)]))) for p, (hd, s) in zip(ps, cells)]
    outs = []
    for hd in range(MEM_HEADS):
        subs = pv[hd * nsub:(hd + 1) * nsub]
        outs.append(subs[0] if nsub == 1 else jnp.concatenate(subs, axis=0))
    att = jnp.concatenate(outs, axis=-1)
    y_ref[...] = x1 + _dot(att, wmo_ref[...])


def mem_attend(x, o, w_out, g, w_mq, mq_gain, mk, mv, w_mo, tm, seqlen):
    n, d = x.shape
    mlen, mw = mk.shape[1], mk.shape[2]
    tiles_per_seq = max(1, seqlen // tm)
    seqs_per_tile = max(1, tm // seqlen)
    return pl.pallas_call(
        _memattn_kernel,
        grid=(n // tm,),
        in_specs=[pl.BlockSpec((tm, d), lambda i: (i, 0)),
                  pl.BlockSpec((tm, o.shape[1]), lambda i: (i, 0)),
                  _full(w_out.shape), _full((1, d)), _full(w_mq.shape), _full((1, MEM_HEAD_DIM)),
                  pl.BlockSpec((seqs_per_tile, mlen, mw), lambda i: (i // tiles_per_seq, 0, 0)),
                  pl.BlockSpec((seqs_per_tile, mlen, mw), lambda i: (i // tiles_per_seq, 0, 0)),
                  _full(w_mo.shape)],
        out_specs=pl.BlockSpec((tm, d), lambda i: (i, 0)),
        out_shape=jax.ShapeDtypeStruct((n, d), F32),
        compiler_params=_params("parallel"),
        name="mem_attend",
    )(x, o, w_out, g.reshape(1, d), w_mq, mq_gain.reshape(1, MEM_HEAD_DIM), mk, mv, w_mo)


def _ffn_kernel(x_ref, g_ref, w1_ref, w3_ref, w2_ref, y_ref, *, hid_chunk):
    x = x_ref[...]
    h = _bf(_rms(x, g_ref[...]))
    hidden = w1_ref.shape[1]
    y_ref[...] = x
    for c in range(0, hidden, hid_chunk):
        a = _dot(h, w1_ref[:, c:c + hid_chunk])
        b = _dot(h, w3_ref[:, c:c + hid_chunk])
        y_ref[...] += _dot(_bf(_silu(a) * b), w2_ref[c:c + hid_chunk, :])


def ffn(x, g, w1, w3, w2, tm):
    n, d = x.shape
    hidden = w1.shape[1]
    return pl.pallas_call(
        functools.partial(_ffn_kernel, hid_chunk=256),
        grid=(n // tm,),
        in_specs=[pl.BlockSpec((tm, d), lambda i: (i, 0)), _full((1, d)),
                  _full((d, hidden)), _full((d, hidden)), _full((hidden, d))],
        out_specs=pl.BlockSpec((tm, d), lambda i: (i, 0)),
        out_shape=jax.ShapeDtypeStruct((n, d), F32),
        compiler_params=_params("parallel"),
        name="ffn",
    )(x, g.reshape(1, d), w1, w3, w2)


def _proj_dsa_kernel(x_ref, g_ref, w_ref, qg_ref, kg_ref, kig_ref, bd_ref,
                     q_o, k_o, kb_o, v_o, vb_o, qi_o, kiwi_o, *maybe_kit_o, tstate):
    h = _bf(_rms(x_ref[...], g_ref[...]))
    tm = x_ref.shape[0]
    hq = A_HEADS * A_HEAD_DIM
    bd = bd_ref[...]
    inv_hd = 1.0 / A_HEAD_DIM
    nw = bd.shape[0]

    def head_norm(p, gain):
        hi, mid = _split2(p * p)
        ss = _dot(hi, bd) + _dot(mid, bd)
        return p * lax.rsqrt(ss * inv_hd + EPS) * gain

    def transposed(x):
        return jnp.concatenate([x[r:r + LANES, :].T for r in range(0, tm, LANES)], axis=1)

    def store_state(o_ref, x, col0):
        if not tstate:
            o_ref[:, col0:col0 + x.shape[1]] = x
            return
        for j in range(0, x.shape[1], LANES):
            xt = transposed(x[:, j:j + LANES])
            hd = (col0 + j) // A_HEAD_DIM
            for half in range(LANES // A_HEAD_DIM):
                o_ref[0, hd + half] = xt[half * A_HEAD_DIM:(half + 1) * A_HEAD_DIM]

    step = 512
    for c in range(0, hq, step):
        pq = _dot(h, w_ref[:, c:c + step])
        pk = _dot(h, w_ref[:, hq + c:hq + c + step])
        pv = _dot(h, w_ref[:, 2 * hq + c:2 * hq + c + step])
        for j in range(0, step, nw):
            sl = slice(c + j, c + j + nw)
            qn = head_norm(pq[:, j:j + nw], qg_ref[:, sl])
            q_o[:, sl] = _bf(qn * (A_HEAD_DIM ** -0.5))
            kn = head_norm(pk[:, j:j + nw], kg_ref[:, sl])
            store_state(k_o, kn, c + j)
            kb_o[:, sl] = _bf(kn)
        store_state(v_o, pv, c)
        vb_o[:, c:c + step] = _bf(pv)
    qiw = IDX_HEADS * LANES
    for c in range(0, qiw, step):
        qi_o[:, c:c + step] = _bf(_dot(h, w_ref[:, 3 * hq + c:3 * hq + c + step]))
    p = _dot(h, w_ref[:, 3 * hq + qiw:3 * hq + qiw + LANES])
    lane = lax.broadcasted_iota(I32, p.shape, 1)
    is_ki = lane < IDX_DIM
    ss = jnp.sum(jnp.where(is_ki, p * p, 0.0), axis=-1, keepdims=True)
    kin = p * lax.rsqrt(ss * (1.0 / IDX_DIM) + EPS) * kig_ref[...]
    kiwi = jnp.where(is_ki, kin, p)
    kiwi_o[...] = kiwi
    if tstate:
        maybe_kit_o[0][0] = transposed(kiwi)[0:IDX_DIM]


def proj_dsa(x, g, wa, q_gain, k_gain, ki_gain, tm, nseq, tstate):
    n, d = x.shape
    hq = A_HEADS * A_HEAD_DIM
    m = wa.shape[1]
    seqlen = n // nseq
    tiles = seqlen // tm
    qg = jnp.tile(q_gain, A_HEADS).reshape(1, hq)
    kg = jnp.tile(k_gain, A_HEADS).reshape(1, hq)
    kig = jnp.concatenate([ki_gain, jnp.ones((LANES - IDX_DIM,), F32)]).reshape(1, LANES)
    r = jnp.arange(2 * LANES)
    bd = _bf((r[:, None] // A_HEAD_DIM) == (r[None, :] // A_HEAD_DIM))
    row = lambda w: pl.BlockSpec((tm, w), lambda i: (i, 0))
    if tstate:
        state = pl.BlockSpec((1, A_HEADS, A_HEAD_DIM, tm), lambda i: (i // tiles, 0, 0, i % tiles))
        state_shape = jax.ShapeDtypeStruct((nseq, A_HEADS, A_HEAD_DIM, seqlen), F32)
        extra_specs = [pl.BlockSpec((1, IDX_DIM, tm), lambda i: (i // tiles, 0, i % tiles))]
        extra_shapes = [jax.ShapeDtypeStruct((nseq, IDX_DIM, seqlen), F32)]
    else:
        state, state_shape, extra_specs, extra_shapes = row(hq), jax.ShapeDtypeStruct((n, hq), F32), [], []
    return pl.pallas_call(
        functools.partial(_proj_dsa_kernel, tstate=tstate),
        grid=(n // tm,),
        in_specs=[row(d), _full((1, d)), _full((d, m)), _full((1, hq)), _full((1, hq)),
                  _full((1, LANES)), _full(bd.shape)],
        out_specs=[row(hq), state, row(hq), state, row(hq), row(IDX_HEADS * LANES), row(LANES)] + extra_specs,
        out_shape=[jax.ShapeDtypeStruct((n, hq), BF16), state_shape,
                   jax.ShapeDtypeStruct((n, hq), BF16), state_shape,
                   jax.ShapeDtypeStruct((n, hq), BF16),
                   jax.ShapeDtypeStruct((n, IDX_HEADS * LANES), BF16),
                   jax.ShapeDtypeStruct((n, LANES), F32)] + extra_shapes,
        compiler_params=_params("parallel"),
        name="proj_dsa",
    )(x, g.reshape(1, d), wa, qg, kg, kig, bd)


def dsa_weight(w_in):
    hq = A_HEADS * A_HEAD_DIM
    o3 = 3 * hq
    o4 = o3 + IDX_HEADS * IDX_DIM
    d = w_in.shape[0]
    wqi = w_in[:, o3:o4].reshape(d, IDX_HEADS, IDX_DIM)
    wqi = jnp.pad(wqi, ((0, 0), (0, 0), (0, LANES - IDX_DIM))).reshape(d, IDX_HEADS * LANES)
    tail = jnp.pad(w_in[:, o4:], ((0, 0), (0, LANES - (w_in.shape[1] - o4))))
    return _bf(jnp.concatenate([w_in[:, :o3], wqi, tail], axis=1))


def _sortable(s):
    b = pltpu.bitcast(s, I32)
    b = jnp.where(b == jnp.int32(-2 ** 31), 0, b)
    return jnp.where(b < 0, b ^ jnp.int32(0x7FFFFFFF), b)


def _index_mask_kernel(qi_ref, wit_ref, *rest, nref, groups, ltot, topk, causal, case_rows):
    ki_refs = rest[:nref]
    o_ref = rest[nref]
    key_ref, sel_ref, jv_ref = rest[nref + 1:]
    tq = qi_ref.shape[0]
    i = pl.program_id(1)
    lpad = sel_ref.shape[0]
    wit = wit_ref[0]
    if case_rows:
        ncase = (jnp.maximum((i + 1) * tq, topk) + case_rows - 1) // case_rows
        used_rows = ncase * case_rows
    else:
        used_rows = None

    def score_group(dst0, rows, sources):
        accs = []
        for rp, lead, src0, width, _ in sources:
            ref = ki_refs[rp]
            ki = _bf(ref[src0:src0 + rows, :] if lead is None else ref[lead, src0:src0 + rows, :])
            acc = jnp.zeros((rows, tq), F32)
            for hd in range(IDX_HEADS):
                rel = _dot_nt(ki, qi_ref[:, hd * LANES:hd * LANES + width])
                acc = acc + wit[hd:hd + 1, :] * jnp.maximum(rel, 0.0)
            accs.append(acc)
        if len(accs) == 1:
            acc = accs[0]
        else:
            lane = lax.broadcasted_iota(I32, (rows, tq), 1)
            acc = jnp.where(lane < tq // 2, accs[0], accs[1])
        s = acc * ((IDX_DIM ** -0.5) * (IDX_HEADS ** -0.5))
        if causal:
            kpos = dst0 + lax.broadcasted_iota(I32, (rows, tq), 0)
            qpos = i * tq + lax.broadcasted_iota(I32, (rows, tq), 1)
            s = jnp.where((kpos >> CHUNK_SHIFT) <= (qpos >> CHUNK_SHIFT), s, NEG)
        key_ref[dst0:dst0 + rows, :] = _sortable(s)

    for dst0, rows, sources in groups:
        if used_rows is None:
            score_group(dst0, rows, sources)
        else:
            pl.when(dst0 < used_rows)(functools.partial(score_group, dst0, rows, sources))

    def select(nrows):
        idx_bits = max(1, (nrows - 1).bit_length())

        def count(pred):
            c = pred.astype(I32).reshape(COUNT_CHAINS, nrows // COUNT_CHAINS, tq)
            return jnp.sum(jnp.sum(c, axis=1), axis=0, keepdims=True)

        def ge_count(cand):
            return count(key_ref[0:nrows, :] >= cand)

        t0 = jnp.full((1, tq), -2 ** 31, I32)
        t = jnp.where(ge_count(jnp.zeros((1, tq), I32)) >= topk, 0, t0)

        def vbody(it, t):
            cand = t + (jnp.int32(1) << (30 - it))
            return jnp.where(ge_count(cand) >= topk, cand, t)

        t = lax.fori_loop(0, 31, vbody, t)
        keys = key_ref[0:nrows, :]
        gt = keys > t
        eq = keys == t
        need = topk - count(gt)
        rowi = lax.broadcasted_iota(I32, (nrows, tq), 0)

        def jbody(it, jv):
            cand = jv + (jnp.int32(1) << (idx_bits - 1 - it))
            below = count(eq & (rowi < cand))
            return jnp.where(below < need, cand, jv)

        jv_ref[...] = jnp.full((1, tq), nrows, I32)
        has_tie = jnp.max(count(eq) - need) > 0

        @pl.when(has_tie)
        def _():
            jv_ref[...] = lax.fori_loop(0, idx_bits, jbody, jnp.zeros((1, tq), I32))

        sel = gt | (eq & (rowi <= jv_ref[...]))
        if causal:
            qpos = i * tq + lax.broadcasted_iota(I32, (nrows, tq), 1)
            sel = sel & ((rowi >> CHUNK_SHIFT) <= (qpos >> CHUNK_SHIFT))
        sel_ref[0:nrows, :] = jnp.where(sel, 0.0, NEG)
        nreal = -(-nrows // TILE)
        if nreal * TILE > nrows:
            sel_ref[nrows:nreal * TILE, :] = jnp.full((nreal * TILE - nrows, tq), NEG, F32)
        for kt in range(lpad // TILE):
            if kt < nreal:
                o_ref[0, 0, kt] = sel_ref[kt * TILE:(kt + 1) * TILE, :].T
            else:
                o_ref[0, 0, kt] = jnp.full((tq, TILE), NEG, F32)

    if case_rows:
        for k in range(ltot // case_rows):
            pl.when(ncase == k + 1)(functools.partial(select, (k + 1) * case_rows))
    else:
        select(ltot)


def index_mask(qi, wit, ki_arrays, ki_specs, groups, ltot, nstep, nqb, causal):
    lpad = -(-ltot // KTILE) * KTILE
    nt = lpad // TILE
    topk = min(TOPK_MAX, ltot // 4)
    case_rows = MASK_CASE_ROWS if (causal and ltot % MASK_CASE_ROWS == 0 and ltot > MASK_CASE_ROWS) else 0
    kern = functools.partial(_index_mask_kernel, nref=len(ki_arrays), groups=tuple(groups), ltot=ltot,
                             topk=topk, causal=causal, case_rows=case_rows)
    return pl.pallas_call(
        kern,
        grid=(nstep, nqb),
        in_specs=[pl.BlockSpec((TILE, qi.shape[1]), lambda b, i: (b * nqb + i, 0)),
                  pl.BlockSpec((1, IDX_HEADS, TILE), lambda b, i: (b * nqb + i, 0, 0))] + list(ki_specs),
        out_specs=pl.BlockSpec((1, 1, nt, TILE, TILE), lambda b, i: (b, i, 0, 0, 0)),
        out_shape=jax.ShapeDtypeStruct((nstep, nqb, nt, TILE, TILE), F32),
        scratch_shapes=[pltpu.VMEM((ltot, TILE), I32), pltpu.VMEM((lpad, TILE), F32),
                        pltpu.VMEM((1, TILE), I32)],
        compiler_params=_params("parallel", "parallel"),
        name="index_mask",
    )(qi, wit, *ki_arrays)


def _row_groups(row0, rows, make_sources):
    out = []
    for off in range(0, rows, SCORE_ROWS):
        out.append((row0 + off, min(SCORE_ROWS, rows - off), tuple(make_sources(off))))
    return out


def _attn_core(q_ref, o_ref, qs_ref, l_ref, mx_ref, acc_ref, groups, tq):
    npairs = A_HEADS // 2
    lane = lax.broadcasted_iota(I32, (1, LANES), 1)
    keep_lo = _bf((lane < A_HEAD_DIM).astype(F32))
    keep_hi = _bf((lane >= A_HEAD_DIM).astype(F32))
    for pair in range(npairs):
        qp = q_ref[:, pair * LANES:(pair + 1) * LANES]
        qs_ref[pair, 0:tq, :] = qp * keep_lo
        qs_ref[pair, tq:2 * tq, :] = qp * keep_hi
    mx_ref[...] = jnp.full(mx_ref.shape, NEG, F32)
    acc_ref[...] = jnp.zeros(acc_ref.shape, F32)
    ones = jnp.ones((KTILE, LANES), BF16)

    def over_tiles(count, body):
        if isinstance(count, int) and count == 1:
            body(0, 0)
        else:
            lax.fori_loop(0, count, body, 0)

    for count, base, k_tile, _, mask_sub, bias_sub in groups:
        def p1(kt, carry, base=base, k_tile=k_tile, mask_sub=mask_sub, bias_sub=bias_sub):
            masks = [mask_sub(kt, sub) for sub in range(KTILE // LANES)]
            for pair in range(npairs):
                sl = slice(pair * LANES, (pair + 1) * LANES)
                l = _dot_nt(qs_ref[pair], k_tile(kt, sl))
                for half in range(2):
                    rs = slice(half * tq, (half + 1) * tq)
                    mx = mx_ref[pair, rs, :]
                    for sub in range(KTILE // LANES):
                        cs = slice(sub * LANES, (sub + 1) * LANES)
                        blk = l[rs, cs] + (masks[sub] + bias_sub(kt, sub, 2 * pair + half))
                        l_ref[pair, base + kt, rs, cs] = blk
                        mx = jnp.maximum(mx, blk)
                    mx_ref[pair, rs, :] = mx
            return carry

        over_tiles(count, p1)

    for pair in range(npairs):
        m = jnp.max(mx_ref[pair], axis=-1, keepdims=True)
        mx_ref[pair] = jnp.broadcast_to(m, mx_ref.shape[1:])

    for count, base, _, v_tile, _, _ in groups:
        def p2(kt, carry, base=base, v_tile=v_tile):
            for pair in range(npairs):
                sl = slice(pair * LANES, (pair + 1) * LANES)
                m = mx_ref[pair]
                p = jnp.concatenate(
                    [jnp.exp(l_ref[pair, base + kt, :, sub * LANES:(sub + 1) * LANES] - m)
                     for sub in range(KTILE // LANES)], axis=1)
                vaug = jnp.concatenate([v_tile(kt, sl), ones], axis=1)
                acc_ref[pair] += _dot(_bf(p), vaug)
            return carry

        over_tiles(count, p2)

    lane_full = lax.broadcasted_iota(I32, (tq, LANES), 1)
    for pair in range(npairs):
        a = acc_ref[pair]
        o = a[:, 0:LANES] / a[:, LANES:2 * LANES]
        o_ref[:, pair * LANES:(pair + 1) * LANES] = _bf(jnp.where(lane_full < A_HEAD_DIM, o[0:tq], o[tq:2 * tq]))


def _attn_scratch(tq, ntiles):
    npairs = A_HEADS // 2
    return [pltpu.VMEM((npairs, 2 * tq, LANES), BF16),
            pltpu.VMEM((npairs, ntiles, 2 * tq, KTILE), F32),
            pltpu.VMEM((npairs, 2 * tq, LANES), F32),
            pltpu.VMEM((npairs, 2 * tq, 2 * LANES), F32)]


def _attn_prompt_kernel(q_ref, k_ref, v_ref, mask_ref, bias_ref, o_ref, qs_ref, l_ref, mx_ref, acc_ref):
    i = pl.program_id(1)
    per = KTILE // TILE

    def rows(kt):
        return pl.ds(pl.multiple_of(kt * KTILE, KTILE), KTILE)

    def bias_sub(kt, sub, hd):
        s = kt * per + sub
        sel = jnp.where(s == i, 2, jnp.where(s == i - 1, 1, 0))
        return bias_ref[sel, hd]

    group = (i // per + 1, 0,
             lambda kt, sl: k_ref[rows(kt), sl],
             lambda kt, sl: v_ref[rows(kt), sl],
             lambda kt, sub: mask_ref[0, 0, kt * per + sub],
             bias_sub)
    _attn_core(q_ref, o_ref, qs_ref, l_ref, mx_ref, acc_ref, [group], TILE)


def attn_prompt(q, kb, vb, mask, bias, nseq, seqlen):
    n, hq = q.shape
    nqb = seqlen // TILE
    nt = mask.shape[2]
    return pl.pallas_call(
        _attn_prompt_kernel,
        grid=(nseq, nqb),
        in_specs=[pl.BlockSpec((TILE, hq), lambda b, i: (b * nqb + i, 0)),
                  pl.BlockSpec((seqlen, hq), lambda b, i: (b, 0)),
                  pl.BlockSpec((seqlen, hq), lambda b, i: (b, 0)),
                  pl.BlockSpec((1, 1, nt, TILE, TILE), lambda b, i: (b, i, 0, 0, 0)),
                  _full(bias.shape)],
        out_specs=pl.BlockSpec((TILE, hq), lambda b, i: (b * nqb + i, 0)),
        out_shape=jax.ShapeDtypeStruct((n, hq), BF16),
        scratch_shapes=_attn_scratch(TILE, seqlen // KTILE),
        compiler_params=_params("parallel", "arbitrary"),
        name="attn_prompt",
    )(q, kb, vb, mask, bias)


def _attn_sample_kernel(q_ref, kc_ref, vc_ref, kn_ref, vn_ref, mask_ref, bias_ref, o_ref,
                        l_ref, mx_ref, acc_ref, *, nblk):
    ph = pl.program_id(1)
    b = pl.program_id(2)
    tq = q_ref.shape[0]
    per = KTILE // TILE
    tiles_per_blk = kc_ref.shape[3] // KTILE
    ncache = nblk * tiles_per_blk
    last_sub = ncache * per - 1
    heads = range(A_HEADS)

    def head_tile(ref, c0, hd):
        return _bf(ref[0, hd, :, c0:c0 + KTILE])

    def pass1(slot, ref, c0, mask_sub, bias_sub):
        ls = [_dot(q_ref[:, hd * A_HEAD_DIM:(hd + 1) * A_HEAD_DIM], head_tile(ref, c0, hd))
              for hd in heads]
        masks = [mask_sub(sub) for sub in range(per)]
        for hd in heads:
            mx = mx_ref[hd]
            for sub in range(per):
                cs = slice(sub * LANES, (sub + 1) * LANES)
                blk = ls[hd][:, cs] + (masks[sub] + bias_sub(sub, hd))
                l_ref[hd, slot, :, cs] = blk
                mx = jnp.maximum(mx, blk)
            mx_ref[hd] = mx

    def pass2(slot, ref, c0):
        ones = jnp.ones((LANES - A_HEAD_DIM, KTILE), BF16)
        for hd in heads:
            m = mx_ref[hd]
            p = jnp.concatenate([jnp.exp(l_ref[hd, slot, :, sub * LANES:(sub + 1) * LANES] - m)
                                 for sub in range(per)], axis=1)
            vaug = jnp.concatenate([head_tile(ref, c0, hd), ones], axis=0)
            acc_ref[hd] += _dot_nt(_bf(p), vaug)

    @pl.when((ph == 0) & (b == 0))
    def _():
        mx_ref[...] = jnp.full(mx_ref.shape, NEG, F32)
        acc_ref[...] = jnp.zeros(acc_ref.shape, F32)

    @pl.when(ph == 0)
    def _():
        for t in range(tiles_per_blk):
            g = b * tiles_per_blk + t
            pass1(g, kc_ref, t * KTILE,
                  lambda sub, g=g: mask_ref[0, 0, g * per + sub],
                  lambda sub, hd, g=g: bias_ref[jnp.where(g * per + sub == last_sub, 1, 0), hd])

    @pl.when((ph == 0) & (b == nblk - 1))
    def _():
        pass1(ncache, kn_ref, 0,
              lambda sub: mask_ref[0, 0, ncache * per + sub],
              lambda sub, hd: bias_ref[2 if sub == 0 else 0, hd])
        for hd in heads:
            m = jnp.max(mx_ref[hd], axis=-1, keepdims=True)
            mx_ref[hd] = jnp.broadcast_to(m, mx_ref.shape[1:])

    @pl.when(ph == 1)
    def _():
        for t in range(tiles_per_blk):
            pass2(b * tiles_per_blk + t, vc_ref, t * KTILE)

    @pl.when((ph == 1) & (b == nblk - 1))
    def _():
        pass2(ncache, vn_ref, 0)
        for pair in range(A_HEADS // 2):
            halves = []
            for hd in (2 * pair, 2 * pair + 1):
                a = acc_ref[hd]
                halves.append(a[:, 0:A_HEAD_DIM] / a[:, A_HEAD_DIM:2 * A_HEAD_DIM])
            o_ref[:, pair * LANES:(pair + 1) * LANES] = _bf(jnp.concatenate(halves, axis=1))


def attn_sample(q, kc, vc, kn, vn, mask, bias, nseq, tq):
    n, hq = q.shape
    past = kc.shape[3]
    cblk = min(CACHE_BLOCK, past)
    nblk = past // cblk
    ncache = past // KTILE
    nt = mask.shape[2]
    per = TILE // tq
    cache_k = pl.BlockSpec((1, A_HEADS, A_HEAD_DIM, cblk),
                           lambda s, ph, b: (s, 0, 0, b * (1 - ph) + (nblk - 1) * ph))
    cache_v = pl.BlockSpec((1, A_HEADS, A_HEAD_DIM, cblk), lambda s, ph, b: (s, 0, 0, b * ph))
    fresh = pl.BlockSpec((1, A_HEADS, A_HEAD_DIM, KTILE), lambda s, ph, b: (s, 0, 0, 0))
    return pl.pallas_call(
        functools.partial(_attn_sample_kernel, nblk=nblk),
        grid=(nseq, 2, nblk),
        in_specs=[pl.BlockSpec((tq, hq), lambda s, ph, b: (s, 0)),
                  cache_k, cache_v, fresh, fresh,
                  pl.BlockSpec((1, 1, nt, tq, TILE), lambda s, ph, b: (s // per, 0, 0, s % per, 0)),
                  pl.BlockSpec(bias.shape, lambda s, ph, b: (0, 0, 0, 0))],
        out_specs=pl.BlockSpec((tq, hq), lambda s, ph, b: (s, 0)),
        out_shape=jax.ShapeDtypeStruct((n, hq), BF16),
        scratch_shapes=[pltpu.VMEM((A_HEADS, ncache + 1, tq, KTILE), F32),
                        pltpu.VMEM((A_HEADS, tq, LANES), F32),
                        pltpu.VMEM((A_HEADS, tq, LANES), F32)],
        compiler_params=_params("parallel", "arbitrary", "arbitrary"),
        name="attn_sample",
    )(q, kc, vc, kn, vn, mask, bias)


def _t5_bucket(rel):
    half = N_BUCKETS // 2
    max_exact = half // 2
    n = jnp.abs(rel)
    nf = jnp.maximum(n, 1).astype(F32)
    large = max_exact + (jnp.log(nf / max_exact) / math.log(MAX_DISTANCE / max_exact)
                         * (half - max_exact)).astype(I32)
    large = jnp.minimum(large, half - 1)
    return jnp.where(rel > 0, half, 0) + jnp.where(n < max_exact, n, large)


def bias_tiles(rel_bias):
    heads = rel_bias.shape[1]
    span = 2 * TILE - 1

    def toeplitz(shift):
        rel = jnp.arange(span, dtype=I32) - (TILE - 1) + shift
        tab = rel_bias.astype(F32)[_t5_bucket(rel)].T
        strip = jnp.tile(jnp.pad(tab, ((0, 0), (0, 1))), (1, TILE))[:, :TILE * span]
        return strip.reshape(heads, TILE, span)[:, :, TILE - 1:]

    far = rel_bias.astype(F32)[_t5_bucket(jnp.full((1,), -(TILE + 1), I32))]
    far = jnp.broadcast_to(far.T[:, :, None], (heads, TILE, TILE))
    return jnp.stack([far, toeplitz(-TILE), toeplitz(0)])


def dsa_core_prompt(pr, nseq, seqlen, bias):
    q, _, kb, _, vb, qi, kiwi = pr
    nqb = seqlen // TILE
    wit = kiwi[:, IDX_DIM:IDX_DIM + IDX_HEADS].reshape(nseq * nqb, TILE, IDX_HEADS).swapaxes(1, 2)
    groups = _row_groups(0, seqlen, lambda off: [(0, None, off, LANES, None)])
    mask = index_mask(qi, wit, [kiwi], [pl.BlockSpec((seqlen, LANES), lambda b, i: (b, 0))],
                      groups, seqlen, nseq, nqb, causal=True)
    return attn_prompt(q, kb, vb, mask, bias, nseq, seqlen)


def dsa_core_sample(pr, k_cache, v_cache, ki_cache, nseq, tq, bias):
    q, kf, _, vf, _, qi, kiwi = pr
    past = k_cache.shape[1]
    per = TILE // tq
    wit = kiwi[:, IDX_DIM:IDX_DIM + IDX_HEADS].reshape(nseq // per, TILE, IDX_HEADS).swapaxes(1, 2)
    groups = (_row_groups(0, past, lambda off: [(0, j, off, IDX_DIM, j) for j in range(per)])
              + _row_groups(past, tq, lambda off: [(1, None, j * tq + off, LANES, j) for j in range(per)]))
    mask = index_mask(
        qi, wit, [ki_cache, kiwi],
        [pl.BlockSpec((per, past, IDX_DIM), lambda b, i: (b, 0, 0)),
         pl.BlockSpec((TILE, LANES), lambda b, i: (b, 0))],
        groups, past + tq, nseq // per, 1, causal=False)
    to_hdk = lambda a: a.transpose(0, 2, 3, 1)
    pad_keys = ((0, 0), (0, 0), (0, 0), (0, KTILE - tq))
    kn = jnp.pad(to_hdk(kf.reshape(nseq, tq, A_HEADS, A_HEAD_DIM)), pad_keys)
    vn = jnp.pad(to_hdk(vf.reshape(nseq, tq, A_HEADS, A_HEAD_DIM)), pad_keys)
    return attn_sample(q, to_hdk(k_cache), to_hdk(v_cache), kn, vn, mask, bias[:, :, :tq, :], nseq, tq)


def _log_sigmoid(x):
    return jnp.minimum(x, 0.0) - jnp.log1p(jnp.exp(-jnp.abs(x)))


def _mlstm_kernel(p_ref, gs_ref, gr_ref, gbc_ref, gbr_ref, hg_ref, c0_ref, n0_ref, m0_ref,
                  hs_ref, c_ref, n_ref, m_ref, cs, ns, ms):
    c_id = pl.program_id(1)
    nh = B_HEADS
    npair = nh // 2
    hw = LANES
    L = p_ref.shape[0]
    L2 = 2 * L

    @pl.when(c_id == 0)
    def _():
        cs[...] = jnp.zeros(cs.shape, F32)
        ns[...] = jnp.zeros(ns.shape, F32)
        ns[:, 0:B_QK_DIM] = n0_ref[0]
        for hd in range(nh):
            cs[hd // 2, (hd % 2) * hw:(hd % 2) * hw + B_QK_DIM, :] = c0_ref[0, hd]
        ms[...] = m0_ref[0]

    r = lax.broadcasted_iota(I32, (L2, L2), 0)
    c = lax.broadcasted_iota(I32, (L2, L2), 1)
    same = (r >> CHUNK_SHIFT) == (c >> CHUNK_SHIFT)
    incl = same & (c <= r)
    gs = gs_ref[0] + gbc_ref[...]
    gr = gr_ref[0] + gbr_ref[...]
    bcol = _dot_exact_lhs01(_bf01(incl), _log_sigmoid(gs[:, npair:]))
    brow = _dot_exact_rhs01(_log_sigmoid(gr[npair:, :]), _bf01(same & (r <= c)))
    icol = gs[:, :npair]
    irow = gr[:npair, :]
    rowc = lax.broadcasted_iota(I32, (L2, 1), 0)
    is_top = rowc < L
    top = is_top.astype(F32)
    bot = 1.0 - top
    row2 = lax.broadcasted_iota(I32, (2 * hw, 1), 0)
    hg = hg_ref[...]
    mall = ms[...]

    def stack(base, pr):
        a = base + 2 * pr * hw
        return jnp.concatenate([p_ref[:, a:a + hw], p_ref[:, a + hw:a + 2 * hw]], axis=0)

    def bd(x):
        return _bf(jnp.concatenate([x * top, x * bot], axis=1))

    pairs = range(npair)
    qf = [stack(0, pr) for pr in pairs]
    kf = [stack(nh * hw, pr) * (B_QK_DIM ** -0.5) for pr in pairs]
    vb = [_bf(stack(2 * nh * hw, pr)) for pr in pairs]
    bc = [bcol[:, pr:pr + 1] for pr in pairs]
    ic = [icol[:, pr:pr + 1] for pr in pairs]
    mc = [mall[:, pr:pr + 1] for pr in pairs]
    d = [jnp.where(incl, bc[pr] - brow[pr:pr + 1, :] + irow[pr:pr + 1, :], NEG) for pr in pairs]
    inter = [bc[pr] + mc[pr] for pr in pairs]
    mt = [jnp.maximum(inter[pr], jnp.max(d[pr], axis=-1, keepdims=True)) for pr in pairs]
    s = [_dot_nt(_bf(qf[pr]), _bf(kf[pr])) * jnp.exp(d[pr] - mt[pr]) for pr in pairs]
    wst = [jnp.exp(inter[pr] - mt[pr]) for pr in pairs]
    cmat = [cs[pr] for pr in pairs]
    num = [_dot(_bf(s[pr]), vb[pr]) + wst[pr] * _dot(bd(qf[pr]), _bf(cmat[pr])) for pr in pairs]
    for pr in pairs:
        n0 = ns[2 * pr:2 * pr + 1, :]
        n1 = ns[2 * pr + 1:2 * pr + 2, :]
        qn = jnp.sum(qf[pr] * jnp.where(is_top, n0, n1), axis=-1, keepdims=True)
        den = jnp.sum(s[pr], axis=-1, keepdims=True) + wst[pr] * qn
        hs = num[pr] / jnp.maximum(jnp.abs(den), jnp.exp(-mt[pr]))
        bl0 = bc[pr][L - 1:L, :]
        bl1 = bc[pr][L2 - 1:L2, :]
        bl = jnp.where(is_top, bl0, bl1)
        dec = bl - bc[pr] + ic[pr]
        blm = bl + mc[pr]
        mnew0 = jnp.maximum(blm[0:1, :], jnp.max(dec[0:L], axis=0, keepdims=True))
        mnew1 = jnp.maximum(blm[L:L + 1, :], jnp.max(dec[L:L2], axis=0, keepdims=True))
        mnew = jnp.where(is_top, mnew0, mnew1)
        wk = jnp.exp(dec - mnew)
        ws = jnp.exp(blm - mnew)
        kw = kf[pr] * wk
        ws2 = jnp.where(row2 < hw, ws[0:1, :], ws[L:L + 1, :])
        cs[pr] = ws2 * cmat[pr] + _dot_tn(bd(kw), vb[pr])
        ns[2 * pr:2 * pr + 1, :] = ws[0:1, :] * n0 + jnp.sum(kw[0:L], axis=0, keepdims=True)
        ns[2 * pr + 1:2 * pr + 2, :] = ws[L:L + 1, :] * n1 + jnp.sum(kw[L:L2], axis=0, keepdims=True)
        ms[:, pr:pr + 1] = mnew
        on = _bf(_rms(hs, hg) * _sigmoid(stack(3 * nh * hw, pr)))
        oa = 2 * pr * B_V_DIM
        hs_ref[:, oa:oa + B_V_DIM] = on[0:L]
        hs_ref[:, oa + B_V_DIM:oa + 2 * B_V_DIM] = on[L:L2]

    @pl.when(c_id == pl.num_programs(1) - 1)
    def _():
        for hd in range(nh):
            c_ref[0, hd] = cs[hd // 2, (hd % 2) * hw:(hd % 2) * hw + B_QK_DIM, :]
        n_ref[0] = ns[:, 0:B_QK_DIM]
        m_ref[0] = ms[...]


def mlstm_weight(w_in):
    d = w_in.shape[0]
    nq = B_HEADS * B_QK_DIM
    nv = B_HEADS * B_V_DIM

    def padh(w):
        w = w.reshape(d, B_HEADS, B_QK_DIM)
        return jnp.pad(w, ((0, 0), (0, 0), (0, LANES - B_QK_DIM))).reshape(d, B_HEADS * LANES)

    main = 2 * nq + 2 * nv
    tail = jnp.pad(w_in[:, main:], ((0, 0), (0, LANES - 2 * B_HEADS)))
    return _bf(jnp.concatenate([padh(w_in[:, :nq]), padh(w_in[:, nq:2 * nq]), w_in[:, 2 * nq:main], tail], axis=1))


def mlstm_core(p, gate_bias, h_gain, c0, n0, m0, nseq, nchunk):
    n, m = p.shape
    nh = B_HEADS
    npair = nh // 2
    gcol = 4 * nh * LANES
    pre = p[:, gcol:gcol + 2 * nh].reshape(nseq * nchunk, CHUNK, 2, npair, 2)
    gs = pre.transpose(0, 4, 1, 2, 3).reshape(nseq * nchunk, 2 * CHUNK, 2 * npair)
    gr = pre.transpose(0, 2, 3, 4, 1).reshape(nseq * nchunk, 2 * npair, 2 * CHUNK)
    gb = gate_bias.reshape(2, npair, 2)
    gbc = jnp.repeat(gb.transpose(2, 0, 1).reshape(2, 2 * npair), CHUNK, axis=0)
    m0s = jnp.repeat(m0.reshape(nseq, npair, 2).transpose(0, 2, 1), CHUNK, axis=1)
    o, c_new, n_new, m_new = pl.pallas_call(
        _mlstm_kernel,
        grid=(nseq, nchunk),
        in_specs=[pl.BlockSpec((CHUNK, m), lambda s, c: (s * nchunk + c, 0)),
                  pl.BlockSpec((1, 2 * CHUNK, 2 * npair), lambda s, c: (s * nchunk + c, 0, 0)),
                  pl.BlockSpec((1, 2 * npair, 2 * CHUNK), lambda s, c: (s * nchunk + c, 0, 0)),
                  _full((2 * CHUNK, 2 * npair)), _full((2 * npair, 2 * CHUNK)), _full((1, B_V_DIM)),
                  pl.BlockSpec((1, nh, B_QK_DIM, B_V_DIM), lambda s, c: (s, 0, 0, 0)),
                  pl.BlockSpec((1, nh, B_QK_DIM), lambda s, c: (s, 0, 0)),
                  pl.BlockSpec((1, 2 * CHUNK, npair), lambda s, c: (s, 0, 0))],
        out_specs=[pl.BlockSpec((CHUNK, nh * B_V_DIM), lambda s, c: (s * nchunk + c, 0)),
                   pl.BlockSpec((1, nh, B_QK_DIM, B_V_DIM), lambda s, c: (s, 0, 0, 0)),
                   pl.BlockSpec((1, nh, B_QK_DIM), lambda s, c: (s, 0, 0)),
                   pl.BlockSpec((1, 2 * CHUNK, npair), lambda s, c: (s, 0, 0))],
        out_shape=[jax.ShapeDtypeStruct((n, nh * B_V_DIM), BF16),
                   jax.ShapeDtypeStruct((nseq, nh, B_QK_DIM, B_V_DIM), F32),
                   jax.ShapeDtypeStruct((nseq, nh, B_QK_DIM), F32),
                   jax.ShapeDtypeStruct((nseq, 2 * CHUNK, npair), F32)],
        scratch_shapes=[pltpu.VMEM((npair, 2 * LANES, B_V_DIM), F32), pltpu.VMEM((nh, LANES), F32),
                        pltpu.VMEM((2 * CHUNK, npair), F32)],
        compiler_params=_params("parallel", "arbitrary"),
        name="mlstm",
    )(p, gs, gr, gbc, gbc.T, h_gain.reshape(1, B_V_DIM), c0, n0, m0s)
    m_heads = m_new[:, ::CHUNK, :].transpose(0, 2, 1).reshape(nseq, nh)
    return o, c_new, n_new, m_heads


def _split2(x):
    hi = _bf(x)
    return hi, _bf(x - hi.astype(F32))


def _cat3_lhs(x):
    hi, mid = _split2(x)
    return jnp.concatenate([hi, hi, mid], axis=1)


def _cat3_rhs(x):
    hi, mid = _split2(x)
    return jnp.concatenate([hi, mid, hi], axis=0)


def _gdn_kernel(p_ref, gs_ref, gr_ref, cw_ref, alc_ref, alr_ref, dtc_ref, dtr_ref, og_ref, s0_ref, cb0_ref,
                o_ref, s_ref, cb_ref, ss, tail):
    c_id = pl.program_id(1)
    nh = C_HEADS
    cdim = nh * (2 * C_DK + C_DV)
    L = p_ref.shape[0]
    L2 = 2 * L
    nprev = CONV_W - 1

    @pl.when(c_id == 0)
    def _():
        ss[...] = s0_ref[0]
        tail[...] = jnp.zeros(tail.shape, F32)
        tail[8 - nprev:8, :] = cb0_ref[0]

    x = p_ref[:, 0:cdim]
    ext = jnp.concatenate([tail[...], x], axis=0)
    conv = ext[8:8 + L] * cw_ref[CONV_W - 1:CONV_W, :]
    for j in range(CONV_W - 1):
        conv = conv + ext[8 - nprev + j:8 - nprev + j + L] * cw_ref[j:j + 1, :]
    tail[...] = x[L - 8:L, :]
    cf = _silu(conv)

    npair = nh // 2
    gs = gs_ref[0]
    gr = gr_ref[0]
    beta = _sigmoid(gs[:, 0:npair])
    g_col = -jnp.exp(alc_ref[...]) * _softplus(gs[:, npair:] + dtc_ref[...])
    g_row = -jnp.exp(alr_ref[...]) * _softplus(gr[npair:, :] + dtr_ref[...])
    r = lax.broadcasted_iota(I32, (L2, L2), 0)
    c = lax.broadcasted_iota(I32, (L2, L2), 1)
    same = (r >> CHUNK_SHIFT) == (c >> CHUNK_SHIFT)
    incl = same & (c <= r)
    strict = same & (c < r)
    eye = (c == r).astype(F32)
    gc_col = _dot_exact_lhs01(_bf01(incl), g_col)
    gc_row = _dot_exact_rhs01(g_row, _bf01(same & (r <= c)))
    rowc = lax.broadcasted_iota(I32, (L2, 1), 0)
    top = (rowc < L).astype(F32)
    bot = 1.0 - top
    row2 = lax.broadcasted_iota(I32, (2 * C_DK, 1), 0)
    og = og_ref[...]

    def stack(base, pr):
        a = base + 2 * pr * C_DK
        return jnp.concatenate([cf[:, a:a + C_DK], cf[:, a + C_DK:a + 2 * C_DK]], axis=0)

    def bd(x):
        return _bf(jnp.concatenate([x * top, x * bot], axis=1))

    pairs = range(npair)
    qc, kc, kcb, dm, amat, rhs, gcols, egcs = [], [], [], [], [], [], [], []
    for pr in pairs:
        qraw = stack(0, pr)
        kraw = stack(nh * C_DK, pr)
        vc = stack(2 * nh * C_DK, pr)
        qc.append(qraw * lax.rsqrt(jnp.sum(qraw * qraw, axis=-1, keepdims=True) + EPS) * (C_DK ** -0.5))
        kc.append(kraw * lax.rsqrt(jnp.sum(kraw * kraw, axis=-1, keepdims=True) + EPS))
        bc = beta[:, pr:pr + 1]
        gcol = gc_col[:, pr:pr + 1]
        grow = gc_row[pr:pr + 1, :]
        dm.append(jnp.where(incl, jnp.exp(jnp.where(incl, gcol - grow, 0.0)), 0.0))
        kb = kc[pr] * bc
        kcb.append(_bf(kc[pr]))
        amat.append(jnp.where(strict, _dot_nt(_bf(kb), kcb[pr]) * dm[pr], 0.0))
        egc = jnp.exp(gcol)
        rhs.append(jnp.concatenate([vc * bc, kb * egc], axis=-1))
        gcols.append(gcol)
        egcs.append(egc)
    tinv = [eye - amat[pr] for pr in pairs]
    pw_l = [_cat3_lhs(-amat[pr]) for pr in pairs]
    pw_r = [_cat3_rhs(-amat[pr]) for pr in pairs]
    for _ in range(CHUNK_SHIFT - 1):
        pw = [_dot(pw_l[pr], pw_r[pr]) for pr in pairs]
        pw_l = [_cat3_lhs(pw[pr]) for pr in pairs]
        pw_r = [_cat3_rhs(pw[pr]) for pr in pairs]
        tinv = [tinv[pr] + _dot(_cat3_lhs(tinv[pr]), pw_r[pr]) for pr in pairs]
    sol = [_dot(_cat3_lhs(tinv[pr]), _cat3_rhs(rhs[pr])) for pr in pairs]
    attn = [_dot_nt(_bf(qc[pr]), kcb[pr]) * dm[pr] for pr in pairs]
    smat = [ss[pr] for pr in pairs]
    sb = [_bf(smat[pr]) for pr in pairs]
    vnew = [sol[pr][:, :C_DV] - _dot(bd(sol[pr][:, C_DV:]), sb[pr]) for pr in pairs]
    o = [_dot(bd(qc[pr] * egcs[pr]), sb[pr]) + _dot(_bf(attn[pr]), _bf(vnew[pr])) for pr in pairs]
    for pr in pairs:
        gl0 = gcols[pr][L - 1:L, :]
        gl1 = gcols[pr][L2 - 1:L2, :]
        ke = kc[pr] * jnp.exp(jnp.where(rowc < L, gl0, gl1) - gcols[pr])
        decay = jnp.exp(jnp.where(row2 < C_DK, gl0, gl1))
        ss[pr] = smat[pr] * decay + _dot_tn(bd(ke), _bf(vnew[pr]))
    for pr in pairs:
        za = cdim + 2 * pr * C_DV
        z = jnp.concatenate([p_ref[:, za:za + C_DV], p_ref[:, za + C_DV:za + 2 * C_DV]], axis=0)
        on = _bf(_rms(o[pr], og) * _silu(z))
        oa = 2 * pr * C_DV
        o_ref[:, oa:oa + C_DV] = on[0:L]
        o_ref[:, oa + C_DV:oa + 2 * C_DV] = on[L:L2]

    @pl.when(c_id == pl.num_programs(1) - 1)
    def _():
        s_ref[0] = ss[...]
        cb_ref[0] = tail[8 - nprev:8, :]


def gdn_weight(w_in):
    main = C_HEADS * (2 * C_DK + C_DV) + C_HEADS * C_DV
    tail = jnp.pad(w_in[:, main:], ((0, 0), (0, LANES - 2 * C_HEADS)))
    return _bf(jnp.concatenate([w_in[:, :main], tail], axis=1))


def gdn_core(p, conv_w, a_log, dt_bias, o_gain, s0, cb0, nseq, nchunk):
    n, m = p.shape
    nh = C_HEADS
    npair = nh // 2
    cdim = nh * (2 * C_DK + C_DV)
    gcolumn = cdim + nh * C_DV
    pre = p[:, gcolumn:gcolumn + 2 * nh].reshape(nseq * nchunk, CHUNK, 2, npair, 2)
    gs = pre.transpose(0, 4, 1, 2, 3).reshape(nseq * nchunk, 2 * CHUNK, 2 * npair)
    gr = pre.transpose(0, 2, 3, 4, 1).reshape(nseq * nchunk, 2 * npair, 2 * CHUNK)

    def col(v):
        return jnp.repeat(v.reshape(npair, 2).T, CHUNK, axis=0)

    o, s_new, cb_new = pl.pallas_call(
        _gdn_kernel,
        grid=(nseq, nchunk),
        in_specs=[pl.BlockSpec((CHUNK, m), lambda s, c: (s * nchunk + c, 0)),
                  pl.BlockSpec((1, 2 * CHUNK, 2 * npair), lambda s, c: (s * nchunk + c, 0, 0)),
                  pl.BlockSpec((1, 2 * npair, 2 * CHUNK), lambda s, c: (s * nchunk + c, 0, 0)),
                  _full((CONV_W, cdim)),
                  _full((2 * CHUNK, npair)), _full((npair, 2 * CHUNK)),
                  _full((2 * CHUNK, npair)), _full((npair, 2 * CHUNK)),
                  _full((1, C_DV)),
                  pl.BlockSpec((1, npair, 2 * C_DK, C_DV), lambda s, c: (s, 0, 0, 0)),
                  pl.BlockSpec((1, CONV_W - 1, cdim), lambda s, c: (s, 0, 0))],
        out_specs=[pl.BlockSpec((CHUNK, nh * C_DV), lambda s, c: (s * nchunk + c, 0)),
                   pl.BlockSpec((1, npair, 2 * C_DK, C_DV), lambda s, c: (s, 0, 0, 0)),
                   pl.BlockSpec((1, CONV_W - 1, cdim), lambda s, c: (s, 0, 0))],
        out_shape=[jax.ShapeDtypeStruct((n, nh * C_DV), BF16),
                   jax.ShapeDtypeStruct((nseq, npair, 2 * C_DK, C_DV), F32),
                   jax.ShapeDtypeStruct((nseq, CONV_W - 1, cdim), F32)],
        scratch_shapes=[pltpu.VMEM((npair, 2 * C_DK, C_DV), F32), pltpu.VMEM((8, cdim), F32)],
        compiler_params=_params("parallel", "arbitrary"),
        name="gdn",
    )(p, gs, gr, conv_w, col(a_log), col(a_log).T, col(dt_bias), col(dt_bias).T, o_gain.reshape(1, C_DV),
      s0.reshape(nseq, npair, 2 * C_DK, C_DV), cb0)
    return o, s_new.reshape(nseq, nh, C_DK, C_DV), cb_new


def _trunk(x, nseq, seqlen, mem_k, mem_v, st, W, bias, is_prompt):
    d = x.shape[-1]
    n = nseq * seqlen
    x = x.reshape(n, d)
    tm = min(256, n)
    tm_mem = min(MEM_ROWS, n)
    new = {}
    for i in range(4):
        kind = i % 3
        mx = W["mixer"][i]
        if kind == 0:
            tstate = is_prompt and seqlen % tm == 0
            pr = proj_dsa(x, W["norm_mix"][i], mx["w_in"], mx["q_gain"], mx["k_gain"], mx["ki_gain"], tm,
                          nseq, tstate)
            if is_prompt:
                o = dsa_core_prompt(pr[:7], nseq, seqlen, bias)
            else:
                o = dsa_core_sample(pr, *st[i], nseq, seqlen, bias)
            if tstate:
                new[i] = (pr[1].transpose(0, 3, 1, 2), pr[3].transpose(0, 3, 1, 2), pr[7].transpose(0, 2, 1))
            else:
                new[i] = (pr[1].reshape(nseq, seqlen, A_HEADS, A_HEAD_DIM),
                          pr[3].reshape(nseq, seqlen, A_HEADS, A_HEAD_DIM),
                          pr[6][:, :IDX_DIM].reshape(nseq, seqlen, IDX_DIM))
        elif kind == 1:
            p = proj(x, W["norm_mix"][i], mx["w_in"], tm)
            o, c_new, n_new, m_new = mlstm_core(p, mx["gate_bias"], mx["h_gain"], *st[i], nseq, seqlen // CHUNK)
            new[i] = (c_new, n_new, m_new.reshape(nseq, B_HEADS))
        else:
            p = proj(x, W["norm_mix"][i], mx["w_in"], tm)
            o, s_new, cb_new = gdn_core(p, mx["conv_w"], mx["a_log"], mx["dt_bias"], mx["o_gain"], *st[i],
                                        nseq, seqlen // CHUNK)
            new[i] = (s_new, cb_new)
        x = mem_attend(x, o, mx["w_out"], W["norm_mem"][i], W["w_mq"][i], W["mq_gain"][i],
                       mem_k[i], mem_v[i], W["w_mo"][i], tm_mem, seqlen)
        x = ffn(x, W["norm_ffn"][i], W["w_ffn1"][i], W["w_ffn3"][i], W["w_ffn2"][i], min(FFN_ROWS, n))
    return x.reshape(nseq, seqlen, d), new


def kernel(x_prompt, x_sample, mem_prompt, cache_l0_k, cache_l0_v, cache_l0_kidx, state_l1_C, state_l1_n, state_l1_m, state_l2_S, state_l2_conv, cache_l3_k, cache_l3_v, cache_l3_kidx, cache_mem_k, cache_mem_v, rel_bias, norm_mix, norm_mem, norm_ffn, mem_norm, w_mq, w_mk, w_mv, w_mo, mq_gain, mk_gain, w_ffn1, w_ffn3, w_ffn2, a0_w_in, a0_w_out, a0_q_gain, a0_k_gain, a0_kidx_gain, b1_w_in, b1_gate_bias, b1_h_gain, b1_w_out, c2_w_in, c2_conv_w, c2_a_log, c2_dt_bias, c2_o_gain, c2_w_out, a3_w_in, a3_w_out, a3_q_gain, a3_k_gain, a3_kidx_gain):
    B, T, D = x_prompt.shape
    S, Ts, _ = x_sample.shape
    depth = w_mq.shape[0]
    mlen = mem_prompt.shape[1]
    mw = MEM_HEADS * MEM_HEAD_DIM

    def dsa_w(w_in, w_out, qg, kg, kig):
        return dict(w_in=dsa_weight(w_in), w_out=_bf(w_out), q_gain=qg, k_gain=kg, ki_gain=kig)

    W = dict(
        norm_mix=norm_mix, norm_mem=norm_mem, norm_ffn=norm_ffn,
        w_mq=_bf(w_mq), w_mo=_bf(w_mo), mq_gain=mq_gain,
        w_ffn1=_bf(w_ffn1), w_ffn3=_bf(w_ffn3), w_ffn2=_bf(w_ffn2),
        mixer={
            0: dsa_w(a0_w_in, a0_w_out, a0_q_gain, a0_k_gain, a0_kidx_gain),
            1: dict(w_in=mlstm_weight(b1_w_in), gate_bias=b1_gate_bias, h_gain=b1_h_gain, w_out=_bf(b1_w_out)),
            2: dict(w_in=gdn_weight(c2_w_in), conv_w=c2_conv_w, a_log=c2_a_log, dt_bias=c2_dt_bias,
                    o_gain=c2_o_gain, w_out=_bf(c2_w_out)),
            3: dsa_w(a3_w_in, a3_w_out, a3_q_gain, a3_k_gain, a3_kidx_gain),
        },
    )
    bias = bias_tiles(rel_bias)

    mk_p, mv_p = mem_kv(mem_prompt.reshape(B * mlen, D), mem_norm, w_mk, w_mv, mk_gain)
    mk_p = mk_p.reshape(depth, B, mlen, mw)
    mv_p = mv_p.reshape(depth, B, mlen, mw)
    st_p = {
        0: None,
        1: (jnp.zeros((B, B_HEADS, B_QK_DIM, B_V_DIM), F32), jnp.zeros((B, B_HEADS, B_QK_DIM), F32),
            jnp.full((B, B_HEADS), NEG, F32)),
        2: (jnp.zeros((B, C_HEADS, C_DK, C_DV), F32), jnp.zeros((B, CONV_W - 1, state_l2_conv.shape[-1]), F32)),
        3: None,
    }
    y_p, np_ = _trunk(x_prompt, B, T, mk_p, mv_p, st_p, W, bias, True)

    st_s = {
        0: (cache_l0_k, cache_l0_v, cache_l0_kidx),
        1: (state_l1_C, state_l1_n, state_l1_m),
        2: (state_l2_S, state_l2_conv),
        3: (cache_l3_k, cache_l3_v, cache_l3_kidx),
    }
    mk_s = cache_mem_k.reshape(depth, S, mlen, mw)
    mv_s = cache_mem_v.reshape(depth, S, mlen, mw)
    y_s, ns_ = _trunk(x_sample, S, Ts, mk_s, mv_s, st_s, W, bias, False)

    shp = (depth, B, mlen, MEM_HEADS, MEM_HEAD_DIM)
    return (y_p, y_s,
            *np_[0], *np_[1], *np_[2], *np_[3], mk_p.reshape(shp), mv_p.reshape(shp),
            *ns_[0], *ns_[1], *ns_[2], *ns_[3])
```

```python
import functools
import math

import jax
import jax.numpy as jnp
from jax import lax
from jax.experimental import pallas as pl
from jax.experimental.pallas import tpu as pltpu

F32 = jnp.float32
BF16 = jnp.bfloat16
I32 = jnp.int32

EPS = 1e-6
NEG = -1e30
CHUNK = 64
CHUNK_SHIFT = 6
LANES = 128
TILE = 128
KTILE = 256
COUNT_CHAINS = 8
SCORE_ROWS = 256
PROJ_ROWS = 512
FFN_ROWS = 512
MEM_ROWS = 512
SCAN_SEQS = 2
CACHE_BLOCK = 1024
MASK_CASE_ROWS = 512
VMEM_LIMIT = 56 * 1024 * 1024

A_HEADS, A_HEAD_DIM = 16, 64
IDX_HEADS, IDX_DIM = 8, 64
TOPK_MAX = 256
N_BUCKETS, MAX_DISTANCE = 32, 128
B_HEADS, B_QK_DIM, B_V_DIM = 8, 64, 128
C_HEADS, C_DK, C_DV = 8, 128, 128
CONV_W = 4
MEM_HEADS, MEM_HEAD_DIM = 4, 128


def _bf(x):
    return x.astype(BF16)


def _bf01(mask):
    return mask.astype(F32).astype(BF16)


def _dot(a, b):
    return jnp.dot(a, b, preferred_element_type=F32)


def _dot_nt(a, b):
    return lax.dot_general(a, b, (((1,), (1,)), ((), ())), preferred_element_type=F32)


def _dot_tn(a, b):
    return lax.dot_general(a, b, (((0,), (0,)), ((), ())), preferred_element_type=F32)


def _split3(x):
    hi = _bf(x)
    r1 = x - hi.astype(F32)
    mid = _bf(r1)
    lo = _bf(r1 - mid.astype(F32))
    return hi, mid, lo


def _dot_exact_rhs01(x, m01):
    hi, mid, lo = _split3(x)
    return _dot(hi, m01) + _dot(mid, m01) + _dot(lo, m01)


def _dot_exact_lhs01(m01, x):
    hi, mid, lo = _split3(x)
    return _dot(m01, hi) + _dot(m01, mid) + _dot(m01, lo)


def _dot_f32(a, b):
    ah, am, al = _split3(a)
    bh, bm, bl = _split3(b)
    return (_dot(ah, bh) + (_dot(ah, bm) + _dot(am, bh))
            + (_dot(am, bm) + _dot(ah, bl) + _dot(al, bh)))


def _rms(x, g):
    ms = jnp.mean(x * x, axis=-1, keepdims=True)
    return x * lax.rsqrt(ms + EPS) * g


def _sigmoid(x):
    return 1.0 / (1.0 + jnp.exp(-x))


def _silu(x):
    return x * _sigmoid(x)


def _softplus(x):
    return jnp.maximum(x, 0.0) + jnp.log1p(jnp.exp(-jnp.abs(x)))


def _params(*sem):
    return pltpu.CompilerParams(dimension_semantics=sem, vmem_limit_bytes=VMEM_LIMIT)


def _full(shape):
    n = len(shape)
    return pl.BlockSpec(shape, lambda *_: (0,) * n)


def _proj_kernel(x_ref, g_ref, w_ref, o_ref, *, col_chunk):
    h = _bf(_rms(x_ref[...], g_ref[...]))
    m = w_ref.shape[1]
    for c in range(0, m, col_chunk):
        e = min(c + col_chunk, m)
        o_ref[:, c:e] = _dot(h, w_ref[:, c:e])


def proj(x, g, w, tm):
    n, d = x.shape
    m = w.shape[1]
    return pl.pallas_call(
        functools.partial(_proj_kernel, col_chunk=512),
        grid=(n // tm,),
        in_specs=[pl.BlockSpec((tm, d), lambda i: (i, 0)), _full((1, d)), _full((d, m))],
        out_specs=pl.BlockSpec((tm, m), lambda i: (i, 0)),
        out_shape=jax.ShapeDtypeStruct((n, m), F32),
        compiler_params=_params("parallel"),
        name="proj",
    )(x, g.reshape(1, d), w)


def _memkv_kernel(x_ref, g_ref, wk_ref, wv_ref, kg_ref, k_ref, v_ref):
    h = _bf(_rms(x_ref[...], g_ref[0]))
    k = _dot(h, wk_ref[0])
    v_ref[0] = _dot(h, wv_ref[0])
    kg = kg_ref[0]
    for hd in range(MEM_HEADS):
        sl = slice(hd * MEM_HEAD_DIM, (hd + 1) * MEM_HEAD_DIM)
        k_ref[0, :, sl] = _rms(k[:, sl], kg)


def mem_kv(mem2d, mem_norm, w_mk, w_mv, mk_gain, tm=256):
    n, d = mem2d.shape
    depth = w_mk.shape[0]
    mw = w_mk.shape[2]
    return pl.pallas_call(
        _memkv_kernel,
        grid=(depth, n // tm),
        in_specs=[pl.BlockSpec((tm, d), lambda l, i: (i, 0)),
                  pl.BlockSpec((1, 1, d), lambda l, i: (l, 0, 0)),
                  pl.BlockSpec((1, d, mw), lambda l, i: (l, 0, 0)),
                  pl.BlockSpec((1, d, mw), lambda l, i: (l, 0, 0)),
                  pl.BlockSpec((1, 1, MEM_HEAD_DIM), lambda l, i: (l, 0, 0))],
        out_specs=[pl.BlockSpec((1, tm, mw), lambda l, i: (l, i, 0)),
                   pl.BlockSpec((1, tm, mw), lambda l, i: (l, i, 0))],
        out_shape=[jax.ShapeDtypeStruct((depth, n, mw), F32)] * 2,
        compiler_params=_params("parallel", "parallel"),
        name="mem_kv",
    )(mem2d, mem_norm.reshape(depth, 1, d), _bf(w_mk), _bf(w_mv), mk_gain.reshape(depth, 1, MEM_HEAD_DIM))


def _memattn_kernel(x_ref, o_ref, wo_ref, g_ref, wq_ref, qg_ref, mk_ref, mv_ref, wmo_ref, y_ref):
    x1 = x_ref[...] + _dot(o_ref[...], wo_ref[...])
    h = _bf(_rms(x1, g_ref[...]))
    q = _dot(h, wq_ref[...])
    qg = qg_ref[...]
    scale = MEM_HEAD_DIM ** -0.5
    nsub = mk_ref.shape[0]
    rows = x1.shape[0] // nsub
    cells = [(hd, s) for hd in range(MEM_HEADS) for s in range(nsub)]
    hsl = lambda hd: slice(hd * MEM_HEAD_DIM, (hd + 1) * MEM_HEAD_DIM)
    qh = [_bf(_rms(q[:, hsl(hd)], qg)) for hd in range(MEM_HEADS)]
    logits = [_dot_nt(qh[hd][s * rows:(s + 1) * rows], _bf(mk_ref[s, :, hsl(hd)])) * scale for hd, s in cells]
    ps = [jnp.exp(l - jnp.max(l, axis=-1, keepdims=True)) for l in logits]
    ps = [_bf(p / jnp.sum(p, axis=-1, keepdims=True)) for p in ps]
    pv = [_bf(_dot(p, _bf(mv_ref[s, :, hsl(hd)]))) for p, (hd, s) in zip(ps, cells)]
    outs = []
    for hd in range(MEM_HEADS):
        subs = pv[hd * nsub:(hd + 1) * nsub]
        outs.append(subs[0] if nsub == 1 else jnp.concatenate(subs, axis=0))
    att = jnp.concatenate(outs, axis=-1)
    y_ref[...] = x1 + _dot(att, wmo_ref[...])


def mem_attend(x, o, w_out, g, w_mq, mq_gain, mk, mv, w_mo, tm, seqlen):
    n, d = x.shape
    mlen, mw = mk.shape[1], mk.shape[2]
    tiles_per_seq = max(1, seqlen // tm)
    seqs_per_tile = max(1, tm // seqlen)
    return pl.pallas_call(
        _memattn_kernel,
        grid=(n // tm,),
        in_specs=[pl.BlockSpec((tm, d), lambda i: (i, 0)),
                  pl.BlockSpec((tm, o.shape[1]), lambda i: (i, 0)),
                  _full(w_out.shape), _full((1, d)), _full(w_mq.shape), _full((1, MEM_HEAD_DIM)),
                  pl.BlockSpec((seqs_per_tile, mlen, mw), lambda i: (i // tiles_per_seq, 0, 0)),
                  pl.BlockSpec((seqs_per_tile, mlen, mw), lambda i: (i // tiles_per_seq, 0, 0)),
                  _full(w_mo.shape)],
        out_specs=pl.BlockSpec((tm, d), lambda i: (i, 0)),
        out_shape=jax.ShapeDtypeStruct((n, d), F32),
        compiler_params=_params("parallel"),
        name="mem_attend",
    )(x, o, w_out, g.reshape(1, d), w_mq, mq_gain.reshape(1, MEM_HEAD_DIM), mk, mv, w_mo)


def _ffn_kernel(x_ref, g_ref, w1_ref, w3_ref, w2_ref, y_ref, *, hid_chunk):
    x = x_ref[...]
    h = _bf(_rms(x, g_ref[...]))
    hidden = w1_ref.shape[1]
    y_ref[...] = x
    for c in range(0, hidden, hid_chunk):
        a = _dot(h, w1_ref[:, c:c + hid_chunk])
        b = _dot(h, w3_ref[:, c:c + hid_chunk])
        y_ref[...] += _dot(_bf(_silu(a) * b), w2_ref[c:c + hid_chunk, :])


def ffn(x, g, w1, w3, w2, tm):
    n, d = x.shape
    hidden = w1.shape[1]
    return pl.pallas_call(
        functools.partial(_ffn_kernel, hid_chunk=256),
        grid=(n // tm,),
        in_specs=[pl.BlockSpec((tm, d), lambda i: (i, 0)), _full((1, d)),
                  _full((d, hidden)), _full((d, hidden)), _full((hidden, d))],
        out_specs=pl.BlockSpec((tm, d), lambda i: (i, 0)),
        out_shape=jax.ShapeDtypeStruct((n, d), F32),
        compiler_params=_params("parallel"),
        name="ffn",
    )(x, g.reshape(1, d), w1, w3, w2)


def _proj_dsa_kernel(x_ref, g_ref, w_ref, qg_ref, kg_ref, kig_ref, bd_ref,
                     q_o, k_o, kb_o, v_o, vb_o, qi_o, kiwi_o, *maybe_kit_o, tstate):
    h = _bf(_rms(x_ref[...], g_ref[...]))
    tm = x_ref.shape[0]
    hq = A_HEADS * A_HEAD_DIM
    bd = bd_ref[...]
    inv_hd = 1.0 / A_HEAD_DIM
    nw = bd.shape[0]

    def head_norm(p, gain):
        hi, mid = _split2(p * p)
        ss = _dot(hi, bd) + _dot(mid, bd)
        return p * lax.rsqrt(ss * inv_hd + EPS) * gain

    def transposed(x):
        return jnp.concatenate([x[r:r + LANES, :].T for r in range(0, tm, LANES)], axis=1)

    def store_state(o_ref, x, col0):
        if not tstate:
            o_ref[:, col0:col0 + x.shape[1]] = x
            return
        for j in range(0, x.shape[1], LANES):
            xt = transposed(x[:, j:j + LANES])
            hd = (col0 + j) // A_HEAD_DIM
            for half in range(LANES // A_HEAD_DIM):
                o_ref[0, hd + half] = xt[half * A_HEAD_DIM:(half + 1) * A_HEAD_DIM]

    step = 512
    for c in range(0, hq, step):
        pq = _dot(h, w_ref[:, c:c + step])
        pk = _dot(h, w_ref[:, hq + c:hq + c + step])
        pv = _dot(h, w_ref[:, 2 * hq + c:2 * hq + c + step])
        for j in range(0, step, nw):
            sl = slice(c + j, c + j + nw)
            qn = head_norm(pq[:, j:j + nw], qg_ref[:, sl])
            q_o[:, sl] = _bf(qn * (A_HEAD_DIM ** -0.5))
            kn = head_norm(pk[:, j:j + nw], kg_ref[:, sl])
            store_state(k_o, kn, c + j)
            kb_o[:, sl] = _bf(kn)
        store_state(v_o, pv, c)
        vb_o[:, c:c + step] = _bf(pv)
    qiw = IDX_HEADS * LANES
    for c in range(0, qiw, step):
        qi_o[:, c:c + step] = _bf(_dot(h, w_ref[:, 3 * hq + c:3 * hq + c + step]))
    p = _dot(h, w_ref[:, 3 * hq + qiw:3 * hq + qiw + LANES])
    lane = lax.broadcasted_iota(I32, p.shape, 1)
    is_ki = lane < IDX_DIM
    ss = jnp.sum(jnp.where(is_ki, p * p, 0.0), axis=-1, keepdims=True)
    kin = p * lax.rsqrt(ss * (1.0 / IDX_DIM) + EPS) * kig_ref[...]
    kiwi = jnp.where(is_ki, kin, p)
    kiwi_o[...] = kiwi
    if tstate:
        maybe_kit_o[0][0] = transposed(kiwi)[0:IDX_DIM]


def proj_dsa(x, g, wa, q_gain, k_gain, ki_gain, tm, nseq, tstate):
    n, d = x.shape
    hq = A_HEADS * A_HEAD_DIM
    m = wa.shape[1]
    seqlen = n // nseq
    tiles = seqlen // tm
    qg = jnp.tile(q_gain, A_HEADS).reshape(1, hq)
    kg = jnp.tile(k_gain, A_HEADS).reshape(1, hq)
    kig = jnp.concatenate([ki_gain, jnp.ones((LANES - IDX_DIM,), F32)]).reshape(1, LANES)
    r = jnp.arange(2 * LANES)
    bd = _bf((r[:, None] // A_HEAD_DIM) == (r[None, :] // A_HEAD_DIM))
    row = lambda w: pl.BlockSpec((tm, w), lambda i: (i, 0))
    if tstate:
        state = pl.BlockSpec((1, A_HEADS, A_HEAD_DIM, tm), lambda i: (i // tiles, 0, 0, i % tiles))
        state_shape = jax.ShapeDtypeStruct((nseq, A_HEADS, A_HEAD_DIM, seqlen), F32)
        extra_specs = [pl.BlockSpec((1, IDX_DIM, tm), lambda i: (i // tiles, 0, i % tiles))]
        extra_shapes = [jax.ShapeDtypeStruct((nseq, IDX_DIM, seqlen), F32)]
    else:
        state, state_shape, extra_specs, extra_shapes = row(hq), jax.ShapeDtypeStruct((n, hq), F32), [], []
    return pl.pallas_call(
        functools.partial(_proj_dsa_kernel, tstate=tstate),
        grid=(n // tm,),
        in_specs=[row(d), _full((1, d)), _full((d, m)), _full((1, hq)), _full((1, hq)),
                  _full((1, LANES)), _full(bd.shape)],
        out_specs=[row(hq), state, row(hq), state, row(hq), row(IDX_HEADS * LANES), row(LANES)] + extra_specs,
        out_shape=[jax.ShapeDtypeStruct((n, hq), BF16), state_shape,
                   jax.ShapeDtypeStruct((n, hq), BF16), state_shape,
                   jax.ShapeDtypeStruct((n, hq), BF16),
                   jax.ShapeDtypeStruct((n, IDX_HEADS * LANES), BF16),
                   jax.ShapeDtypeStruct((n, LANES), F32)] + extra_shapes,
        compiler_params=_params("parallel"),
        name="proj_dsa",
    )(x, g.reshape(1, d), wa, qg, kg, kig, bd)


def dsa_weight(w_in):
    hq = A_HEADS * A_HEAD_DIM
    o3 = 3 * hq
    o4 = o3 + IDX_HEADS * IDX_DIM
    d = w_in.shape[0]
    wqi = w_in[:, o3:o4].reshape(d, IDX_HEADS, IDX_DIM)
    wqi = jnp.pad(wqi, ((0, 0), (0, 0), (0, LANES - IDX_DIM))).reshape(d, IDX_HEADS * LANES)
    tail = jnp.pad(w_in[:, o4:], ((0, 0), (0, LANES - (w_in.shape[1] - o4))))
    return _bf(jnp.concatenate([w_in[:, :o3], wqi, tail], axis=1))


def _sortable(s):
    b = pltpu.bitcast(s, I32)
    b = jnp.where(b == jnp.int32(-2 ** 31), 0, b)
    return jnp.where(b < 0, b ^ jnp.int32(0x7FFFFFFF), b)


def _index_mask_kernel(qi_ref, wit_ref, *rest, nref, groups, ltot, topk, causal, case_rows):
    ki_refs = rest[:nref]
    o_ref = rest[nref]
    key_ref, sel_ref, jv_ref = rest[nref + 1:]
    tq = qi_ref.shape[0]
    i = pl.program_id(1)
    lpad = sel_ref.shape[0]
    wit = wit_ref[0]
    if case_rows:
        ncase = (jnp.maximum((i + 1) * tq, topk) + case_rows - 1) // case_rows
        used_rows = ncase * case_rows
    else:
        used_rows = None

    def score_group(dst0, rows, sources):
        accs = []
        for rp, lead, src0, width, _ in sources:
            ref = ki_refs[rp]
            ki = _bf(ref[src0:src0 + rows, :] if lead is None else ref[lead, src0:src0 + rows, :])
            acc = jnp.zeros((rows, tq), F32)
            for hd in range(IDX_HEADS):
                rel = _dot_nt(ki, qi_ref[:, hd * LANES:hd * LANES + width])
                acc = acc + wit[hd:hd + 1, :] * jnp.maximum(rel, 0.0)
            accs.append(acc)
        if len(accs) == 1:
            acc = accs[0]
        else:
            lane = lax.broadcasted_iota(I32, (rows, tq), 1)
            acc = jnp.where(lane < tq // 2, accs[0], accs[1])
        s = acc * ((IDX_DIM ** -0.5) * (IDX_HEADS ** -0.5))
        if causal:
            kpos = dst0 + lax.broadcasted_iota(I32, (rows, tq), 0)
            qpos = i * tq + lax.broadcasted_iota(I32, (rows, tq), 1)
            s = jnp.where((kpos >> CHUNK_SHIFT) <= (qpos >> CHUNK_SHIFT), s, NEG)
        key_ref[dst0:dst0 + rows, :] = _sortable(s)

    for dst0, rows, sources in groups:
        if used_rows is None:
            score_group(dst0, rows, sources)
        else:
            pl.when(dst0 < used_rows)(functools.partial(score_group, dst0, rows, sources))

    def select(nrows):
        idx_bits = max(1, (nrows - 1).bit_length())

        def count(pred):
            c = pred.astype(I32).reshape(COUNT_CHAINS, nrows // COUNT_CHAINS, tq)
            return jnp.sum(jnp.sum(c, axis=1), axis=0, keepdims=True)

        def ge_count(cand):
            return count(key_ref[0:nrows, :] >= cand)

        t0 = jnp.full((1, tq), -2 ** 31, I32)
        t = jnp.where(ge_count(jnp.zeros((1, tq), I32)) >= topk, 0, t0)

        def vbody(it, t):
            cand = t + (jnp.int32(1) << (30 - it))
            return jnp.where(ge_count(cand) >= topk, cand, t)

        t = lax.fori_loop(0, 31, vbody, t)
        keys = key_ref[0:nrows, :]
        gt = keys > t
        eq = keys == t
        need = topk - count(gt)
        rowi = lax.broadcasted_iota(I32, (nrows, tq), 0)

        def jbody(it, jv):
            cand = jv + (jnp.int32(1) << (idx_bits - 1 - it))
            below = count(eq & (rowi < cand))
            return jnp.where(below < need, cand, jv)

        jv_ref[...] = jnp.full((1, tq), nrows, I32)
        has_tie = jnp.max(count(eq) - need) > 0

        @pl.when(has_tie)
        def _():
            jv_ref[...] = lax.fori_loop(0, idx_bits, jbody, jnp.zeros((1, tq), I32))

        sel = gt | (eq & (rowi <= jv_ref[...]))
        if causal:
            qpos = i * tq + lax.broadcasted_iota(I32, (nrows, tq), 1)
            sel = sel & ((rowi >> CHUNK_SHIFT) <= (qpos >> CHUNK_SHIFT))
        sel_ref[0:nrows, :] = jnp.where(sel, 0.0, NEG)
        nreal = -(-nrows // TILE)
        if nreal * TILE > nrows:
            sel_ref[nrows:nreal * TILE, :] = jnp.full((nreal * TILE - nrows, tq), NEG, F32)
        for kt in range(lpad // TILE):
            if kt < nreal:
                o_ref[0, 0, kt] = sel_ref[kt * TILE:(kt + 1) * TILE, :].T
            else:
                o_ref[0, 0, kt] = jnp.full((tq, TILE), NEG, F32)

    if case_rows:
        for k in range(ltot // case_rows):
            pl.when(ncase == k + 1)(functools.partial(select, (k + 1) * case_rows))
    else:
        select(ltot)


def index_mask(qi, wit, ki_arrays, ki_specs, groups, ltot, nstep, nqb, causal):
    lpad = -(-ltot // KTILE) * KTILE
    nt = lpad // TILE
    topk = min(TOPK_MAX, ltot // 4)
    case_rows = MASK_CASE_ROWS if (causal and ltot % MASK_CASE_ROWS == 0 and ltot > MASK_CASE_ROWS) else 0
    kern = functools.partial(_index_mask_kernel, nref=len(ki_arrays), groups=tuple(groups), ltot=ltot,
                             topk=topk, causal=causal, case_rows=case_rows)
    return pl.pallas_call(
        kern,
        grid=(nstep, nqb),
        in_specs=[pl.BlockSpec((TILE, qi.shape[1]), lambda b, i: (b * nqb + i, 0)),
                  pl.BlockSpec((1, IDX_HEADS, TILE), lambda b, i: (b * nqb + i, 0, 0))] + list(ki_specs),
        out_specs=pl.BlockSpec((1, 1, nt, TILE, TILE), lambda b, i: (b, i, 0, 0, 0)),
        out_shape=jax.ShapeDtypeStruct((nstep, nqb, nt, TILE, TILE), F32),
        scratch_shapes=[pltpu.VMEM((ltot, TILE), I32), pltpu.VMEM((lpad, TILE), F32),
                        pltpu.VMEM((1, TILE), I32)],
        compiler_params=_params("parallel", "parallel"),
        name="index_mask",
    )(qi, wit, *ki_arrays)


def _row_groups(row0, rows, make_sources):
    out = []
    for off in range(0, rows, SCORE_ROWS):
        out.append((row0 + off, min(SCORE_ROWS, rows - off), tuple(make_sources(off))))
    return out


def _attn_core(q_ref, o_ref, qs_ref, l_ref, mx_ref, acc_ref, groups, tq):
    npairs = A_HEADS // 2
    lane = lax.broadcasted_iota(I32, (1, LANES), 1)
    keep_lo = _bf((lane < A_HEAD_DIM).astype(F32))
    keep_hi = _bf((lane >= A_HEAD_DIM).astype(F32))
    for pair in range(npairs):
        qp = q_ref[:, pair * LANES:(pair + 1) * LANES]
        qs_ref[pair, 0:tq, :] = qp * keep_lo
        qs_ref[pair, tq:2 * tq, :] = qp * keep_hi
    mx_ref[...] = jnp.full(mx_ref.shape, NEG, F32)
    acc_ref[...] = jnp.zeros(acc_ref.shape, F32)
    ones = jnp.ones((KTILE, LANES), BF16)

    def over_tiles(count, body):
        if isinstance(count, int) and count == 1:
            body(0, 0)
        else:
            lax.fori_loop(0, count, body, 0)

    for count, base, k_tile, _, mask_sub, bias_sub in groups:
        def p1(kt, carry, base=base, k_tile=k_tile, mask_sub=mask_sub, bias_sub=bias_sub):
            masks = [mask_sub(kt, sub) for sub in range(KTILE // LANES)]
            for pair in range(npairs):
                sl = slice(pair * LANES, (pair + 1) * LANES)
                l = _dot_nt(qs_ref[pair], k_tile(kt, sl))
                for half in range(2):
                    rs = slice(half * tq, (half + 1) * tq)
                    mx = mx_ref[pair, rs, :]
                    for sub in range(KTILE // LANES):
                        cs = slice(sub * LANES, (sub + 1) * LANES)
                        blk = l[rs, cs] + (masks[sub] + bias_sub(kt, sub, 2 * pair + half))
                        l_ref[pair, base + kt, rs, cs] = blk
                        mx = jnp.maximum(mx, blk)
                    mx_ref[pair, rs, :] = mx
            return carry

        over_tiles(count, p1)

    for pair in range(npairs):
        m = jnp.max(mx_ref[pair], axis=-1, keepdims=True)
        mx_ref[pair] = jnp.broadcast_to(m, mx_ref.shape[1:])

    for count, base, _, v_tile, _, _ in groups:
        def p2(kt, carry, base=base, v_tile=v_tile):
            for pair in range(npairs):
                sl = slice(pair * LANES, (pair + 1) * LANES)
                m = mx_ref[pair]
                p = jnp.concatenate(
                    [jnp.exp(l_ref[pair, base + kt, :, sub * LANES:(sub + 1) * LANES] - m)
                     for sub in range(KTILE // LANES)], axis=1)
                vaug = jnp.concatenate([v_tile(kt, sl), ones], axis=1)
                acc_ref[pair] += _dot(_bf(p), vaug)
            return carry

        over_tiles(count, p2)

    lane_full = lax.broadcasted_iota(I32, (tq, LANES), 1)
    for pair in range(npairs):
        a = acc_ref[pair]
        o = a[:, 0:LANES] / a[:, LANES:2 * LANES]
        o_ref[:, pair * LANES:(pair + 1) * LANES] = _bf(jnp.where(lane_full < A_HEAD_DIM, o[0:tq], o[tq:2 * tq]))


def _attn_scratch(tq, ntiles):
    npairs = A_HEADS // 2
    return [pltpu.VMEM((npairs, 2 * tq, LANES), BF16),
            pltpu.VMEM((npairs, ntiles, 2 * tq, KTILE), F32),
            pltpu.VMEM((npairs, 2 * tq, LANES), F32),
            pltpu.VMEM((npairs, 2 * tq, 2 * LANES), F32)]


def _attn_prompt_kernel(q_ref, k_ref, v_ref, mask_ref, bias_ref, o_ref, qs_ref, l_ref, mx_ref, acc_ref):
    i = pl.program_id(1)
    per = KTILE // TILE

    def rows(kt):
        return pl.ds(pl.multiple_of(kt * KTILE, KTILE), KTILE)

    def bias_sub(kt, sub, hd):
        s = kt * per + sub
        sel = jnp.where(s == i, 2, jnp.where(s == i - 1, 1, 0))
        return bias_ref[sel, hd]

    group = (i // per + 1, 0,
             lambda kt, sl: k_ref[rows(kt), sl],
             lambda kt, sl: v_ref[rows(kt), sl],
             lambda kt, sub: mask_ref[0, 0, kt * per + sub],
             bias_sub)
    _attn_core(q_ref, o_ref, qs_ref, l_ref, mx_ref, acc_ref, [group], TILE)


def attn_prompt(q, kb, vb, mask, bias, nseq, seqlen):
    n, hq = q.shape
    nqb = seqlen // TILE
    nt = mask.shape[2]
    return pl.pallas_call(
        _attn_prompt_kernel,
        grid=(nseq, nqb),
        in_specs=[pl.BlockSpec((TILE, hq), lambda b, i: (b * nqb + i, 0)),
                  pl.BlockSpec((seqlen, hq), lambda b, i: (b, 0)),
                  pl.BlockSpec((seqlen, hq), lambda b, i: (b, 0)),
                  pl.BlockSpec((1, 1, nt, TILE, TILE), lambda b, i: (b, i, 0, 0, 0)),
                  _full(bias.shape)],
        out_specs=pl.BlockSpec((TILE, hq), lambda b, i: (b * nqb + i, 0)),
        out_shape=jax.ShapeDtypeStruct((n, hq), BF16),
        scratch_shapes=_attn_scratch(TILE, seqlen // KTILE),
        compiler_params=_params("parallel", "arbitrary"),
        name="attn_prompt",
    )(q, kb, vb, mask, bias)


def _attn_sample_kernel(q_ref, kc_ref, vc_ref, kn_ref, vn_ref, mask_ref, bias_ref, o_ref,
                        l_ref, mx_ref, acc_ref, *, nblk):
    ph = pl.program_id(1)
    b = pl.program_id(2)
    tq = q_ref.shape[0]
    per = KTILE // TILE
    tiles_per_blk = kc_ref.shape[3] // KTILE
    ncache = nblk * tiles_per_blk
    last_sub = ncache * per - 1
    heads = range(A_HEADS)

    def head_tile(ref, c0, hd):
        return _bf(ref[0, hd, :, c0:c0 + KTILE])

    def pass1(slot, ref, c0, mask_sub, bias_sub):
        ls = [_dot(q_ref[:, hd * A_HEAD_DIM:(hd + 1) * A_HEAD_DIM], head_tile(ref, c0, hd))
              for hd in heads]
        masks = [mask_sub(sub) for sub in range(per)]
        for hd in heads:
            mx = mx_ref[hd]
            for sub in range(per):
                cs = slice(sub * LANES, (sub + 1) * LANES)
                blk = ls[hd][:, cs] + (masks[sub] + bias_sub(sub, hd))
                l_ref[hd, slot, :, cs] = blk
                mx = jnp.maximum(mx, blk)
            mx_ref[hd] = mx

    def pass2(slot, ref, c0):
        ones = jnp.ones((LANES - A_HEAD_DIM, KTILE), BF16)
        for hd in heads:
            m = mx_ref[hd]
            p = jnp.concatenate([jnp.exp(l_ref[hd, slot, :, sub * LANES:(sub + 1) * LANES] - m)
                                 for sub in range(per)], axis=1)
            vaug = jnp.concatenate([head_tile(ref, c0, hd), ones], axis=0)
            acc_ref[hd] += _dot_nt(_bf(p), vaug)

    @pl.when((ph == 0) & (b == 0))
    def _():
        mx_ref[...] = jnp.full(mx_ref.shape, NEG, F32)
        acc_ref[...] = jnp.zeros(acc_ref.shape, F32)

    @pl.when(ph == 0)
    def _():
        for t in range(tiles_per_blk):
            g = b * tiles_per_blk + t
            pass1(g, kc_ref, t * KTILE,
                  lambda sub, g=g: mask_ref[0, 0, g * per + sub],
                  lambda sub, hd, g=g: bias_ref[jnp.where(g * per + sub == last_sub, 1, 0), hd])

    @pl.when((ph == 0) & (b == nblk - 1))
    def _():
        pass1(ncache, kn_ref, 0,
              lambda sub: mask_ref[0, 0, ncache * per + sub],
              lambda sub, hd: bias_ref[2 if sub == 0 else 0, hd])
        for hd in heads:
            m = jnp.max(mx_ref[hd], axis=-1, keepdims=True)
            mx_ref[hd] = jnp.broadcast_to(m, mx_ref.shape[1:])

    @pl.when(ph == 1)
    def _():
        for t in range(tiles_per_blk):
            pass2(b * tiles_per_blk + t, vc_ref, t * KTILE)

    @pl.when((ph == 1) & (b == nblk - 1))
    def _():
        pass2(ncache, vn_ref, 0)
        for pair in range(A_HEADS // 2):
            halves = []
            for hd in (2 * pair, 2 * pair + 1):
                a = acc_ref[hd]
                halves.append(a[:, 0:A_HEAD_DIM] / a[:, A_HEAD_DIM:2 * A_HEAD_DIM])
            o_ref[:, pair * LANES:(pair + 1) * LANES] = _bf(jnp.concatenate(halves, axis=1))


def attn_sample(q, kc, vc, kn, vn, mask, bias, nseq, tq):
    n, hq = q.shape
    past = kc.shape[3]
    cblk = min(CACHE_BLOCK, past)
    nblk = past // cblk
    ncache = past // KTILE
    nt = mask.shape[2]
    per = TILE // tq
    cache_k = pl.BlockSpec((1, A_HEADS, A_HEAD_DIM, cblk),
                           lambda s, ph, b: (s, 0, 0, b * (1 - ph) + (nblk - 1) * ph))
    cache_v = pl.BlockSpec((1, A_HEADS, A_HEAD_DIM, cblk), lambda s, ph, b: (s, 0, 0, b * ph))
    fresh = pl.BlockSpec((1, A_HEADS, A_HEAD_DIM, KTILE), lambda s, ph, b: (s, 0, 0, 0))
    return pl.pallas_call(
        functools.partial(_attn_sample_kernel, nblk=nblk),
        grid=(nseq, 2, nblk),
        in_specs=[pl.BlockSpec((tq, hq), lambda s, ph, b: (s, 0)),
                  cache_k, cache_v, fresh, fresh,
                  pl.BlockSpec((1, 1, nt, tq, TILE), lambda s, ph, b: (s // per, 0, 0, s % per, 0)),
                  pl.BlockSpec(bias.shape, lambda s, ph, b: (0, 0, 0, 0))],
        out_specs=pl.BlockSpec((tq, hq), lambda s, ph, b: (s, 0)),
        out_shape=jax.ShapeDtypeStruct((n, hq), BF16),
        scratch_shapes=[pltpu.VMEM((A_HEADS, ncache + 1, tq, KTILE), F32),
                        pltpu.VMEM((A_HEADS, tq, LANES), F32),
                        pltpu.VMEM((A_HEADS, tq, LANES), F32)],
        compiler_params=_params("parallel", "arbitrary", "arbitrary"),
        name="attn_sample",
    )(q, kc, vc, kn, vn, mask, bias)


def _t5_bucket(rel):
    half = N_BUCKETS // 2
    max_exact = half // 2
    n = jnp.abs(rel)
    nf = jnp.maximum(n, 1).astype(F32)
    large = max_exact + (jnp.log(nf / max_exact) / math.log(MAX_DISTANCE / max_exact)
                         * (half - max_exact)).astype(I32)
    large = jnp.minimum(large, half - 1)
    return jnp.where(rel > 0, half, 0) + jnp.where(n < max_exact, n, large)


def bias_tiles(rel_bias):
    heads = rel_bias.shape[1]
    span = 2 * TILE - 1

    def toeplitz(shift):
        rel = jnp.arange(span, dtype=I32) - (TILE - 1) + shift
        tab = rel_bias.astype(F32)[_t5_bucket(rel)].T
        strip = jnp.tile(jnp.pad(tab, ((0, 0), (0, 1))), (1, TILE))[:, :TILE * span]
        return strip.reshape(heads, TILE, span)[:, :, TILE - 1:]

    far = rel_bias.astype(F32)[_t5_bucket(jnp.full((1,), -(TILE + 1), I32))]
    far = jnp.broadcast_to(far.T[:, :, None], (heads, TILE, TILE))
    return jnp.stack([far, toeplitz(-TILE), toeplitz(0)])


def dsa_core_prompt(pr, nseq, seqlen, bias):
    q, _, kb, _, vb, qi, kiwi = pr
    nqb = seqlen // TILE
    wit = kiwi[:, IDX_DIM:IDX_DIM + IDX_HEADS].reshape(nseq * nqb, TILE, IDX_HEADS).swapaxes(1, 2)
    groups = _row_groups(0, seqlen, lambda off: [(0, None, off, LANES, None)])
    mask = index_mask(qi, wit, [kiwi], [pl.BlockSpec((seqlen, LANES), lambda b, i: (b, 0))],
                      groups, seqlen, nseq, nqb, causal=True)
    return attn_prompt(q, kb, vb, mask, bias, nseq, seqlen)


def dsa_core_sample(pr, k_cache, v_cache, ki_cache, nseq, tq, bias):
    q, kf, _, vf, _, qi, kiwi = pr
    past = k_cache.shape[1]
    per = TILE // tq
    wit = kiwi[:, IDX_DIM:IDX_DIM + IDX_HEADS].reshape(nseq // per, TILE, IDX_HEADS).swapaxes(1, 2)
    groups = (_row_groups(0, past, lambda off: [(0, j, off, IDX_DIM, j) for j in range(per)])
              + _row_groups(past, tq, lambda off: [(1, None, j * tq + off, LANES, j) for j in range(per)]))
    mask = index_mask(
        qi, wit, [ki_cache, kiwi],
        [pl.BlockSpec((per, past, IDX_DIM), lambda b, i: (b, 0, 0)),
         pl.BlockSpec((TILE, LANES), lambda b, i: (b, 0))],
        groups, past + tq, nseq // per, 1, causal=False)
    to_hdk = lambda a: a.transpose(0, 2, 3, 1)
    pad_keys = ((0, 0), (0, 0), (0, 0), (0, KTILE - tq))
    kn = jnp.pad(to_hdk(kf.reshape(nseq, tq, A_HEADS, A_HEAD_DIM)), pad_keys)
    vn = jnp.pad(to_hdk(vf.reshape(nseq, tq, A_HEADS, A_HEAD_DIM)), pad_keys)
    return attn_sample(q, to_hdk(k_cache), to_hdk(v_cache), kn, vn, mask, bias[:, :, :tq, :], nseq, tq)


def _log_sigmoid(x):
    return jnp.minimum(x, 0.0) - jnp.log1p(jnp.exp(-jnp.abs(x)))


def _mlstm_kernel(p_ref, gs_ref, gr_ref, gbc_ref, gbr_ref, hg_ref, c0_ref, n0_ref, m0_ref,
                  hs_ref, c_ref, n_ref, m_ref, cs, ns, ms):
    c_id = pl.program_id(1)
    nh = B_HEADS
    npair = nh // 2
    hw = LANES
    nsq = p_ref.shape[0]
    L = p_ref.shape[1]
    L2 = 2 * L

    @pl.when(c_id == 0)
    def _():
        cs[...] = jnp.zeros(cs.shape, F32)
        ns[...] = jnp.zeros(ns.shape, F32)
        for sq in range(nsq):
            ns[sq, :, 0:B_QK_DIM] = n0_ref[sq]
            for hd in range(nh):
                cs[sq, hd // 2, (hd % 2) * hw:(hd % 2) * hw + B_QK_DIM, :] = c0_ref[sq, hd]
        ms[...] = m0_ref[...]

    r = lax.broadcasted_iota(I32, (L2, L2), 0)
    c = lax.broadcasted_iota(I32, (L2, L2), 1)
    same = (r >> CHUNK_SHIFT) == (c >> CHUNK_SHIFT)
    incl = same & (c <= r)
    tri_l = _bf01(incl)
    tri_u = _bf01(same & (r <= c))
    rowc = lax.broadcasted_iota(I32, (L2, 1), 0)
    is_top = rowc < L
    top = is_top.astype(F32)
    bot = 1.0 - top
    row2 = lax.broadcasted_iota(I32, (2 * hw, 1), 0)
    hg = hg_ref[...]

    def stack(base, sq, pr):
        a = base + 2 * pr * hw
        return jnp.concatenate([p_ref[sq, :, a:a + hw], p_ref[sq, :, a + hw:a + 2 * hw]], axis=0)

    def bd(x):
        return _bf(jnp.concatenate([x * top, x * bot], axis=1))

    seqs = range(nsq)
    gs = [gs_ref[sq, 0] + gbc_ref[...] for sq in seqs]
    gr = [gr_ref[sq, 0] + gbr_ref[...] for sq in seqs]
    bcol = [_dot_exact_lhs01(tri_l, _log_sigmoid(gs[sq][:, npair:])) for sq in seqs]
    brow = [_dot_exact_rhs01(_log_sigmoid(gr[sq][npair:, :]), tri_u) for sq in seqs]
    mall = [ms[sq] for sq in seqs]
    cells = [(sq, pr) for sq in seqs for pr in range(npair)]
    ids = range(len(cells))
    qf = [stack(0, sq, pr) for sq, pr in cells]
    kf = [stack(nh * hw, sq, pr) * (B_QK_DIM ** -0.5) for sq, pr in cells]
    vb = [_bf(stack(2 * nh * hw, sq, pr)) for sq, pr in cells]
    bc = [bcol[sq][:, pr:pr + 1] for sq, pr in cells]
    ic = [gs[sq][:, pr:pr + 1] for sq, pr in cells]
    mc = [mall[sq][:, pr:pr + 1] for sq, pr in cells]
    d = [jnp.where(incl, bc[i] - brow[sq][pr:pr + 1, :] + gr[sq][pr:pr + 1, :], NEG)
         for i, (sq, pr) in enumerate(cells)]
    inter = [bc[i] + mc[i] for i in ids]
    mt = [jnp.maximum(inter[i], jnp.max(d[i], axis=-1, keepdims=True)) for i in ids]
    s = [_dot_nt(_bf(qf[i]), _bf(kf[i])) * jnp.exp(d[i] - mt[i]) for i in ids]
    wst = [jnp.exp(inter[i] - mt[i]) for i in ids]
    cmat = [cs[sq, pr] for sq, pr in cells]
    num = [_dot(_bf(s[i]), vb[i]) + wst[i] * _dot(bd(qf[i]), _bf(cmat[i])) for i in ids]
    for i, (sq, pr) in enumerate(cells):
        n0 = ns[sq, 2 * pr:2 * pr + 1, :]
        n1 = ns[sq, 2 * pr + 1:2 * pr + 2, :]
        qn = jnp.sum(qf[i] * jnp.where(is_top, n0, n1), axis=-1, keepdims=True)
        den = jnp.sum(s[i], axis=-1, keepdims=True) + wst[i] * qn
        hs = num[i] / jnp.maximum(jnp.abs(den), jnp.exp(-mt[i]))
        bl0 = bc[i][L - 1:L, :]
        bl1 = bc[i][L2 - 1:L2, :]
        bl = jnp.where(is_top, bl0, bl1)
        dec = bl - bc[i] + ic[i]
        blm = bl + mc[i]
        mnew0 = jnp.maximum(blm[0:1, :], jnp.max(dec[0:L], axis=0, keepdims=True))
        mnew1 = jnp.maximum(blm[L:L + 1, :], jnp.max(dec[L:L2], axis=0, keepdims=True))
        mnew = jnp.where(is_top, mnew0, mnew1)
        wk = jnp.exp(dec - mnew)
        ws = jnp.exp(blm - mnew)
        kw = kf[i] * wk
        ws2 = jnp.where(row2 < hw, ws[0:1, :], ws[L:L + 1, :])
        cs[sq, pr] = ws2 * cmat[i] + _dot_tn(bd(kw), vb[i])
        ns[sq, 2 * pr:2 * pr + 1, :] = ws[0:1, :] * n0 + jnp.sum(kw[0:L], axis=0, keepdims=True)
        ns[sq, 2 * pr + 1:2 * pr + 2, :] = ws[L:L + 1, :] * n1 + jnp.sum(kw[L:L2], axis=0, keepdims=True)
        ms[sq, :, pr:pr + 1] = mnew
        on = _bf(_rms(hs, hg) * _sigmoid(stack(3 * nh * hw, sq, pr)))
        oa = 2 * pr * B_V_DIM
        hs_ref[sq, :, oa:oa + B_V_DIM] = on[0:L]
        hs_ref[sq, :, oa + B_V_DIM:oa + 2 * B_V_DIM] = on[L:L2]

    @pl.when(c_id == pl.num_programs(1) - 1)
    def _():
        for sq in range(nsq):
            for hd in range(nh):
                c_ref[sq, hd] = cs[sq, hd // 2, (hd % 2) * hw:(hd % 2) * hw + B_QK_DIM, :]
            n_ref[sq] = ns[sq, :, 0:B_QK_DIM]
        m_ref[...] = ms[...]


def mlstm_weight(w_in):
    d = w_in.shape[0]
    nq = B_HEADS * B_QK_DIM
    nv = B_HEADS * B_V_DIM

    def padh(w):
        w = w.reshape(d, B_HEADS, B_QK_DIM)
        return jnp.pad(w, ((0, 0), (0, 0), (0, LANES - B_QK_DIM))).reshape(d, B_HEADS * LANES)

    main = 2 * nq + 2 * nv
    tail = jnp.pad(w_in[:, main:], ((0, 0), (0, LANES - 2 * B_HEADS)))
    return _bf(jnp.concatenate([padh(w_in[:, :nq]), padh(w_in[:, nq:2 * nq]), w_in[:, 2 * nq:main], tail], axis=1))


def mlstm_core(p, gate_bias, h_gain, c0, n0, m0, nseq, nchunk):
    n, m = p.shape
    nh = B_HEADS
    npair = nh // 2
    gcol = 4 * nh * LANES
    nsq = SCAN_SEQS if nseq % SCAN_SEQS == 0 else 1
    seqlen = n // nseq
    pre = p[:, gcol:gcol + 2 * nh].reshape(nseq, nchunk, CHUNK, 2, npair, 2)
    gs = pre.transpose(0, 1, 5, 2, 3, 4).reshape(nseq, nchunk, 2 * CHUNK, 2 * npair)
    gr = pre.transpose(0, 1, 3, 4, 5, 2).reshape(nseq, nchunk, 2 * npair, 2 * CHUNK)
    gb = gate_bias.reshape(2, npair, 2)
    gbc = jnp.repeat(gb.transpose(2, 0, 1).reshape(2, 2 * npair), CHUNK, axis=0)
    m0s = jnp.repeat(m0.reshape(nseq, npair, 2).transpose(0, 2, 1), CHUNK, axis=1)
    o, c_new, n_new, m_new = pl.pallas_call(
        _mlstm_kernel,
        grid=(nseq // nsq, nchunk),
        in_specs=[pl.BlockSpec((nsq, CHUNK, m), lambda s, c: (s, c, 0)),
                  pl.BlockSpec((nsq, 1, 2 * CHUNK, 2 * npair), lambda s, c: (s, c, 0, 0)),
                  pl.BlockSpec((nsq, 1, 2 * npair, 2 * CHUNK), lambda s, c: (s, c, 0, 0)),
                  _full((2 * CHUNK, 2 * npair)), _full((2 * npair, 2 * CHUNK)), _full((1, B_V_DIM)),
                  pl.BlockSpec((nsq, nh, B_QK_DIM, B_V_DIM), lambda s, c: (s, 0, 0, 0)),
                  pl.BlockSpec((nsq, nh, B_QK_DIM), lambda s, c: (s, 0, 0)),
                  pl.BlockSpec((nsq, 2 * CHUNK, npair), lambda s, c: (s, 0, 0))],
        out_specs=[pl.BlockSpec((nsq, CHUNK, nh * B_V_DIM), lambda s, c: (s, c, 0)),
                   pl.BlockSpec((nsq, nh, B_QK_DIM, B_V_DIM), lambda s, c: (s, 0, 0, 0)),
                   pl.BlockSpec((nsq, nh, B_QK_DIM), lambda s, c: (s, 0, 0)),
                   pl.BlockSpec((nsq, 2 * CHUNK, npair), lambda s, c: (s, 0, 0))],
        out_shape=[jax.ShapeDtypeStruct((nseq, seqlen, nh * B_V_DIM), BF16),
                   jax.ShapeDtypeStruct((nseq, nh, B_QK_DIM, B_V_DIM), F32),
                   jax.ShapeDtypeStruct((nseq, nh, B_QK_DIM), F32),
                   jax.ShapeDtypeStruct((nseq, 2 * CHUNK, npair), F32)],
        scratch_shapes=[pltpu.VMEM((nsq, npair, 2 * LANES, B_V_DIM), F32), pltpu.VMEM((nsq, nh, LANES), F32),
                        pltpu.VMEM((nsq, 2 * CHUNK, npair), F32)],
        compiler_params=_params("parallel", "arbitrary"),
        name="mlstm",
    )(p.reshape(nseq, seqlen, m), gs, gr, gbc, gbc.T, h_gain.reshape(1, B_V_DIM), c0, n0, m0s)
    m_heads = m_new[:, ::CHUNK, :].transpose(0, 2, 1).reshape(nseq, nh)
    return o.reshape(n, nh * B_V_DIM), c_new, n_new, m_heads


def _split2(x):
    hi = _bf(x)
    return hi, _bf(x - hi.astype(F32))


def _cat3_lhs(x):
    hi, mid = _split2(x)
    return jnp.concatenate([hi, hi, mid], axis=1)


def _cat3_rhs(x):
    hi, mid = _split2(x)
    return jnp.concatenate([hi, mid, hi], axis=0)


def _gdn_kernel(p_ref, gs_ref, gr_ref, cw_ref, alc_ref, alr_ref, dtc_ref, dtr_ref, og_ref, s0_ref, cb0_ref,
                o_ref, s_ref, cb_ref, ss, tail):
    c_id = pl.program_id(1)
    nh = C_HEADS
    cdim = nh * (2 * C_DK + C_DV)
    L = p_ref.shape[0]
    L2 = 2 * L
    nprev = CONV_W - 1

    @pl.when(c_id == 0)
    def _():
        ss[...] = s0_ref[0]
        tail[...] = jnp.zeros(tail.shape, F32)
        tail[8 - nprev:8, :] = cb0_ref[0]

    x = p_ref[:, 0:cdim]
    ext = jnp.concatenate([tail[...], x], axis=0)
    conv = ext[8:8 + L] * cw_ref[CONV_W - 1:CONV_W, :]
    for j in range(CONV_W - 1):
        conv = conv + ext[8 - nprev + j:8 - nprev + j + L] * cw_ref[j:j + 1, :]
    tail[...] = x[L - 8:L, :]
    cf = _silu(conv)

    npair = nh // 2
    gs = gs_ref[0]
    gr = gr_ref[0]
    beta = _sigmoid(gs[:, 0:npair])
    g_col = -jnp.exp(alc_ref[...]) * _softplus(gs[:, npair:] + dtc_ref[...])
    g_row = -jnp.exp(alr_ref[...]) * _softplus(gr[npair:, :] + dtr_ref[...])
    r = lax.broadcasted_iota(I32, (L2, L2), 0)
    c = lax.broadcasted_iota(I32, (L2, L2), 1)
    same = (r >> CHUNK_SHIFT) == (c >> CHUNK_SHIFT)
    incl = same & (c <= r)
    strict = same & (c < r)
    eye = (c == r).astype(F32)
    gc_col = _dot_exact_lhs01(_bf01(incl), g_col)
    gc_row = _dot_exact_rhs01(g_row, _bf01(same & (r <= c)))
    rowc = lax.broadcasted_iota(I32, (L2, 1), 0)
    top = (rowc < L).astype(F32)
    bot = 1.0 - top
    row2 = lax.broadcasted_iota(I32, (2 * C_DK, 1), 0)
    og = og_ref[...]

    def stack(base, pr):
        a = base + 2 * pr * C_DK
        return jnp.concatenate([cf[:, a:a + C_DK], cf[:, a + C_DK:a + 2 * C_DK]], axis=0)

    def bd(x):
        return _bf(jnp.concatenate([x * top, x * bot], axis=1))

    pairs = range(npair)
    qc, kc, kcb, dm, amat, rhs, gcols, egcs = [], [], [], [], [], [], [], []
    for pr in pairs:
        qraw = stack(0, pr)
        kraw = stack(nh * C_DK, pr)
        vc = stack(2 * nh * C_DK, pr)
        qc.append(qraw * lax.rsqrt(jnp.sum(qraw * qraw, axis=-1, keepdims=True) + EPS) * (C_DK ** -0.5))
        kc.append(kraw * lax.rsqrt(jnp.sum(kraw * kraw, axis=-1, keepdims=True) + EPS))
        bc = beta[:, pr:pr + 1]
        gcol = gc_col[:, pr:pr + 1]
        grow = gc_row[pr:pr + 1, :]
        dm.append(jnp.where(incl, jnp.exp(jnp.where(incl, gcol - grow, 0.0)), 0.0))
        kb = kc[pr] * bc
        kcb.append(_bf(kc[pr]))
        amat.append(jnp.where(strict, _dot_nt(_bf(kb), kcb[pr]) * dm[pr], 0.0))
        egc = jnp.exp(gcol)
        rhs.append(jnp.concatenate([vc * bc, kb * egc], axis=-1))
        gcols.append(gcol)
        egcs.append(egc)
    tinv = [eye - amat[pr] for pr in pairs]
    pw_l = [_cat3_lhs(-amat[pr]) for pr in pairs]
    pw_r = [_cat3_rhs(-amat[pr]) for pr in pairs]
    for _ in range(CHUNK_SHIFT - 1):
        pw = [_dot(pw_l[pr], pw_r[pr]) for pr in pairs]
        pw_l = [_cat3_lhs(pw[pr]) for pr in pairs]
        pw_r = [_cat3_rhs(pw[pr]) for pr in pairs]
        tinv = [tinv[pr] + _dot(_cat3_lhs(tinv[pr]), pw_r[pr]) for pr in pairs]
    sol = [_dot(_cat3_lhs(tinv[pr]), _cat3_rhs(rhs[pr])) for pr in pairs]
    attn = [_dot_nt(_bf(qc[pr]), kcb[pr]) * dm[pr] for pr in pairs]
    smat = [ss[pr] for pr in pairs]
    sb = [_bf(smat[pr]) for pr in pairs]
    vnew = [sol[pr][:, :C_DV] - _dot(bd(sol[pr][:, C_DV:]), sb[pr]) for pr in pairs]
    o = [_dot(bd(qc[pr] * egcs[pr]), sb[pr]) + _dot(_bf(attn[pr]), _bf(vnew[pr])) for pr in pairs]
    for pr in pairs:
        gl0 = gcols[pr][L - 1:L, :]
        gl1 = gcols[pr][L2 - 1:L2, :]
        ke = kc[pr] * jnp.exp(jnp.where(rowc < L, gl0, gl1) - gcols[pr])
        decay = jnp.exp(jnp.where(row2 < C_DK, gl0, gl1))
        ss[pr] = smat[pr] * decay + _dot_tn(bd(ke), _bf(vnew[pr]))
    for pr in pairs:
        za = cdim + 2 * pr * C_DV
        z = jnp.concatenate([p_ref[:, za:za + C_DV], p_ref[:, za + C_DV:za + 2 * C_DV]], axis=0)
        on = _bf(_rms(o[pr], og) * _silu(z))
        oa = 2 * pr * C_DV
        o_ref[:, oa:oa + C_DV] = on[0:L]
        o_ref[:, oa + C_DV:oa + 2 * C_DV] = on[L:L2]

    @pl.when(c_id == pl.num_programs(1) - 1)
    def _():
        s_ref[0] = ss[...]
        cb_ref[0] = tail[8 - nprev:8, :]


def gdn_weight(w_in):
    main = C_HEADS * (2 * C_DK + C_DV) + C_HEADS * C_DV
    tail = jnp.pad(w_in[:, main:], ((0, 0), (0, LANES - 2 * C_HEADS)))
    return _bf(jnp.concatenate([w_in[:, :main], tail], axis=1))


def gdn_core(p, conv_w, a_log, dt_bias, o_gain, s0, cb0, nseq, nchunk):
    n, m = p.shape
    nh = C_HEADS
    npair = nh // 2
    cdim = nh * (2 * C_DK + C_DV)
    gcolumn = cdim + nh * C_DV
    pre = p[:, gcolumn:gcolumn + 2 * nh].reshape(nseq * nchunk, CHUNK, 2, npair, 2)
    gs = pre.transpose(0, 4, 1, 2, 3).reshape(nseq * nchunk, 2 * CHUNK, 2 * npair)
    gr = pre.transpose(0, 2, 3, 4, 1).reshape(nseq * nchunk, 2 * npair, 2 * CHUNK)

    def col(v):
        return jnp.repeat(v.reshape(npair, 2).T, CHUNK, axis=0)

    o, s_new, cb_new = pl.pallas_call(
        _gdn_kernel,
        grid=(nseq, nchunk),
        in_specs=[pl.BlockSpec((CHUNK, m), lambda s, c: (s * nchunk + c, 0)),
                  pl.BlockSpec((1, 2 * CHUNK, 2 * npair), lambda s, c: (s * nchunk + c, 0, 0)),
                  pl.BlockSpec((1, 2 * npair, 2 * CHUNK), lambda s, c: (s * nchunk + c, 0, 0)),
                  _full((CONV_W, cdim)),
                  _full((2 * CHUNK, npair)), _full((npair, 2 * CHUNK)),
                  _full((2 * CHUNK, npair)), _full((npair, 2 * CHUNK)),
                  _full((1, C_DV)),
                  pl.BlockSpec((1, npair, 2 * C_DK, C_DV), lambda s, c: (s, 0, 0, 0)),
                  pl.BlockSpec((1, CONV_W - 1, cdim), lambda s, c: (s, 0, 0))],
        out_specs=[pl.BlockSpec((CHUNK, nh * C_DV), lambda s, c: (s * nchunk + c, 0)),
                   pl.BlockSpec((1, npair, 2 * C_DK, C_DV), lambda s, c: (s, 0, 0, 0)),
                   pl.BlockSpec((1, CONV_W - 1, cdim), lambda s, c: (s, 0, 0))],
        out_shape=[jax.ShapeDtypeStruct((n, nh * C_DV), BF16),
                   jax.ShapeDtypeStruct((nseq, npair, 2 * C_DK, C_DV), F32),
                   jax.ShapeDtypeStruct((nseq, CONV_W - 1, cdim), F32)],
        scratch_shapes=[pltpu.VMEM((npair, 2 * C_DK, C_DV), F32), pltpu.VMEM((8, cdim), F32)],
        compiler_params=_params("parallel", "arbitrary"),
        name="gdn",
    )(p, gs, gr, conv_w, col(a_log), col(a_log).T, col(dt_bias), col(dt_bias).T, o_gain.reshape(1, C_DV),
      s0.reshape(nseq, npair, 2 * C_DK, C_DV), cb0)
    return o, s_new.reshape(nseq, nh, C_DK, C_DV), cb_new


def _trunk(x, nseq, seqlen, mem_k, mem_v, st, W, bias, is_prompt):
    d = x.shape[-1]
    n = nseq * seqlen
    x = x.reshape(n, d)
    tm = min(PROJ_ROWS, n)
    tm_mem = min(MEM_ROWS, n)
    new = {}
    for i in range(4):
        kind = i % 3
        mx = W["mixer"][i]
        if kind == 0:
            tstate = is_prompt and seqlen % tm == 0
            pr = proj_dsa(x, W["norm_mix"][i], mx["w_in"], mx["q_gain"], mx["k_gain"], mx["ki_gain"], tm,
                          nseq, tstate)
            if is_prompt:
                o = dsa_core_prompt(pr[:7], nseq, seqlen, bias)
            else:
                o = dsa_core_sample(pr, *st[i], nseq, seqlen, bias)
            if tstate:
                new[i] = (pr[1].transpose(0, 3, 1, 2), pr[3].transpose(0, 3, 1, 2), pr[7].transpose(0, 2, 1))
            else:
                new[i] = (pr[1].reshape(nseq, seqlen, A_HEADS, A_HEAD_DIM),
                          pr[3].reshape(nseq, seqlen, A_HEADS, A_HEAD_DIM),
                          pr[6][:, :IDX_DIM].reshape(nseq, seqlen, IDX_DIM))
        elif kind == 1:
            p = proj(x, W["norm_mix"][i], mx["w_in"], tm)
            o, c_new, n_new, m_new = mlstm_core(p, mx["gate_bias"], mx["h_gain"], *st[i], nseq, seqlen // CHUNK)
            new[i] = (c_new, n_new, m_new.reshape(nseq, B_HEADS))
        else:
            p = proj(x, W["norm_mix"][i], mx["w_in"], tm)
            o, s_new, cb_new = gdn_core(p, mx["conv_w"], mx["a_log"], mx["dt_bias"], mx["o_gain"], *st[i],
                                        nseq, seqlen // CHUNK)
            new[i] = (s_new, cb_new)
        x = mem_attend(x, o, mx["w_out"], W["norm_mem"][i], W["w_mq"][i], W["mq_gain"][i],
                       mem_k[i], mem_v[i], W["w_mo"][i], tm_mem, seqlen)
        x = ffn(x, W["norm_ffn"][i], W["w_ffn1"][i], W["w_ffn3"][i], W["w_ffn2"][i], min(FFN_ROWS, n))
    return x.reshape(nseq, seqlen, d), new


def kernel(x_prompt, x_sample, mem_prompt, cache_l0_k, cache_l0_v, cache_l0_kidx, state_l1_C, state_l1_n, state_l1_m, state_l2_S, state_l2_conv, cache_l3_k, cache_l3_v, cache_l3_kidx, cache_mem_k, cache_mem_v, rel_bias, norm_mix, norm_mem, norm_ffn, mem_norm, w_mq, w_mk, w_mv, w_mo, mq_gain, mk_gain, w_ffn1, w_ffn3, w_ffn2, a0_w_in, a0_w_out, a0_q_gain, a0_k_gain, a0_kidx_gain, b1_w_in, b1_gate_bias, b1_h_gain, b1_w_out, c2_w_in, c2_conv_w, c2_a_log, c2_dt_bias, c2_o_gain, c2_w_out, a3_w_in, a3_w_out, a3_q_gain, a3_k_gain, a3_kidx_gain):
    B, T, D = x_prompt.shape
    S, Ts, _ = x_sample.shape
    depth = w_mq.shape[0]
    mlen = mem_prompt.shape[1]
    mw = MEM_HEADS * MEM_HEAD_DIM

    def dsa_w(w_in, w_out, qg, kg, kig):
        return dict(w_in=dsa_weight(w_in), w_out=_bf(w_out), q_gain=qg, k_gain=kg, ki_gain=kig)

    W = dict(
        norm_mix=norm_mix, norm_mem=norm_mem, norm_ffn=norm_ffn,
        w_mq=_bf(w_mq), w_mo=_bf(w_mo), mq_gain=mq_gain,
        w_ffn1=_bf(w_ffn1), w_ffn3=_bf(w_ffn3), w_ffn2=_bf(w_ffn2),
        mixer={
            0: dsa_w(a0_w_in, a0_w_out, a0_q_gain, a0_k_gain, a0_kidx_gain),
            1: dict(w_in=mlstm_weight(b1_w_in), gate_bias=b1_gate_bias, h_gain=b1_h_gain, w_out=_bf(b1_w_out)),
            2: dict(w_in=gdn_weight(c2_w_in), conv_w=c2_conv_w, a_log=c2_a_log, dt_bias=c2_dt_bias,
                    o_gain=c2_o_gain, w_out=_bf(c2_w_out)),
            3: dsa_w(a3_w_in, a3_w_out, a3_q_gain, a3_k_gain, a3_kidx_gain),
        },
    )
    bias = bias_tiles(rel_bias)

    mk_p, mv_p = mem_kv(mem_prompt.reshape(B * mlen, D), mem_norm, w_mk, w_mv, mk_gain)
    mk_p = mk_p.reshape(depth, B, mlen, mw)
    mv_p = mv_p.reshape(depth, B, mlen, mw)
    st_p = {
        0: None,
        1: (jnp.zeros((B, B_HEADS, B_QK_DIM, B_V_DIM), F32), jnp.zeros((B, B_HEADS, B_QK_DIM), F32),
            jnp.full((B, B_HEADS), NEG, F32)),
        2: (jnp.zeros((B, C_HEADS, C_DK, C_DV), F32), jnp.zeros((B, CONV_W - 1, state_l2_conv.shape[-1]), F32)),
        3: None,
    }
    y_p, np_ = _trunk(x_prompt, B, T, mk_p, mv_p, st_p, W, bias, True)

    st_s = {
        0: (cache_l0_k, cache_l0_v, cache_l0_kidx),
        1: (state_l1_C, state_l1_n, state_l1_m),
        2: (state_l2_S, state_l2_conv),
        3: (cache_l3_k, cache_l3_v, cache_l3_kidx),
    }
    mk_s = cache_mem_k.reshape(depth, S, mlen, mw)
    mv_s = cache_mem_v.reshape(depth, S, mlen, mw)
    y_s, ns_ = _trunk(x_sample, S, Ts, mk_s, mv_s, st_s, W, bias, False)

    shp = (depth, B, mlen, MEM_HEADS, MEM_HEAD_DIM)
    return (y_p, y_s,
            *np_[0], *np_[1], *np_[2], *np_[3], mk_p.reshape(shp), mv_p.reshape(shp),
            *ns_[0], *ns_[1], *ns_[2], *ns_[3])
```

```python
import functools
import math

import jax
import jax.numpy as jnp
from jax import lax
from jax.experimental import pallas as pl
from jax.experimental.pallas import tpu as pltpu

F32 = jnp.float32
BF16 = jnp.bfloat16
I32 = jnp.int32

EPS = 1e-6
NEG = -1e30
CHUNK = 64
CHUNK_SHIFT = 6
LANES = 128
TILE = 128
KTILE = 256
COUNT_CHAINS = 8
SCORE_ROWS = 256
PROJ_ROWS = 512
FFN_ROWS = 512
MEM_ROWS = 512
SCAN_SEQS = 2
CACHE_BLOCK = 2048
MASK_CASE_ROWS = 512
VMEM_LIMIT = 56 * 1024 * 1024

A_HEADS, A_HEAD_DIM = 16, 64
IDX_HEADS, IDX_DIM = 8, 64
TOPK_MAX = 256
N_BUCKETS, MAX_DISTANCE = 32, 128
B_HEADS, B_QK_DIM, B_V_DIM = 8, 64, 128
C_HEADS, C_DK, C_DV = 8, 128, 128
CONV_W = 4
MEM_HEADS, MEM_HEAD_DIM = 4, 128


def _bf(x):
    return x.astype(BF16)


def _bf01(mask):
    return mask.astype(F32).astype(BF16)


def _dot(a, b):
    return jnp.dot(a, b, preferred_element_type=F32)


def _dot_nt(a, b):
    return lax.dot_general(a, b, (((1,), (1,)), ((), ())), preferred_element_type=F32)


def _dot_tn(a, b):
    return lax.dot_general(a, b, (((0,), (0,)), ((), ())), preferred_element_type=F32)


def _split3(x):
    hi = _bf(x)
    r1 = x - hi.astype(F32)
    mid = _bf(r1)
    lo = _bf(r1 - mid.astype(F32))
    return hi, mid, lo


def _dot_exact_rhs01(x, m01):
    hi, mid, lo = _split3(x)
    return _dot(hi, m01) + _dot(mid, m01) + _dot(lo, m01)


def _dot_exact_lhs01(m01, x):
    hi, mid, lo = _split3(x)
    return _dot(m01, hi) + _dot(m01, mid) + _dot(m01, lo)


def _dot_f32(a, b):
    ah, am, al = _split3(a)
    bh, bm, bl = _split3(b)
    return (_dot(ah, bh) + (_dot(ah, bm) + _dot(am, bh))
            + (_dot(am, bm) + _dot(ah, bl) + _dot(al, bh)))


def _rms(x, g):
    ms = jnp.mean(x * x, axis=-1, keepdims=True)
    return x * lax.rsqrt(ms + EPS) * g


def _sigmoid(x):
    return 1.0 / (1.0 + jnp.exp(-x))


def _silu(x):
    return x * _sigmoid(x)


def _softplus(x):
    return jnp.maximum(x, 0.0) + jnp.log1p(jnp.exp(-jnp.abs(x)))


def _params(*sem):
    return pltpu.CompilerParams(dimension_semantics=sem, vmem_limit_bytes=VMEM_LIMIT)


def _full(shape):
    n = len(shape)
    return pl.BlockSpec(shape, lambda *_: (0,) * n)


def _proj_kernel(x_ref, g_ref, w_ref, o_ref, tail_ref, *, col_chunk):
    h = _bf(_rms(x_ref[...], g_ref[...]))
    m = w_ref.shape[1]
    for c in range(0, m - LANES, col_chunk):
        e = min(c + col_chunk, m - LANES)
        o_ref[:, c:e] = _dot(h, w_ref[:, c:e])
    tail = _dot(h, w_ref[:, m - LANES:m])
    o_ref[:, m - LANES:m] = tail
    tail_ref[...] = tail


def proj(x, g, w, tm):
    n, d = x.shape
    m = w.shape[1]
    return pl.pallas_call(
        functools.partial(_proj_kernel, col_chunk=512),
        grid=(n // tm,),
        in_specs=[pl.BlockSpec((tm, d), lambda i: (i, 0)), _full((1, d)), _full((d, m))],
        out_specs=[pl.BlockSpec((tm, m), lambda i: (i, 0)), pl.BlockSpec((tm, LANES), lambda i: (i, 0))],
        out_shape=[jax.ShapeDtypeStruct((n, m), F32), jax.ShapeDtypeStruct((n, LANES), F32)],
        compiler_params=_params("parallel"),
        name="proj",
    )(x, g.reshape(1, d), w)


def _memkv_kernel(x_ref, g_ref, wk_ref, wv_ref, kg_ref, k_ref, v_ref):
    h = _bf(_rms(x_ref[...], g_ref[0]))
    k = _dot(h, wk_ref[0])
    v_ref[0] = _dot(h, wv_ref[0])
    kg = kg_ref[0]
    for hd in range(MEM_HEADS):
        sl = slice(hd * MEM_HEAD_DIM, (hd + 1) * MEM_HEAD_DIM)
        k_ref[0, :, sl] = _rms(k[:, sl], kg)


def mem_kv(mem2d, mem_norm, w_mk, w_mv, mk_gain, tm=256):
    n, d = mem2d.shape
    depth = w_mk.shape[0]
    mw = w_mk.shape[2]
    return pl.pallas_call(
        _memkv_kernel,
        grid=(depth, n // tm),
        in_specs=[pl.BlockSpec((tm, d), lambda l, i: (i, 0)),
                  pl.BlockSpec((1, 1, d), lambda l, i: (l, 0, 0)),
                  pl.BlockSpec((1, d, mw), lambda l, i: (l, 0, 0)),
                  pl.BlockSpec((1, d, mw), lambda l, i: (l, 0, 0)),
                  pl.BlockSpec((1, 1, MEM_HEAD_DIM), lambda l, i: (l, 0, 0))],
        out_specs=[pl.BlockSpec((1, tm, mw), lambda l, i: (l, i, 0)),
                   pl.BlockSpec((1, tm, mw), lambda l, i: (l, i, 0))],
        out_shape=[jax.ShapeDtypeStruct((depth, n, mw), F32)] * 2,
        compiler_params=_params("parallel", "parallel"),
        name="mem_kv",
    )(mem2d, mem_norm.reshape(depth, 1, d), _bf(w_mk), _bf(w_mv), mk_gain.reshape(depth, 1, MEM_HEAD_DIM))


def _memattn_kernel(x_ref, o_ref, wo_ref, g_ref, wq_ref, qg_ref, mk_ref, mv_ref, wmo_ref, y_ref):
    x1 = x_ref[...] + _dot(o_ref[...], wo_ref[...])
    h = _bf(_rms(x1, g_ref[...]))
    q = _dot(h, wq_ref[...])
    qg = qg_ref[...]
    scale = MEM_HEAD_DIM ** -0.5
    nsub = mk_ref.shape[0]
    rows = x1.shape[0] // nsub
    cells = [(hd, s) for hd in range(MEM_HEADS) for s in range(nsub)]
    hsl = lambda hd: slice(hd * MEM_HEAD_DIM, (hd + 1) * MEM_HEAD_DIM)
    qh = [_bf(_rms(q[:, hsl(hd)], qg)) for hd in range(MEM_HEADS)]
    logits = [_dot_nt(qh[hd][s * rows:(s + 1) * rows], _bf(mk_ref[s, :, hsl(hd)])) * scale for hd, s in cells]
    ps = [jnp.exp(l - jnp.max(l, axis=-1, keepdims=True)) for l in logits]
    ps = [_bf(p / jnp.sum(p, axis=-1, keepdims=True)) for p in ps]
    pv = [_bf(_dot(p, _bf(mv_ref[s, :, hsl(hd)]))) for p, (hd, s) in zip(ps, cells)]
    outs = []
    for hd in range(MEM_HEADS):
        subs = pv[hd * nsub:(hd + 1) * nsub]
        outs.append(subs[0] if nsub == 1 else jnp.concatenate(subs, axis=0))
    att = jnp.concatenate(outs, axis=-1)
    y_ref[...] = x1 + _dot(att, wmo_ref[...])


def mem_attend(x, o, w_out, g, w_mq, mq_gain, mk, mv, w_mo, tm, seqlen):
    n, d = x.shape
    mlen, mw = mk.shape[1], mk.shape[2]
    tiles_per_seq = max(1, seqlen // tm)
    seqs_per_tile = max(1, tm // seqlen)
    return pl.pallas_call(
        _memattn_kernel,
        grid=(n // tm,),
        in_specs=[pl.BlockSpec((tm, d), lambda i: (i, 0)),
                  pl.BlockSpec((tm, o.shape[1]), lambda i: (i, 0)),
                  _full(w_out.shape), _full((1, d)), _full(w_mq.shape), _full((1, MEM_HEAD_DIM)),
                  pl.BlockSpec((seqs_per_tile, mlen, mw), lambda i: (i // tiles_per_seq, 0, 0)),
                  pl.BlockSpec((seqs_per_tile, mlen, mw), lambda i: (i // tiles_per_seq, 0, 0)),
                  _full(w_mo.shape)],
        out_specs=pl.BlockSpec((tm, d), lambda i: (i, 0)),
        out_shape=jax.ShapeDtypeStruct((n, d), F32),
        compiler_params=_params("parallel"),
        name="mem_attend",
    )(x, o, w_out, g.reshape(1, d), w_mq, mq_gain.reshape(1, MEM_HEAD_DIM), mk, mv, w_mo)


def _ffn_kernel(x_ref, g_ref, w1_ref, w3_ref, w2_ref, y_ref, *, hid_chunk):
    x = x_ref[...]
    h = _bf(_rms(x, g_ref[...]))
    hidden = w1_ref.shape[1]
    y_ref[...] = x
    for c in range(0, hidden, hid_chunk):
        a = _dot(h, w1_ref[:, c:c + hid_chunk])
        b = _dot(h, w3_ref[:, c:c + hid_chunk])
        y_ref[...] += _dot(_bf(_silu(a) * b), w2_ref[c:c + hid_chunk, :])


def ffn(x, g, w1, w3, w2, tm):
    n, d = x.shape
    hidden = w1.shape[1]
    return pl.pallas_call(
        functools.partial(_ffn_kernel, hid_chunk=256),
        grid=(n // tm,),
        in_specs=[pl.BlockSpec((tm, d), lambda i: (i, 0)), _full((1, d)),
                  _full((d, hidden)), _full((d, hidden)), _full((hidden, d))],
        out_specs=pl.BlockSpec((tm, d), lambda i: (i, 0)),
        out_shape=jax.ShapeDtypeStruct((n, d), F32),
        compiler_params=_params("parallel"),
        name="ffn",
    )(x, g.reshape(1, d), w1, w3, w2)


def _proj_dsa_kernel(x_ref, g_ref, w_ref, qg_ref, kg_ref, kig_ref, bd_ref,
                     q_o, k_o, kb_o, v_o, vb_o, qi_o, kiwi_o, *maybe_kit_o, tstate):
    h = _bf(_rms(x_ref[...], g_ref[...]))
    tm = x_ref.shape[0]
    hq = A_HEADS * A_HEAD_DIM
    bd = bd_ref[...]
    inv_hd = 1.0 / A_HEAD_DIM
    nw = bd.shape[0]

    def head_norm(p, gain):
        hi, mid = _split2(p * p)
        ss = _dot(hi, bd) + _dot(mid, bd)
        return p * lax.rsqrt(ss * inv_hd + EPS) * gain

    def transposed(x):
        return jnp.concatenate([x[r:r + LANES, :].T for r in range(0, tm, LANES)], axis=1)

    def store_state(o_ref, x, col0):
        if not tstate:
            o_ref[:, col0:col0 + x.shape[1]] = x
            return
        for j in range(0, x.shape[1], LANES):
            xt = transposed(x[:, j:j + LANES])
            hd = (col0 + j) // A_HEAD_DIM
            for half in range(LANES // A_HEAD_DIM):
                o_ref[0, hd + half] = xt[half * A_HEAD_DIM:(half + 1) * A_HEAD_DIM]

    step = 512
    for c in range(0, hq, step):
        pq = _dot(h, w_ref[:, c:c + step])
        pk = _dot(h, w_ref[:, hq + c:hq + c + step])
        pv = _dot(h, w_ref[:, 2 * hq + c:2 * hq + c + step])
        for j in range(0, step, nw):
            sl = slice(c + j, c + j + nw)
            qn = head_norm(pq[:, j:j + nw], qg_ref[:, sl])
            q_o[:, sl] = _bf(qn * (A_HEAD_DIM ** -0.5))
            kn = head_norm(pk[:, j:j + nw], kg_ref[:, sl])
            store_state(k_o, kn, c + j)
            kb_o[:, sl] = _bf(kn)
        store_state(v_o, pv, c)
        vb_o[:, c:c + step] = _bf(pv)
    qiw = IDX_HEADS * LANES
    for c in range(0, qiw, step):
        qi_o[:, c:c + step] = _bf(_dot(h, w_ref[:, 3 * hq + c:3 * hq + c + step]))
    p = _dot(h, w_ref[:, 3 * hq + qiw:3 * hq + qiw + LANES])
    lane = lax.broadcasted_iota(I32, p.shape, 1)
    is_ki = lane < IDX_DIM
    ss = jnp.sum(jnp.where(is_ki, p * p, 0.0), axis=-1, keepdims=True)
    kin = p * lax.rsqrt(ss * (1.0 / IDX_DIM) + EPS) * kig_ref[...]
    kiwi = jnp.where(is_ki, kin, p)
    kiwi_o[...] = kiwi
    if tstate:
        maybe_kit_o[0][0] = transposed(kiwi)[0:IDX_DIM]


def proj_dsa(x, g, wa, q_gain, k_gain, ki_gain, tm, nseq, tstate):
    n, d = x.shape
    hq = A_HEADS * A_HEAD_DIM
    m = wa.shape[1]
    seqlen = n // nseq
    tiles = seqlen // tm
    qg = jnp.tile(q_gain, A_HEADS).reshape(1, hq)
    kg = jnp.tile(k_gain, A_HEADS).reshape(1, hq)
    kig = jnp.concatenate([ki_gain, jnp.ones((LANES - IDX_DIM,), F32)]).reshape(1, LANES)
    r = jnp.arange(2 * LANES)
    bd = _bf((r[:, None] // A_HEAD_DIM) == (r[None, :] // A_HEAD_DIM))
    row = lambda w: pl.BlockSpec((tm, w), lambda i: (i, 0))
    if tstate:
        state = pl.BlockSpec((1, A_HEADS, A_HEAD_DIM, tm), lambda i: (i // tiles, 0, 0, i % tiles))
        state_shape = jax.ShapeDtypeStruct((nseq, A_HEADS, A_HEAD_DIM, seqlen), F32)
        extra_specs = [pl.BlockSpec((1, IDX_DIM, tm), lambda i: (i // tiles, 0, i % tiles))]
        extra_shapes = [jax.ShapeDtypeStruct((nseq, IDX_DIM, seqlen), F32)]
    else:
        state, state_shape, extra_specs, extra_shapes = row(hq), jax.ShapeDtypeStruct((n, hq), F32), [], []
    return pl.pallas_call(
        functools.partial(_proj_dsa_kernel, tstate=tstate),
        grid=(n // tm,),
        in_specs=[row(d), _full((1, d)), _full((d, m)), _full((1, hq)), _full((1, hq)),
                  _full((1, LANES)), _full(bd.shape)],
        out_specs=[row(hq), state, row(hq), state, row(hq), row(IDX_HEADS * LANES), row(LANES)] + extra_specs,
        out_shape=[jax.ShapeDtypeStruct((n, hq), BF16), state_shape,
                   jax.ShapeDtypeStruct((n, hq), BF16), state_shape,
                   jax.ShapeDtypeStruct((n, hq), BF16),
                   jax.ShapeDtypeStruct((n, IDX_HEADS * LANES), BF16),
                   jax.ShapeDtypeStruct((n, LANES), F32)] + extra_shapes,
        compiler_params=_params("parallel"),
        name="proj_dsa",
    )(x, g.reshape(1, d), wa, qg, kg, kig, bd)


def dsa_weight(w_in):
    hq = A_HEADS * A_HEAD_DIM
    o3 = 3 * hq
    o4 = o3 + IDX_HEADS * IDX_DIM
    d = w_in.shape[0]
    wqi = w_in[:, o3:o4].reshape(d, IDX_HEADS, IDX_DIM)
    wqi = jnp.pad(wqi, ((0, 0), (0, 0), (0, LANES - IDX_DIM))).reshape(d, IDX_HEADS * LANES)
    tail = jnp.pad(w_in[:, o4:], ((0, 0), (0, LANES - (w_in.shape[1] - o4))))
    return _bf(jnp.concatenate([w_in[:, :o3], wqi, tail], axis=1))


def _sortable(s):
    b = pltpu.bitcast(s, I32)
    b = jnp.where(b == jnp.int32(-2 ** 31), 0, b)
    return jnp.where(b < 0, b ^ jnp.int32(0x7FFFFFFF), b)


def _index_mask_kernel(qi_ref, wit_ref, *rest, nref, groups, ltot, topk, causal, case_rows):
    ki_refs = rest[:nref]
    o_ref = rest[nref]
    key_ref, sel_ref, jv_ref = rest[nref + 1:]
    tq = qi_ref.shape[0]
    i = pl.program_id(1)
    lpad = sel_ref.shape[0]
    wit = wit_ref[0]
    if case_rows:
        ncase = (jnp.maximum((i + 1) * tq, topk) + case_rows - 1) // case_rows
        used_rows = ncase * case_rows
    else:
        used_rows = None

    def score_group(dst0, rows, sources):
        accs = []
        for rp, lead, src0, width, _ in sources:
            ref = ki_refs[rp]
            ki = _bf(ref[src0:src0 + rows, :] if lead is None else ref[lead, src0:src0 + rows, :])
            acc = jnp.zeros((rows, tq), F32)
            for hd in range(IDX_HEADS):
                rel = _dot_nt(ki, qi_ref[:, hd * LANES:hd * LANES + width])
                acc = acc + wit[hd:hd + 1, :] * jnp.maximum(rel, 0.0)
            accs.append(acc)
        if len(accs) == 1:
            acc = accs[0]
        else:
            lane = lax.broadcasted_iota(I32, (rows, tq), 1)
            acc = jnp.where(lane < tq // 2, accs[0], accs[1])
        s = acc * ((IDX_DIM ** -0.5) * (IDX_HEADS ** -0.5))
        if causal:
            kpos = dst0 + lax.broadcasted_iota(I32, (rows, tq), 0)
            qpos = i * tq + lax.broadcasted_iota(I32, (rows, tq), 1)
            s = jnp.where((kpos >> CHUNK_SHIFT) <= (qpos >> CHUNK_SHIFT), s, NEG)
        key_ref[dst0:dst0 + rows, :] = _sortable(s)

    for dst0, rows, sources in groups:
        if used_rows is None:
            score_group(dst0, rows, sources)
        else:
            pl.when(dst0 < used_rows)(functools.partial(score_group, dst0, rows, sources))

    def select(nrows):
        idx_bits = max(1, (nrows - 1).bit_length())

        def count(pred):
            c = pred.astype(I32).reshape(COUNT_CHAINS, nrows // COUNT_CHAINS, tq)
            return jnp.sum(jnp.sum(c, axis=1), axis=0, keepdims=True)

        def ge_count(cand):
            return count(key_ref[0:nrows, :] >= cand)

        t0 = jnp.full((1, tq), -2 ** 31, I32)
        t = jnp.where(ge_count(jnp.zeros((1, tq), I32)) >= topk, 0, t0)

        def vbody(it, t):
            cand = t + (jnp.int32(1) << (30 - it))
            return jnp.where(ge_count(cand) >= topk, cand, t)

        t = lax.fori_loop(0, 31, vbody, t)
        keys = key_ref[0:nrows, :]
        gt = keys > t
        eq = keys == t
        need = topk - count(gt)
        rowi = lax.broadcasted_iota(I32, (nrows, tq), 0)

        def jbody(it, jv):
            cand = jv + (jnp.int32(1) << (idx_bits - 1 - it))
            below = count(eq & (rowi < cand))
            return jnp.where(below < need, cand, jv)

        jv_ref[...] = jnp.full((1, tq), nrows, I32)
        has_tie = jnp.max(count(eq) - need) > 0

        @pl.when(has_tie)
        def _():
            jv_ref[...] = lax.fori_loop(0, idx_bits, jbody, jnp.zeros((1, tq), I32))

        sel = gt | (eq & (rowi <= jv_ref[...]))
        if causal:
            qpos = i * tq + lax.broadcasted_iota(I32, (nrows, tq), 1)
            sel = sel & ((rowi >> CHUNK_SHIFT) <= (qpos >> CHUNK_SHIFT))
        sel_ref[0:nrows, :] = jnp.where(sel, 0.0, NEG)
        nreal = -(-nrows // TILE)
        if nreal * TILE > nrows:
            sel_ref[nrows:nreal * TILE, :] = jnp.full((nreal * TILE - nrows, tq), NEG, F32)
        for kt in range(lpad // TILE):
            if kt < nreal:
                o_ref[0, 0, kt] = sel_ref[kt * TILE:(kt + 1) * TILE, :].T
            else:
                o_ref[0, 0, kt] = jnp.full((tq, TILE), NEG, F32)

    if case_rows:
        for k in range(ltot // case_rows):
            pl.when(ncase == k + 1)(functools.partial(select, (k + 1) * case_rows))
    else:
        select(ltot)


def index_mask(qi, wit, ki_arrays, ki_specs, groups, ltot, nstep, nqb, causal):
    lpad = -(-ltot // KTILE) * KTILE
    nt = lpad // TILE
    topk = min(TOPK_MAX, ltot // 4)
    case_rows = MASK_CASE_ROWS if (causal and ltot % MASK_CASE_ROWS == 0 and ltot > MASK_CASE_ROWS) else 0
    kern = functools.partial(_index_mask_kernel, nref=len(ki_arrays), groups=tuple(groups), ltot=ltot,
                             topk=topk, causal=causal, case_rows=case_rows)
    return pl.pallas_call(
        kern,
        grid=(nstep, nqb),
        in_specs=[pl.BlockSpec((TILE, qi.shape[1]), lambda b, i: (b * nqb + i, 0)),
                  pl.BlockSpec((1, IDX_HEADS, TILE), lambda b, i: (b * nqb + i, 0, 0))] + list(ki_specs),
        out_specs=pl.BlockSpec((1, 1, nt, TILE, TILE), lambda b, i: (b, i, 0, 0, 0)),
        out_shape=jax.ShapeDtypeStruct((nstep, nqb, nt, TILE, TILE), F32),
        scratch_shapes=[pltpu.VMEM((ltot, TILE), I32), pltpu.VMEM((lpad, TILE), F32),
                        pltpu.VMEM((1, TILE), I32)],
        compiler_params=_params("parallel", "parallel"),
        name="index_mask",
    )(qi, wit, *ki_arrays)


def _row_groups(row0, rows, make_sources):
    out = []
    for off in range(0, rows, SCORE_ROWS):
        out.append((row0 + off, min(SCORE_ROWS, rows - off), tuple(make_sources(off))))
    return out


def _attn_core(q_ref, o_ref, qs_ref, l_ref, mx_ref, acc_ref, groups, tq):
    npairs = A_HEADS // 2
    lane = lax.broadcasted_iota(I32, (1, LANES), 1)
    keep_lo = _bf((lane < A_HEAD_DIM).astype(F32))
    keep_hi = _bf((lane >= A_HEAD_DIM).astype(F32))
    for pair in range(npairs):
        qp = q_ref[:, pair * LANES:(pair + 1) * LANES]
        qs_ref[pair, 0:tq, :] = qp * keep_lo
        qs_ref[pair, tq:2 * tq, :] = qp * keep_hi
    mx_ref[...] = jnp.full(mx_ref.shape, NEG, F32)
    acc_ref[...] = jnp.zeros(acc_ref.shape, F32)
    ones = jnp.ones((KTILE, LANES), BF16)

    def over_tiles(count, body):
        if isinstance(count, int) and count == 1:
            body(0, 0)
        else:
            lax.fori_loop(0, count, body, 0)

    for count, base, k_tile, _, mask_sub, bias_sub in groups:
        def p1(kt, carry, base=base, k_tile=k_tile, mask_sub=mask_sub, bias_sub=bias_sub):
            masks = [mask_sub(kt, sub) for sub in range(KTILE // LANES)]
            for pair in range(npairs):
                sl = slice(pair * LANES, (pair + 1) * LANES)
                l = _dot_nt(qs_ref[pair], k_tile(kt, sl))
                for half in range(2):
                    rs = slice(half * tq, (half + 1) * tq)
                    mx = mx_ref[pair, rs, :]
                    for sub in range(KTILE // LANES):
                        cs = slice(sub * LANES, (sub + 1) * LANES)
                        blk = l[rs, cs] + (masks[sub] + bias_sub(kt, sub, 2 * pair + half))
                        l_ref[pair, base + kt, rs, cs] = blk
                        mx = jnp.maximum(mx, blk)
                    mx_ref[pair, rs, :] = mx
            return carry

        over_tiles(count, p1)

    for pair in range(npairs):
        m = jnp.max(mx_ref[pair], axis=-1, keepdims=True)
        mx_ref[pair] = jnp.broadcast_to(m, mx_ref.shape[1:])

    for count, base, _, v_tile, _, _ in groups:
        def p2(kt, carry, base=base, v_tile=v_tile):
            for pair in range(npairs):
                sl = slice(pair * LANES, (pair + 1) * LANES)
                m = mx_ref[pair]
                p = jnp.concatenate(
                    [jnp.exp(l_ref[pair, base + kt, :, sub * LANES:(sub + 1) * LANES] - m)
                     for sub in range(KTILE // LANES)], axis=1)
                vaug = jnp.concatenate([v_tile(kt, sl), ones], axis=1)
                acc_ref[pair] += _dot(_bf(p), vaug)
            return carry

        over_tiles(count, p2)

    lane_full = lax.broadcasted_iota(I32, (tq, LANES), 1)
    for pair in range(npairs):
        a = acc_ref[pair]
        o = a[:, 0:LANES] / a[:, LANES:2 * LANES]
        o_ref[:, pair * LANES:(pair + 1) * LANES] = _bf(jnp.where(lane_full < A_HEAD_DIM, o[0:tq], o[tq:2 * tq]))


def _attn_scratch(tq, ntiles):
    npairs = A_HEADS // 2
    return [pltpu.VMEM((npairs, 2 * tq, LANES), BF16),
            pltpu.VMEM((npairs, ntiles, 2 * tq, KTILE), F32),
            pltpu.VMEM((npairs, 2 * tq, LANES), F32),
            pltpu.VMEM((npairs, 2 * tq, 2 * LANES), F32)]


def _attn_prompt_kernel(q_ref, k_ref, v_ref, mask_ref, bias_ref, o_ref, qs_ref, l_ref, mx_ref, acc_ref):
    i = pl.program_id(1)
    per = KTILE // TILE

    def rows(kt):
        return pl.ds(pl.multiple_of(kt * KTILE, KTILE), KTILE)

    def bias_sub(kt, sub, hd):
        s = kt * per + sub
        sel = jnp.where(s == i, 2, jnp.where(s == i - 1, 1, 0))
        return bias_ref[sel, hd]

    group = (i // per + 1, 0,
             lambda kt, sl: k_ref[rows(kt), sl],
             lambda kt, sl: v_ref[rows(kt), sl],
             lambda kt, sub: mask_ref[0, 0, kt * per + sub],
             bias_sub)
    _attn_core(q_ref, o_ref, qs_ref, l_ref, mx_ref, acc_ref, [group], TILE)


def attn_prompt(q, kb, vb, mask, bias, nseq, seqlen):
    n, hq = q.shape
    nqb = seqlen // TILE
    nt = mask.shape[2]
    return pl.pallas_call(
        _attn_prompt_kernel,
        grid=(nseq, nqb),
        in_specs=[pl.BlockSpec((TILE, hq), lambda b, i: (b * nqb + i, 0)),
                  pl.BlockSpec((seqlen, hq), lambda b, i: (b, 0)),
                  pl.BlockSpec((seqlen, hq), lambda b, i: (b, 0)),
                  pl.BlockSpec((1, 1, nt, TILE, TILE), lambda b, i: (b, i, 0, 0, 0)),
                  _full(bias.shape)],
        out_specs=pl.BlockSpec((TILE, hq), lambda b, i: (b * nqb + i, 0)),
        out_shape=jax.ShapeDtypeStruct((n, hq), BF16),
        scratch_shapes=_attn_scratch(TILE, seqlen // KTILE),
        compiler_params=_params("parallel", "arbitrary"),
        name="attn_prompt",
    )(q, kb, vb, mask, bias)


def _attn_sample_kernel(q_ref, kc_ref, vc_ref, kn_ref, vn_ref, mask_ref, bias_ref, o_ref,
                        l_ref, mx_ref, acc_ref, *, nblk):
    ph = pl.program_id(1)
    b = pl.program_id(2)
    tq = q_ref.shape[0]
    per = KTILE // TILE
    tiles_per_blk = kc_ref.shape[3] // KTILE
    ncache = nblk * tiles_per_blk
    last_sub = ncache * per - 1
    heads = range(A_HEADS)

    def head_tile(ref, c0, hd):
        return _bf(ref[0, hd, :, c0:c0 + KTILE])

    def pass1(slot, ref, c0, mask_sub, bias_sub):
        ls = [_dot(q_ref[:, hd * A_HEAD_DIM:(hd + 1) * A_HEAD_DIM], head_tile(ref, c0, hd))
              for hd in heads]
        masks = [mask_sub(sub) for sub in range(per)]
        for hd in heads:
            mx = mx_ref[hd]
            for sub in range(per):
                cs = slice(sub * LANES, (sub + 1) * LANES)
                blk = ls[hd][:, cs] + (masks[sub] + bias_sub(sub, hd))
                l_ref[hd, slot, :, cs] = blk
                mx = jnp.maximum(mx, blk)
            mx_ref[hd] = mx

    def pass2(slot, ref, c0):
        ones = jnp.ones((LANES - A_HEAD_DIM, KTILE), BF16)
        for hd in heads:
            m = mx_ref[hd]
            p = jnp.concatenate([jnp.exp(l_ref[hd, slot, :, sub * LANES:(sub + 1) * LANES] - m)
                                 for sub in range(per)], axis=1)
            vaug = jnp.concatenate([head_tile(ref, c0, hd), ones], axis=0)
            acc_ref[hd] += _dot_nt(_bf(p), vaug)

    @pl.when((ph == 0) & (b == 0))
    def _():
        mx_ref[...] = jnp.full(mx_ref.shape, NEG, F32)
        acc_ref[...] = jnp.zeros(acc_ref.shape, F32)

    @pl.when(ph == 0)
    def _():
        for t in range(tiles_per_blk):
            g = b * tiles_per_blk + t
            pass1(g, kc_ref, t * KTILE,
                  lambda sub, g=g: mask_ref[0, 0, g * per + sub],
                  lambda sub, hd, g=g: bias_ref[jnp.where(g * per + sub == last_sub, 1, 0), hd])

    @pl.when((ph == 0) & (b == nblk - 1))
    def _():
        pass1(ncache, kn_ref, 0,
              lambda sub: mask_ref[0, 0, ncache * per + sub],
              lambda sub, hd: bias_ref[2 if sub == 0 else 0, hd])
        for hd in heads:
            m = jnp.max(mx_ref[hd], axis=-1, keepdims=True)
            mx_ref[hd] = jnp.broadcast_to(m, mx_ref.shape[1:])

    @pl.when(ph == 1)
    def _():
        for t in range(tiles_per_blk):
            pass2(b * tiles_per_blk + t, vc_ref, t * KTILE)

    @pl.when((ph == 1) & (b == nblk - 1))
    def _():
        pass2(ncache, vn_ref, 0)
        for pair in range(A_HEADS // 2):
            halves = []
            for hd in (2 * pair, 2 * pair + 1):
                a = acc_ref[hd]
                halves.append(a[:, 0:A_HEAD_DIM] / a[:, A_HEAD_DIM:2 * A_HEAD_DIM])
            o_ref[:, pair * LANES:(pair + 1) * LANES] = _bf(jnp.concatenate(halves, axis=1))


def attn_sample(q, kc, vc, kn, vn, mask, bias, nseq, tq):
    n, hq = q.shape
    past = kc.shape[3]
    cblk = min(CACHE_BLOCK, past)
    nblk = past // cblk
    ncache = past // KTILE
    nt = mask.shape[2]
    per = TILE // tq
    cache_k = pl.BlockSpec((1, A_HEADS, A_HEAD_DIM, cblk),
                           lambda s, ph, b: (s, 0, 0, b * (1 - ph) + (nblk - 1) * ph))
    cache_v = pl.BlockSpec((1, A_HEADS, A_HEAD_DIM, cblk), lambda s, ph, b: (s, 0, 0, b * ph))
    fresh = pl.BlockSpec((1, A_HEADS, A_HEAD_DIM, KTILE), lambda s, ph, b: (s, 0, 0, 0))
    return pl.pallas_call(
        functools.partial(_attn_sample_kernel, nblk=nblk),
        grid=(nseq, 2, nblk),
        in_specs=[pl.BlockSpec((tq, hq), lambda s, ph, b: (s, 0)),
                  cache_k, cache_v, fresh, fresh,
                  pl.BlockSpec((1, 1, nt, tq, TILE), lambda s, ph, b: (s // per, 0, 0, s % per, 0)),
                  pl.BlockSpec(bias.shape, lambda s, ph, b: (0, 0, 0, 0))],
        out_specs=pl.BlockSpec((tq, hq), lambda s, ph, b: (s, 0)),
        out_shape=jax.ShapeDtypeStruct((n, hq), BF16),
        scratch_shapes=[pltpu.VMEM((A_HEADS, ncache + 1, tq, KTILE), F32),
                        pltpu.VMEM((A_HEADS, tq, LANES), F32),
                        pltpu.VMEM((A_HEADS, tq, LANES), F32)],
        compiler_params=_params("parallel", "arbitrary", "arbitrary"),
        name="attn_sample",
    )(q, kc, vc, kn, vn, mask, bias)


def _t5_bucket(rel):
    half = N_BUCKETS // 2
    max_exact = half // 2
    n = jnp.abs(rel)
    nf = jnp.maximum(n, 1).astype(F32)
    large = max_exact + (jnp.log(nf / max_exact) / math.log(MAX_DISTANCE / max_exact)
                         * (half - max_exact)).astype(I32)
    large = jnp.minimum(large, half - 1)
    return jnp.where(rel > 0, half, 0) + jnp.where(n < max_exact, n, large)


def bias_tiles(rel_bias):
    heads = rel_bias.shape[1]
    span = 2 * TILE - 1

    def toeplitz(shift):
        rel = jnp.arange(span, dtype=I32) - (TILE - 1) + shift
        tab = rel_bias.astype(F32)[_t5_bucket(rel)].T
        strip = jnp.tile(jnp.pad(tab, ((0, 0), (0, 1))), (1, TILE))[:, :TILE * span]
        return strip.reshape(heads, TILE, span)[:, :, TILE - 1:]

    far = rel_bias.astype(F32)[_t5_bucket(jnp.full((1,), -(TILE + 1), I32))]
    far = jnp.broadcast_to(far.T[:, :, None], (heads, TILE, TILE))
    return jnp.stack([far, toeplitz(-TILE), toeplitz(0)])


def dsa_core_prompt(pr, nseq, seqlen, bias):
    q, _, kb, _, vb, qi, kiwi = pr
    nqb = seqlen // TILE
    wit = kiwi[:, IDX_DIM:IDX_DIM + IDX_HEADS].reshape(nseq * nqb, TILE, IDX_HEADS).swapaxes(1, 2)
    groups = _row_groups(0, seqlen, lambda off: [(0, None, off, LANES, None)])
    mask = index_mask(qi, wit, [kiwi], [pl.BlockSpec((seqlen, LANES), lambda b, i: (b, 0))],
                      groups, seqlen, nseq, nqb, causal=True)
    return attn_prompt(q, kb, vb, mask, bias, nseq, seqlen)


def dsa_core_sample(pr, k_cache, v_cache, ki_cache, nseq, tq, bias):
    q, kf, _, vf, _, qi, kiwi = pr
    past = k_cache.shape[1]
    per = TILE // tq
    wit = kiwi[:, IDX_DIM:IDX_DIM + IDX_HEADS].reshape(nseq // per, TILE, IDX_HEADS).swapaxes(1, 2)
    groups = (_row_groups(0, past, lambda off: [(0, j, off, IDX_DIM, j) for j in range(per)])
              + _row_groups(past, tq, lambda off: [(1, None, j * tq + off, LANES, j) for j in range(per)]))
    mask = index_mask(
        qi, wit, [ki_cache, kiwi],
        [pl.BlockSpec((per, past, IDX_DIM), lambda b, i: (b, 0, 0)),
         pl.BlockSpec((TILE, LANES), lambda b, i: (b, 0))],
        groups, past + tq, nseq // per, 1, causal=False)
    to_hdk = lambda a: a.transpose(0, 2, 3, 1)
    pad_keys = ((0, 0), (0, 0), (0, 0), (0, KTILE - tq))
    kn = jnp.pad(to_hdk(kf.reshape(nseq, tq, A_HEADS, A_HEAD_DIM)), pad_keys)
    vn = jnp.pad(to_hdk(vf.reshape(nseq, tq, A_HEADS, A_HEAD_DIM)), pad_keys)
    return attn_sample(q, to_hdk(k_cache), to_hdk(v_cache), kn, vn, mask, bias[:, :, :tq, :], nseq, tq)


def _pair_triangles():
    r = jnp.arange(2 * CHUNK)[:, None]
    c = jnp.arange(2 * CHUNK)[None, :]
    same = (r // CHUNK) == (c // CHUNK)
    return _bf(same & (c <= r)), _bf(same & (r <= c))


def _log_sigmoid(x):
    return jnp.minimum(x, 0.0) - jnp.log1p(jnp.exp(-jnp.abs(x)))


def _mlstm_kernel(p_ref, gs_ref, gr_ref, gbc_ref, gbr_ref, hg_ref, tl_ref, tu_ref, c0_ref, n0_ref, m0_ref,
                  hs_ref, c_ref, n_ref, m_ref, cs, ns, ms):
    c_id = pl.program_id(1)
    nh = B_HEADS
    npair = nh // 2
    hw = LANES
    nsq = p_ref.shape[0]
    L = p_ref.shape[1]
    L2 = 2 * L

    @pl.when(c_id == 0)
    def _():
        cs[...] = jnp.zeros(cs.shape, F32)
        ns[...] = jnp.zeros(ns.shape, F32)
        for sq in range(nsq):
            ns[sq, :, 0:B_QK_DIM] = n0_ref[sq]
            for hd in range(nh):
                cs[sq, hd // 2, (hd % 2) * hw:(hd % 2) * hw + B_QK_DIM, :] = c0_ref[sq, hd]
        ms[...] = m0_ref[...]

    r = lax.broadcasted_iota(I32, (L2, L2), 0)
    c = lax.broadcasted_iota(I32, (L2, L2), 1)
    same = (r >> CHUNK_SHIFT) == (c >> CHUNK_SHIFT)
    incl = same & (c <= r)
    tri_l = tl_ref[...]
    tri_u = tu_ref[...]
    rowc = lax.broadcasted_iota(I32, (L2, 1), 0)
    is_top = rowc < L
    top = is_top.astype(F32)
    bot = 1.0 - top
    row2 = lax.broadcasted_iota(I32, (2 * hw, 1), 0)
    hg = hg_ref[...]

    def stack(base, sq, pr):
        a = base + 2 * pr * hw
        return jnp.concatenate([p_ref[sq, :, a:a + hw], p_ref[sq, :, a + hw:a + 2 * hw]], axis=0)

    def bd(x):
        return _bf(jnp.concatenate([x * top, x * bot], axis=1))

    seqs = range(nsq)
    gs = [gs_ref[sq, 0] + gbc_ref[...] for sq in seqs]
    gr = [gr_ref[sq, 0] + gbr_ref[...] for sq in seqs]
    bcol = [_dot_exact_lhs01(tri_l, _log_sigmoid(gs[sq][:, npair:])) for sq in seqs]
    brow = [_dot_exact_rhs01(_log_sigmoid(gr[sq][npair:, :]), tri_u) for sq in seqs]
    mall = [ms[sq] for sq in seqs]
    cells = [(sq, pr) for sq in seqs for pr in range(npair)]
    ids = range(len(cells))
    qf = [stack(0, sq, pr) for sq, pr in cells]
    kf = [stack(nh * hw, sq, pr) * (B_QK_DIM ** -0.5) for sq, pr in cells]
    vb = [_bf(stack(2 * nh * hw, sq, pr)) for sq, pr in cells]
    bc = [bcol[sq][:, pr:pr + 1] for sq, pr in cells]
    ic = [gs[sq][:, pr:pr + 1] for sq, pr in cells]
    mc = [mall[sq][:, pr:pr + 1] for sq, pr in cells]
    d = [jnp.where(incl, bc[i] - brow[sq][pr:pr + 1, :] + gr[sq][pr:pr + 1, :], NEG)
         for i, (sq, pr) in enumerate(cells)]
    inter = [bc[i] + mc[i] for i in ids]
    mt = [jnp.maximum(inter[i], jnp.max(d[i], axis=-1, keepdims=True)) for i in ids]
    s = [_dot_nt(_bf(qf[i]), _bf(kf[i])) * jnp.exp(d[i] - mt[i]) for i in ids]
    wst = [jnp.exp(inter[i] - mt[i]) for i in ids]
    cmat = [cs[sq, pr] for sq, pr in cells]
    num = [_dot(_bf(s[i]), vb[i]) + wst[i] * _dot(bd(qf[i]), _bf(cmat[i])) for i in ids]
    for i, (sq, pr) in enumerate(cells):
        n0 = ns[sq, 2 * pr:2 * pr + 1, :]
        n1 = ns[sq, 2 * pr + 1:2 * pr + 2, :]
        qn = jnp.sum(qf[i] * jnp.where(is_top, n0, n1), axis=-1, keepdims=True)
        den = jnp.sum(s[i], axis=-1, keepdims=True) + wst[i] * qn
        hs = num[i] / jnp.maximum(jnp.abs(den), jnp.exp(-mt[i]))
        bl0 = bc[i][L - 1:L, :]
        bl1 = bc[i][L2 - 1:L2, :]
        bl = jnp.where(is_top, bl0, bl1)
        dec = bl - bc[i] + ic[i]
        blm = bl + mc[i]
        mnew0 = jnp.maximum(blm[0:1, :], jnp.max(dec[0:L], axis=0, keepdims=True))
        mnew1 = jnp.maximum(blm[L:L + 1, :], jnp.max(dec[L:L2], axis=0, keepdims=True))
        mnew = jnp.where(is_top, mnew0, mnew1)
        wk = jnp.exp(dec - mnew)
        ws = jnp.exp(blm - mnew)
        kw = kf[i] * wk
        ws2 = jnp.where(row2 < hw, ws[0:1, :], ws[L:L + 1, :])
        cs[sq, pr] = ws2 * cmat[i] + _dot_tn(bd(kw), vb[i])
        ns[sq, 2 * pr:2 * pr + 1, :] = ws[0:1, :] * n0 + jnp.sum(kw[0:L], axis=0, keepdims=True)
        ns[sq, 2 * pr + 1:2 * pr + 2, :] = ws[L:L + 1, :] * n1 + jnp.sum(kw[L:L2], axis=0, keepdims=True)
        ms[sq, :, pr:pr + 1] = mnew
        on = _bf(_rms(hs, hg) * _sigmoid(stack(3 * nh * hw, sq, pr)))
        oa = 2 * pr * B_V_DIM
        hs_ref[sq, :, oa:oa + B_V_DIM] = on[0:L]
        hs_ref[sq, :, oa + B_V_DIM:oa + 2 * B_V_DIM] = on[L:L2]

    @pl.when(c_id == pl.num_programs(1) - 1)
    def _():
        for sq in range(nsq):
            for hd in range(nh):
                c_ref[sq, hd] = cs[sq, hd // 2, (hd % 2) * hw:(hd % 2) * hw + B_QK_DIM, :]
            n_ref[sq] = ns[sq, :, 0:B_QK_DIM]
        m_ref[...] = ms[...]


def mlstm_weight(w_in):
    d = w_in.shape[0]
    nq = B_HEADS * B_QK_DIM
    nv = B_HEADS * B_V_DIM

    def padh(w):
        w = w.reshape(d, B_HEADS, B_QK_DIM)
        return jnp.pad(w, ((0, 0), (0, 0), (0, LANES - B_QK_DIM))).reshape(d, B_HEADS * LANES)

    main = 2 * nq + 2 * nv
    tail = jnp.pad(w_in[:, main:], ((0, 0), (0, LANES - 2 * B_HEADS)))
    return _bf(jnp.concatenate([padh(w_in[:, :nq]), padh(w_in[:, nq:2 * nq]), w_in[:, 2 * nq:main], tail], axis=1))


def mlstm_core(p, gates, gate_bias, h_gain, c0, n0, m0, nseq, nchunk):
    n, m = p.shape
    nh = B_HEADS
    npair = nh // 2
    gcol = 4 * nh * LANES
    nsq = SCAN_SEQS if nseq % SCAN_SEQS == 0 else 1
    seqlen = n // nseq
    pre = gates[:, :2 * nh].reshape(nseq, nchunk, CHUNK, 2, npair, 2)
    gs = pre.transpose(0, 1, 5, 2, 3, 4).reshape(nseq, nchunk, 2 * CHUNK, 2 * npair)
    gr = pre.transpose(0, 1, 3, 4, 5, 2).reshape(nseq, nchunk, 2 * npair, 2 * CHUNK)
    gb = gate_bias.reshape(2, npair, 2)
    gbc = jnp.repeat(gb.transpose(2, 0, 1).reshape(2, 2 * npair), CHUNK, axis=0)
    m0s = jnp.repeat(m0.reshape(nseq, npair, 2).transpose(0, 2, 1), CHUNK, axis=1)
    o, c_new, n_new, m_new = pl.pallas_call(
        _mlstm_kernel,
        grid=(nseq // nsq, nchunk),
        in_specs=[pl.BlockSpec((nsq, CHUNK, m), lambda s, c: (s, c, 0)),
                  pl.BlockSpec((nsq, 1, 2 * CHUNK, 2 * npair), lambda s, c: (s, c, 0, 0)),
                  pl.BlockSpec((nsq, 1, 2 * npair, 2 * CHUNK), lambda s, c: (s, c, 0, 0)),
                  _full((2 * CHUNK, 2 * npair)), _full((2 * npair, 2 * CHUNK)), _full((1, B_V_DIM)),
                  _full((2 * CHUNK, 2 * CHUNK)), _full((2 * CHUNK, 2 * CHUNK)),
                  pl.BlockSpec((nsq, nh, B_QK_DIM, B_V_DIM), lambda s, c: (s, 0, 0, 0)),
                  pl.BlockSpec((nsq, nh, B_QK_DIM), lambda s, c: (s, 0, 0)),
                  pl.BlockSpec((nsq, 2 * CHUNK, npair), lambda s, c: (s, 0, 0))],
        out_specs=[pl.BlockSpec((nsq, CHUNK, nh * B_V_DIM), lambda s, c: (s, c, 0)),
                   pl.BlockSpec((nsq, nh, B_QK_DIM, B_V_DIM), lambda s, c: (s, 0, 0, 0)),
                   pl.BlockSpec((nsq, nh, B_QK_DIM), lambda s, c: (s, 0, 0)),
                   pl.BlockSpec((nsq, 2 * CHUNK, npair), lambda s, c: (s, 0, 0))],
        out_shape=[jax.ShapeDtypeStruct((nseq, seqlen, nh * B_V_DIM), BF16),
                   jax.ShapeDtypeStruct((nseq, nh, B_QK_DIM, B_V_DIM), F32),
                   jax.ShapeDtypeStruct((nseq, nh, B_QK_DIM), F32),
                   jax.ShapeDtypeStruct((nseq, 2 * CHUNK, npair), F32)],
        scratch_shapes=[pltpu.VMEM((nsq, npair, 2 * LANES, B_V_DIM), F32), pltpu.VMEM((nsq, nh, LANES), F32),
                        pltpu.VMEM((nsq, 2 * CHUNK, npair), F32)],
        compiler_params=_params("parallel", "arbitrary"),
        name="mlstm",
    )(p.reshape(nseq, seqlen, m), gs, gr, gbc, gbc.T, h_gain.reshape(1, B_V_DIM), *_pair_triangles(),
      c0, n0, m0s)
    m_heads = m_new[:, ::CHUNK, :].transpose(0, 2, 1).reshape(nseq, nh)
    return o.reshape(n, nh * B_V_DIM), c_new, n_new, m_heads


def _split2(x):
    hi = _bf(x)
    return hi, _bf(x - hi.astype(F32))


def _cat3_lhs(x):
    hi, mid = _split2(x)
    return jnp.concatenate([hi, hi, mid], axis=1)


def _cat3_rhs(x):
    hi, mid = _split2(x)
    return jnp.concatenate([hi, mid, hi], axis=0)


def _gdn_kernel(p_ref, gs_ref, gr_ref, cw_ref, alc_ref, alr_ref, dtc_ref, dtr_ref, og_ref, tl_ref, tu_ref,
                s0_ref, cb0_ref, o_ref, s_ref, cb_ref, ss, tail):
    c_id = pl.program_id(1)
    nh = C_HEADS
    cdim = nh * (2 * C_DK + C_DV)
    L = p_ref.shape[0]
    L2 = 2 * L
    nprev = CONV_W - 1

    @pl.when(c_id == 0)
    def _():
        ss[...] = s0_ref[0]
        tail[...] = jnp.zeros(tail.shape, F32)
        tail[8 - nprev:8, :] = cb0_ref[0]

    x = p_ref[:, 0:cdim]
    ext = jnp.concatenate([tail[...], x], axis=0)
    conv = ext[8:8 + L] * cw_ref[CONV_W - 1:CONV_W, :]
    for j in range(CONV_W - 1):
        conv = conv + ext[8 - nprev + j:8 - nprev + j + L] * cw_ref[j:j + 1, :]
    tail[...] = x[L - 8:L, :]
    cf = _silu(conv)

    npair = nh // 2
    gs = gs_ref[0]
    gr = gr_ref[0]
    beta = _sigmoid(gs[:, 0:npair])
    g_col = -jnp.exp(alc_ref[...]) * _softplus(gs[:, npair:] + dtc_ref[...])
    g_row = -jnp.exp(alr_ref[...]) * _softplus(gr[npair:, :] + dtr_ref[...])
    r = lax.broadcasted_iota(I32, (L2, L2), 0)
    c = lax.broadcasted_iota(I32, (L2, L2), 1)
    same = (r >> CHUNK_SHIFT) == (c >> CHUNK_SHIFT)
    incl = same & (c <= r)
    strict = same & (c < r)
    eye = (c == r).astype(F32)
    gc_col = _dot_exact_lhs01(tl_ref[...], g_col)
    gc_row = _dot_exact_rhs01(g_row, tu_ref[...])
    rowc = lax.broadcasted_iota(I32, (L2, 1), 0)
    top = (rowc < L).astype(F32)
    bot = 1.0 - top
    row2 = lax.broadcasted_iota(I32, (2 * C_DK, 1), 0)
    og = og_ref[...]

    def stack(base, pr):
        a = base + 2 * pr * C_DK
        return jnp.concatenate([cf[:, a:a + C_DK], cf[:, a + C_DK:a + 2 * C_DK]], axis=0)

    def bd(x):
        return _bf(jnp.concatenate([x * top, x * bot], axis=1))

    pairs = range(npair)
    qc, kc, kcb, dm, amat, rhs, gcols, egcs = [], [], [], [], [], [], [], []
    for pr in pairs:
        qraw = stack(0, pr)
        kraw = stack(nh * C_DK, pr)
        vc = stack(2 * nh * C_DK, pr)
        qc.append(qraw * lax.rsqrt(jnp.sum(qraw * qraw, axis=-1, keepdims=True) + EPS) * (C_DK ** -0.5))
        kc.append(kraw * lax.rsqrt(jnp.sum(kraw * kraw, axis=-1, keepdims=True) + EPS))
        bc = beta[:, pr:pr + 1]
        gcol = gc_col[:, pr:pr + 1]
        grow = gc_row[pr:pr + 1, :]
        dm.append(jnp.where(incl, jnp.exp(jnp.where(incl, gcol - grow, 0.0)), 0.0))
        kb = kc[pr] * bc
        kcb.append(_bf(kc[pr]))
        amat.append(jnp.where(strict, _dot_nt(_bf(kb), kcb[pr]) * dm[pr], 0.0))
        egc = jnp.exp(gcol)
        rhs.append(jnp.concatenate([vc * bc, kb * egc], axis=-1))
        gcols.append(gcol)
        egcs.append(egc)
    tinv = [eye - amat[pr] for pr in pairs]
    pw_l = [_cat3_lhs(-amat[pr]) for pr in pairs]
    pw_r = [_cat3_rhs(-amat[pr]) for pr in pairs]
    for _ in range(CHUNK_SHIFT - 1):
        pw = [_dot(pw_l[pr], pw_r[pr]) for pr in pairs]
        pw_l = [_cat3_lhs(pw[pr]) for pr in pairs]
        pw_r = [_cat3_rhs(pw[pr]) for pr in pairs]
        tinv = [tinv[pr] + _dot(_cat3_lhs(tinv[pr]), pw_r[pr]) for pr in pairs]
    sol = [_dot(_cat3_lhs(tinv[pr]), _cat3_rhs(rhs[pr])) for pr in pairs]
    attn = [_dot_nt(_bf(qc[pr]), kcb[pr]) * dm[pr] for pr in pairs]
    smat = [ss[pr] for pr in pairs]
    sb = [_bf(smat[pr]) for pr in pairs]
    vnew = [sol[pr][:, :C_DV] - _dot(bd(sol[pr][:, C_DV:]), sb[pr]) for pr in pairs]
    o = [_dot(bd(qc[pr] * egcs[pr]), sb[pr]) + _dot(_bf(attn[pr]), _bf(vnew[pr])) for pr in pairs]
    for pr in pairs:
        gl0 = gcols[pr][L - 1:L, :]
        gl1 = gcols[pr][L2 - 1:L2, :]
        ke = kc[pr] * jnp.exp(jnp.where(rowc < L, gl0, gl1) - gcols[pr])
        decay = jnp.exp(jnp.where(row2 < C_DK, gl0, gl1))
        ss[pr] = smat[pr] * decay + _dot_tn(bd(ke), _bf(vnew[pr]))
    for pr in pairs:
        za = cdim + 2 * pr * C_DV
        z = jnp.concatenate([p_ref[:, za:za + C_DV], p_ref[:, za + C_DV:za + 2 * C_DV]], axis=0)
        on = _bf(_rms(o[pr], og) * _silu(z))
        oa = 2 * pr * C_DV
        o_ref[:, oa:oa + C_DV] = on[0:L]
        o_ref[:, oa + C_DV:oa + 2 * C_DV] = on[L:L2]

    @pl.when(c_id == pl.num_programs(1) - 1)
    def _():
        s_ref[0] = ss[...]
        cb_ref[0] = tail[8 - nprev:8, :]


def gdn_weight(w_in):
    main = C_HEADS * (2 * C_DK + C_DV) + C_HEADS * C_DV
    tail = jnp.pad(w_in[:, main:], ((0, 0), (0, LANES - 2 * C_HEADS)))
    return _bf(jnp.concatenate([w_in[:, :main], tail], axis=1))


def gdn_core(p, gates, conv_w, a_log, dt_bias, o_gain, s0, cb0, nseq, nchunk):
    n, m = p.shape
    nh = C_HEADS
    npair = nh // 2
    cdim = nh * (2 * C_DK + C_DV)
    gcolumn = cdim + nh * C_DV
    pre = gates[:, :2 * nh].reshape(nseq * nchunk, CHUNK, 2, npair, 2)
    gs = pre.transpose(0, 4, 1, 2, 3).reshape(nseq * nchunk, 2 * CHUNK, 2 * npair)
    gr = pre.transpose(0, 2, 3, 4, 1).reshape(nseq * nchunk, 2 * npair, 2 * CHUNK)

    def col(v):
        return jnp.repeat(v.reshape(npair, 2).T, CHUNK, axis=0)

    o, s_new, cb_new = pl.pallas_call(
        _gdn_kernel,
        grid=(nseq, nchunk),
        in_specs=[pl.BlockSpec((CHUNK, m), lambda s, c: (s * nchunk + c, 0)),
                  pl.BlockSpec((1, 2 * CHUNK, 2 * npair), lambda s, c: (s * nchunk + c, 0, 0)),
                  pl.BlockSpec((1, 2 * npair, 2 * CHUNK), lambda s, c: (s * nchunk + c, 0, 0)),
                  _full((CONV_W, cdim)),
                  _full((2 * CHUNK, npair)), _full((npair, 2 * CHUNK)),
                  _full((2 * CHUNK, npair)), _full((npair, 2 * CHUNK)),
                  _full((1, C_DV)), _full((2 * CHUNK, 2 * CHUNK)), _full((2 * CHUNK, 2 * CHUNK)),
                  pl.BlockSpec((1, npair, 2 * C_DK, C_DV), lambda s, c: (s, 0, 0, 0)),
                  pl.BlockSpec((1, CONV_W - 1, cdim), lambda s, c: (s, 0, 0))],
        out_specs=[pl.BlockSpec((CHUNK, nh * C_DV), lambda s, c: (s * nchunk + c, 0)),
                   pl.BlockSpec((1, npair, 2 * C_DK, C_DV), lambda s, c: (s, 0, 0, 0)),
                   pl.BlockSpec((1, CONV_W - 1, cdim), lambda s, c: (s, 0, 0))],
        out_shape=[jax.ShapeDtypeStruct((n, nh * C_DV), BF16),
                   jax.ShapeDtypeStruct((nseq, npair, 2 * C_DK, C_DV), F32),
                   jax.ShapeDtypeStruct((nseq, CONV_W - 1, cdim), F32)],
        scratch_shapes=[pltpu.VMEM((npair, 2 * C_DK, C_DV), F32), pltpu.VMEM((8, cdim), F32)],
        compiler_params=_params("parallel", "arbitrary"),
        name="gdn",
    )(p, gs, gr, conv_w, col(a_log), col(a_log).T, col(dt_bias), col(dt_bias).T, o_gain.reshape(1, C_DV),
      *_pair_triangles(),
      s0.reshape(nseq, npair, 2 * C_DK, C_DV), cb0)
    return o, s_new.reshape(nseq, nh, C_DK, C_DV), cb_new


def _trunk(x, nseq, seqlen, mem_k, mem_v, st, W, bias, is_prompt):
    d = x.shape[-1]
    n = nseq * seqlen
    x = x.reshape(n, d)
    tm = min(PROJ_ROWS, n)
    tm_mem = min(MEM_ROWS, n)
    new = {}
    for i in range(4):
        kind = i % 3
        mx = W["mixer"][i]
        if kind == 0:
            tstate = is_prompt and seqlen % tm == 0
            pr = proj_dsa(x, W["norm_mix"][i], mx["w_in"], mx["q_gain"], mx["k_gain"], mx["ki_gain"], tm,
                          nseq, tstate)
            if is_prompt:
                o = dsa_core_prompt(pr[:7], nseq, seqlen, bias)
            else:
                o = dsa_core_sample(pr, *st[i], nseq, seqlen, bias)
            if tstate:
                new[i] = (pr[1].transpose(0, 3, 1, 2), pr[3].transpose(0, 3, 1, 2), pr[7].transpose(0, 2, 1))
            else:
                new[i] = (pr[1].reshape(nseq, seqlen, A_HEADS, A_HEAD_DIM),
                          pr[3].reshape(nseq, seqlen, A_HEADS, A_HEAD_DIM),
                          pr[6][:, :IDX_DIM].reshape(nseq, seqlen, IDX_DIM))
        elif kind == 1:
            p, gates = proj(x, W["norm_mix"][i], mx["w_in"], tm)
            o, c_new, n_new, m_new = mlstm_core(p, gates, mx["gate_bias"], mx["h_gain"], *st[i],
                                                nseq, seqlen // CHUNK)
            new[i] = (c_new, n_new, m_new.reshape(nseq, B_HEADS))
        else:
            p, gates = proj(x, W["norm_mix"][i], mx["w_in"], tm)
            o, s_new, cb_new = gdn_core(p, gates, mx["conv_w"], mx["a_log"], mx["dt_bias"], mx["o_gain"], *st[i],
                                        nseq, seqlen // CHUNK)
            new[i] = (s_new, cb_new)
        x = mem_attend(x, o, mx["w_out"], W["norm_mem"][i], W["w_mq"][i], W["mq_gain"][i],
                       mem_k[i], mem_v[i], W["w_mo"][i], tm_mem, seqlen)
        x = ffn(x, W["norm_ffn"][i], W["w_ffn1"][i], W["w_ffn3"][i], W["w_ffn2"][i], min(FFN_ROWS, n))
    return x.reshape(nseq, seqlen, d), new


def kernel(x_prompt, x_sample, mem_prompt, cache_l0_k, cache_l0_v, cache_l0_kidx, state_l1_C, state_l1_n, state_l1_m, state_l2_S, state_l2_conv, cache_l3_k, cache_l3_v, cache_l3_kidx, cache_mem_k, cache_mem_v, rel_bias, norm_mix, norm_mem, norm_ffn, mem_norm, w_mq, w_mk, w_mv, w_mo, mq_gain, mk_gain, w_ffn1, w_ffn3, w_ffn2, a0_w_in, a0_w_out, a0_q_gain, a0_k_gain, a0_kidx_gain, b1_w_in, b1_gate_bias, b1_h_gain, b1_w_out, c2_w_in, c2_conv_w, c2_a_log, c2_dt_bias, c2_o_gain, c2_w_out, a3_w_in, a3_w_out, a3_q_gain, a3_k_gain, a3_kidx_gain):
    B, T, D = x_prompt.shape
    S, Ts, _ = x_sample.shape
    depth = w_mq.shape[0]
    mlen = mem_prompt.shape[1]
    mw = MEM_HEADS * MEM_HEAD_DIM

    def dsa_w(w_in, w_out, qg, kg, kig):
        return dict(w_in=dsa_weight(w_in), w_out=_bf(w_out), q_gain=qg, k_gain=kg, ki_gain=kig)

    W = dict(
        norm_mix=norm_mix, norm_mem=norm_mem, norm_ffn=norm_ffn,
        w_mq=_bf(w_mq), w_mo=_bf(w_mo), mq_gain=mq_gain,
        w_ffn1=_bf(w_ffn1), w_ffn3=_bf(w_ffn3), w_ffn2=_bf(w_ffn2),
        mixer={
            0: dsa_w(a0_w_in, a0_w_out, a0_q_gain, a0_k_gain, a0_kidx_gain),
            1: dict(w_in=mlstm_weight(b1_w_in), gate_bias=b1_gate_bias, h_gain=b1_h_gain, w_out=_bf(b1_w_out)),
            2: dict(w_in=gdn_weight(c2_w_in), conv_w=c2_conv_w, a_log=c2_a_log, dt_bias=c2_dt_bias,
                    o_gain=c2_o_gain, w_out=_bf(c2_w_out)),
            3: dsa_w(a3_w_in, a3_w_out, a3_q_gain, a3_k_gain, a3_kidx_gain),
        },
    )
    bias = bias_tiles(rel_bias)

    mk_p, mv_p = mem_kv(mem_prompt.reshape(B * mlen, D), mem_norm, w_mk, w_mv, mk_gain)
    mk_p = mk_p.reshape(depth, B, mlen, mw)
    mv_p = mv_p.reshape(depth, B, mlen, mw)
    st_p = {
        0: None,
        1: (jnp.zeros((B, B_HEADS, B_QK_DIM, B_V_DIM), F32), jnp.zeros((B, B_HEADS, B_QK_DIM), F32),
            jnp.full((B, B_HEADS), NEG, F32)),
        2: (jnp.zeros((B, C_HEADS, C_DK, C_DV), F32), jnp.zeros((B, CONV_W - 1, state_l2_conv.shape[-1]), F32)),
        3: None,
    }
    y_p, np_ = _trunk(x_prompt, B, T, mk_p, mv_p, st_p, W, bias, True)

    st_s = {
        0: (cache_l0_k, cache_l0_v, cache_l0_kidx),
        1: (state_l1_C, state_l1_n, state_l1_m),
        2: (state_l2_S, state_l2_conv),
        3: (cache_l3_k, cache_l3_v, cache_l3_kidx),
    }
    mk_s = cache_mem_k.reshape(depth, S, mlen, mw)
    mv_s = cache_mem_v.reshape(depth, S, mlen, mw)
    y_s, ns_ = _trunk(x_sample, S, Ts, mk_s, mv_s, st_s, W, bias, False)

    shp = (depth, B, mlen, MEM_HEADS, MEM_HEAD_DIM)
    return (y_p, y_s,
            *np_[0], *np_[1], *np_[2], *np_[3], mk_p.reshape(shp), mv_p.reshape(shp),
            *ns_[0], *ns_[1], *ns_[2], *ns_[3])
```

```python
import functools
import math

import jax
import jax.numpy as jnp
from jax import lax
from jax.experimental import pallas as pl
from jax.experimental.pallas import tpu as pltpu

F32 = jnp.float32
BF16 = jnp.bfloat16
I32 = jnp.int32

EPS = 1e-6
NEG = -1e30
CHUNK = 64
CHUNK_SHIFT = 6
LANES = 128
TILE = 128
KTILE = 256
COUNT_CHAINS = 8
SCORE_ROWS = 256
PROJ_ROWS = 512
FFN_ROWS = 512
MEM_ROWS = 512
SCAN_SEQS = 2
CACHE_BLOCK = 2048
MASK_CASE_ROWS = 256
VMEM_LIMIT = 56 * 1024 * 1024

A_HEADS, A_HEAD_DIM = 16, 64
LOG2E = math.log2(math.e)
QK_SCALE = (A_HEAD_DIM ** -0.5) * LOG2E
IDX_HEADS, IDX_DIM = 8, 64
TOPK_MAX = 256
N_BUCKETS, MAX_DISTANCE = 32, 128
B_HEADS, B_QK_DIM, B_V_DIM = 8, 64, 128
C_HEADS, C_DK, C_DV = 8, 128, 128
CONV_W = 4
MEM_HEADS, MEM_HEAD_DIM = 4, 128


def _bf(x):
    return x.astype(BF16)


def _bf01(mask):
    return mask.astype(F32).astype(BF16)


def _dot(a, b):
    return jnp.dot(a, b, preferred_element_type=F32)


def _dot_nt(a, b):
    return lax.dot_general(a, b, (((1,), (1,)), ((), ())), preferred_element_type=F32)


def _dot_tn(a, b):
    return lax.dot_general(a, b, (((0,), (0,)), ((), ())), preferred_element_type=F32)


def _split3(x):
    hi = _bf(x)
    r1 = x - hi.astype(F32)
    mid = _bf(r1)
    lo = _bf(r1 - mid.astype(F32))
    return hi, mid, lo


def _dot_exact_rhs01(x, m01):
    hi, mid, lo = _split3(x)
    return _dot(hi, m01) + _dot(mid, m01) + _dot(lo, m01)


def _dot_exact_lhs01(m01, x):
    hi, mid, lo = _split3(x)
    return _dot(m01, hi) + _dot(m01, mid) + _dot(m01, lo)


def _dot_f32(a, b):
    ah, am, al = _split3(a)
    bh, bm, bl = _split3(b)
    return (_dot(ah, bh) + (_dot(ah, bm) + _dot(am, bh))
            + (_dot(am, bm) + _dot(ah, bl) + _dot(al, bh)))


def _rms(x, g):
    ms = jnp.mean(x * x, axis=-1, keepdims=True)
    return x * lax.rsqrt(ms + EPS) * g


def _sigmoid(x):
    return 1.0 / (1.0 + jnp.exp(-x))


def _silu(x):
    return x * _sigmoid(x)


def _softplus(x):
    return jnp.maximum(x, 0.0) + jnp.log1p(jnp.exp(-jnp.abs(x)))


def _params(*sem):
    return pltpu.CompilerParams(dimension_semantics=sem, vmem_limit_bytes=VMEM_LIMIT)


def _full(shape):
    n = len(shape)
    return pl.BlockSpec(shape, lambda *_: (0,) * n)


def _proj_kernel(x_ref, g_ref, w_ref, o_ref, tail_ref, *, col_chunk):
    h = _bf(_rms(x_ref[...], g_ref[...]))
    m = w_ref.shape[1]
    for c in range(0, m - LANES, col_chunk):
        e = min(c + col_chunk, m - LANES)
        o_ref[:, c:e] = _dot(h, w_ref[:, c:e])
    tail = _dot(h, w_ref[:, m - LANES:m])
    o_ref[:, m - LANES:m] = tail
    tail_ref[...] = tail


def proj(x, g, w, tm):
    n, d = x.shape
    m = w.shape[1]
    return pl.pallas_call(
        functools.partial(_proj_kernel, col_chunk=512),
        grid=(n // tm,),
        in_specs=[pl.BlockSpec((tm, d), lambda i: (i, 0)), _full((1, d)), _full((d, m))],
        out_specs=[pl.BlockSpec((tm, m), lambda i: (i, 0)), pl.BlockSpec((tm, LANES), lambda i: (i, 0))],
        out_shape=[jax.ShapeDtypeStruct((n, m), F32), jax.ShapeDtypeStruct((n, LANES), F32)],
        compiler_params=_params("parallel"),
        name="proj",
    )(x, g.reshape(1, d), w)


def _memkv_kernel(x_ref, g_ref, wk_ref, wv_ref, kg_ref, k_ref, v_ref):
    h = _bf(_rms(x_ref[...], g_ref[0]))
    k = _dot(h, wk_ref[0])
    v_ref[0] = _dot(h, wv_ref[0])
    kg = kg_ref[0]
    for hd in range(MEM_HEADS):
        sl = slice(hd * MEM_HEAD_DIM, (hd + 1) * MEM_HEAD_DIM)
        k_ref[0, :, sl] = _rms(k[:, sl], kg)


def mem_kv(mem2d, mem_norm, w_mk, w_mv, mk_gain, tm=256):
    n, d = mem2d.shape
    depth = w_mk.shape[0]
    mw = w_mk.shape[2]
    return pl.pallas_call(
        _memkv_kernel,
        grid=(depth, n // tm),
        in_specs=[pl.BlockSpec((tm, d), lambda l, i: (i, 0)),
                  pl.BlockSpec((1, 1, d), lambda l, i: (l, 0, 0)),
                  pl.BlockSpec((1, d, mw), lambda l, i: (l, 0, 0)),
                  pl.BlockSpec((1, d, mw), lambda l, i: (l, 0, 0)),
                  pl.BlockSpec((1, 1, MEM_HEAD_DIM), lambda l, i: (l, 0, 0))],
        out_specs=[pl.BlockSpec((1, tm, mw), lambda l, i: (l, i, 0)),
                   pl.BlockSpec((1, tm, mw), lambda l, i: (l, i, 0))],
        out_shape=[jax.ShapeDtypeStruct((depth, n, mw), F32)] * 2,
        compiler_params=_params("parallel", "parallel"),
        name="mem_kv",
    )(mem2d, mem_norm.reshape(depth, 1, d), _bf(w_mk), _bf(w_mv), mk_gain.reshape(depth, 1, MEM_HEAD_DIM))


def _memattn_kernel(x_ref, o_ref, wo_ref, g_ref, wq_ref, qg_ref, mk_ref, mv_ref, wmo_ref, y_ref):
    x1 = x_ref[...] + _dot(o_ref[...], wo_ref[...])
    h = _bf(_rms(x1, g_ref[...]))
    q = _dot(h, wq_ref[...])
    qg = qg_ref[...]
    scale = MEM_HEAD_DIM ** -0.5
    nsub = mk_ref.shape[0]
    rows = x1.shape[0] // nsub
    cells = [(hd, s) for hd in range(MEM_HEADS) for s in range(nsub)]
    hsl = lambda hd: slice(hd * MEM_HEAD_DIM, (hd + 1) * MEM_HEAD_DIM)
    qh = [_bf(_rms(q[:, hsl(hd)], qg)) for hd in range(MEM_HEADS)]
    logits = [_dot_nt(qh[hd][s * rows:(s + 1) * rows], _bf(mk_ref[s, :, hsl(hd)])) * scale for hd, s in cells]
    ps = [jnp.exp(l - jnp.max(l, axis=-1, keepdims=True)) for l in logits]
    ps = [_bf(p / jnp.sum(p, axis=-1, keepdims=True)) for p in ps]
    pv = [_bf(_dot(p, _bf(mv_ref[s, :, hsl(hd)]))) for p, (hd, s) in zip(ps, cells)]
    outs = []
    for hd in range(MEM_HEADS):
        subs = pv[hd * nsub:(hd + 1) * nsub]
        outs.append(subs[0] if nsub == 1 else jnp.concatenate(subs, axis=0))
    att = jnp.concatenate(outs, axis=-1)
    y_ref[...] = x1 + _dot(att, wmo_ref[...])


def mem_attend(x, o, w_out, g, w_mq, mq_gain, mk, mv, w_mo, tm, seqlen):
    n, d = x.shape
    mlen, mw = mk.shape[1], mk.shape[2]
    tiles_per_seq = max(1, seqlen // tm)
    seqs_per_tile = max(1, tm // seqlen)
    return pl.pallas_call(
        _memattn_kernel,
        grid=(n // tm,),
        in_specs=[pl.BlockSpec((tm, d), lambda i: (i, 0)),
                  pl.BlockSpec((tm, o.shape[1]), lambda i: (i, 0)),
                  _full(w_out.shape), _full((1, d)), _full(w_mq.shape), _full((1, MEM_HEAD_DIM)),
                  pl.BlockSpec((seqs_per_tile, mlen, mw), lambda i: (i // tiles_per_seq, 0, 0)),
                  pl.BlockSpec((seqs_per_tile, mlen, mw), lambda i: (i // tiles_per_seq, 0, 0)),
                  _full(w_mo.shape)],
        out_specs=pl.BlockSpec((tm, d), lambda i: (i, 0)),
        out_shape=jax.ShapeDtypeStruct((n, d), F32),
        compiler_params=_params("parallel"),
        name="mem_attend",
    )(x, o, w_out, g.reshape(1, d), w_mq, mq_gain.reshape(1, MEM_HEAD_DIM), mk, mv, w_mo)


def _ffn_kernel(x_ref, g_ref, w1_ref, w3_ref, w2_ref, y_ref, *, hid_chunk):
    x = x_ref[...]
    h = _bf(_rms(x, g_ref[...]))
    hidden = w1_ref.shape[1]
    y_ref[...] = x
    for c in range(0, hidden, hid_chunk):
        a = _dot(h, w1_ref[:, c:c + hid_chunk])
        b = _dot(h, w3_ref[:, c:c + hid_chunk])
        y_ref[...] += _dot(_bf(_silu(a) * b), w2_ref[c:c + hid_chunk, :])


def ffn(x, g, w1, w3, w2, tm):
    n, d = x.shape
    hidden = w1.shape[1]
    return pl.pallas_call(
        functools.partial(_ffn_kernel, hid_chunk=256),
        grid=(n // tm,),
        in_specs=[pl.BlockSpec((tm, d), lambda i: (i, 0)), _full((1, d)),
                  _full((d, hidden)), _full((d, hidden)), _full((hidden, d))],
        out_specs=pl.BlockSpec((tm, d), lambda i: (i, 0)),
        out_shape=jax.ShapeDtypeStruct((n, d), F32),
        compiler_params=_params("parallel"),
        name="ffn",
    )(x, g.reshape(1, d), w1, w3, w2)


def _proj_dsa_kernel(x_ref, g_ref, w_ref, qg_ref, kg_ref, kig_ref, bd_ref,
                     q_o, k_o, kb_o, v_o, vb_o, qi_o, kiwi_o, *maybe_kit_o, tstate):
    h = _bf(_rms(x_ref[...], g_ref[...]))
    tm = x_ref.shape[0]
    hq = A_HEADS * A_HEAD_DIM
    bd = bd_ref[...]
    inv_hd = 1.0 / A_HEAD_DIM
    nw = bd.shape[0]

    def head_norm(p, gain):
        hi, mid = _split2(p * p)
        ss = _dot(hi, bd) + _dot(mid, bd)
        return p * lax.rsqrt(ss * inv_hd + EPS) * gain

    def transposed(x):
        return jnp.concatenate([x[r:r + LANES, :].T for r in range(0, tm, LANES)], axis=1)

    def store_state(o_ref, x, col0):
        if not tstate:
            o_ref[:, col0:col0 + x.shape[1]] = x
            return
        for j in range(0, x.shape[1], LANES):
            xt = transposed(x[:, j:j + LANES])
            hd = (col0 + j) // A_HEAD_DIM
            for half in range(LANES // A_HEAD_DIM):
                o_ref[0, hd + half] = xt[half * A_HEAD_DIM:(half + 1) * A_HEAD_DIM]

    step = 512
    for c in range(0, hq, step):
        pq = _dot(h, w_ref[:, c:c + step])
        pk = _dot(h, w_ref[:, hq + c:hq + c + step])
        pv = _dot(h, w_ref[:, 2 * hq + c:2 * hq + c + step])
        for j in range(0, step, nw):
            sl = slice(c + j, c + j + nw)
            qn = head_norm(pq[:, j:j + nw], qg_ref[:, sl])
            q_o[:, sl] = _bf(qn * QK_SCALE)
            kn = head_norm(pk[:, j:j + nw], kg_ref[:, sl])
            store_state(k_o, kn, c + j)
            kb_o[:, sl] = _bf(kn)
        store_state(v_o, pv, c)
        vb_o[:, c:c + step] = _bf(pv)
    qiw = IDX_HEADS * LANES
    for c in range(0, qiw, step):
        qi_o[:, c:c + step] = _bf(_dot(h, w_ref[:, 3 * hq + c:3 * hq + c + step]))
    p = _dot(h, w_ref[:, 3 * hq + qiw:3 * hq + qiw + LANES])
    lane = lax.broadcasted_iota(I32, p.shape, 1)
    is_ki = lane < IDX_DIM
    ss = jnp.sum(jnp.where(is_ki, p * p, 0.0), axis=-1, keepdims=True)
    kin = p * lax.rsqrt(ss * (1.0 / IDX_DIM) + EPS) * kig_ref[...]
    kiwi = jnp.where(is_ki, kin, p)
    kiwi_o[...] = kiwi
    if tstate:
        maybe_kit_o[0][0] = transposed(kiwi)[0:IDX_DIM]


def proj_dsa(x, g, wa, q_gain, k_gain, ki_gain, tm, nseq, tstate):
    n, d = x.shape
    hq = A_HEADS * A_HEAD_DIM
    m = wa.shape[1]
    seqlen = n // nseq
    tiles = seqlen // tm
    qg = jnp.tile(q_gain, A_HEADS).reshape(1, hq)
    kg = jnp.tile(k_gain, A_HEADS).reshape(1, hq)
    kig = jnp.concatenate([ki_gain, jnp.ones((LANES - IDX_DIM,), F32)]).reshape(1, LANES)
    r = jnp.arange(2 * LANES)
    bd = _bf((r[:, None] // A_HEAD_DIM) == (r[None, :] // A_HEAD_DIM))
    row = lambda w: pl.BlockSpec((tm, w), lambda i: (i, 0))
    if tstate:
        state = pl.BlockSpec((1, A_HEADS, A_HEAD_DIM, tm), lambda i: (i // tiles, 0, 0, i % tiles))
        state_shape = jax.ShapeDtypeStruct((nseq, A_HEADS, A_HEAD_DIM, seqlen), F32)
        extra_specs = [pl.BlockSpec((1, IDX_DIM, tm), lambda i: (i // tiles, 0, i % tiles))]
        extra_shapes = [jax.ShapeDtypeStruct((nseq, IDX_DIM, seqlen), F32)]
    else:
        state, state_shape, extra_specs, extra_shapes = row(hq), jax.ShapeDtypeStruct((n, hq), F32), [], []
    return pl.pallas_call(
        functools.partial(_proj_dsa_kernel, tstate=tstate),
        grid=(n // tm,),
        in_specs=[row(d), _full((1, d)), _full((d, m)), _full((1, hq)), _full((1, hq)),
                  _full((1, LANES)), _full(bd.shape)],
        out_specs=[row(hq), state, row(hq), state, row(hq), row(IDX_HEADS * LANES), row(LANES)] + extra_specs,
        out_shape=[jax.ShapeDtypeStruct((n, hq), BF16), state_shape,
                   jax.ShapeDtypeStruct((n, hq), BF16), state_shape,
                   jax.ShapeDtypeStruct((n, hq), BF16),
                   jax.ShapeDtypeStruct((n, IDX_HEADS * LANES), BF16),
                   jax.ShapeDtypeStruct((n, LANES), F32)] + extra_shapes,
        compiler_params=_params("parallel"),
        name="proj_dsa",
    )(x, g.reshape(1, d), wa, qg, kg, kig, bd)


def dsa_weight(w_in):
    hq = A_HEADS * A_HEAD_DIM
    o3 = 3 * hq
    o4 = o3 + IDX_HEADS * IDX_DIM
    d = w_in.shape[0]
    wqi = w_in[:, o3:o4].reshape(d, IDX_HEADS, IDX_DIM)
    wqi = jnp.pad(wqi, ((0, 0), (0, 0), (0, LANES - IDX_DIM))).reshape(d, IDX_HEADS * LANES)
    tail = jnp.pad(w_in[:, o4:], ((0, 0), (0, LANES - (w_in.shape[1] - o4))))
    return _bf(jnp.concatenate([w_in[:, :o3], wqi, tail], axis=1))


def _sortable(s):
    b = pltpu.bitcast(s, I32)
    b = jnp.where(b == jnp.int32(-2 ** 31), 0, b)
    return jnp.where(b < 0, b ^ jnp.int32(0x7FFFFFFF), b)


def _index_mask_kernel(qi_ref, wit_ref, *rest, nref, groups, ltot, topk, causal, case_rows):
    ki_refs = rest[:nref]
    o_ref = rest[nref]
    key_ref, sel_ref, jv_ref = rest[nref + 1:]
    tq = qi_ref.shape[0]
    i = pl.program_id(1)
    lpad = sel_ref.shape[0]
    wit = wit_ref[0]
    if case_rows:
        ncase = (jnp.maximum((i + 1) * tq, topk) + case_rows - 1) // case_rows
        used_rows = ncase * case_rows
    else:
        used_rows = None

    def score_group(dst0, rows, sources):
        accs = []
        for rp, lead, src0, width, _ in sources:
            ref = ki_refs[rp]
            ki = _bf(ref[src0:src0 + rows, :] if lead is None else ref[lead, src0:src0 + rows, :])
            acc = jnp.zeros((rows, tq), F32)
            for hd in range(IDX_HEADS):
                rel = _dot_nt(ki, qi_ref[:, hd * LANES:hd * LANES + width])
                acc = acc + wit[hd:hd + 1, :] * jnp.maximum(rel, 0.0)
            accs.append(acc)
        if len(accs) == 1:
            acc = accs[0]
        else:
            lane = lax.broadcasted_iota(I32, (rows, tq), 1)
            acc = jnp.where(lane < tq // 2, accs[0], accs[1])
        s = acc * ((IDX_DIM ** -0.5) * (IDX_HEADS ** -0.5))
        if causal:
            kpos = dst0 + lax.broadcasted_iota(I32, (rows, tq), 0)
            qpos = i * tq + lax.broadcasted_iota(I32, (rows, tq), 1)
            s = jnp.where((kpos >> CHUNK_SHIFT) <= (qpos >> CHUNK_SHIFT), s, NEG)
        key_ref[dst0:dst0 + rows, :] = _sortable(s)

    for dst0, rows, sources in groups:
        if used_rows is None:
            score_group(dst0, rows, sources)
        else:
            pl.when(dst0 < used_rows)(functools.partial(score_group, dst0, rows, sources))

    def select(nrows):
        idx_bits = max(1, (nrows - 1).bit_length())

        def count(pred):
            c = pred.astype(I32).reshape(COUNT_CHAINS, nrows // COUNT_CHAINS, tq)
            return jnp.sum(jnp.sum(c, axis=1), axis=0, keepdims=True)

        def ge_count(cand):
            return count(key_ref[0:nrows, :] >= cand)

        t0 = jnp.full((1, tq), -2 ** 31, I32)
        t = jnp.where(ge_count(jnp.zeros((1, tq), I32)) >= topk, 0, t0)

        def vbody(it, t):
            cand = t + (jnp.int32(1) << (30 - it))
            return jnp.where(ge_count(cand) >= topk, cand, t)

        t = lax.fori_loop(0, 31, vbody, t)
        keys = key_ref[0:nrows, :]
        gt = keys > t
        eq = keys == t
        need = topk - count(gt)
        rowi = lax.broadcasted_iota(I32, (nrows, tq), 0)

        def jbody(it, jv):
            cand = jv + (jnp.int32(1) << (idx_bits - 1 - it))
            below = count(eq & (rowi < cand))
            return jnp.where(below < need, cand, jv)

        jv_ref[...] = jnp.full((1, tq), nrows, I32)
        has_tie = jnp.max(count(eq) - need) > 0

        @pl.when(has_tie)
        def _():
            jv_ref[...] = lax.fori_loop(0, idx_bits, jbody, jnp.zeros((1, tq), I32))

        sel = gt | (eq & (rowi <= jv_ref[...]))
        if causal:
            qpos = i * tq + lax.broadcasted_iota(I32, (nrows, tq), 1)
            sel = sel & ((rowi >> CHUNK_SHIFT) <= (qpos >> CHUNK_SHIFT))
        sel_ref[0:nrows, :] = jnp.where(sel, 0.0, NEG)
        nreal = -(-nrows // TILE)
        if nreal * TILE > nrows:
            sel_ref[nrows:nreal * TILE, :] = jnp.full((nreal * TILE - nrows, tq), NEG, F32)
        for kt in range(lpad // TILE):
            if kt < nreal:
                o_ref[0, 0, kt] = sel_ref[kt * TILE:(kt + 1) * TILE, :].T
            else:
                o_ref[0, 0, kt] = jnp.full((tq, TILE), NEG, F32)

    if case_rows:
        for k in range(ltot // case_rows):
            pl.when(ncase == k + 1)(functools.partial(select, (k + 1) * case_rows))
    else:
        select(ltot)


def index_mask(qi, wit, ki_arrays, ki_specs, groups, ltot, nstep, nqb, causal):
    lpad = -(-ltot // KTILE) * KTILE
    nt = lpad // TILE
    topk = min(TOPK_MAX, ltot // 4)
    case_rows = MASK_CASE_ROWS if (causal and ltot % MASK_CASE_ROWS == 0 and ltot > MASK_CASE_ROWS) else 0
    kern = functools.partial(_index_mask_kernel, nref=len(ki_arrays), groups=tuple(groups), ltot=ltot,
                             topk=topk, causal=causal, case_rows=case_rows)
    return pl.pallas_call(
        kern,
        grid=(nstep, nqb),
        in_specs=[pl.BlockSpec((TILE, qi.shape[1]), lambda b, i: (b * nqb + i, 0)),
                  pl.BlockSpec((1, IDX_HEADS, TILE), lambda b, i: (b * nqb + i, 0, 0))] + list(ki_specs),
        out_specs=pl.BlockSpec((1, 1, nt, TILE, TILE), lambda b, i: (b, i, 0, 0, 0)),
        out_shape=jax.ShapeDtypeStruct((nstep, nqb, nt, TILE, TILE), F32),
        scratch_shapes=[pltpu.VMEM((ltot, TILE), I32), pltpu.VMEM((lpad, TILE), F32),
                        pltpu.VMEM((1, TILE), I32)],
        compiler_params=_params("parallel", "parallel"),
        name="index_mask",
    )(qi, wit, *ki_arrays)


def _row_groups(row0, rows, make_sources):
    out = []
    for off in range(0, rows, SCORE_ROWS):
        out.append((row0 + off, min(SCORE_ROWS, rows - off), tuple(make_sources(off))))
    return out


def _attn_core(q_ref, o_ref, qs_ref, l_ref, mx_ref, acc_ref, groups, tq):
    npairs = A_HEADS // 2
    lane = lax.broadcasted_iota(I32, (1, LANES), 1)
    keep_lo = _bf((lane < A_HEAD_DIM).astype(F32))
    keep_hi = _bf((lane >= A_HEAD_DIM).astype(F32))
    for pair in range(npairs):
        qp = q_ref[:, pair * LANES:(pair + 1) * LANES]
        qs_ref[pair, 0:tq, :] = qp * keep_lo
        qs_ref[pair, tq:2 * tq, :] = qp * keep_hi
    mx_ref[...] = jnp.full(mx_ref.shape, NEG, F32)
    acc_ref[...] = jnp.zeros(acc_ref.shape, F32)
    ones = jnp.ones((KTILE, LANES), BF16)

    def over_tiles(count, body):
        if isinstance(count, int) and count == 1:
            body(0, 0)
        else:
            lax.fori_loop(0, count, body, 0)

    for count, base, k_tile, _, mask_sub, bias_sub in groups:
        def p1(kt, carry, base=base, k_tile=k_tile, mask_sub=mask_sub, bias_sub=bias_sub):
            masks = [mask_sub(kt, sub) for sub in range(KTILE // LANES)]
            for pair in range(npairs):
                sl = slice(pair * LANES, (pair + 1) * LANES)
                l = _dot_nt(qs_ref[pair], k_tile(kt, sl))
                for half in range(2):
                    rs = slice(half * tq, (half + 1) * tq)
                    mx = mx_ref[pair, rs, :]
                    for sub in range(KTILE // LANES):
                        cs = slice(sub * LANES, (sub + 1) * LANES)
                        blk = l[rs, cs] + (masks[sub] + bias_sub(kt, sub, 2 * pair + half))
                        l_ref[pair, base + kt, rs, cs] = blk
                        mx = jnp.maximum(mx, blk)
                    mx_ref[pair, rs, :] = mx
            return carry

        over_tiles(count, p1)

    for pair in range(npairs):
        m = jnp.max(mx_ref[pair], axis=-1, keepdims=True)
        mx_ref[pair] = jnp.broadcast_to(m, mx_ref.shape[1:])

    for count, base, _, v_tile, _, _ in groups:
        def p2(kt, carry, base=base, v_tile=v_tile):
            for pair in range(npairs):
                sl = slice(pair * LANES, (pair + 1) * LANES)
                m = mx_ref[pair]
                p = jnp.concatenate(
                    [jnp.exp2(l_ref[pair, base + kt, :, sub * LANES:(sub + 1) * LANES] - m)
                     for sub in range(KTILE // LANES)], axis=1)
                vaug = jnp.concatenate([v_tile(kt, sl), ones], axis=1)
                acc_ref[pair] += _dot(_bf(p), vaug)
            return carry

        over_tiles(count, p2)

    lane_full = lax.broadcasted_iota(I32, (tq, LANES), 1)
    for pair in range(npairs):
        a = acc_ref[pair]
        o = a[:, 0:LANES] / a[:, LANES:2 * LANES]
        o_ref[:, pair * LANES:(pair + 1) * LANES] = _bf(jnp.where(lane_full < A_HEAD_DIM, o[0:tq], o[tq:2 * tq]))


def _attn_scratch(tq, ntiles):
    npairs = A_HEADS // 2
    return [pltpu.VMEM((npairs, 2 * tq, LANES), BF16),
            pltpu.VMEM((npairs, ntiles, 2 * tq, KTILE), F32),
            pltpu.VMEM((npairs, 2 * tq, LANES), F32),
            pltpu.VMEM((npairs, 2 * tq, 2 * LANES), F32)]


def _attn_prompt_kernel(q_ref, k_ref, v_ref, mask_ref, bias_ref, o_ref, qs_ref, l_ref, mx_ref, acc_ref):
    i = pl.program_id(1)
    per = KTILE // TILE

    def rows(kt):
        return pl.ds(pl.multiple_of(kt * KTILE, KTILE), KTILE)

    def bias_sub(kt, sub, hd):
        s = kt * per + sub
        sel = jnp.where(s == i, 2, jnp.where(s == i - 1, 1, 0))
        return bias_ref[sel, hd]

    group = (i // per + 1, 0,
             lambda kt, sl: k_ref[rows(kt), sl],
             lambda kt, sl: v_ref[rows(kt), sl],
             lambda kt, sub: mask_ref[0, 0, kt * per + sub],
             bias_sub)
    _attn_core(q_ref, o_ref, qs_ref, l_ref, mx_ref, acc_ref, [group], TILE)


def attn_prompt(q, kb, vb, mask, bias, nseq, seqlen):
    n, hq = q.shape
    nqb = seqlen // TILE
    nt = mask.shape[2]
    return pl.pallas_call(
        _attn_prompt_kernel,
        grid=(nseq, nqb),
        in_specs=[pl.BlockSpec((TILE, hq), lambda b, i: (b * nqb + i, 0)),
                  pl.BlockSpec((seqlen, hq), lambda b, i: (b, 0)),
                  pl.BlockSpec((seqlen, hq), lambda b, i: (b, 0)),
                  pl.BlockSpec((1, 1, nt, TILE, TILE), lambda b, i: (b, i, 0, 0, 0)),
                  _full(bias.shape)],
        out_specs=pl.BlockSpec((TILE, hq), lambda b, i: (b * nqb + i, 0)),
        out_shape=jax.ShapeDtypeStruct((n, hq), BF16),
        scratch_shapes=_attn_scratch(TILE, seqlen // KTILE),
        compiler_params=_params("parallel", "arbitrary"),
        name="attn_prompt",
    )(q, kb, vb, mask, bias)


def _attn_sample_kernel(q_ref, kc_ref, vc_ref, kn_ref, vn_ref, mask_ref, bias_ref, o_ref,
                        l_ref, mx_ref, acc_ref, *, nblk):
    ph = pl.program_id(1)
    b = pl.program_id(2)
    tq = q_ref.shape[0]
    per = KTILE // TILE
    tiles_per_blk = kc_ref.shape[3] // KTILE
    ncache = nblk * tiles_per_blk
    last_sub = ncache * per - 1
    heads = range(A_HEADS)

    def head_tile(ref, c0, hd):
        return _bf(ref[0, hd, :, c0:c0 + KTILE])

    def pass1(slot, ref, c0, mask_sub, bias_sub):
        ls = [_dot(q_ref[:, hd * A_HEAD_DIM:(hd + 1) * A_HEAD_DIM], head_tile(ref, c0, hd))
              for hd in heads]
        masks = [mask_sub(sub) for sub in range(per)]
        for hd in heads:
            mx = mx_ref[hd]
            for sub in range(per):
                cs = slice(sub * LANES, (sub + 1) * LANES)
                blk = ls[hd][:, cs] + (masks[sub] + bias_sub(sub, hd))
                l_ref[hd, slot, :, cs] = blk
                mx = jnp.maximum(mx, blk)
            mx_ref[hd] = mx

    def pass2(slot, ref, c0):
        ones = jnp.ones((LANES - A_HEAD_DIM, KTILE), BF16)
        for hd in heads:
            m = mx_ref[hd]
            p = jnp.concatenate([jnp.exp2(l_ref[hd, slot, :, sub * LANES:(sub + 1) * LANES] - m)
                                 for sub in range(per)], axis=1)
            vaug = jnp.concatenate([head_tile(ref, c0, hd), ones], axis=0)
            acc_ref[hd] += _dot_nt(_bf(p), vaug)

    @pl.when((ph == 0) & (b == 0))
    def _():
        mx_ref[...] = jnp.full(mx_ref.shape, NEG, F32)
        acc_ref[...] = jnp.zeros(acc_ref.shape, F32)

    @pl.when(ph == 0)
    def _():
        for t in range(tiles_per_blk):
            g = b * tiles_per_blk + t
            pass1(g, kc_ref, t * KTILE,
                  lambda sub, g=g: mask_ref[0, 0, g * per + sub],
                  lambda sub, hd, g=g: bias_ref[jnp.where(g * per + sub == last_sub, 1, 0), hd])

    @pl.when((ph == 0) & (b == nblk - 1))
    def _():
        pass1(ncache, kn_ref, 0,
              lambda sub: mask_ref[0, 0, ncache * per + sub],
              lambda sub, hd: bias_ref[2 if sub == 0 else 0, hd])
        for hd in heads:
            m = jnp.max(mx_ref[hd], axis=-1, keepdims=True)
            mx_ref[hd] = jnp.broadcast_to(m, mx_ref.shape[1:])

    @pl.when(ph == 1)
    def _():
        for t in range(tiles_per_blk):
            pass2(b * tiles_per_blk + t, vc_ref, t * KTILE)

    @pl.when((ph == 1) & (b == nblk - 1))
    def _():
        pass2(ncache, vn_ref, 0)
        for pair in range(A_HEADS // 2):
            halves = []
            for hd in (2 * pair, 2 * pair + 1):
                a = acc_ref[hd]
                halves.append(a[:, 0:A_HEAD_DIM] / a[:, A_HEAD_DIM:2 * A_HEAD_DIM])
            o_ref[:, pair * LANES:(pair + 1) * LANES] = _bf(jnp.concatenate(halves, axis=1))


def attn_sample(q, kc, vc, kn, vn, mask, bias, nseq, tq):
    n, hq = q.shape
    past = kc.shape[3]
    cblk = min(CACHE_BLOCK, past)
    nblk = past // cblk
    ncache = past // KTILE
    nt = mask.shape[2]
    per = TILE // tq
    cache_k = pl.BlockSpec((1, A_HEADS, A_HEAD_DIM, cblk),
                           lambda s, ph, b: (s, 0, 0, b * (1 - ph) + (nblk - 1) * ph))
    cache_v = pl.BlockSpec((1, A_HEADS, A_HEAD_DIM, cblk), lambda s, ph, b: (s, 0, 0, b * ph))
    fresh = pl.BlockSpec((1, A_HEADS, A_HEAD_DIM, KTILE), lambda s, ph, b: (s, 0, 0, 0))
    return pl.pallas_call(
        functools.partial(_attn_sample_kernel, nblk=nblk),
        grid=(nseq, 2, nblk),
        in_specs=[pl.BlockSpec((tq, hq), lambda s, ph, b: (s, 0)),
                  cache_k, cache_v, fresh, fresh,
                  pl.BlockSpec((1, 1, nt, tq, TILE), lambda s, ph, b: (s // per, 0, 0, s % per, 0)),
                  pl.BlockSpec(bias.shape, lambda s, ph, b: (0, 0, 0, 0))],
        out_specs=pl.BlockSpec((tq, hq), lambda s, ph, b: (s, 0)),
        out_shape=jax.ShapeDtypeStruct((n, hq), BF16),
        scratch_shapes=[pltpu.VMEM((A_HEADS, ncache + 1, tq, KTILE), F32),
                        pltpu.VMEM((A_HEADS, tq, LANES), F32),
                        pltpu.VMEM((A_HEADS, tq, LANES), F32)],
        compiler_params=_params("parallel", "arbitrary", "arbitrary"),
        name="attn_sample",
    )(q, kc, vc, kn, vn, mask, bias)


def _t5_bucket(rel):
    half = N_BUCKETS // 2
    max_exact = half // 2
    n = jnp.abs(rel)
    nf = jnp.maximum(n, 1).astype(F32)
    large = max_exact + (jnp.log(nf / max_exact) / math.log(MAX_DISTANCE / max_exact)
                         * (half - max_exact)).astype(I32)
    large = jnp.minimum(large, half - 1)
    return jnp.where(rel > 0, half, 0) + jnp.where(n < max_exact, n, large)


def bias_tiles(rel_bias):
    heads = rel_bias.shape[1]
    span = 2 * TILE - 1

    def toeplitz(shift):
        rel = jnp.arange(span, dtype=I32) - (TILE - 1) + shift
        tab = rel_bias.astype(F32)[_t5_bucket(rel)].T
        strip = jnp.tile(jnp.pad(tab, ((0, 0), (0, 1))), (1, TILE))[:, :TILE * span]
        return strip.reshape(heads, TILE, span)[:, :, TILE - 1:]

    far = rel_bias.astype(F32)[_t5_bucket(jnp.full((1,), -(TILE + 1), I32))]
    far = jnp.broadcast_to(far.T[:, :, None], (heads, TILE, TILE))
    return jnp.stack([far, toeplitz(-TILE), toeplitz(0)]) * LOG2E


def dsa_core_prompt(pr, nseq, seqlen, bias):
    q, _, kb, _, vb, qi, kiwi = pr
    nqb = seqlen // TILE
    wit = kiwi[:, IDX_DIM:IDX_DIM + IDX_HEADS].reshape(nseq * nqb, TILE, IDX_HEADS).swapaxes(1, 2)
    groups = _row_groups(0, seqlen, lambda off: [(0, None, off, LANES, None)])
    mask = index_mask(qi, wit, [kiwi], [pl.BlockSpec((seqlen, LANES), lambda b, i: (b, 0))],
                      groups, seqlen, nseq, nqb, causal=True)
    return attn_prompt(q, kb, vb, mask, bias, nseq, seqlen)


def dsa_core_sample(pr, k_cache, v_cache, ki_cache, nseq, tq, bias):
    q, kf, _, vf, _, qi, kiwi = pr
    past = k_cache.shape[1]
    per = TILE // tq
    wit = kiwi[:, IDX_DIM:IDX_DIM + IDX_HEADS].reshape(nseq // per, TILE, IDX_HEADS).swapaxes(1, 2)
    groups = (_row_groups(0, past, lambda off: [(0, j, off, IDX_DIM, j) for j in range(per)])
              + _row_groups(past, tq, lambda off: [(1, None, j * tq + off, LANES, j) for j in range(per)]))
    mask = index_mask(
        qi, wit, [ki_cache, kiwi],
        [pl.BlockSpec((per, past, IDX_DIM), lambda b, i: (b, 0, 0)),
         pl.BlockSpec((TILE, LANES), lambda b, i: (b, 0))],
        groups, past + tq, nseq // per, 1, causal=False)
    to_hdk = lambda a: a.transpose(0, 2, 3, 1)
    pad_keys = ((0, 0), (0, 0), (0, 0), (0, KTILE - tq))
    kn = jnp.pad(to_hdk(kf.reshape(nseq, tq, A_HEADS, A_HEAD_DIM)), pad_keys)
    vn = jnp.pad(to_hdk(vf.reshape(nseq, tq, A_HEADS, A_HEAD_DIM)), pad_keys)
    return attn_sample(q, to_hdk(k_cache), to_hdk(v_cache), kn, vn, mask, bias[:, :, :tq, :], nseq, tq)


def _pair_triangles():
    r = jnp.arange(2 * CHUNK)[:, None]
    c = jnp.arange(2 * CHUNK)[None, :]
    same = (r // CHUNK) == (c // CHUNK)
    return _bf(same & (c <= r)), _bf(same & (r <= c))


def _log_sigmoid(x):
    return jnp.minimum(x, 0.0) - jnp.log1p(jnp.exp(-jnp.abs(x)))


def _mlstm_kernel(p_ref, gs_ref, gr_ref, gbc_ref, gbr_ref, hg_ref, tl_ref, tu_ref, c0_ref, n0_ref, m0_ref,
                  hs_ref, c_ref, n_ref, m_ref, cs, ns, ms):
    c_id = pl.program_id(1)
    nh = B_HEADS
    npair = nh // 2
    hw = LANES
    nsq = p_ref.shape[0]
    L = p_ref.shape[1]
    L2 = 2 * L

    @pl.when(c_id == 0)
    def _():
        cs[...] = jnp.zeros(cs.shape, F32)
        ns[...] = jnp.zeros(ns.shape, F32)
        for sq in range(nsq):
            ns[sq, :, 0:B_QK_DIM] = n0_ref[sq]
            for hd in range(nh):
                cs[sq, hd // 2, (hd % 2) * hw:(hd % 2) * hw + B_QK_DIM, :] = c0_ref[sq, hd]
        ms[...] = m0_ref[...]

    r = lax.broadcasted_iota(I32, (L2, L2), 0)
    c = lax.broadcasted_iota(I32, (L2, L2), 1)
    same = (r >> CHUNK_SHIFT) == (c >> CHUNK_SHIFT)
    incl = same & (c <= r)
    tri_l = tl_ref[...]
    tri_u = tu_ref[...]
    rowc = lax.broadcasted_iota(I32, (L2, 1), 0)
    is_top = rowc < L
    top = is_top.astype(F32)
    bot = 1.0 - top
    row2 = lax.broadcasted_iota(I32, (2 * hw, 1), 0)
    hg = hg_ref[...]

    def stack(base, sq, pr):
        a = base + 2 * pr * hw
        return jnp.concatenate([p_ref[sq, :, a:a + hw], p_ref[sq, :, a + hw:a + 2 * hw]], axis=0)

    def bd(x):
        return _bf(jnp.concatenate([x * top, x * bot], axis=1))

    seqs = range(nsq)
    gs = [gs_ref[sq, 0] + gbc_ref[...] for sq in seqs]
    gr = [gr_ref[sq, 0] + gbr_ref[...] for sq in seqs]
    bcol = [_dot_exact_lhs01(tri_l, _log_sigmoid(gs[sq][:, npair:])) for sq in seqs]
    brow = [_dot_exact_rhs01(_log_sigmoid(gr[sq][npair:, :]), tri_u) for sq in seqs]
    mall = [ms[sq] for sq in seqs]
    cells = [(sq, pr) for sq in seqs for pr in range(npair)]
    ids = range(len(cells))
    qf = [stack(0, sq, pr) for sq, pr in cells]
    kf = [stack(nh * hw, sq, pr) * (B_QK_DIM ** -0.5) for sq, pr in cells]
    vb = [_bf(stack(2 * nh * hw, sq, pr)) for sq, pr in cells]
    bc = [bcol[sq][:, pr:pr + 1] for sq, pr in cells]
    ic = [gs[sq][:, pr:pr + 1] for sq, pr in cells]
    mc = [mall[sq][:, pr:pr + 1] for sq, pr in cells]
    d = [jnp.where(incl, bc[i] - brow[sq][pr:pr + 1, :] + gr[sq][pr:pr + 1, :], NEG)
         for i, (sq, pr) in enumerate(cells)]
    inter = [bc[i] + mc[i] for i in ids]
    mt = [jnp.maximum(inter[i], jnp.max(d[i], axis=-1, keepdims=True)) for i in ids]
    s = [_dot_nt(_bf(qf[i]), _bf(kf[i])) * jnp.exp(d[i] - mt[i]) for i in ids]
    wst = [jnp.exp(inter[i] - mt[i]) for i in ids]
    cmat = [cs[sq, pr] for sq, pr in cells]
    num = [_dot(_bf(s[i]), vb[i]) + wst[i] * _dot(bd(qf[i]), _bf(cmat[i])) for i in ids]
    for i, (sq, pr) in enumerate(cells):
        n0 = ns[sq, 2 * pr:2 * pr + 1, :]
        n1 = ns[sq, 2 * pr + 1:2 * pr + 2, :]
        qn = jnp.sum(qf[i] * jnp.where(is_top, n0, n1), axis=-1, keepdims=True)
        den = jnp.sum(s[i], axis=-1, keepdims=True) + wst[i] * qn
        hs = num[i] / jnp.maximum(jnp.abs(den), jnp.exp(-mt[i]))
        bl0 = bc[i][L - 1:L, :]
        bl1 = bc[i][L2 - 1:L2, :]
        bl = jnp.where(is_top, bl0, bl1)
        dec = bl - bc[i] + ic[i]
        blm = bl + mc[i]
        mnew0 = jnp.maximum(blm[0:1, :], jnp.max(dec[0:L], axis=0, keepdims=True))
        mnew1 = jnp.maximum(blm[L:L + 1, :], jnp.max(dec[L:L2], axis=0, keepdims=True))
        mnew = jnp.where(is_top, mnew0, mnew1)
        wk = jnp.exp(dec - mnew)
        ws = jnp.exp(blm - mnew)
        kw = kf[i] * wk
        ws2 = jnp.where(row2 < hw, ws[0:1, :], ws[L:L + 1, :])
        cs[sq, pr] = ws2 * cmat[i] + _dot_tn(bd(kw), vb[i])
        ns[sq, 2 * pr:2 * pr + 1, :] = ws[0:1, :] * n0 + jnp.sum(kw[0:L], axis=0, keepdims=True)
        ns[sq, 2 * pr + 1:2 * pr + 2, :] = ws[L:L + 1, :] * n1 + jnp.sum(kw[L:L2], axis=0, keepdims=True)
        ms[sq, :, pr:pr + 1] = mnew
        on = _bf(_rms(hs, hg) * _sigmoid(stack(3 * nh * hw, sq, pr)))
        oa = 2 * pr * B_V_DIM
        hs_ref[sq, :, oa:oa + B_V_DIM] = on[0:L]
        hs_ref[sq, :, oa + B_V_DIM:oa + 2 * B_V_DIM] = on[L:L2]

    @pl.when(c_id == pl.num_programs(1) - 1)
    def _():
        for sq in range(nsq):
            for hd in range(nh):
                c_ref[sq, hd] = cs[sq, hd // 2, (hd % 2) * hw:(hd % 2) * hw + B_QK_DIM, :]
            n_ref[sq] = ns[sq, :, 0:B_QK_DIM]
        m_ref[...] = ms[...]


def mlstm_weight(w_in):
    d = w_in.shape[0]
    nq = B_HEADS * B_QK_DIM
    nv = B_HEADS * B_V_DIM

    def padh(w):
        w = w.reshape(d, B_HEADS, B_QK_DIM)
        return jnp.pad(w, ((0, 0), (0, 0), (0, LANES - B_QK_DIM))).reshape(d, B_HEADS * LANES)

    main = 2 * nq + 2 * nv
    tail = jnp.pad(w_in[:, main:], ((0, 0), (0, LANES - 2 * B_HEADS)))
    return _bf(jnp.concatenate([padh(w_in[:, :nq]), padh(w_in[:, nq:2 * nq]), w_in[:, 2 * nq:main], tail], axis=1))


def mlstm_core(p, gates, gate_bias, h_gain, c0, n0, m0, nseq, nchunk):
    n, m = p.shape
    nh = B_HEADS
    npair = nh // 2
    gcol = 4 * nh * LANES
    nsq = SCAN_SEQS if nseq % SCAN_SEQS == 0 else 1
    seqlen = n // nseq
    pre = gates[:, :2 * nh].reshape(nseq, nchunk, CHUNK, 2, npair, 2)
    gs = pre.transpose(0, 1, 5, 2, 3, 4).reshape(nseq, nchunk, 2 * CHUNK, 2 * npair)
    gr = pre.transpose(0, 1, 3, 4, 5, 2).reshape(nseq, nchunk, 2 * npair, 2 * CHUNK)
    gb = gate_bias.reshape(2, npair, 2)
    gbc = jnp.repeat(gb.transpose(2, 0, 1).reshape(2, 2 * npair), CHUNK, axis=0)
    m0s = jnp.repeat(m0.reshape(nseq, npair, 2).transpose(0, 2, 1), CHUNK, axis=1)
    o, c_new, n_new, m_new = pl.pallas_call(
        _mlstm_kernel,
        grid=(nseq // nsq, nchunk),
        in_specs=[pl.BlockSpec((nsq, CHUNK, m), lambda s, c: (s, c, 0)),
                  pl.BlockSpec((nsq, 1, 2 * CHUNK, 2 * npair), lambda s, c: (s, c, 0, 0)),
                  pl.BlockSpec((nsq, 1, 2 * npair, 2 * CHUNK), lambda s, c: (s, c, 0, 0)),
                  _full((2 * CHUNK, 2 * npair)), _full((2 * npair, 2 * CHUNK)), _full((1, B_V_DIM)),
                  _full((2 * CHUNK, 2 * CHUNK)), _full((2 * CHUNK, 2 * CHUNK)),
                  pl.BlockSpec((nsq, nh, B_QK_DIM, B_V_DIM), lambda s, c: (s, 0, 0, 0)),
                  pl.BlockSpec((nsq, nh, B_QK_DIM), lambda s, c: (s, 0, 0)),
                  pl.BlockSpec((nsq, 2 * CHUNK, npair), lambda s, c: (s, 0, 0))],
        out_specs=[pl.BlockSpec((nsq, CHUNK, nh * B_V_DIM), lambda s, c: (s, c, 0)),
                   pl.BlockSpec((nsq, nh, B_QK_DIM, B_V_DIM), lambda s, c: (s, 0, 0, 0)),
                   pl.BlockSpec((nsq, nh, B_QK_DIM), lambda s, c: (s, 0, 0)),
                   pl.BlockSpec((nsq, 2 * CHUNK, npair), lambda s, c: (s, 0, 0))],
        out_shape=[jax.ShapeDtypeStruct((nseq, seqlen, nh * B_V_DIM), BF16),
                   jax.ShapeDtypeStruct((nseq, nh, B_QK_DIM, B_V_DIM), F32),
                   jax.ShapeDtypeStruct((nseq, nh, B_QK_DIM), F32),
                   jax.ShapeDtypeStruct((nseq, 2 * CHUNK, npair), F32)],
        scratch_shapes=[pltpu.VMEM((nsq, npair, 2 * LANES, B_V_DIM), F32), pltpu.VMEM((nsq, nh, LANES), F32),
                        pltpu.VMEM((nsq, 2 * CHUNK, npair), F32)],
        compiler_params=_params("parallel", "arbitrary"),
        name="mlstm",
    )(p.reshape(nseq, seqlen, m), gs, gr, gbc, gbc.T, h_gain.reshape(1, B_V_DIM), *_pair_triangles(),
      c0, n0, m0s)
    m_heads = m_new[:, ::CHUNK, :].transpose(0, 2, 1).reshape(nseq, nh)
    return o.reshape(n, nh * B_V_DIM), c_new, n_new, m_heads


def _split2(x):
    hi = _bf(x)
    return hi, _bf(x - hi.astype(F32))


def _cat3_lhs(x):
    hi, mid = _split2(x)
    return jnp.concatenate([hi, hi, mid], axis=1)


def _cat3_rhs(x):
    hi, mid = _split2(x)
    return jnp.concatenate([hi, mid, hi], axis=0)


def _gdn_kernel(p_ref, gs_ref, gr_ref, cw_ref, alc_ref, alr_ref, dtc_ref, dtr_ref, og_ref, tl_ref, tu_ref,
                s0_ref, cb0_ref, o_ref, s_ref, cb_ref, ss, tail):
    c_id = pl.program_id(1)
    nh = C_HEADS
    cdim = nh * (2 * C_DK + C_DV)
    L = p_ref.shape[0]
    L2 = 2 * L
    nprev = CONV_W - 1

    @pl.when(c_id == 0)
    def _():
        ss[...] = s0_ref[0]
        tail[...] = jnp.zeros(tail.shape, F32)
        tail[8 - nprev:8, :] = cb0_ref[0]

    x = p_ref[:, 0:cdim]
    ext = jnp.concatenate([tail[...], x], axis=0)
    conv = ext[8:8 + L] * cw_ref[CONV_W - 1:CONV_W, :]
    for j in range(CONV_W - 1):
        conv = conv + ext[8 - nprev + j:8 - nprev + j + L] * cw_ref[j:j + 1, :]
    tail[...] = x[L - 8:L, :]
    cf = _silu(conv)

    npair = nh // 2
    gs = gs_ref[0]
    gr = gr_ref[0]
    beta = _sigmoid(gs[:, 0:npair])
    g_col = -jnp.exp(alc_ref[...]) * _softplus(gs[:, npair:] + dtc_ref[...])
    g_row = -jnp.exp(alr_ref[...]) * _softplus(gr[npair:, :] + dtr_ref[...])
    r = lax.broadcasted_iota(I32, (L2, L2), 0)
    c = lax.broadcasted_iota(I32, (L2, L2), 1)
    same = (r >> CHUNK_SHIFT) == (c >> CHUNK_SHIFT)
    incl = same & (c <= r)
    strict = same & (c < r)
    eye = (c == r).astype(F32)
    gc_col = _dot_exact_lhs01(tl_ref[...], g_col)
    gc_row = _dot_exact_rhs01(g_row, tu_ref[...])
    rowc = lax.broadcasted_iota(I32, (L2, 1), 0)
    top = (rowc < L).astype(F32)
    bot = 1.0 - top
    row2 = lax.broadcasted_iota(I32, (2 * C_DK, 1), 0)
    og = og_ref[...]

    def stack(base, pr):
        a = base + 2 * pr * C_DK
        return jnp.concatenate([cf[:, a:a + C_DK], cf[:, a + C_DK:a + 2 * C_DK]], axis=0)

    def bd(x):
        return _bf(jnp.concatenate([x * top, x * bot], axis=1))

    pairs = range(npair)
    qc, kc, kcb, dm, amat, rhs, gcols, egcs = [], [], [], [], [], [], [], []
    for pr in pairs:
        qraw = stack(0, pr)
        kraw = stack(nh * C_DK, pr)
        vc = stack(2 * nh * C_DK, pr)
        qc.append(qraw * lax.rsqrt(jnp.sum(qraw * qraw, axis=-1, keepdims=True) + EPS) * (C_DK ** -0.5))
        kc.append(kraw * lax.rsqrt(jnp.sum(kraw * kraw, axis=-1, keepdims=True) + EPS))
        bc = beta[:, pr:pr + 1]
        gcol = gc_col[:, pr:pr + 1]
        grow = gc_row[pr:pr + 1, :]
        dm.append(jnp.where(incl, jnp.exp(jnp.where(incl, gcol - grow, 0.0)), 0.0))
        kb = kc[pr] * bc
        kcb.append(_bf(kc[pr]))
        amat.append(jnp.where(strict, _dot_nt(_bf(kb), kcb[pr]) * dm[pr], 0.0))
        egc = jnp.exp(gcol)
        rhs.append(jnp.concatenate([vc * bc, kb * egc], axis=-1))
        gcols.append(gcol)
        egcs.append(egc)
    tinv = [eye - amat[pr] for pr in pairs]
    pw_l = [_cat3_lhs(-amat[pr]) for pr in pairs]
    pw_r = [_cat3_rhs(-amat[pr]) for pr in pairs]
    for _ in range(CHUNK_SHIFT - 1):
        pw = [_dot(pw_l[pr], pw_r[pr]) for pr in pairs]
        pw_l = [_cat3_lhs(pw[pr]) for pr in pairs]
        pw_r = [_cat3_rhs(pw[pr]) for pr in pairs]
        tinv = [tinv[pr] + _dot(_cat3_lhs(tinv[pr]), pw_r[pr]) for pr in pairs]
    sol = [_dot(_cat3_lhs(tinv[pr]), _cat3_rhs(rhs[pr])) for pr in pairs]
    attn = [_dot_nt(_bf(qc[pr]), kcb[pr]) * dm[pr] for pr in pairs]
    smat = [ss[pr] for pr in pairs]
    sb = [_bf(smat[pr]) for pr in pairs]
    vnew = [sol[pr][:, :C_DV] - _dot(bd(sol[pr][:, C_DV:]), sb[pr]) for pr in pairs]
    o = [_dot(bd(qc[pr] * egcs[pr]), sb[pr]) + _dot(_bf(attn[pr]), _bf(vnew[pr])) for pr in pairs]
    for pr in pairs:
        gl0 = gcols[pr][L - 1:L, :]
        gl1 = gcols[pr][L2 - 1:L2, :]
        ke = kc[pr] * jnp.exp(jnp.where(rowc < L, gl0, gl1) - gcols[pr])
        decay = jnp.exp(jnp.where(row2 < C_DK, gl0, gl1))
        ss[pr] = smat[pr] * decay + _dot_tn(bd(ke), _bf(vnew[pr]))
    for pr in pairs:
        za = cdim + 2 * pr * C_DV
        z = jnp.concatenate([p_ref[:, za:za + C_DV], p_ref[:, za + C_DV:za + 2 * C_DV]], axis=0)
        on = _bf(_rms(o[pr], og) * _silu(z))
        oa = 2 * pr * C_DV
        o_ref[:, oa:oa + C_DV] = on[0:L]
        o_ref[:, oa + C_DV:oa + 2 * C_DV] = on[L:L2]

    @pl.when(c_id == pl.num_programs(1) - 1)
    def _():
        s_ref[0] = ss[...]
        cb_ref[0] = tail[8 - nprev:8, :]


def gdn_weight(w_in):
    main = C_HEADS * (2 * C_DK + C_DV) + C_HEADS * C_DV
    tail = jnp.pad(w_in[:, main:], ((0, 0), (0, LANES - 2 * C_HEADS)))
    return _bf(jnp.concatenate([w_in[:, :main], tail], axis=1))


def gdn_core(p, gates, conv_w, a_log, dt_bias, o_gain, s0, cb0, nseq, nchunk):
    n, m = p.shape
    nh = C_HEADS
    npair = nh // 2
    cdim = nh * (2 * C_DK + C_DV)
    gcolumn = cdim + nh * C_DV
    pre = gates[:, :2 * nh].reshape(nseq * nchunk, CHUNK, 2, npair, 2)
    gs = pre.transpose(0, 4, 1, 2, 3).reshape(nseq * nchunk, 2 * CHUNK, 2 * npair)
    gr = pre.transpose(0, 2, 3, 4, 1).reshape(nseq * nchunk, 2 * npair, 2 * CHUNK)

    def col(v):
        return jnp.repeat(v.reshape(npair, 2).T, CHUNK, axis=0)

    o, s_new, cb_new = pl.pallas_call(
        _gdn_kernel,
        grid=(nseq, nchunk),
        in_specs=[pl.BlockSpec((CHUNK, m), lambda s, c: (s * nchunk + c, 0)),
                  pl.BlockSpec((1, 2 * CHUNK, 2 * npair), lambda s, c: (s * nchunk + c, 0, 0)),
                  pl.BlockSpec((1, 2 * npair, 2 * CHUNK), lambda s, c: (s * nchunk + c, 0, 0)),
                  _full((CONV_W, cdim)),
                  _full((2 * CHUNK, npair)), _full((npair, 2 * CHUNK)),
                  _full((2 * CHUNK, npair)), _full((npair, 2 * CHUNK)),
                  _full((1, C_DV)), _full((2 * CHUNK, 2 * CHUNK)), _full((2 * CHUNK, 2 * CHUNK)),
                  pl.BlockSpec((1, npair, 2 * C_DK, C_DV), lambda s, c: (s, 0, 0, 0)),
                  pl.BlockSpec((1, CONV_W - 1, cdim), lambda s, c: (s, 0, 0))],
        out_specs=[pl.BlockSpec((CHUNK, nh * C_DV), lambda s, c: (s * nchunk + c, 0)),
                   pl.BlockSpec((1, npair, 2 * C_DK, C_DV), lambda s, c: (s, 0, 0, 0)),
                   pl.BlockSpec((1, CONV_W - 1, cdim), lambda s, c: (s, 0, 0))],
        out_shape=[jax.ShapeDtypeStruct((n, nh * C_DV), BF16),
                   jax.ShapeDtypeStruct((nseq, npair, 2 * C_DK, C_DV), F32),
                   jax.ShapeDtypeStruct((nseq, CONV_W - 1, cdim), F32)],
        scratch_shapes=[pltpu.VMEM((npair, 2 * C_DK, C_DV), F32), pltpu.VMEM((8, cdim), F32)],
        compiler_params=_params("parallel", "arbitrary"),
        name="gdn",
    )(p, gs, gr, conv_w, col(a_log), col(a_log).T, col(dt_bias), col(dt_bias).T, o_gain.reshape(1, C_DV),
      *_pair_triangles(),
      s0.reshape(nseq, npair, 2 * C_DK, C_DV), cb0)
    return o, s_new.reshape(nseq, nh, C_DK, C_DV), cb_new


def _trunk(x, nseq, seqlen, mem_k, mem_v, st, W, bias, is_prompt):
    d = x.shape[-1]
    n = nseq * seqlen
    x = x.reshape(n, d)
    tm = min(PROJ_ROWS, n)
    tm_mem = min(MEM_ROWS, n)
    new = {}
    for i in range(4):
        kind = i % 3
        mx = W["mixer"][i]
        if kind == 0:
            tstate = is_prompt and seqlen % tm == 0
            pr = proj_dsa(x, W["norm_mix"][i], mx["w_in"], mx["q_gain"], mx["k_gain"], mx["ki_gain"], tm,
                          nseq, tstate)
            if is_prompt:
                o = dsa_core_prompt(pr[:7], nseq, seqlen, bias)
            else:
                o = dsa_core_sample(pr, *st[i], nseq, seqlen, bias)
            if tstate:
                new[i] = (pr[1].transpose(0, 3, 1, 2), pr[3].transpose(0, 3, 1, 2), pr[7].transpose(0, 2, 1))
            else:
                new[i] = (pr[1].reshape(nseq, seqlen, A_HEADS, A_HEAD_DIM),
                          pr[3].reshape(nseq, seqlen, A_HEADS, A_HEAD_DIM),
                          pr[6][:, :IDX_DIM].reshape(nseq, seqlen, IDX_DIM))
        elif kind == 1:
            p, gates = proj(x, W["norm_mix"][i], mx["w_in"], tm)
            o, c_new, n_new, m_new = mlstm_core(p, gates, mx["gate_bias"], mx["h_gain"], *st[i],
                                                nseq, seqlen // CHUNK)
            new[i] = (c_new, n_new, m_new.reshape(nseq, B_HEADS))
        else:
            p, gates = proj(x, W["norm_mix"][i], mx["w_in"], tm)
            o, s_new, cb_new = gdn_core(p, gates, mx["conv_w"], mx["a_log"], mx["dt_bias"], mx["o_gain"], *st[i],
                                        nseq, seqlen // CHUNK)
            new[i] = (s_new, cb_new)
        x = mem_attend(x, o, mx["w_out"], W["norm_mem"][i], W["w_mq"][i], W["mq_gain"][i],
                       mem_k[i], mem_v[i], W["w_mo"][i], tm_mem, seqlen)
        x = ffn(x, W["norm_ffn"][i], W["w_ffn1"][i], W["w_ffn3"][i], W["w_ffn2"][i], min(FFN_ROWS, n))
    return x.reshape(nseq, seqlen, d), new


def kernel(x_prompt, x_sample, mem_prompt, cache_l0_k, cache_l0_v, cache_l0_kidx, state_l1_C, state_l1_n, state_l1_m, state_l2_S, state_l2_conv, cache_l3_k, cache_l3_v, cache_l3_kidx, cache_mem_k, cache_mem_v, rel_bias, norm_mix, norm_mem, norm_ffn, mem_norm, w_mq, w_mk, w_mv, w_mo, mq_gain, mk_gain, w_ffn1, w_ffn3, w_ffn2, a0_w_in, a0_w_out, a0_q_gain, a0_k_gain, a0_kidx_gain, b1_w_in, b1_gate_bias, b1_h_gain, b1_w_out, c2_w_in, c2_conv_w, c2_a_log, c2_dt_bias, c2_o_gain, c2_w_out, a3_w_in, a3_w_out, a3_q_gain, a3_k_gain, a3_kidx_gain):
    B, T, D = x_prompt.shape
    S, Ts, _ = x_sample.shape
    depth = w_mq.shape[0]
    mlen = mem_prompt.shape[1]
    mw = MEM_HEADS * MEM_HEAD_DIM

    def dsa_w(w_in, w_out, qg, kg, kig):
        return dict(w_in=dsa_weight(w_in), w_out=_bf(w_out), q_gain=qg, k_gain=kg, ki_gain=kig)

    W = dict(
        norm_mix=norm_mix, norm_mem=norm_mem, norm_ffn=norm_ffn,
        w_mq=_bf(w_mq), w_mo=_bf(w_mo), mq_gain=mq_gain,
        w_ffn1=_bf(w_ffn1), w_ffn3=_bf(w_ffn3), w_ffn2=_bf(w_ffn2),
        mixer={
            0: dsa_w(a0_w_in, a0_w_out, a0_q_gain, a0_k_gain, a0_kidx_gain),
            1: dict(w_in=mlstm_weight(b1_w_in), gate_bias=b1_gate_bias, h_gain=b1_h_gain, w_out=_bf(b1_w_out)),
            2: dict(w_in=gdn_weight(c2_w_in), conv_w=c2_conv_w, a_log=c2_a_log, dt_bias=c2_dt_bias,
                    o_gain=c2_o_gain, w_out=_bf(c2_w_out)),
            3: dsa_w(a3_w_in, a3_w_out, a3_q_gain, a3_k_gain, a3_kidx_gain),
        },
    )
    bias = bias_tiles(rel_bias)

    mk_p, mv_p = mem_kv(mem_prompt.reshape(B * mlen, D), mem_norm, w_mk, w_mv, mk_gain)
    mk_p = mk_p.reshape(depth, B, mlen, mw)
    mv_p = mv_p.reshape(depth, B, mlen, mw)
    st_p = {
        0: None,
        1: (jnp.zeros((B, B_HEADS, B_QK_DIM, B_V_DIM), F32), jnp.zeros((B, B_HEADS, B_QK_DIM), F32),
            jnp.full((B, B_HEADS), NEG, F32)),
        2: (jnp.zeros((B, C_HEADS, C_DK, C_DV), F32), jnp.zeros((B, CONV_W - 1, state_l2_conv.shape[-1]), F32)),
        3: None,
    }
    y_p, np_ = _trunk(x_prompt, B, T, mk_p, mv_p, st_p, W, bias, True)

    st_s = {
        0: (cache_l0_k, cache_l0_v, cache_l0_kidx),
        1: (state_l1_C, state_l1_n, state_l1_m),
        2: (state_l2_S, state_l2_conv),
        3: (cache_l3_k, cache_l3_v, cache_l3_kidx),
    }
    mk_s = cache_mem_k.reshape(depth, S, mlen, mw)
    mv_s = cache_mem_v.reshape(depth, S, mlen, mw)
    y_s, ns_ = _trunk(x_sample, S, Ts, mk_s, mv_s, st_s, W, bias, False)

    shp = (depth, B, mlen, MEM_HEADS, MEM_HEAD_DIM)
    return (y_p, y_s,
            *np_[0], *np_[1], *np_[2], *np_[3], mk_p.reshape(shp), mv_p.reshape(shp),
            *ns_[0], *ns_[1], *ns_[2], *ns_[3])
```

```python
import functools
import math

import jax
import jax.numpy as jnp
from jax import lax
from jax.experimental import pallas as pl
from jax.experimental.pallas import tpu as pltpu

F32 = jnp.float32
BF16 = jnp.bfloat16
I32 = jnp.int32

EPS = 1e-6
NEG = -1e30
CHUNK = 64
CHUNK_SHIFT = 6
LANES = 128
TILE = 128
KTILE = 256
COUNT_CHAINS = 8
SCORE_ROWS = 256
PROJ_ROWS = 512
FFN_ROWS = 512
MEM_ROWS = 512
SCAN_SEQS = 2
CACHE_BLOCK = 1024
MASK_CASE_ROWS = 256
VMEM_LIMIT = 56 * 1024 * 1024

A_HEADS, A_HEAD_DIM = 16, 64
LOG2E = math.log2(math.e)
QK_SCALE = (A_HEAD_DIM ** -0.5) * LOG2E
IDX_HEADS, IDX_DIM = 8, 64
TOPK_MAX = 256
N_BUCKETS, MAX_DISTANCE = 32, 128
B_HEADS, B_QK_DIM, B_V_DIM = 8, 64, 128
C_HEADS, C_DK, C_DV = 8, 128, 128
CONV_W = 4
MEM_HEADS, MEM_HEAD_DIM = 4, 128


def _bf(x):
    return x.astype(BF16)


def _bf01(mask):
    return mask.astype(F32).astype(BF16)


def _dot(a, b):
    return jnp.dot(a, b, preferred_element_type=F32)


def _dot_nt(a, b):
    return lax.dot_general(a, b, (((1,), (1,)), ((), ())), preferred_element_type=F32)


def _dot_tn(a, b):
    return lax.dot_general(a, b, (((0,), (0,)), ((), ())), preferred_element_type=F32)


def _split3(x):
    hi = _bf(x)
    r1 = x - hi.astype(F32)
    mid = _bf(r1)
    lo = _bf(r1 - mid.astype(F32))
    return hi, mid, lo


def _dot_exact_rhs01(x, m01):
    hi, mid, lo = _split3(x)
    return _dot(hi, m01) + _dot(mid, m01) + _dot(lo, m01)


def _dot_exact_lhs01(m01, x):
    hi, mid, lo = _split3(x)
    return _dot(m01, hi) + _dot(m01, mid) + _dot(m01, lo)


def _dot_f32(a, b):
    ah, am, al = _split3(a)
    bh, bm, bl = _split3(b)
    return (_dot(ah, bh) + (_dot(ah, bm) + _dot(am, bh))
            + (_dot(am, bm) + _dot(ah, bl) + _dot(al, bh)))


def _rms(x, g):
    ms = jnp.mean(x * x, axis=-1, keepdims=True)
    return x * lax.rsqrt(ms + EPS) * g


def _sigmoid(x):
    return 1.0 / (1.0 + jnp.exp(-x))


def _silu(x):
    return x * _sigmoid(x)


def _softplus(x):
    return jnp.maximum(x, 0.0) + jnp.log1p(jnp.exp(-jnp.abs(x)))


def _params(*sem):
    return pltpu.CompilerParams(dimension_semantics=sem, vmem_limit_bytes=VMEM_LIMIT)


def _full(shape):
    n = len(shape)
    return pl.BlockSpec(shape, lambda *_: (0,) * n)


def _proj_kernel(x_ref, g_ref, w_ref, o_ref, tail_ref, *, col_chunk):
    h = _bf(_rms(x_ref[...], g_ref[...]))
    m = w_ref.shape[1]
    for c in range(0, m - LANES, col_chunk):
        e = min(c + col_chunk, m - LANES)
        o_ref[:, c:e] = _dot(h, w_ref[:, c:e])
    tail = _dot(h, w_ref[:, m - LANES:m])
    o_ref[:, m - LANES:m] = tail
    tail_ref[...] = tail


def proj(x, g, w, tm):
    n, d = x.shape
    m = w.shape[1]
    return pl.pallas_call(
        functools.partial(_proj_kernel, col_chunk=512),
        grid=(n // tm,),
        in_specs=[pl.BlockSpec((tm, d), lambda i: (i, 0)), _full((1, d)), _full((d, m))],
        out_specs=[pl.BlockSpec((tm, m), lambda i: (i, 0)), pl.BlockSpec((tm, LANES), lambda i: (i, 0))],
        out_shape=[jax.ShapeDtypeStruct((n, m), F32), jax.ShapeDtypeStruct((n, LANES), F32)],
        compiler_params=_params("parallel"),
        name="proj",
    )(x, g.reshape(1, d), w)


def _memkv_kernel(x_ref, g_ref, wk_ref, wv_ref, kg_ref, k_ref, v_ref):
    h = _bf(_rms(x_ref[...], g_ref[0]))
    k = _dot(h, wk_ref[0])
    v_ref[0] = _dot(h, wv_ref[0])
    kg = kg_ref[0]
    for hd in range(MEM_HEADS):
        sl = slice(hd * MEM_HEAD_DIM, (hd + 1) * MEM_HEAD_DIM)
        k_ref[0, :, sl] = _rms(k[:, sl], kg)


def mem_kv(mem2d, mem_norm, w_mk, w_mv, mk_gain, tm=256):
    n, d = mem2d.shape
    depth = w_mk.shape[0]
    mw = w_mk.shape[2]
    return pl.pallas_call(
        _memkv_kernel,
        grid=(depth, n // tm),
        in_specs=[pl.BlockSpec((tm, d), lambda l, i: (i, 0)),
                  pl.BlockSpec((1, 1, d), lambda l, i: (l, 0, 0)),
                  pl.BlockSpec((1, d, mw), lambda l, i: (l, 0, 0)),
                  pl.BlockSpec((1, d, mw), lambda l, i: (l, 0, 0)),
                  pl.BlockSpec((1, 1, MEM_HEAD_DIM), lambda l, i: (l, 0, 0))],
        out_specs=[pl.BlockSpec((1, tm, mw), lambda l, i: (l, i, 0)),
                   pl.BlockSpec((1, tm, mw), lambda l, i: (l, i, 0))],
        out_shape=[jax.ShapeDtypeStruct((depth, n, mw), F32)] * 2,
        compiler_params=_params("parallel", "parallel"),
        name="mem_kv",
    )(mem2d, mem_norm.reshape(depth, 1, d), _bf(w_mk), _bf(w_mv), mk_gain.reshape(depth, 1, MEM_HEAD_DIM))


def _memattn_kernel(x_ref, o_ref, wo_ref, g_ref, wq_ref, qg_ref, mk_ref, mv_ref, wmo_ref, y_ref):
    x1 = x_ref[...] + _dot(o_ref[...], wo_ref[...])
    h = _bf(_rms(x1, g_ref[...]))
    q = _dot(h, wq_ref[...])
    qg = qg_ref[...]
    scale = MEM_HEAD_DIM ** -0.5
    nsub = mk_ref.shape[0]
    rows = x1.shape[0] // nsub
    cells = [(hd, s) for hd in range(MEM_HEADS) for s in range(nsub)]
    hsl = lambda hd: slice(hd * MEM_HEAD_DIM, (hd + 1) * MEM_HEAD_DIM)
    qh = [_bf(_rms(q[:, hsl(hd)], qg)) for hd in range(MEM_HEADS)]
    logits = [_dot_nt(qh[hd][s * rows:(s + 1) * rows], _bf(mk_ref[s, :, hsl(hd)])) * scale for hd, s in cells]
    ps = [jnp.exp(l - jnp.max(l, axis=-1, keepdims=True)) for l in logits]
    ps = [_bf(p / jnp.sum(p, axis=-1, keepdims=True)) for p in ps]
    pv = [_bf(_dot(p, _bf(mv_ref[s, :, hsl(hd)]))) for p, (hd, s) in zip(ps, cells)]
    outs = []
    for hd in range(MEM_HEADS):
        subs = pv[hd * nsub:(hd + 1) * nsub]
        outs.append(subs[0] if nsub == 1 else jnp.concatenate(subs, axis=0))
    att = jnp.concatenate(outs, axis=-1)
    y_ref[...] = x1 + _dot(att, wmo_ref[...])


def mem_attend(x, o, w_out, g, w_mq, mq_gain, mk, mv, w_mo, tm, seqlen):
    n, d = x.shape
    mlen, mw = mk.shape[1], mk.shape[2]
    tiles_per_seq = max(1, seqlen // tm)
    seqs_per_tile = max(1, tm // seqlen)
    return pl.pallas_call(
        _memattn_kernel,
        grid=(n // tm,),
        in_specs=[pl.BlockSpec((tm, d), lambda i: (i, 0)),
                  pl.BlockSpec((tm, o.shape[1]), lambda i: (i, 0)),
                  _full(w_out.shape), _full((1, d)), _full(w_mq.shape), _full((1, MEM_HEAD_DIM)),
                  pl.BlockSpec((seqs_per_tile, mlen, mw), lambda i: (i // tiles_per_seq, 0, 0)),
                  pl.BlockSpec((seqs_per_tile, mlen, mw), lambda i: (i // tiles_per_seq, 0, 0)),
                  _full(w_mo.shape)],
        out_specs=pl.BlockSpec((tm, d), lambda i: (i, 0)),
        out_shape=jax.ShapeDtypeStruct((n, d), F32),
        compiler_params=_params("parallel"),
        name="mem_attend",
    )(x, o, w_out, g.reshape(1, d), w_mq, mq_gain.reshape(1, MEM_HEAD_DIM), mk, mv, w_mo)


def _ffn_kernel(x_ref, g_ref, w1_ref, w3_ref, w2_ref, y_ref, *, hid_chunk):
    x = x_ref[...]
    h = _bf(_rms(x, g_ref[...]))
    hidden = w1_ref.shape[1]
    y_ref[...] = x
    for c in range(0, hidden, hid_chunk):
        a = _dot(h, w1_ref[:, c:c + hid_chunk])
        b = _dot(h, w3_ref[:, c:c + hid_chunk])
        y_ref[...] += _dot(_bf(_silu(a) * b), w2_ref[c:c + hid_chunk, :])


def ffn(x, g, w1, w3, w2, tm):
    n, d = x.shape
    hidden = w1.shape[1]
    return pl.pallas_call(
        functools.partial(_ffn_kernel, hid_chunk=256),
        grid=(n // tm,),
        in_specs=[pl.BlockSpec((tm, d), lambda i: (i, 0)), _full((1, d)),
                  _full((d, hidden)), _full((d, hidden)), _full((hidden, d))],
        out_specs=pl.BlockSpec((tm, d), lambda i: (i, 0)),
        out_shape=jax.ShapeDtypeStruct((n, d), F32),
        compiler_params=_params("parallel"),
        name="ffn",
    )(x, g.reshape(1, d), w1, w3, w2)


def _proj_dsa_kernel(x_ref, g_ref, w_ref, qg_ref, kg_ref, kig_ref, bd_ref,
                     q_o, k_o, kb_o, v_o, vb_o, qi_o, kiwi_o, *maybe_kit_o, tstate):
    h = _bf(_rms(x_ref[...], g_ref[...]))
    tm = x_ref.shape[0]
    hq = A_HEADS * A_HEAD_DIM
    bd = bd_ref[...]
    inv_hd = 1.0 / A_HEAD_DIM
    nw = bd.shape[0]

    def head_norm(p, gain):
        hi, mid = _split2(p * p)
        ss = _dot(hi, bd) + _dot(mid, bd)
        return p * lax.rsqrt(ss * inv_hd + EPS) * gain

    def transposed(x):
        return jnp.concatenate([x[r:r + LANES, :].T for r in range(0, tm, LANES)], axis=1)

    def store_state(o_ref, x, col0):
        if not tstate:
            o_ref[:, col0:col0 + x.shape[1]] = x
            return
        for j in range(0, x.shape[1], LANES):
            xt = transposed(x[:, j:j + LANES])
            hd = (col0 + j) // A_HEAD_DIM
            for half in range(LANES // A_HEAD_DIM):
                o_ref[0, hd + half] = xt[half * A_HEAD_DIM:(half + 1) * A_HEAD_DIM]

    step = 512
    for c in range(0, hq, step):
        pq = _dot(h, w_ref[:, c:c + step])
        pk = _dot(h, w_ref[:, hq + c:hq + c + step])
        pv = _dot(h, w_ref[:, 2 * hq + c:2 * hq + c + step])
        for j in range(0, step, nw):
            sl = slice(c + j, c + j + nw)
            qn = head_norm(pq[:, j:j + nw], qg_ref[:, sl])
            q_o[:, sl] = _bf(qn * QK_SCALE)
            kn = head_norm(pk[:, j:j + nw], kg_ref[:, sl])
            store_state(k_o, kn, c + j)
            kb_o[:, sl] = _bf(kn)
        store_state(v_o, pv, c)
        vb_o[:, c:c + step] = _bf(pv)
    qiw = IDX_HEADS * LANES
    for c in range(0, qiw, step):
        qi_o[:, c:c + step] = _bf(_dot(h, w_ref[:, 3 * hq + c:3 * hq + c + step]))
    p = _dot(h, w_ref[:, 3 * hq + qiw:3 * hq + qiw + LANES])
    lane = lax.broadcasted_iota(I32, p.shape, 1)
    is_ki = lane < IDX_DIM
    ss = jnp.sum(jnp.where(is_ki, p * p, 0.0), axis=-1, keepdims=True)
    kin = p * lax.rsqrt(ss * (1.0 / IDX_DIM) + EPS) * kig_ref[...]
    kiwi = jnp.where(is_ki, kin, p)
    kiwi_o[...] = kiwi
    if tstate:
        maybe_kit_o[0][0] = transposed(kiwi)[0:IDX_DIM]


def proj_dsa(x, g, wa, q_gain, k_gain, ki_gain, tm, nseq, tstate):
    n, d = x.shape
    hq = A_HEADS * A_HEAD_DIM
    m = wa.shape[1]
    seqlen = n // nseq
    tiles = seqlen // tm
    qg = jnp.tile(q_gain, A_HEADS).reshape(1, hq)
    kg = jnp.tile(k_gain, A_HEADS).reshape(1, hq)
    kig = jnp.concatenate([ki_gain, jnp.ones((LANES - IDX_DIM,), F32)]).reshape(1, LANES)
    r = jnp.arange(2 * LANES)
    bd = _bf((r[:, None] // A_HEAD_DIM) == (r[None, :] // A_HEAD_DIM))
    row = lambda w: pl.BlockSpec((tm, w), lambda i: (i, 0))
    if tstate:
        state = pl.BlockSpec((1, A_HEADS, A_HEAD_DIM, tm), lambda i: (i // tiles, 0, 0, i % tiles))
        state_shape = jax.ShapeDtypeStruct((nseq, A_HEADS, A_HEAD_DIM, seqlen), F32)
        extra_specs = [pl.BlockSpec((1, IDX_DIM, tm), lambda i: (i // tiles, 0, i % tiles))]
        extra_shapes = [jax.ShapeDtypeStruct((nseq, IDX_DIM, seqlen), F32)]
    else:
        state, state_shape, extra_specs, extra_shapes = row(hq), jax.ShapeDtypeStruct((n, hq), F32), [], []
    return pl.pallas_call(
        functools.partial(_proj_dsa_kernel, tstate=tstate),
        grid=(n // tm,),
        in_specs=[row(d), _full((1, d)), _full((d, m)), _full((1, hq)), _full((1, hq)),
                  _full((1, LANES)), _full(bd.shape)],
        out_specs=[row(hq), state, row(hq), state, row(hq), row(IDX_HEADS * LANES), row(LANES)] + extra_specs,
        out_shape=[jax.ShapeDtypeStruct((n, hq), BF16), state_shape,
                   jax.ShapeDtypeStruct((n, hq), BF16), state_shape,
                   jax.ShapeDtypeStruct((n, hq), BF16),
                   jax.ShapeDtypeStruct((n, IDX_HEADS * LANES), BF16),
                   jax.ShapeDtypeStruct((n, LANES), F32)] + extra_shapes,
        compiler_params=_params("parallel"),
        name="proj_dsa",
    )(x, g.reshape(1, d), wa, qg, kg, kig, bd)


def dsa_weight(w_in):
    hq = A_HEADS * A_HEAD_DIM
    o3 = 3 * hq
    o4 = o3 + IDX_HEADS * IDX_DIM
    d = w_in.shape[0]
    wqi = w_in[:, o3:o4].reshape(d, IDX_HEADS, IDX_DIM)
    wqi = jnp.pad(wqi, ((0, 0), (0, 0), (0, LANES - IDX_DIM))).reshape(d, IDX_HEADS * LANES)
    tail = jnp.pad(w_in[:, o4:], ((0, 0), (0, LANES - (w_in.shape[1] - o4))))
    return _bf(jnp.concatenate([w_in[:, :o3], wqi, tail], axis=1))


def _sortable(s):
    b = pltpu.bitcast(s, I32)
    b = jnp.where(b == jnp.int32(-2 ** 31), 0, b)
    return jnp.where(b < 0, b ^ jnp.int32(0x7FFFFFFF), b)


def _index_mask_kernel(qi_ref, wit_ref, *rest, nref, groups, ltot, topk, causal, case_rows):
    ki_refs = rest[:nref]
    o_ref = rest[nref]
    key_ref, sel_ref, jv_ref = rest[nref + 1:]
    tq = qi_ref.shape[0]
    i = pl.program_id(1)
    lpad = sel_ref.shape[0]
    wit = wit_ref[0]
    if case_rows:
        ncase = (jnp.maximum((i + 1) * tq, topk) + case_rows - 1) // case_rows
        used_rows = ncase * case_rows
    else:
        used_rows = None

    def score_group(dst0, rows, sources):
        accs = []
        for rp, lead, src0, width, _ in sources:
            ref = ki_refs[rp]
            ki = _bf(ref[src0:src0 + rows, :] if lead is None else ref[lead, src0:src0 + rows, :])
            acc = jnp.zeros((rows, tq), F32)
            for hd in range(IDX_HEADS):
                rel = _dot_nt(ki, qi_ref[:, hd * LANES:hd * LANES + width])
                acc = acc + wit[hd:hd + 1, :] * jnp.maximum(rel, 0.0)
            accs.append(acc)
        if len(accs) == 1:
            acc = accs[0]
        else:
            lane = lax.broadcasted_iota(I32, (rows, tq), 1)
            acc = jnp.where(lane < tq // 2, accs[0], accs[1])
        s = acc * ((IDX_DIM ** -0.5) * (IDX_HEADS ** -0.5))
        if causal:
            kpos = dst0 + lax.broadcasted_iota(I32, (rows, tq), 0)
            qpos = i * tq + lax.broadcasted_iota(I32, (rows, tq), 1)
            s = jnp.where((kpos >> CHUNK_SHIFT) <= (qpos >> CHUNK_SHIFT), s, NEG)
        key_ref[dst0:dst0 + rows, :] = _sortable(s)

    for dst0, rows, sources in groups:
        if used_rows is None:
            score_group(dst0, rows, sources)
        else:
            pl.when(dst0 < used_rows)(functools.partial(score_group, dst0, rows, sources))

    def select(nrows):
        idx_bits = max(1, (nrows - 1).bit_length())

        def count(pred):
            c = pred.astype(I32).reshape(COUNT_CHAINS, nrows // COUNT_CHAINS, tq)
            return jnp.sum(jnp.sum(c, axis=1), axis=0, keepdims=True)

        def ge_count(cand):
            return count(key_ref[0:nrows, :] >= cand)

        t0 = jnp.full((1, tq), -2 ** 31, I32)
        t = jnp.where(ge_count(jnp.zeros((1, tq), I32)) >= topk, 0, t0)

        def vbody(it, t):
            cand = t + (jnp.int32(1) << (30 - it))
            return jnp.where(ge_count(cand) >= topk, cand, t)

        t = lax.fori_loop(0, 31, vbody, t)
        keys = key_ref[0:nrows, :]
        gt = keys > t
        eq = keys == t
        need = topk - count(gt)
        rowi = lax.broadcasted_iota(I32, (nrows, tq), 0)

        def jbody(it, jv):
            cand = jv + (jnp.int32(1) << (idx_bits - 1 - it))
            below = count(eq & (rowi < cand))
            return jnp.where(below < need, cand, jv)

        jv_ref[...] = jnp.full((1, tq), nrows, I32)
        has_tie = jnp.max(count(eq) - need) > 0

        @pl.when(has_tie)
        def _():
            jv_ref[...] = lax.fori_loop(0, idx_bits, jbody, jnp.zeros((1, tq), I32))

        sel = gt | (eq & (rowi <= jv_ref[...]))
        if causal:
            qpos = i * tq + lax.broadcasted_iota(I32, (nrows, tq), 1)
            sel = sel & ((rowi >> CHUNK_SHIFT) <= (qpos >> CHUNK_SHIFT))
        sel_ref[0:nrows, :] = jnp.where(sel, 0.0, NEG)
        nreal = -(-nrows // TILE)
        if nreal * TILE > nrows:
            sel_ref[nrows:nreal * TILE, :] = jnp.full((nreal * TILE - nrows, tq), NEG, F32)
        for kt in range(lpad // TILE):
            if kt < nreal:
                o_ref[0, 0, kt] = _bf(sel_ref[kt * TILE:(kt + 1) * TILE, :].T)
            else:
                o_ref[0, 0, kt] = jnp.full((tq, TILE), NEG, BF16)

    if case_rows:
        for k in range(ltot // case_rows):
            pl.when(ncase == k + 1)(functools.partial(select, (k + 1) * case_rows))
    else:
        select(ltot)


def index_mask(qi, wit, ki_arrays, ki_specs, groups, ltot, nstep, nqb, causal):
    lpad = -(-ltot // KTILE) * KTILE
    nt = lpad // TILE
    topk = min(TOPK_MAX, ltot // 4)
    case_rows = MASK_CASE_ROWS if (causal and ltot % MASK_CASE_ROWS == 0 and ltot > MASK_CASE_ROWS) else 0
    kern = functools.partial(_index_mask_kernel, nref=len(ki_arrays), groups=tuple(groups), ltot=ltot,
                             topk=topk, causal=causal, case_rows=case_rows)
    return pl.pallas_call(
        kern,
        grid=(nstep, nqb),
        in_specs=[pl.BlockSpec((TILE, qi.shape[1]), lambda b, i: (b * nqb + i, 0)),
                  pl.BlockSpec((1, IDX_HEADS, TILE), lambda b, i: (b * nqb + i, 0, 0))] + list(ki_specs),
        out_specs=pl.BlockSpec((1, 1, nt, TILE, TILE), lambda b, i: (b, i, 0, 0, 0)),
        out_shape=jax.ShapeDtypeStruct((nstep, nqb, nt, TILE, TILE), BF16),
        scratch_shapes=[pltpu.VMEM((ltot, TILE), I32), pltpu.VMEM((lpad, TILE), F32),
                        pltpu.VMEM((1, TILE), I32)],
        compiler_params=_params("parallel", "parallel"),
        name="index_mask",
    )(qi, wit, *ki_arrays)


def _row_groups(row0, rows, make_sources):
    out = []
    for off in range(0, rows, SCORE_ROWS):
        out.append((row0 + off, min(SCORE_ROWS, rows - off), tuple(make_sources(off))))
    return out


def _attn_core(q_ref, o_ref, qs_ref, l_ref, mx_ref, acc_ref, groups, tq):
    npairs = A_HEADS // 2
    lane = lax.broadcasted_iota(I32, (1, LANES), 1)
    keep_lo = _bf((lane < A_HEAD_DIM).astype(F32))
    keep_hi = _bf((lane >= A_HEAD_DIM).astype(F32))
    for pair in range(npairs):
        qp = q_ref[:, pair * LANES:(pair + 1) * LANES]
        qs_ref[pair, 0:tq, :] = qp * keep_lo
        qs_ref[pair, tq:2 * tq, :] = qp * keep_hi
    mx_ref[...] = jnp.full(mx_ref.shape, NEG, F32)
    acc_ref[...] = jnp.zeros(acc_ref.shape, F32)
    ones = jnp.ones((KTILE, LANES), BF16)

    def over_tiles(count, body):
        if isinstance(count, int) and count == 1:
            body(0, 0)
        else:
            lax.fori_loop(0, count, body, 0)

    for count, base, k_tile, _, mask_sub, bias_sub in groups:
        def p1(kt, carry, base=base, k_tile=k_tile, mask_sub=mask_sub, bias_sub=bias_sub):
            masks = [mask_sub(kt, sub).astype(F32) for sub in range(KTILE // LANES)]
            for pair in range(npairs):
                sl = slice(pair * LANES, (pair + 1) * LANES)
                l = _dot_nt(qs_ref[pair], k_tile(kt, sl))
                for half in range(2):
                    rs = slice(half * tq, (half + 1) * tq)
                    mx = mx_ref[pair, rs, :]
                    for sub in range(KTILE // LANES):
                        cs = slice(sub * LANES, (sub + 1) * LANES)
                        blk = l[rs, cs] + (masks[sub] + bias_sub(kt, sub, 2 * pair + half))
                        l_ref[pair, base + kt, rs, cs] = blk
                        mx = jnp.maximum(mx, blk)
                    mx_ref[pair, rs, :] = mx
            return carry

        over_tiles(count, p1)

    for pair in range(npairs):
        m = jnp.max(mx_ref[pair], axis=-1, keepdims=True)
        mx_ref[pair] = jnp.broadcast_to(m, mx_ref.shape[1:])

    for count, base, _, v_tile, _, _ in groups:
        def p2(kt, carry, base=base, v_tile=v_tile):
            for pair in range(npairs):
                sl = slice(pair * LANES, (pair + 1) * LANES)
                m = mx_ref[pair]
                p = jnp.concatenate(
                    [jnp.exp2(l_ref[pair, base + kt, :, sub * LANES:(sub + 1) * LANES] - m)
                     for sub in range(KTILE // LANES)], axis=1)
                vaug = jnp.concatenate([v_tile(kt, sl), ones], axis=1)
                acc_ref[pair] += _dot(_bf(p), vaug)
            return carry

        over_tiles(count, p2)

    lane_full = lax.broadcasted_iota(I32, (tq, LANES), 1)
    for pair in range(npairs):
        a = acc_ref[pair]
        o = a[:, 0:LANES] / a[:, LANES:2 * LANES]
        o_ref[:, pair * LANES:(pair + 1) * LANES] = _bf(jnp.where(lane_full < A_HEAD_DIM, o[0:tq], o[tq:2 * tq]))


def _attn_scratch(tq, ntiles):
    npairs = A_HEADS // 2
    return [pltpu.VMEM((npairs, 2 * tq, LANES), BF16),
            pltpu.VMEM((npairs, ntiles, 2 * tq, KTILE), F32),
            pltpu.VMEM((npairs, 2 * tq, LANES), F32),
            pltpu.VMEM((npairs, 2 * tq, 2 * LANES), F32)]


def _attn_prompt_kernel(q_ref, k_ref, v_ref, mask_ref, bias_ref, o_ref, qs_ref, l_ref, mx_ref, acc_ref):
    i = pl.program_id(1)
    per = KTILE // TILE

    def rows(kt):
        return pl.ds(pl.multiple_of(kt * KTILE, KTILE), KTILE)

    def bias_sub(kt, sub, hd):
        s = kt * per + sub
        sel = jnp.where(s == i, 2, jnp.where(s == i - 1, 1, 0))
        return bias_ref[sel, hd]

    group = (i // per + 1, 0,
             lambda kt, sl: k_ref[rows(kt), sl],
             lambda kt, sl: v_ref[rows(kt), sl],
             lambda kt, sub: mask_ref[0, 0, kt * per + sub],
             bias_sub)
    _attn_core(q_ref, o_ref, qs_ref, l_ref, mx_ref, acc_ref, [group], TILE)


def attn_prompt(q, kb, vb, mask, bias, nseq, seqlen):
    n, hq = q.shape
    nqb = seqlen // TILE
    nt = mask.shape[2]
    return pl.pallas_call(
        _attn_prompt_kernel,
        grid=(nseq, nqb),
        in_specs=[pl.BlockSpec((TILE, hq), lambda b, i: (b * nqb + i, 0)),
                  pl.BlockSpec((seqlen, hq), lambda b, i: (b, 0)),
                  pl.BlockSpec((seqlen, hq), lambda b, i: (b, 0)),
                  pl.BlockSpec((1, 1, nt, TILE, TILE), lambda b, i: (b, i, 0, 0, 0)),
                  _full(bias.shape)],
        out_specs=pl.BlockSpec((TILE, hq), lambda b, i: (b * nqb + i, 0)),
        out_shape=jax.ShapeDtypeStruct((n, hq), BF16),
        scratch_shapes=_attn_scratch(TILE, seqlen // KTILE),
        compiler_params=_params("parallel", "arbitrary"),
        name="attn_prompt",
    )(q, kb, vb, mask, bias)


def _attn_sample_kernel(q_ref, kc_ref, vc_ref, kn_ref, vn_ref, mask_ref, bias_ref, o_ref,
                        l_ref, mx_ref, acc_ref, *, nblk):
    ph = pl.program_id(1)
    b = pl.program_id(2)
    tq = q_ref.shape[0]
    per = KTILE // TILE
    tiles_per_blk = kc_ref.shape[3] // KTILE
    ncache = nblk * tiles_per_blk
    last_sub = ncache * per - 1
    heads = range(A_HEADS)

    def head_tile(ref, c0, hd):
        return _bf(ref[0, hd, :, c0:c0 + KTILE])

    def pass1(slot, ref, c0, mask_sub, bias_sub):
        ls = [_dot(q_ref[:, hd * A_HEAD_DIM:(hd + 1) * A_HEAD_DIM], head_tile(ref, c0, hd))
              for hd in heads]
        masks = [mask_sub(sub).astype(F32) for sub in range(per)]
        for hd in heads:
            mx = mx_ref[hd]
            for sub in range(per):
                cs = slice(sub * LANES, (sub + 1) * LANES)
                blk = ls[hd][:, cs] + (masks[sub] + bias_sub(sub, hd))
                l_ref[hd, slot, :, cs] = blk
                mx = jnp.maximum(mx, blk)
            mx_ref[hd] = mx

    def pass2(slot, ref, c0):
        ones = jnp.ones((LANES - A_HEAD_DIM, KTILE), BF16)
        for hd in heads:
            m = mx_ref[hd]
            p = jnp.concatenate([jnp.exp2(l_ref[hd, slot, :, sub * LANES:(sub + 1) * LANES] - m)
                                 for sub in range(per)], axis=1)
            vaug = jnp.concatenate([head_tile(ref, c0, hd), ones], axis=0)
            acc_ref[hd] += _dot_nt(_bf(p), vaug)

    @pl.when((ph == 0) & (b == 0))
    def _():
        mx_ref[...] = jnp.full(mx_ref.shape, NEG, F32)
        acc_ref[...] = jnp.zeros(acc_ref.shape, F32)

    @pl.when(ph == 0)
    def _():
        for t in range(tiles_per_blk):
            g = b * tiles_per_blk + t
            pass1(g, kc_ref, t * KTILE,
                  lambda sub, g=g: mask_ref[0, 0, g * per + sub],
                  lambda sub, hd, g=g: bias_ref[jnp.where(g * per + sub == last_sub, 1, 0), hd])

    @pl.when((ph == 0) & (b == nblk - 1))
    def _():
        pass1(ncache, kn_ref, 0,
              lambda sub: mask_ref[0, 0, ncache * per + sub],
              lambda sub, hd: bias_ref[2 if sub == 0 else 0, hd])
        for hd in heads:
            m = jnp.max(mx_ref[hd], axis=-1, keepdims=True)
            mx_ref[hd] = jnp.broadcast_to(m, mx_ref.shape[1:])

    @pl.when(ph == 1)
    def _():
        for t in range(tiles_per_blk):
            pass2(b * tiles_per_blk + t, vc_ref, t * KTILE)

    @pl.when((ph == 1) & (b == nblk - 1))
    def _():
        pass2(ncache, vn_ref, 0)
        for pair in range(A_HEADS // 2):
            halves = []
            for hd in (2 * pair, 2 * pair + 1):
                a = acc_ref[hd]
                halves.append(a[:, 0:A_HEAD_DIM] / a[:, A_HEAD_DIM:2 * A_HEAD_DIM])
            o_ref[:, pair * LANES:(pair + 1) * LANES] = _bf(jnp.concatenate(halves, axis=1))


def attn_sample(q, kc, vc, kn, vn, mask, bias, nseq, tq):
    n, hq = q.shape
    past = kc.shape[3]
    cblk = min(CACHE_BLOCK, past)
    nblk = past // cblk
    ncache = past // KTILE
    nt = mask.shape[2]
    per = TILE // tq
    cache_k = pl.BlockSpec((1, A_HEADS, A_HEAD_DIM, cblk),
                           lambda s, ph, b: (s, 0, 0, b * (1 - ph) + (nblk - 1) * ph))
    cache_v = pl.BlockSpec((1, A_HEADS, A_HEAD_DIM, cblk), lambda s, ph, b: (s, 0, 0, b * ph))
    fresh = pl.BlockSpec((1, A_HEADS, A_HEAD_DIM, KTILE), lambda s, ph, b: (s, 0, 0, 0))
    return pl.pallas_call(
        functools.partial(_attn_sample_kernel, nblk=nblk),
        grid=(nseq, 2, nblk),
        in_specs=[pl.BlockSpec((tq, hq), lambda s, ph, b: (s, 0)),
                  cache_k, cache_v, fresh, fresh,
                  pl.BlockSpec((1, 1, nt, tq, TILE), lambda s, ph, b: (s // per, 0, 0, s % per, 0)),
                  pl.BlockSpec(bias.shape, lambda s, ph, b: (0, 0, 0, 0))],
        out_specs=pl.BlockSpec((tq, hq), lambda s, ph, b: (s, 0)),
        out_shape=jax.ShapeDtypeStruct((n, hq), BF16),
        scratch_shapes=[pltpu.VMEM((A_HEADS, ncache + 1, tq, KTILE), F32),
                        pltpu.VMEM((A_HEADS, tq, LANES), F32),
                        pltpu.VMEM((A_HEADS, tq, LANES), F32)],
        compiler_params=_params("parallel", "arbitrary", "arbitrary"),
        name="attn_sample",
    )(q, kc, vc, kn, vn, mask, bias)


def _t5_bucket(rel):
    half = N_BUCKETS // 2
    max_exact = half // 2
    n = jnp.abs(rel)
    nf = jnp.maximum(n, 1).astype(F32)
    large = max_exact + (jnp.log(nf / max_exact) / math.log(MAX_DISTANCE / max_exact)
                         * (half - max_exact)).astype(I32)
    large = jnp.minimum(large, half - 1)
    return jnp.where(rel > 0, half, 0) + jnp.where(n < max_exact, n, large)


def bias_tiles(rel_bias):
    heads = rel_bias.shape[1]
    span = 2 * TILE - 1

    def toeplitz(shift):
        rel = jnp.arange(span, dtype=I32) - (TILE - 1) + shift
        tab = rel_bias.astype(F32)[_t5_bucket(rel)].T
        strip = jnp.tile(jnp.pad(tab, ((0, 0), (0, 1))), (1, TILE))[:, :TILE * span]
        return strip.reshape(heads, TILE, span)[:, :, TILE - 1:]

    far = rel_bias.astype(F32)[_t5_bucket(jnp.full((1,), -(TILE + 1), I32))]
    far = jnp.broadcast_to(far.T[:, :, None], (heads, TILE, TILE))
    return jnp.stack([far, toeplitz(-TILE), toeplitz(0)]) * LOG2E


def dsa_core_prompt(pr, nseq, seqlen, bias):
    q, _, kb, _, vb, qi, kiwi = pr
    nqb = seqlen // TILE
    wit = kiwi[:, IDX_DIM:IDX_DIM + IDX_HEADS].reshape(nseq * nqb, TILE, IDX_HEADS).swapaxes(1, 2)
    groups = _row_groups(0, seqlen, lambda off: [(0, None, off, LANES, None)])
    mask = index_mask(qi, wit, [kiwi], [pl.BlockSpec((seqlen, LANES), lambda b, i: (b, 0))],
                      groups, seqlen, nseq, nqb, causal=True)
    return attn_prompt(q, kb, vb, mask, bias, nseq, seqlen)


def dsa_core_sample(pr, k_cache, v_cache, ki_cache, nseq, tq, bias):
    q, kf, _, vf, _, qi, kiwi = pr
    past = k_cache.shape[1]
    per = TILE // tq
    wit = kiwi[:, IDX_DIM:IDX_DIM + IDX_HEADS].reshape(nseq // per, TILE, IDX_HEADS).swapaxes(1, 2)
    groups = (_row_groups(0, past, lambda off: [(0, j, off, IDX_DIM, j) for j in range(per)])
              + _row_groups(past, tq, lambda off: [(1, None, j * tq + off, LANES, j) for j in range(per)]))
    mask = index_mask(
        qi, wit, [ki_cache, kiwi],
        [pl.BlockSpec((per, past, IDX_DIM), lambda b, i: (b, 0, 0)),
         pl.BlockSpec((TILE, LANES), lambda b, i: (b, 0))],
        groups, past + tq, nseq // per, 1, causal=False)
    to_hdk = lambda a: a.transpose(0, 2, 3, 1)
    pad_keys = ((0, 0), (0, 0), (0, 0), (0, KTILE - tq))
    kn = jnp.pad(to_hdk(kf.reshape(nseq, tq, A_HEADS, A_HEAD_DIM)), pad_keys)
    vn = jnp.pad(to_hdk(vf.reshape(nseq, tq, A_HEADS, A_HEAD_DIM)), pad_keys)
    return attn_sample(q, to_hdk(k_cache), to_hdk(v_cache), kn, vn, mask, bias[:, :, :tq, :], nseq, tq)


def _pair_triangles():
    r = jnp.arange(2 * CHUNK)[:, None]
    c = jnp.arange(2 * CHUNK)[None, :]
    same = (r // CHUNK) == (c // CHUNK)
    return _bf(same & (c <= r)), _bf(same & (r <= c))


def _log_sigmoid(x):
    return jnp.minimum(x, 0.0) - jnp.log1p(jnp.exp(-jnp.abs(x)))


def _mlstm_kernel(p_ref, gs_ref, gr_ref, gbc_ref, gbr_ref, hg_ref, tl_ref, tu_ref, c0_ref, n0_ref, m0_ref,
                  hs_ref, c_ref, n_ref, m_ref, cs, ns, ms):
    c_id = pl.program_id(1)
    nh = B_HEADS
    npair = nh // 2
    hw = LANES
    nsq = p_ref.shape[0]
    L = p_ref.shape[1]
    L2 = 2 * L

    @pl.when(c_id == 0)
    def _():
        cs[...] = jnp.zeros(cs.shape, F32)
        ns[...] = jnp.zeros(ns.shape, F32)
        for sq in range(nsq):
            ns[sq, :, 0:B_QK_DIM] = n0_ref[sq]
            for hd in range(nh):
                cs[sq, hd // 2, (hd % 2) * hw:(hd % 2) * hw + B_QK_DIM, :] = c0_ref[sq, hd]
        ms[...] = m0_ref[...]

    r = lax.broadcasted_iota(I32, (L2, L2), 0)
    c = lax.broadcasted_iota(I32, (L2, L2), 1)
    same = (r >> CHUNK_SHIFT) == (c >> CHUNK_SHIFT)
    incl = same & (c <= r)
    tri_l = tl_ref[...]
    tri_u = tu_ref[...]
    rowc = lax.broadcasted_iota(I32, (L2, 1), 0)
    is_top = rowc < L
    top = is_top.astype(F32)
    bot = 1.0 - top
    row2 = lax.broadcasted_iota(I32, (2 * hw, 1), 0)
    hg = hg_ref[...]

    def stack(base, sq, pr):
        a = base + 2 * pr * hw
        return jnp.concatenate([p_ref[sq, :, a:a + hw], p_ref[sq, :, a + hw:a + 2 * hw]], axis=0)

    def bd(x):
        return _bf(jnp.concatenate([x * top, x * bot], axis=1))

    seqs = range(nsq)
    gs = [gs_ref[sq, 0] + gbc_ref[...] for sq in seqs]
    gr = [gr_ref[sq, 0] + gbr_ref[...] for sq in seqs]
    bcol = [_dot_exact_lhs01(tri_l, _log_sigmoid(gs[sq][:, npair:])) for sq in seqs]
    brow = [_dot_exact_rhs01(_log_sigmoid(gr[sq][npair:, :]), tri_u) for sq in seqs]
    mall = [ms[sq] for sq in seqs]
    cells = [(sq, pr) for sq in seqs for pr in range(npair)]
    ids = range(len(cells))
    qf = [stack(0, sq, pr) for sq, pr in cells]
    kf = [stack(nh * hw, sq, pr) * (B_QK_DIM ** -0.5) for sq, pr in cells]
    vb = [_bf(stack(2 * nh * hw, sq, pr)) for sq, pr in cells]
    bc = [bcol[sq][:, pr:pr + 1] for sq, pr in cells]
    ic = [gs[sq][:, pr:pr + 1] for sq, pr in cells]
    mc = [mall[sq][:, pr:pr + 1] for sq, pr in cells]
    d = [jnp.where(incl, bc[i] - brow[sq][pr:pr + 1, :] + gr[sq][pr:pr + 1, :], NEG)
         for i, (sq, pr) in enumerate(cells)]
    inter = [bc[i] + mc[i] for i in ids]
    mt = [jnp.maximum(inter[i], jnp.max(d[i], axis=-1, keepdims=True)) for i in ids]
    s = [_dot_nt(_bf(qf[i]), _bf(kf[i])) * jnp.exp(d[i] - mt[i]) for i in ids]
    wst = [jnp.exp(inter[i] - mt[i]) for i in ids]
    cmat = [cs[sq, pr] for sq, pr in cells]
    num = [_dot(_bf(s[i]), vb[i]) + wst[i] * _dot(bd(qf[i]), _bf(cmat[i])) for i in ids]
    for i, (sq, pr) in enumerate(cells):
        n0 = ns[sq, 2 * pr:2 * pr + 1, :]
        n1 = ns[sq, 2 * pr + 1:2 * pr + 2, :]
        qn = jnp.sum(qf[i] * jnp.where(is_top, n0, n1), axis=-1, keepdims=True)
        den = jnp.sum(s[i], axis=-1, keepdims=True) + wst[i] * qn
        hs = num[i] / jnp.maximum(jnp.abs(den), jnp.exp(-mt[i]))
        bl0 = bc[i][L - 1:L, :]
        bl1 = bc[i][L2 - 1:L2, :]
        bl = jnp.where(is_top, bl0, bl1)
        dec = bl - bc[i] + ic[i]
        blm = bl + mc[i]
        mnew0 = jnp.maximum(blm[0:1, :], jnp.max(dec[0:L], axis=0, keepdims=True))
        mnew1 = jnp.maximum(blm[L:L + 1, :], jnp.max(dec[L:L2], axis=0, keepdims=True))
        mnew = jnp.where(is_top, mnew0, mnew1)
        wk = jnp.exp(dec - mnew)
        ws = jnp.exp(blm - mnew)
        kw = kf[i] * wk
        ws2 = jnp.where(row2 < hw, ws[0:1, :], ws[L:L + 1, :])
        cs[sq, pr] = ws2 * cmat[i] + _dot_tn(bd(kw), vb[i])
        ns[sq, 2 * pr:2 * pr + 1, :] = ws[0:1, :] * n0 + jnp.sum(kw[0:L], axis=0, keepdims=True)
        ns[sq, 2 * pr + 1:2 * pr + 2, :] = ws[L:L + 1, :] * n1 + jnp.sum(kw[L:L2], axis=0, keepdims=True)
        ms[sq, :, pr:pr + 1] = mnew
        on = _bf(_rms(hs, hg) * _sigmoid(stack(3 * nh * hw, sq, pr)))
        oa = 2 * pr * B_V_DIM
        hs_ref[sq, :, oa:oa + B_V_DIM] = on[0:L]
        hs_ref[sq, :, oa + B_V_DIM:oa + 2 * B_V_DIM] = on[L:L2]

    @pl.when(c_id == pl.num_programs(1) - 1)
    def _():
        for sq in range(nsq):
            for hd in range(nh):
                c_ref[sq, hd] = cs[sq, hd // 2, (hd % 2) * hw:(hd % 2) * hw + B_QK_DIM, :]
            n_ref[sq] = ns[sq, :, 0:B_QK_DIM]
        m_ref[...] = ms[...]


def mlstm_weight(w_in):
    d = w_in.shape[0]
    nq = B_HEADS * B_QK_DIM
    nv = B_HEADS * B_V_DIM

    def padh(w):
        w = w.reshape(d, B_HEADS, B_QK_DIM)
        return jnp.pad(w, ((0, 0), (0, 0), (0, LANES - B_QK_DIM))).reshape(d, B_HEADS * LANES)

    main = 2 * nq + 2 * nv
    tail = jnp.pad(w_in[:, main:], ((0, 0), (0, LANES - 2 * B_HEADS)))
    return _bf(jnp.concatenate([padh(w_in[:, :nq]), padh(w_in[:, nq:2 * nq]), w_in[:, 2 * nq:main], tail], axis=1))


def mlstm_core(p, gates, gate_bias, h_gain, c0, n0, m0, nseq, nchunk):
    n, m = p.shape
    nh = B_HEADS
    npair = nh // 2
    gcol = 4 * nh * LANES
    nsq = SCAN_SEQS if nseq % SCAN_SEQS == 0 else 1
    seqlen = n // nseq
    pre = gates[:, :2 * nh].reshape(nseq, nchunk, CHUNK, 2, npair, 2)
    gs = pre.transpose(0, 1, 5, 2, 3, 4).reshape(nseq, nchunk, 2 * CHUNK, 2 * npair)
    gr = pre.transpose(0, 1, 3, 4, 5, 2).reshape(nseq, nchunk, 2 * npair, 2 * CHUNK)
    gb = gate_bias.reshape(2, npair, 2)
    gbc = jnp.repeat(gb.transpose(2, 0, 1).reshape(2, 2 * npair), CHUNK, axis=0)
    m0s = jnp.repeat(m0.reshape(nseq, npair, 2).transpose(0, 2, 1), CHUNK, axis=1)
    o, c_new, n_new, m_new = pl.pallas_call(
        _mlstm_kernel,
        grid=(nseq // nsq, nchunk),
        in_specs=[pl.BlockSpec((nsq, CHUNK, m), lambda s, c: (s, c, 0)),
                  pl.BlockSpec((nsq, 1, 2 * CHUNK, 2 * npair), lambda s, c: (s, c, 0, 0)),
                  pl.BlockSpec((nsq, 1, 2 * npair, 2 * CHUNK), lambda s, c: (s, c, 0, 0)),
                  _full((2 * CHUNK, 2 * npair)), _full((2 * npair, 2 * CHUNK)), _full((1, B_V_DIM)),
                  _full((2 * CHUNK, 2 * CHUNK)), _full((2 * CHUNK, 2 * CHUNK)),
                  pl.BlockSpec((nsq, nh, B_QK_DIM, B_V_DIM), lambda s, c: (s, 0, 0, 0)),
                  pl.BlockSpec((nsq, nh, B_QK_DIM), lambda s, c: (s, 0, 0)),
                  pl.BlockSpec((nsq, 2 * CHUNK, npair), lambda s, c: (s, 0, 0))],
        out_specs=[pl.BlockSpec((nsq, CHUNK, nh * B_V_DIM), lambda s, c: (s, c, 0)),
                   pl.BlockSpec((nsq, nh, B_QK_DIM, B_V_DIM), lambda s, c: (s, 0, 0, 0)),
                   pl.BlockSpec((nsq, nh, B_QK_DIM), lambda s, c: (s, 0, 0)),
                   pl.BlockSpec((nsq, 2 * CHUNK, npair), lambda s, c: (s, 0, 0))],
        out_shape=[jax.ShapeDtypeStruct((nseq, seqlen, nh * B_V_DIM), BF16),
                   jax.ShapeDtypeStruct((nseq, nh, B_QK_DIM, B_V_DIM), F32),
                   jax.ShapeDtypeStruct((nseq, nh, B_QK_DIM), F32),
                   jax.ShapeDtypeStruct((nseq, 2 * CHUNK, npair), F32)],
        scratch_shapes=[pltpu.VMEM((nsq, npair, 2 * LANES, B_V_DIM), F32), pltpu.VMEM((nsq, nh, LANES), F32),
                        pltpu.VMEM((nsq, 2 * CHUNK, npair), F32)],
        compiler_params=_params("parallel", "arbitrary"),
        name="mlstm",
    )(p.reshape(nseq, seqlen, m), gs, gr, gbc, gbc.T, h_gain.reshape(1, B_V_DIM), *_pair_triangles(),
      c0, n0, m0s)
    m_heads = m_new[:, ::CHUNK, :].transpose(0, 2, 1).reshape(nseq, nh)
    return o.reshape(n, nh * B_V_DIM), c_new, n_new, m_heads


def _split2(x):
    hi = _bf(x)
    return hi, _bf(x - hi.astype(F32))


def _cat3_lhs(x):
    hi, mid = _split2(x)
    return jnp.concatenate([hi, hi, mid], axis=1)


def _cat3_rhs(x):
    hi, mid = _split2(x)
    return jnp.concatenate([hi, mid, hi], axis=0)


def _gdn_kernel(p_ref, gs_ref, gr_ref, cw_ref, alc_ref, alr_ref, dtc_ref, dtr_ref, og_ref, tl_ref, tu_ref,
                s0_ref, cb0_ref, o_ref, s_ref, cb_ref, ss, tail):
    c_id = pl.program_id(1)
    nh = C_HEADS
    cdim = nh * (2 * C_DK + C_DV)
    L = p_ref.shape[0]
    L2 = 2 * L
    nprev = CONV_W - 1

    @pl.when(c_id == 0)
    def _():
        ss[...] = s0_ref[0]
        tail[...] = jnp.zeros(tail.shape, F32)
        tail[8 - nprev:8, :] = cb0_ref[0]

    x = p_ref[:, 0:cdim]
    ext = jnp.concatenate([tail[...], x], axis=0)
    conv = ext[8:8 + L] * cw_ref[CONV_W - 1:CONV_W, :]
    for j in range(CONV_W - 1):
        conv = conv + ext[8 - nprev + j:8 - nprev + j + L] * cw_ref[j:j + 1, :]
    tail[...] = x[L - 8:L, :]
    cf = _silu(conv)

    npair = nh // 2
    gs = gs_ref[0]
    gr = gr_ref[0]
    beta = _sigmoid(gs[:, 0:npair])
    g_col = -jnp.exp(alc_ref[...]) * _softplus(gs[:, npair:] + dtc_ref[...])
    g_row = -jnp.exp(alr_ref[...]) * _softplus(gr[npair:, :] + dtr_ref[...])
    r = lax.broadcasted_iota(I32, (L2, L2), 0)
    c = lax.broadcasted_iota(I32, (L2, L2), 1)
    same = (r >> CHUNK_SHIFT) == (c >> CHUNK_SHIFT)
    incl = same & (c <= r)
    strict = same & (c < r)
    eye = (c == r).astype(F32)
    gc_col = _dot_exact_lhs01(tl_ref[...], g_col)
    gc_row = _dot_exact_rhs01(g_row, tu_ref[...])
    rowc = lax.broadcasted_iota(I32, (L2, 1), 0)
    top = (rowc < L).astype(F32)
    bot = 1.0 - top
    row2 = lax.broadcasted_iota(I32, (2 * C_DK, 1), 0)
    og = og_ref[...]

    def stack(base, pr):
        a = base + 2 * pr * C_DK
        return jnp.concatenate([cf[:, a:a + C_DK], cf[:, a + C_DK:a + 2 * C_DK]], axis=0)

    def bd(x):
        return _bf(jnp.concatenate([x * top, x * bot], axis=1))

    pairs = range(npair)
    qc, kc, kcb, dm, amat, rhs, gcols, egcs = [], [], [], [], [], [], [], []
    for pr in pairs:
        qraw = stack(0, pr)
        kraw = stack(nh * C_DK, pr)
        vc = stack(2 * nh * C_DK, pr)
        qc.append(qraw * lax.rsqrt(jnp.sum(qraw * qraw, axis=-1, keepdims=True) + EPS) * (C_DK ** -0.5))
        kc.append(kraw * lax.rsqrt(jnp.sum(kraw * kraw, axis=-1, keepdims=True) + EPS))
        bc = beta[:, pr:pr + 1]
        gcol = gc_col[:, pr:pr + 1]
        grow = gc_row[pr:pr + 1, :]
        dm.append(jnp.where(incl, jnp.exp(jnp.where(incl, gcol - grow, 0.0)), 0.0))
        kb = kc[pr] * bc
        kcb.append(_bf(kc[pr]))
        amat.append(jnp.where(strict, _dot_nt(_bf(kb), kcb[pr]) * dm[pr], 0.0))
        egc = jnp.exp(gcol)
        rhs.append(jnp.concatenate([vc * bc, kb * egc], axis=-1))
        gcols.append(gcol)
        egcs.append(egc)
    tinv = [eye - amat[pr] for pr in pairs]
    pw_l = [_cat3_lhs(-amat[pr]) for pr in pairs]
    pw_r = [_cat3_rhs(-amat[pr]) for pr in pairs]
    for _ in range(CHUNK_SHIFT - 1):
        pw = [_dot(pw_l[pr], pw_r[pr]) for pr in pairs]
        pw_l = [_cat3_lhs(pw[pr]) for pr in pairs]
        pw_r = [_cat3_rhs(pw[pr]) for pr in pairs]
        tinv = [tinv[pr] + _dot(_cat3_lhs(tinv[pr]), pw_r[pr]) for pr in pairs]
    sol = [_dot(_cat3_lhs(tinv[pr]), _cat3_rhs(rhs[pr])) for pr in pairs]
    attn = [_dot_nt(_bf(qc[pr]), kcb[pr]) * dm[pr] for pr in pairs]
    smat = [ss[pr] for pr in pairs]
    sb = [_bf(smat[pr]) for pr in pairs]
    vnew = [sol[pr][:, :C_DV] - _dot(bd(sol[pr][:, C_DV:]), sb[pr]) for pr in pairs]
    o = [_dot(bd(qc[pr] * egcs[pr]), sb[pr]) + _dot(_bf(attn[pr]), _bf(vnew[pr])) for pr in pairs]
    for pr in pairs:
        gl0 = gcols[pr][L - 1:L, :]
        gl1 = gcols[pr][L2 - 1:L2, :]
        ke = kc[pr] * jnp.exp(jnp.where(rowc < L, gl0, gl1) - gcols[pr])
        decay = jnp.exp(jnp.where(row2 < C_DK, gl0, gl1))
        ss[pr] = smat[pr] * decay + _dot_tn(bd(ke), _bf(vnew[pr]))
    for pr in pairs:
        za = cdim + 2 * pr * C_DV
        z = jnp.concatenate([p_ref[:, za:za + C_DV], p_ref[:, za + C_DV:za + 2 * C_DV]], axis=0)
        on = _bf(_rms(o[pr], og) * _silu(z))
        oa = 2 * pr * C_DV
        o_ref[:, oa:oa + C_DV] = on[0:L]
        o_ref[:, oa + C_DV:oa + 2 * C_DV] = on[L:L2]

    @pl.when(c_id == pl.num_programs(1) - 1)
    def _():
        s_ref[0] = ss[...]
        cb_ref[0] = tail[8 - nprev:8, :]


def gdn_weight(w_in):
    main = C_HEADS * (2 * C_DK + C_DV) + C_HEADS * C_DV
    tail = jnp.pad(w_in[:, main:], ((0, 0), (0, LANES - 2 * C_HEADS)))
    return _bf(jnp.concatenate([w_in[:, :main], tail], axis=1))


def gdn_core(p, gates, conv_w, a_log, dt_bias, o_gain, s0, cb0, nseq, nchunk):
    n, m = p.shape
    nh = C_HEADS
    npair = nh // 2
    cdim = nh * (2 * C_DK + C_DV)
    gcolumn = cdim + nh * C_DV
    pre = gates[:, :2 * nh].reshape(nseq * nchunk, CHUNK, 2, npair, 2)
    gs = pre.transpose(0, 4, 1, 2, 3).reshape(nseq * nchunk, 2 * CHUNK, 2 * npair)
    gr = pre.transpose(0, 2, 3, 4, 1).reshape(nseq * nchunk, 2 * npair, 2 * CHUNK)

    def col(v):
        return jnp.repeat(v.reshape(npair, 2).T, CHUNK, axis=0)

    o, s_new, cb_new = pl.pallas_call(
        _gdn_kernel,
        grid=(nseq, nchunk),
        in_specs=[pl.BlockSpec((CHUNK, m), lambda s, c: (s * nchunk + c, 0)),
                  pl.BlockSpec((1, 2 * CHUNK, 2 * npair), lambda s, c: (s * nchunk + c, 0, 0)),
                  pl.BlockSpec((1, 2 * npair, 2 * CHUNK), lambda s, c: (s * nchunk + c, 0, 0)),
                  _full((CONV_W, cdim)),
                  _full((2 * CHUNK, npair)), _full((npair, 2 * CHUNK)),
                  _full((2 * CHUNK, npair)), _full((npair, 2 * CHUNK)),
                  _full((1, C_DV)), _full((2 * CHUNK, 2 * CHUNK)), _full((2 * CHUNK, 2 * CHUNK)),
                  pl.BlockSpec((1, npair, 2 * C_DK, C_DV), lambda s, c: (s, 0, 0, 0)),
                  pl.BlockSpec((1, CONV_W - 1, cdim), lambda s, c: (s, 0, 0))],
        out_specs=[pl.BlockSpec((CHUNK, nh * C_DV), lambda s, c: (s * nchunk + c, 0)),
                   pl.BlockSpec((1, npair, 2 * C_DK, C_DV), lambda s, c: (s, 0, 0, 0)),
                   pl.BlockSpec((1, CONV_W - 1, cdim), lambda s, c: (s, 0, 0))],
        out_shape=[jax.ShapeDtypeStruct((n, nh * C_DV), BF16),
                   jax.ShapeDtypeStruct((nseq, npair, 2 * C_DK, C_DV), F32),
                   jax.ShapeDtypeStruct((nseq, CONV_W - 1, cdim), F32)],
        scratch_shapes=[pltpu.VMEM((npair, 2 * C_DK, C_DV), F32), pltpu.VMEM((8, cdim), F32)],
        compiler_params=_params("parallel", "arbitrary"),
        name="gdn",
    )(p, gs, gr, conv_w, col(a_log), col(a_log).T, col(dt_bias), col(dt_bias).T, o_gain.reshape(1, C_DV),
      *_pair_triangles(),
      s0.reshape(nseq, npair, 2 * C_DK, C_DV), cb0)
    return o, s_new.reshape(nseq, nh, C_DK, C_DV), cb_new


def _trunk(x, nseq, seqlen, mem_k, mem_v, st, W, bias, is_prompt):
    d = x.shape[-1]
    n = nseq * seqlen
    x = x.reshape(n, d)
    tm = min(PROJ_ROWS, n)
    tm_mem = min(MEM_ROWS, n)
    new = {}
    for i in range(4):
        kind = i % 3
        mx = W["mixer"][i]
        if kind == 0:
            tstate = is_prompt and seqlen % tm == 0
            pr = proj_dsa(x, W["norm_mix"][i], mx["w_in"], mx["q_gain"], mx["k_gain"], mx["ki_gain"], tm,
                          nseq, tstate)
            if is_prompt:
                o = dsa_core_prompt(pr[:7], nseq, seqlen, bias)
            else:
                o = dsa_core_sample(pr, *st[i], nseq, seqlen, bias)
            if tstate:
                new[i] = (pr[1].transpose(0, 3, 1, 2), pr[3].transpose(0, 3, 1, 2), pr[7].transpose(0, 2, 1))
            else:
                new[i] = (pr[1].reshape(nseq, seqlen, A_HEADS, A_HEAD_DIM),
                          pr[3].reshape(nseq, seqlen, A_HEADS, A_HEAD_DIM),
                          pr[6][:, :IDX_DIM].reshape(nseq, seqlen, IDX_DIM))
        elif kind == 1:
            p, gates = proj(x, W["norm_mix"][i], mx["w_in"], tm)
            o, c_new, n_new, m_new = mlstm_core(p, gates, mx["gate_bias"], mx["h_gain"], *st[i],
                                                nseq, seqlen // CHUNK)
            new[i] = (c_new, n_new, m_new.reshape(nseq, B_HEADS))
        else:
            p, gates = proj(x, W["norm_mix"][i], mx["w_in"], tm)
            o, s_new, cb_new = gdn_core(p, gates, mx["conv_w"], mx["a_log"], mx["dt_bias"], mx["o_gain"], *st[i],
                                        nseq, seqlen // CHUNK)
            new[i] = (s_new, cb_new)
        x = mem_attend(x, o, mx["w_out"], W["norm_mem"][i], W["w_mq"][i], W["mq_gain"][i],
                       mem_k[i], mem_v[i], W["w_mo"][i], tm_mem, seqlen)
        x = ffn(x, W["norm_ffn"][i], W["w_ffn1"][i], W["w_ffn3"][i], W["w_ffn2"][i], min(FFN_ROWS, n))
    return x.reshape(nseq, seqlen, d), new


def kernel(x_prompt, x_sample, mem_prompt, cache_l0_k, cache_l0_v, cache_l0_kidx, state_l1_C, state_l1_n, state_l1_m, state_l2_S, state_l2_conv, cache_l3_k, cache_l3_v, cache_l3_kidx, cache_mem_k, cache_mem_v, rel_bias, norm_mix, norm_mem, norm_ffn, mem_norm, w_mq, w_mk, w_mv, w_mo, mq_gain, mk_gain, w_ffn1, w_ffn3, w_ffn2, a0_w_in, a0_w_out, a0_q_gain, a0_k_gain, a0_kidx_gain, b1_w_in, b1_gate_bias, b1_h_gain, b1_w_out, c2_w_in, c2_conv_w, c2_a_log, c2_dt_bias, c2_o_gain, c2_w_out, a3_w_in, a3_w_out, a3_q_gain, a3_k_gain, a3_kidx_gain):
    B, T, D = x_prompt.shape
    S, Ts, _ = x_sample.shape
    depth = w_mq.shape[0]
    mlen = mem_prompt.shape[1]
    mw = MEM_HEADS * MEM_HEAD_DIM

    def dsa_w(w_in, w_out, qg, kg, kig):
        return dict(w_in=dsa_weight(w_in), w_out=_bf(w_out), q_gain=qg, k_gain=kg, ki_gain=kig)

    W = dict(
        norm_mix=norm_mix, norm_mem=norm_mem, norm_ffn=norm_ffn,
        w_mq=_bf(w_mq), w_mo=_bf(w_mo), mq_gain=mq_gain,
        w_ffn1=_bf(w_ffn1), w_ffn3=_bf(w_ffn3), w_ffn2=_bf(w_ffn2),
        mixer={
            0: dsa_w(a0_w_in, a0_w_out, a0_q_gain, a0_k_gain, a0_kidx_gain),
            1: dict(w_in=mlstm_weight(b1_w_in), gate_bias=b1_gate_bias, h_gain=b1_h_gain, w_out=_bf(b1_w_out)),
            2: dict(w_in=gdn_weight(c2_w_in), conv_w=c2_conv_w, a_log=c2_a_log, dt_bias=c2_dt_bias,
                    o_gain=c2_o_gain, w_out=_bf(c2_w_out)),
            3: dsa_w(a3_w_in, a3_w_out, a3_q_gain, a3_k_gain, a3_kidx_gain),
        },
    )
    bias = bias_tiles(rel_bias)

    mk_p, mv_p = mem_kv(mem_prompt.reshape(B * mlen, D), mem_norm, w_mk, w_mv, mk_gain)
    mk_p = mk_p.reshape(depth, B, mlen, mw)
    mv_p = mv_p.reshape(depth, B, mlen, mw)
    st_p = {
        0: None,
        1: (jnp.zeros((B, B_HEADS, B_QK_DIM, B_V_DIM), F32), jnp.zeros((B, B_HEADS, B_QK_DIM), F32),
            jnp.full((B, B_HEADS), NEG, F32)),
        2: (jnp.zeros((B, C_HEADS, C_DK, C_DV), F32), jnp.zeros((B, CONV_W - 1, state_l2_conv.shape[-1]), F32)),
        3: None,
    }
    y_p, np_ = _trunk(x_prompt, B, T, mk_p, mv_p, st_p, W, bias, True)

    st_s = {
        0: (cache_l0_k, cache_l0_v, cache_l0_kidx),
        1: (state_l1_C, state_l1_n, state_l1_m),
        2: (state_l2_S, state_l2_conv),
        3: (cache_l3_k, cache_l3_v, cache_l3_kidx),
    }
    mk_s = cache_mem_k.reshape(depth, S, mlen, mw)
    mv_s = cache_mem_v.reshape(depth, S, mlen, mw)
    y_s, ns_ = _trunk(x_sample, S, Ts, mk_s, mv_s, st_s, W, bias, False)

    shp = (depth, B, mlen, MEM_HEADS, MEM_HEAD_DIM)
    return (y_p, y_s,
            *np_[0], *np_[1], *np_[2], *np_[3], mk_p.reshape(shp), mv_p.reshape(shp),
            *ns_[0], *ns_[1], *ns_[2], *ns_[3])
```

```python
import functools
import math

import jax
import jax.numpy as jnp
from jax import lax
from jax.experimental import pallas as pl
from jax.experimental.pallas import tpu as pltpu

F32 = jnp.float32
BF16 = jnp.bfloat16
I32 = jnp.int32

EPS = 1e-6
NEG = -1e30
CHUNK = 64
CHUNK_SHIFT = 6
LANES = 128
TILE = 128
KTILE = 256
COUNT_CHAINS = 8
SCORE_ROWS = 256
PROJ_ROWS = 512
FFN_ROWS = 512
MEM_ROWS = 512
SCAN_SEQS = 4
CACHE_BLOCK = 1024
MASK_CASE_ROWS = 256
VMEM_LIMIT = 56 * 1024 * 1024

A_HEADS, A_HEAD_DIM = 16, 64
LOG2E = math.log2(math.e)
QK_SCALE = (A_HEAD_DIM ** -0.5) * LOG2E
IDX_HEADS, IDX_DIM = 8, 64
TOPK_MAX = 256
N_BUCKETS, MAX_DISTANCE = 32, 128
B_HEADS, B_QK_DIM, B_V_DIM = 8, 64, 128
C_HEADS, C_DK, C_DV = 8, 128, 128
CONV_W = 4
MEM_HEADS, MEM_HEAD_DIM = 4, 128


def _bf(x):
    return x.astype(BF16)


def _bf01(mask):
    return mask.astype(F32).astype(BF16)


def _dot(a, b):
    return jnp.dot(a, b, preferred_element_type=F32)


def _dot_nt(a, b):
    return lax.dot_general(a, b, (((1,), (1,)), ((), ())), preferred_element_type=F32)


def _dot_tn(a, b):
    return lax.dot_general(a, b, (((0,), (0,)), ((), ())), preferred_element_type=F32)


def _split3(x):
    hi = _bf(x)
    r1 = x - hi.astype(F32)
    mid = _bf(r1)
    lo = _bf(r1 - mid.astype(F32))
    return hi, mid, lo


def _dot_exact_rhs01(x, m01):
    hi, mid, lo = _split3(x)
    return _dot(hi, m01) + _dot(mid, m01) + _dot(lo, m01)


def _dot_exact_lhs01(m01, x):
    hi, mid, lo = _split3(x)
    return _dot(m01, hi) + _dot(m01, mid) + _dot(m01, lo)


def _dot_f32(a, b):
    ah, am, al = _split3(a)
    bh, bm, bl = _split3(b)
    return (_dot(ah, bh) + (_dot(ah, bm) + _dot(am, bh))
            + (_dot(am, bm) + _dot(ah, bl) + _dot(al, bh)))


def _rms(x, g):
    ms = jnp.mean(x * x, axis=-1, keepdims=True)
    return x * lax.rsqrt(ms + EPS) * g


def _sigmoid(x):
    return 1.0 / (1.0 + jnp.exp(-x))


def _silu(x):
    return x * _sigmoid(x)


def _softplus(x):
    return jnp.maximum(x, 0.0) + jnp.log1p(jnp.exp(-jnp.abs(x)))


def _params(*sem):
    return pltpu.CompilerParams(dimension_semantics=sem, vmem_limit_bytes=VMEM_LIMIT)


def _full(shape):
    n = len(shape)
    return pl.BlockSpec(shape, lambda *_: (0,) * n)


def _proj_kernel(x_ref, g_ref, w_ref, o_ref, tail_ref, *, col_chunk):
    h = _bf(_rms(x_ref[...], g_ref[...]))
    m = w_ref.shape[1]
    for c in range(0, m - LANES, col_chunk):
        e = min(c + col_chunk, m - LANES)
        o_ref[:, c:e] = _dot(h, w_ref[:, c:e])
    tail = _dot(h, w_ref[:, m - LANES:m])
    o_ref[:, m - LANES:m] = tail
    tail_ref[...] = tail


def proj(x, g, w, tm):
    n, d = x.shape
    m = w.shape[1]
    return pl.pallas_call(
        functools.partial(_proj_kernel, col_chunk=512),
        grid=(n // tm,),
        in_specs=[pl.BlockSpec((tm, d), lambda i: (i, 0)), _full((1, d)), _full((d, m))],
        out_specs=[pl.BlockSpec((tm, m), lambda i: (i, 0)), pl.BlockSpec((tm, LANES), lambda i: (i, 0))],
        out_shape=[jax.ShapeDtypeStruct((n, m), F32), jax.ShapeDtypeStruct((n, LANES), F32)],
        compiler_params=_params("parallel"),
        name="proj",
    )(x, g.reshape(1, d), w)


def _memkv_kernel(x_ref, g_ref, wk_ref, wv_ref, kg_ref, k_ref, v_ref):
    h = _bf(_rms(x_ref[...], g_ref[0]))
    k = _dot(h, wk_ref[0])
    v_ref[0] = _dot(h, wv_ref[0])
    kg = kg_ref[0]
    for hd in range(MEM_HEADS):
        sl = slice(hd * MEM_HEAD_DIM, (hd + 1) * MEM_HEAD_DIM)
        k_ref[0, :, sl] = _rms(k[:, sl], kg)


def mem_kv(mem2d, mem_norm, w_mk, w_mv, mk_gain, tm=256):
    n, d = mem2d.shape
    depth = w_mk.shape[0]
    mw = w_mk.shape[2]
    return pl.pallas_call(
        _memkv_kernel,
        grid=(depth, n // tm),
        in_specs=[pl.BlockSpec((tm, d), lambda l, i: (i, 0)),
                  pl.BlockSpec((1, 1, d), lambda l, i: (l, 0, 0)),
                  pl.BlockSpec((1, d, mw), lambda l, i: (l, 0, 0)),
                  pl.BlockSpec((1, d, mw), lambda l, i: (l, 0, 0)),
                  pl.BlockSpec((1, 1, MEM_HEAD_DIM), lambda l, i: (l, 0, 0))],
        out_specs=[pl.BlockSpec((1, tm, mw), lambda l, i: (l, i, 0)),
                   pl.BlockSpec((1, tm, mw), lambda l, i: (l, i, 0))],
        out_shape=[jax.ShapeDtypeStruct((depth, n, mw), F32)] * 2,
        compiler_params=_params("parallel", "parallel"),
        name="mem_kv",
    )(mem2d, mem_norm.reshape(depth, 1, d), _bf(w_mk), _bf(w_mv), mk_gain.reshape(depth, 1, MEM_HEAD_DIM))


def _memattn_kernel(x_ref, o_ref, wo_ref, g_ref, wq_ref, qg_ref, mk_ref, mv_ref, wmo_ref, y_ref):
    x1 = x_ref[...] + _dot(o_ref[...], wo_ref[...])
    h = _bf(_rms(x1, g_ref[...]))
    q = _dot(h, wq_ref[...])
    qg = qg_ref[...]
    scale = MEM_HEAD_DIM ** -0.5
    nsub = mk_ref.shape[0]
    rows = x1.shape[0] // nsub
    cells = [(hd, s) for hd in range(MEM_HEADS) for s in range(nsub)]
    hsl = lambda hd: slice(hd * MEM_HEAD_DIM, (hd + 1) * MEM_HEAD_DIM)
    qh = [_bf(_rms(q[:, hsl(hd)], qg)) for hd in range(MEM_HEADS)]
    logits = [_dot_nt(qh[hd][s * rows:(s + 1) * rows], _bf(mk_ref[s, :, hsl(hd)])) * scale for hd, s in cells]
    ps = [jnp.exp(l - jnp.max(l, axis=-1, keepdims=True)) for l in logits]
    ps = [_bf(p / jnp.sum(p, axis=-1, keepdims=True)) for p in ps]
    pv = [_bf(_dot(p, _bf(mv_ref[s, :, hsl(hd)]))) for p, (hd, s) in zip(ps, cells)]
    outs = []
    for hd in range(MEM_HEADS):
        subs = pv[hd * nsub:(hd + 1) * nsub]
        outs.append(subs[0] if nsub == 1 else jnp.concatenate(subs, axis=0))
    att = jnp.concatenate(outs, axis=-1)
    y_ref[...] = x1 + _dot(att, wmo_ref[...])


def mem_attend(x, o, w_out, g, w_mq, mq_gain, mk, mv, w_mo, tm, seqlen):
    n, d = x.shape
    mlen, mw = mk.shape[1], mk.shape[2]
    tiles_per_seq = max(1, seqlen // tm)
    seqs_per_tile = max(1, tm // seqlen)
    return pl.pallas_call(
        _memattn_kernel,
        grid=(n // tm,),
        in_specs=[pl.BlockSpec((tm, d), lambda i: (i, 0)),
                  pl.BlockSpec((tm, o.shape[1]), lambda i: (i, 0)),
                  _full(w_out.shape), _full((1, d)), _full(w_mq.shape), _full((1, MEM_HEAD_DIM)),
                  pl.BlockSpec((seqs_per_tile, mlen, mw), lambda i: (i // tiles_per_seq, 0, 0)),
                  pl.BlockSpec((seqs_per_tile, mlen, mw), lambda i: (i // tiles_per_seq, 0, 0)),
                  _full(w_mo.shape)],
        out_specs=pl.BlockSpec((tm, d), lambda i: (i, 0)),
        out_shape=jax.ShapeDtypeStruct((n, d), F32),
        compiler_params=_params("parallel"),
        name="mem_attend",
    )(x, o, w_out, g.reshape(1, d), w_mq, mq_gain.reshape(1, MEM_HEAD_DIM), mk, mv, w_mo)


def _ffn_kernel(x_ref, g_ref, w1_ref, w3_ref, w2_ref, y_ref, *, hid_chunk):
    x = x_ref[...]
    h = _bf(_rms(x, g_ref[...]))
    hidden = w1_ref.shape[1]
    y_ref[...] = x
    for c in range(0, hidden, hid_chunk):
        a = _dot(h, w1_ref[:, c:c + hid_chunk])
        b = _dot(h, w3_ref[:, c:c + hid_chunk])
        y_ref[...] += _dot(_bf(_silu(a) * b), w2_ref[c:c + hid_chunk, :])


def ffn(x, g, w1, w3, w2, tm):
    n, d = x.shape
    hidden = w1.shape[1]
    return pl.pallas_call(
        functools.partial(_ffn_kernel, hid_chunk=256),
        grid=(n // tm,),
        in_specs=[pl.BlockSpec((tm, d), lambda i: (i, 0)), _full((1, d)),
                  _full((d, hidden)), _full((d, hidden)), _full((hidden, d))],
        out_specs=pl.BlockSpec((tm, d), lambda i: (i, 0)),
        out_shape=jax.ShapeDtypeStruct((n, d), F32),
        compiler_params=_params("parallel"),
        name="ffn",
    )(x, g.reshape(1, d), w1, w3, w2)


def _proj_dsa_kernel(x_ref, g_ref, w_ref, qg_ref, kg_ref, kig_ref, bd_ref,
                     q_o, k_o, kb_o, v_o, vb_o, qi_o, kiwi_o, *maybe_kit_o, tstate):
    h = _bf(_rms(x_ref[...], g_ref[...]))
    tm = x_ref.shape[0]
    hq = A_HEADS * A_HEAD_DIM
    bd = bd_ref[...]
    inv_hd = 1.0 / A_HEAD_DIM
    nw = bd.shape[0]

    def head_norm(p, gain):
        hi, mid = _split2(p * p)
        ss = _dot(hi, bd) + _dot(mid, bd)
        return p * lax.rsqrt(ss * inv_hd + EPS) * gain

    def transposed(x):
        return jnp.concatenate([x[r:r + LANES, :].T for r in range(0, tm, LANES)], axis=1)

    def store_state(o_ref, x, col0):
        if not tstate:
            o_ref[:, col0:col0 + x.shape[1]] = x
            return
        for j in range(0, x.shape[1], LANES):
            xt = transposed(x[:, j:j + LANES])
            hd = (col0 + j) // A_HEAD_DIM
            for half in range(LANES // A_HEAD_DIM):
                o_ref[0, hd + half] = xt[half * A_HEAD_DIM:(half + 1) * A_HEAD_DIM]

    step = 512
    for c in range(0, hq, step):
        pq = _dot(h, w_ref[:, c:c + step])
        pk = _dot(h, w_ref[:, hq + c:hq + c + step])
        pv = _dot(h, w_ref[:, 2 * hq + c:2 * hq + c + step])
        for j in range(0, step, nw):
            sl = slice(c + j, c + j + nw)
            qn = head_norm(pq[:, j:j + nw], qg_ref[:, sl])
            q_o[:, sl] = _bf(qn * QK_SCALE)
            kn = head_norm(pk[:, j:j + nw], kg_ref[:, sl])
            store_state(k_o, kn, c + j)
            kb_o[:, sl] = _bf(kn)
        store_state(v_o, pv, c)
        vb_o[:, c:c + step] = _bf(pv)
    qiw = IDX_HEADS * LANES
    for c in range(0, qiw, step):
        qi_o[:, c:c + step] = _bf(_dot(h, w_ref[:, 3 * hq + c:3 * hq + c + step]))
    p = _dot(h, w_ref[:, 3 * hq + qiw:3 * hq + qiw + LANES])
    lane = lax.broadcasted_iota(I32, p.shape, 1)
    is_ki = lane < IDX_DIM
    ss = jnp.sum(jnp.where(is_ki, p * p, 0.0), axis=-1, keepdims=True)
    kin = p * lax.rsqrt(ss * (1.0 / IDX_DIM) + EPS) * kig_ref[...]
    kiwi = jnp.where(is_ki, kin, p)
    kiwi_o[...] = kiwi
    if tstate:
        maybe_kit_o[0][0] = transposed(kiwi)[0:IDX_DIM]


def proj_dsa(x, g, wa, q_gain, k_gain, ki_gain, tm, nseq, tstate):
    n, d = x.shape
    hq = A_HEADS * A_HEAD_DIM
    m = wa.shape[1]
    seqlen = n // nseq
    tiles = seqlen // tm
    qg = jnp.tile(q_gain, A_HEADS).reshape(1, hq)
    kg = jnp.tile(k_gain, A_HEADS).reshape(1, hq)
    kig = jnp.concatenate([ki_gain, jnp.ones((LANES - IDX_DIM,), F32)]).reshape(1, LANES)
    r = jnp.arange(2 * LANES)
    bd = _bf((r[:, None] // A_HEAD_DIM) == (r[None, :] // A_HEAD_DIM))
    row = lambda w: pl.BlockSpec((tm, w), lambda i: (i, 0))
    if tstate:
        state = pl.BlockSpec((1, A_HEADS, A_HEAD_DIM, tm), lambda i: (i // tiles, 0, 0, i % tiles))
        state_shape = jax.ShapeDtypeStruct((nseq, A_HEADS, A_HEAD_DIM, seqlen), F32)
        extra_specs = [pl.BlockSpec((1, IDX_DIM, tm), lambda i: (i // tiles, 0, i % tiles))]
        extra_shapes = [jax.ShapeDtypeStruct((nseq, IDX_DIM, seqlen), F32)]
    else:
        state, state_shape, extra_specs, extra_shapes = row(hq), jax.ShapeDtypeStruct((n, hq), F32), [], []
    return pl.pallas_call(
        functools.partial(_proj_dsa_kernel, tstate=tstate),
        grid=(n // tm,),
        in_specs=[row(d), _full((1, d)), _full((d, m)), _full((1, hq)), _full((1, hq)),
                  _full((1, LANES)), _full(bd.shape)],
        out_specs=[row(hq), state, row(hq), state, row(hq), row(IDX_HEADS * LANES), row(LANES)] + extra_specs,
        out_shape=[jax.ShapeDtypeStruct((n, hq), BF16), state_shape,
                   jax.ShapeDtypeStruct((n, hq), BF16), state_shape,
                   jax.ShapeDtypeStruct((n, hq), BF16),
                   jax.ShapeDtypeStruct((n, IDX_HEADS * LANES), BF16),
                   jax.ShapeDtypeStruct((n, LANES), F32)] + extra_shapes,
        compiler_params=_params("parallel"),
        name="proj_dsa",
    )(x, g.reshape(1, d), wa, qg, kg, kig, bd)


def dsa_weight(w_in):
    hq = A_HEADS * A_HEAD_DIM
    o3 = 3 * hq
    o4 = o3 + IDX_HEADS * IDX_DIM
    d = w_in.shape[0]
    wqi = w_in[:, o3:o4].reshape(d, IDX_HEADS, IDX_DIM)
    wqi = jnp.pad(wqi, ((0, 0), (0, 0), (0, LANES - IDX_DIM))).reshape(d, IDX_HEADS * LANES)
    tail = jnp.pad(w_in[:, o4:], ((0, 0), (0, LANES - (w_in.shape[1] - o4))))
    return _bf(jnp.concatenate([w_in[:, :o3], wqi, tail], axis=1))


def _sortable(s):
    b = pltpu.bitcast(s, I32)
    b = jnp.where(b == jnp.int32(-2 ** 31), 0, b)
    return jnp.where(b < 0, b ^ jnp.int32(0x7FFFFFFF), b)


def _index_mask_kernel(qi_ref, wit_ref, *rest, nref, groups, ltot, topk, causal, case_rows):
    ki_refs = rest[:nref]
    o_ref = rest[nref]
    key_ref, sel_ref, jv_ref = rest[nref + 1:]
    tq = qi_ref.shape[0]
    i = pl.program_id(1)
    lpad = sel_ref.shape[0]
    wit = wit_ref[0]
    if case_rows:
        ncase = (jnp.maximum((i + 1) * tq, topk) + case_rows - 1) // case_rows
        used_rows = ncase * case_rows
    else:
        used_rows = None

    def score_group(dst0, rows, sources):
        accs = []
        for rp, lead, src0, width, _ in sources:
            ref = ki_refs[rp]
            ki = _bf(ref[src0:src0 + rows, :] if lead is None else ref[lead, src0:src0 + rows, :])
            acc = jnp.zeros((rows, tq), F32)
            for hd in range(IDX_HEADS):
                rel = _dot_nt(ki, qi_ref[:, hd * LANES:hd * LANES + width])
                acc = acc + wit[hd:hd + 1, :] * jnp.maximum(rel, 0.0)
            accs.append(acc)
        if len(accs) == 1:
            acc = accs[0]
        else:
            lane = lax.broadcasted_iota(I32, (rows, tq), 1)
            acc = jnp.where(lane < tq // 2, accs[0], accs[1])
        s = acc * ((IDX_DIM ** -0.5) * (IDX_HEADS ** -0.5))
        if causal:
            kpos = dst0 + lax.broadcasted_iota(I32, (rows, tq), 0)
            qpos = i * tq + lax.broadcasted_iota(I32, (rows, tq), 1)
            s = jnp.where((kpos >> CHUNK_SHIFT) <= (qpos >> CHUNK_SHIFT), s, NEG)
        key_ref[dst0:dst0 + rows, :] = _sortable(s)

    for dst0, rows, sources in groups:
        if used_rows is None:
            score_group(dst0, rows, sources)
        else:
            pl.when(dst0 < used_rows)(functools.partial(score_group, dst0, rows, sources))

    def select(nrows):
        idx_bits = max(1, (nrows - 1).bit_length())

        def count(pred):
            c = pred.astype(I32).reshape(COUNT_CHAINS, nrows // COUNT_CHAINS, tq)
            return jnp.sum(jnp.sum(c, axis=1), axis=0, keepdims=True)

        def ge_count(cand):
            return count(key_ref[0:nrows, :] >= cand)

        t0 = jnp.full((1, tq), -2 ** 31, I32)
        t = jnp.where(ge_count(jnp.zeros((1, tq), I32)) >= topk, 0, t0)

        def vbody(it, t):
            cand = t + (jnp.int32(1) << (30 - it))
            return jnp.where(ge_count(cand) >= topk, cand, t)

        t = lax.fori_loop(0, 31, vbody, t)
        keys = key_ref[0:nrows, :]
        gt = keys > t
        eq = keys == t
        need = topk - count(gt)
        rowi = lax.broadcasted_iota(I32, (nrows, tq), 0)

        def jbody(it, jv):
            cand = jv + (jnp.int32(1) << (idx_bits - 1 - it))
            below = count(eq & (rowi < cand))
            return jnp.where(below < need, cand, jv)

        jv_ref[...] = jnp.full((1, tq), nrows, I32)
        has_tie = jnp.max(count(eq) - need) > 0

        @pl.when(has_tie)
        def _():
            jv_ref[...] = lax.fori_loop(0, idx_bits, jbody, jnp.zeros((1, tq), I32))

        sel = gt | (eq & (rowi <= jv_ref[...]))
        if causal:
            qpos = i * tq + lax.broadcasted_iota(I32, (nrows, tq), 1)
            sel = sel & ((rowi >> CHUNK_SHIFT) <= (qpos >> CHUNK_SHIFT))
        sel_ref[0:nrows, :] = jnp.where(sel, 0.0, NEG)
        nreal = -(-nrows // TILE)
        if nreal * TILE > nrows:
            sel_ref[nrows:nreal * TILE, :] = jnp.full((nreal * TILE - nrows, tq), NEG, F32)
        for kt in range(lpad // TILE):
            if kt < nreal:
                o_ref[0, 0, kt] = _bf(sel_ref[kt * TILE:(kt + 1) * TILE, :].T)
            else:
                o_ref[0, 0, kt] = jnp.full((tq, TILE), NEG, BF16)

    if case_rows:
        for k in range(ltot // case_rows):
            pl.when(ncase == k + 1)(functools.partial(select, (k + 1) * case_rows))
    else:
        select(ltot)


def index_mask(qi, wit, ki_arrays, ki_specs, groups, ltot, nstep, nqb, causal):
    lpad = -(-ltot // KTILE) * KTILE
    nt = lpad // TILE
    topk = min(TOPK_MAX, ltot // 4)
    case_rows = MASK_CASE_ROWS if (causal and ltot % MASK_CASE_ROWS == 0 and ltot > MASK_CASE_ROWS) else 0
    kern = functools.partial(_index_mask_kernel, nref=len(ki_arrays), groups=tuple(groups), ltot=ltot,
                             topk=topk, causal=causal, case_rows=case_rows)
    return pl.pallas_call(
        kern,
        grid=(nstep, nqb),
        in_specs=[pl.BlockSpec((TILE, qi.shape[1]), lambda b, i: (b * nqb + i, 0)),
                  pl.BlockSpec((1, IDX_HEADS, TILE), lambda b, i: (b * nqb + i, 0, 0))] + list(ki_specs),
        out_specs=pl.BlockSpec((1, 1, nt, TILE, TILE), lambda b, i: (b, i, 0, 0, 0)),
        out_shape=jax.ShapeDtypeStruct((nstep, nqb, nt, TILE, TILE), BF16),
        scratch_shapes=[pltpu.VMEM((ltot, TILE), I32), pltpu.VMEM((lpad, TILE), F32),
                        pltpu.VMEM((1, TILE), I32)],
        compiler_params=_params("parallel", "parallel"),
        name="index_mask",
    )(qi, wit, *ki_arrays)


def _row_groups(row0, rows, make_sources):
    out = []
    for off in range(0, rows, SCORE_ROWS):
        out.append((row0 + off, min(SCORE_ROWS, rows - off), tuple(make_sources(off))))
    return out


def _attn_core(q_ref, o_ref, qs_ref, l_ref, mx_ref, acc_ref, groups, tq):
    npairs = A_HEADS // 2
    lane = lax.broadcasted_iota(I32, (1, LANES), 1)
    keep_lo = _bf((lane < A_HEAD_DIM).astype(F32))
    keep_hi = _bf((lane >= A_HEAD_DIM).astype(F32))
    for pair in range(npairs):
        qp = q_ref[:, pair * LANES:(pair + 1) * LANES]
        qs_ref[pair, 0:tq, :] = qp * keep_lo
        qs_ref[pair, tq:2 * tq, :] = qp * keep_hi
    mx_ref[...] = jnp.full(mx_ref.shape, NEG, F32)
    acc_ref[...] = jnp.zeros(acc_ref.shape, F32)
    ones = jnp.ones((KTILE, LANES), BF16)

    def over_tiles(count, body):
        if isinstance(count, int) and count == 1:
            body(0, 0)
        else:
            lax.fori_loop(0, count, body, 0)

    for count, base, k_tile, _, mask_sub, bias_sub in groups:
        def p1(kt, carry, base=base, k_tile=k_tile, mask_sub=mask_sub, bias_sub=bias_sub):
            masks = [mask_sub(kt, sub).astype(F32) for sub in range(KTILE // LANES)]
            for pair in range(npairs):
                sl = slice(pair * LANES, (pair + 1) * LANES)
                l = _dot_nt(qs_ref[pair], k_tile(kt, sl))
                for half in range(2):
                    rs = slice(half * tq, (half + 1) * tq)
                    mx = mx_ref[pair, rs, :]
                    for sub in range(KTILE // LANES):
                        cs = slice(sub * LANES, (sub + 1) * LANES)
                        blk = l[rs, cs] + (masks[sub] + bias_sub(kt, sub, 2 * pair + half))
                        l_ref[pair, base + kt, rs, cs] = blk
                        mx = jnp.maximum(mx, blk)
                    mx_ref[pair, rs, :] = mx
            return carry

        over_tiles(count, p1)

    for pair in range(npairs):
        m = jnp.max(mx_ref[pair], axis=-1, keepdims=True)
        mx_ref[pair] = jnp.broadcast_to(m, mx_ref.shape[1:])

    for count, base, _, v_tile, _, _ in groups:
        def p2(kt, carry, base=base, v_tile=v_tile):
            for pair in range(npairs):
                sl = slice(pair * LANES, (pair + 1) * LANES)
                m = mx_ref[pair]
                p = jnp.concatenate(
                    [jnp.exp2(l_ref[pair, base + kt, :, sub * LANES:(sub + 1) * LANES] - m)
                     for sub in range(KTILE // LANES)], axis=1)
                vaug = jnp.concatenate([v_tile(kt, sl), ones], axis=1)
                acc_ref[pair] += _dot(_bf(p), vaug)
            return carry

        over_tiles(count, p2)

    lane_full = lax.broadcasted_iota(I32, (tq, LANES), 1)
    for pair in range(npairs):
        a = acc_ref[pair]
        o = a[:, 0:LANES] / a[:, LANES:2 * LANES]
        o_ref[:, pair * LANES:(pair + 1) * LANES] = _bf(jnp.where(lane_full < A_HEAD_DIM, o[0:tq], o[tq:2 * tq]))


def _attn_scratch(tq, ntiles):
    npairs = A_HEADS // 2
    return [pltpu.VMEM((npairs, 2 * tq, LANES), BF16),
            pltpu.VMEM((npairs, ntiles, 2 * tq, KTILE), F32),
            pltpu.VMEM((npairs, 2 * tq, LANES), F32),
            pltpu.VMEM((npairs, 2 * tq, 2 * LANES), F32)]


def _attn_prompt_kernel(q_ref, k_ref, v_ref, mask_ref, bias_ref, o_ref, qs_ref, l_ref, mx_ref, acc_ref):
    i = pl.program_id(1)
    per = KTILE // TILE

    def rows(kt):
        return pl.ds(pl.multiple_of(kt * KTILE, KTILE), KTILE)

    def bias_sub(kt, sub, hd):
        s = kt * per + sub
        sel = jnp.where(s == i, 2, jnp.where(s == i - 1, 1, 0))
        return bias_ref[sel, hd]

    group = (i // per + 1, 0,
             lambda kt, sl: k_ref[rows(kt), sl],
             lambda kt, sl: v_ref[rows(kt), sl],
             lambda kt, sub: mask_ref[0, 0, kt * per + sub],
             bias_sub)
    _attn_core(q_ref, o_ref, qs_ref, l_ref, mx_ref, acc_ref, [group], TILE)


def attn_prompt(q, kb, vb, mask, bias, nseq, seqlen):
    n, hq = q.shape
    nqb = seqlen // TILE
    nt = mask.shape[2]
    return pl.pallas_call(
        _attn_prompt_kernel,
        grid=(nseq, nqb),
        in_specs=[pl.BlockSpec((TILE, hq), lambda b, i: (b * nqb + i, 0)),
                  pl.BlockSpec((seqlen, hq), lambda b, i: (b, 0)),
                  pl.BlockSpec((seqlen, hq), lambda b, i: (b, 0)),
                  pl.BlockSpec((1, 1, nt, TILE, TILE), lambda b, i: (b, i, 0, 0, 0)),
                  _full(bias.shape)],
        out_specs=pl.BlockSpec((TILE, hq), lambda b, i: (b * nqb + i, 0)),
        out_shape=jax.ShapeDtypeStruct((n, hq), BF16),
        scratch_shapes=_attn_scratch(TILE, seqlen // KTILE),
        compiler_params=_params("parallel", "arbitrary"),
        name="attn_prompt",
    )(q, kb, vb, mask, bias)


def _attn_sample_kernel(q_ref, kc_ref, vc_ref, kn_ref, vn_ref, mask_ref, bias_ref, o_ref,
                        l_ref, mx_ref, acc_ref, *, nblk):
    ph = pl.program_id(1)
    b = pl.program_id(2)
    tq = q_ref.shape[0]
    per = KTILE // TILE
    tiles_per_blk = kc_ref.shape[3] // KTILE
    ncache = nblk * tiles_per_blk
    last_sub = ncache * per - 1
    heads = range(A_HEADS)

    def head_tile(ref, c0, hd):
        return _bf(ref[0, hd, :, c0:c0 + KTILE])

    def pass1(slot, ref, c0, mask_sub, bias_sub):
        ls = [_dot(q_ref[:, hd * A_HEAD_DIM:(hd + 1) * A_HEAD_DIM], head_tile(ref, c0, hd))
              for hd in heads]
        masks = [mask_sub(sub).astype(F32) for sub in range(per)]
        for hd in heads:
            mx = mx_ref[hd]
            for sub in range(per):
                cs = slice(sub * LANES, (sub + 1) * LANES)
                blk = ls[hd][:, cs] + (masks[sub] + bias_sub(sub, hd))
                l_ref[hd, slot, :, cs] = blk
                mx = jnp.maximum(mx, blk)
            mx_ref[hd] = mx

    def pass2(slot, ref, c0):
        ones = jnp.ones((LANES - A_HEAD_DIM, KTILE), BF16)
        for hd in heads:
            m = mx_ref[hd]
            p = jnp.concatenate([jnp.exp2(l_ref[hd, slot, :, sub * LANES:(sub + 1) * LANES] - m)
                                 for sub in range(per)], axis=1)
            vaug = jnp.concatenate([head_tile(ref, c0, hd), ones], axis=0)
            acc_ref[hd] += _dot_nt(_bf(p), vaug)

    @pl.when((ph == 0) & (b == 0))
    def _():
        mx_ref[...] = jnp.full(mx_ref.shape, NEG, F32)
        acc_ref[...] = jnp.zeros(acc_ref.shape, F32)

    @pl.when(ph == 0)
    def _():
        for t in range(tiles_per_blk):
            g = b * tiles_per_blk + t
            pass1(g, kc_ref, t * KTILE,
                  lambda sub, g=g: mask_ref[0, 0, g * per + sub],
                  lambda sub, hd, g=g: bias_ref[jnp.where(g * per + sub == last_sub, 1, 0), hd])

    @pl.when((ph == 0) & (b == nblk - 1))
    def _():
        pass1(ncache, kn_ref, 0,
              lambda sub: mask_ref[0, 0, ncache * per + sub],
              lambda sub, hd: bias_ref[2 if sub == 0 else 0, hd])
        for hd in heads:
            m = jnp.max(mx_ref[hd], axis=-1, keepdims=True)
            mx_ref[hd] = jnp.broadcast_to(m, mx_ref.shape[1:])

    @pl.when(ph == 1)
    def _():
        for t in range(tiles_per_blk):
            pass2(b * tiles_per_blk + t, vc_ref, t * KTILE)

    @pl.when((ph == 1) & (b == nblk - 1))
    def _():
        pass2(ncache, vn_ref, 0)
        for pair in range(A_HEADS // 2):
            halves = []
            for hd in (2 * pair, 2 * pair + 1):
                a = acc_ref[hd]
                halves.append(a[:, 0:A_HEAD_DIM] / a[:, A_HEAD_DIM:2 * A_HEAD_DIM])
            o_ref[:, pair * LANES:(pair + 1) * LANES] = _bf(jnp.concatenate(halves, axis=1))


def attn_sample(q, kc, vc, kn, vn, mask, bias, nseq, tq):
    n, hq = q.shape
    past = kc.shape[3]
    cblk = min(CACHE_BLOCK, past)
    nblk = past // cblk
    ncache = past // KTILE
    nt = mask.shape[2]
    per = TILE // tq
    cache_k = pl.BlockSpec((1, A_HEADS, A_HEAD_DIM, cblk),
                           lambda s, ph, b: (s, 0, 0, b * (1 - ph) + (nblk - 1) * ph))
    cache_v = pl.BlockSpec((1, A_HEADS, A_HEAD_DIM, cblk), lambda s, ph, b: (s, 0, 0, b * ph))
    fresh = pl.BlockSpec((1, A_HEADS, A_HEAD_DIM, KTILE), lambda s, ph, b: (s, 0, 0, 0))
    return pl.pallas_call(
        functools.partial(_attn_sample_kernel, nblk=nblk),
        grid=(nseq, 2, nblk),
        in_specs=[pl.BlockSpec((tq, hq), lambda s, ph, b: (s, 0)),
                  cache_k, cache_v, fresh, fresh,
                  pl.BlockSpec((1, 1, nt, tq, TILE), lambda s, ph, b: (s // per, 0, 0, s % per, 0)),
                  pl.BlockSpec(bias.shape, lambda s, ph, b: (0, 0, 0, 0))],
        out_specs=pl.BlockSpec((tq, hq), lambda s, ph, b: (s, 0)),
        out_shape=jax.ShapeDtypeStruct((n, hq), BF16),
        scratch_shapes=[pltpu.VMEM((A_HEADS, ncache + 1, tq, KTILE), F32),
                        pltpu.VMEM((A_HEADS, tq, LANES), F32),
                        pltpu.VMEM((A_HEADS, tq, LANES), F32)],
        compiler_params=_params("parallel", "arbitrary", "arbitrary"),
        name="attn_sample",
    )(q, kc, vc, kn, vn, mask, bias)


def _t5_bucket(rel):
    half = N_BUCKETS // 2
    max_exact = half // 2
    n = jnp.abs(rel)
    nf = jnp.maximum(n, 1).astype(F32)
    large = max_exact + (jnp.log(nf / max_exact) / math.log(MAX_DISTANCE / max_exact)
                         * (half - max_exact)).astype(I32)
    large = jnp.minimum(large, half - 1)
    return jnp.where(rel > 0, half, 0) + jnp.where(n < max_exact, n, large)


def bias_tiles(rel_bias):
    heads = rel_bias.shape[1]
    span = 2 * TILE - 1

    def toeplitz(shift):
        rel = jnp.arange(span, dtype=I32) - (TILE - 1) + shift
        tab = rel_bias.astype(F32)[_t5_bucket(rel)].T
        strip = jnp.tile(jnp.pad(tab, ((0, 0), (0, 1))), (1, TILE))[:, :TILE * span]
        return strip.reshape(heads, TILE, span)[:, :, TILE - 1:]

    far = rel_bias.astype(F32)[_t5_bucket(jnp.full((1,), -(TILE + 1), I32))]
    far = jnp.broadcast_to(far.T[:, :, None], (heads, TILE, TILE))
    return jnp.stack([far, toeplitz(-TILE), toeplitz(0)]) * LOG2E


def dsa_core_prompt(pr, nseq, seqlen, bias):
    q, _, kb, _, vb, qi, kiwi = pr
    nqb = seqlen // TILE
    wit = kiwi[:, IDX_DIM:IDX_DIM + IDX_HEADS].reshape(nseq * nqb, TILE, IDX_HEADS).swapaxes(1, 2)
    groups = _row_groups(0, seqlen, lambda off: [(0, None, off, LANES, None)])
    mask = index_mask(qi, wit, [kiwi], [pl.BlockSpec((seqlen, LANES), lambda b, i: (b, 0))],
                      groups, seqlen, nseq, nqb, causal=True)
    return attn_prompt(q, kb, vb, mask, bias, nseq, seqlen)


def dsa_core_sample(pr, k_cache, v_cache, ki_cache, nseq, tq, bias):
    q, kf, _, vf, _, qi, kiwi = pr
    past = k_cache.shape[1]
    per = TILE // tq
    wit = kiwi[:, IDX_DIM:IDX_DIM + IDX_HEADS].reshape(nseq // per, TILE, IDX_HEADS).swapaxes(1, 2)
    groups = (_row_groups(0, past, lambda off: [(0, j, off, IDX_DIM, j) for j in range(per)])
              + _row_groups(past, tq, lambda off: [(1, None, j * tq + off, LANES, j) for j in range(per)]))
    mask = index_mask(
        qi, wit, [ki_cache, kiwi],
        [pl.BlockSpec((per, past, IDX_DIM), lambda b, i: (b, 0, 0)),
         pl.BlockSpec((TILE, LANES), lambda b, i: (b, 0))],
        groups, past + tq, nseq // per, 1, causal=False)
    to_hdk = lambda a: a.transpose(0, 2, 3, 1)
    pad_keys = ((0, 0), (0, 0), (0, 0), (0, KTILE - tq))
    kn = jnp.pad(to_hdk(kf.reshape(nseq, tq, A_HEADS, A_HEAD_DIM)), pad_keys)
    vn = jnp.pad(to_hdk(vf.reshape(nseq, tq, A_HEADS, A_HEAD_DIM)), pad_keys)
    return attn_sample(q, to_hdk(k_cache), to_hdk(v_cache), kn, vn, mask, bias[:, :, :tq, :], nseq, tq)


def _pair_triangles():
    r = jnp.arange(2 * CHUNK)[:, None]
    c = jnp.arange(2 * CHUNK)[None, :]
    same = (r // CHUNK) == (c // CHUNK)
    return _bf(same & (c <= r)), _bf(same & (r <= c))


def _log_sigmoid(x):
    return jnp.minimum(x, 0.0) - jnp.log1p(jnp.exp(-jnp.abs(x)))


def _mlstm_kernel(p_ref, gs_ref, gr_ref, gbc_ref, gbr_ref, hg_ref, tl_ref, tu_ref, c0_ref, n0_ref, m0_ref,
                  hs_ref, c_ref, n_ref, m_ref, cs, ns, ms):
    c_id = pl.program_id(1)
    nh = B_HEADS
    npair = nh // 2
    hw = LANES
    nsq = p_ref.shape[0]
    L = p_ref.shape[1]
    L2 = 2 * L

    @pl.when(c_id == 0)
    def _():
        cs[...] = jnp.zeros(cs.shape, F32)
        ns[...] = jnp.zeros(ns.shape, F32)
        for sq in range(nsq):
            ns[sq, :, 0:B_QK_DIM] = n0_ref[sq]
            for hd in range(nh):
                cs[sq, hd // 2, (hd % 2) * hw:(hd % 2) * hw + B_QK_DIM, :] = c0_ref[sq, hd]
        ms[...] = m0_ref[...]

    r = lax.broadcasted_iota(I32, (L2, L2), 0)
    c = lax.broadcasted_iota(I32, (L2, L2), 1)
    same = (r >> CHUNK_SHIFT) == (c >> CHUNK_SHIFT)
    incl = same & (c <= r)
    tri_l = tl_ref[...]
    tri_u = tu_ref[...]
    rowc = lax.broadcasted_iota(I32, (L2, 1), 0)
    is_top = rowc < L
    top = is_top.astype(F32)
    bot = 1.0 - top
    row2 = lax.broadcasted_iota(I32, (2 * hw, 1), 0)
    hg = hg_ref[...]

    def stack(base, sq, pr):
        a = base + 2 * pr * hw
        return jnp.concatenate([p_ref[sq, :, a:a + hw], p_ref[sq, :, a + hw:a + 2 * hw]], axis=0)

    def bd(x):
        return _bf(jnp.concatenate([x * top, x * bot], axis=1))

    seqs = range(nsq)
    gs = [gs_ref[sq, 0] + gbc_ref[...] for sq in seqs]
    gr = [gr_ref[sq, 0] + gbr_ref[...] for sq in seqs]
    bcol = [_dot_exact_lhs01(tri_l, _log_sigmoid(gs[sq][:, npair:])) for sq in seqs]
    brow = [_dot_exact_rhs01(_log_sigmoid(gr[sq][npair:, :]), tri_u) for sq in seqs]
    mall = [ms[sq] for sq in seqs]
    cells = [(sq, pr) for sq in seqs for pr in range(npair)]
    ids = range(len(cells))
    qf = [stack(0, sq, pr) for sq, pr in cells]
    kf = [stack(nh * hw, sq, pr) * (B_QK_DIM ** -0.5) for sq, pr in cells]
    vb = [_bf(stack(2 * nh * hw, sq, pr)) for sq, pr in cells]
    bc = [bcol[sq][:, pr:pr + 1] for sq, pr in cells]
    ic = [gs[sq][:, pr:pr + 1] for sq, pr in cells]
    mc = [mall[sq][:, pr:pr + 1] for sq, pr in cells]
    d = [jnp.where(incl, bc[i] - brow[sq][pr:pr + 1, :] + gr[sq][pr:pr + 1, :], NEG)
         for i, (sq, pr) in enumerate(cells)]
    inter = [bc[i] + mc[i] for i in ids]
    mt = [jnp.maximum(inter[i], jnp.max(d[i], axis=-1, keepdims=True)) for i in ids]
    s = [_dot_nt(_bf(qf[i]), _bf(kf[i])) * jnp.exp(d[i] - mt[i]) for i in ids]
    wst = [jnp.exp(inter[i] - mt[i]) for i in ids]
    cmat = [cs[sq, pr] for sq, pr in cells]
    num = [_dot(_bf(s[i]), vb[i]) + wst[i] * _dot(bd(qf[i]), _bf(cmat[i])) for i in ids]
    for i, (sq, pr) in enumerate(cells):
        n0 = ns[sq, 2 * pr:2 * pr + 1, :]
        n1 = ns[sq, 2 * pr + 1:2 * pr + 2, :]
        qn = jnp.sum(qf[i] * jnp.where(is_top, n0, n1), axis=-1, keepdims=True)
        den = jnp.sum(s[i], axis=-1, keepdims=True) + wst[i] * qn
        hs = num[i] / jnp.maximum(jnp.abs(den), jnp.exp(-mt[i]))
        bl0 = bc[i][L - 1:L, :]
        bl1 = bc[i][L2 - 1:L2, :]
        bl = jnp.where(is_top, bl0, bl1)
        dec = bl - bc[i] + ic[i]
        blm = bl + mc[i]
        mnew0 = jnp.maximum(blm[0:1, :], jnp.max(dec[0:L], axis=0, keepdims=True))
        mnew1 = jnp.maximum(blm[L:L + 1, :], jnp.max(dec[L:L2], axis=0, keepdims=True))
        mnew = jnp.where(is_top, mnew0, mnew1)
        wk = jnp.exp(dec - mnew)
        ws = jnp.exp(blm - mnew)
        kw = kf[i] * wk
        ws2 = jnp.where(row2 < hw, ws[0:1, :], ws[L:L + 1, :])
        cs[sq, pr] = ws2 * cmat[i] + _dot_tn(bd(kw), vb[i])
        ns[sq, 2 * pr:2 * pr + 1, :] = ws[0:1, :] * n0 + jnp.sum(kw[0:L], axis=0, keepdims=True)
        ns[sq, 2 * pr + 1:2 * pr + 2, :] = ws[L:L + 1, :] * n1 + jnp.sum(kw[L:L2], axis=0, keepdims=True)
        ms[sq, :, pr:pr + 1] = mnew
        on = _bf(_rms(hs, hg) * _sigmoid(stack(3 * nh * hw, sq, pr)))
        oa = 2 * pr * B_V_DIM
        hs_ref[sq, :, oa:oa + B_V_DIM] = on[0:L]
        hs_ref[sq, :, oa + B_V_DIM:oa + 2 * B_V_DIM] = on[L:L2]

    @pl.when(c_id == pl.num_programs(1) - 1)
    def _():
        for sq in range(nsq):
            for hd in range(nh):
                c_ref[sq, hd] = cs[sq, hd // 2, (hd % 2) * hw:(hd % 2) * hw + B_QK_DIM, :]
            n_ref[sq] = ns[sq, :, 0:B_QK_DIM]
        m_ref[...] = ms[...]


def mlstm_weight(w_in):
    d = w_in.shape[0]
    nq = B_HEADS * B_QK_DIM
    nv = B_HEADS * B_V_DIM

    def padh(w):
        w = w.reshape(d, B_HEADS, B_QK_DIM)
        return jnp.pad(w, ((0, 0), (0, 0), (0, LANES - B_QK_DIM))).reshape(d, B_HEADS * LANES)

    main = 2 * nq + 2 * nv
    tail = jnp.pad(w_in[:, main:], ((0, 0), (0, LANES - 2 * B_HEADS)))
    return _bf(jnp.concatenate([padh(w_in[:, :nq]), padh(w_in[:, nq:2 * nq]), w_in[:, 2 * nq:main], tail], axis=1))


def mlstm_core(p, gates, gate_bias, h_gain, c0, n0, m0, nseq, nchunk):
    n, m = p.shape
    nh = B_HEADS
    npair = nh // 2
    gcol = 4 * nh * LANES
    nsq = SCAN_SEQS if nseq % SCAN_SEQS == 0 else 1
    seqlen = n // nseq
    pre = gates[:, :2 * nh].reshape(nseq, nchunk, CHUNK, 2, npair, 2)
    gs = pre.transpose(0, 1, 5, 2, 3, 4).reshape(nseq, nchunk, 2 * CHUNK, 2 * npair)
    gr = pre.transpose(0, 1, 3, 4, 5, 2).reshape(nseq, nchunk, 2 * npair, 2 * CHUNK)
    gb = gate_bias.reshape(2, npair, 2)
    gbc = jnp.repeat(gb.transpose(2, 0, 1).reshape(2, 2 * npair), CHUNK, axis=0)
    m0s = jnp.repeat(m0.reshape(nseq, npair, 2).transpose(0, 2, 1), CHUNK, axis=1)
    o, c_new, n_new, m_new = pl.pallas_call(
        _mlstm_kernel,
        grid=(nseq // nsq, nchunk),
        in_specs=[pl.BlockSpec((nsq, CHUNK, m), lambda s, c: (s, c, 0)),
                  pl.BlockSpec((nsq, 1, 2 * CHUNK, 2 * npair), lambda s, c: (s, c, 0, 0)),
                  pl.BlockSpec((nsq, 1, 2 * npair, 2 * CHUNK), lambda s, c: (s, c, 0, 0)),
                  _full((2 * CHUNK, 2 * npair)), _full((2 * npair, 2 * CHUNK)), _full((1, B_V_DIM)),
                  _full((2 * CHUNK, 2 * CHUNK)), _full((2 * CHUNK, 2 * CHUNK)),
                  pl.BlockSpec((nsq, nh, B_QK_DIM, B_V_DIM), lambda s, c: (s, 0, 0, 0)),
                  pl.BlockSpec((nsq, nh, B_QK_DIM), lambda s, c: (s, 0, 0)),
                  pl.BlockSpec((nsq, 2 * CHUNK, npair), lambda s, c: (s, 0, 0))],
        out_specs=[pl.BlockSpec((nsq, CHUNK, nh * B_V_DIM), lambda s, c: (s, c, 0)),
                   pl.BlockSpec((nsq, nh, B_QK_DIM, B_V_DIM), lambda s, c: (s, 0, 0, 0)),
                   pl.BlockSpec((nsq, nh, B_QK_DIM), lambda s, c: (s, 0, 0)),
                   pl.BlockSpec((nsq, 2 * CHUNK, npair), lambda s, c: (s, 0, 0))],
        out_shape=[jax.ShapeDtypeStruct((nseq, seqlen, nh * B_V_DIM), BF16),
                   jax.ShapeDtypeStruct((nseq, nh, B_QK_DIM, B_V_DIM), F32),
                   jax.ShapeDtypeStruct((nseq, nh, B_QK_DIM), F32),
                   jax.ShapeDtypeStruct((nseq, 2 * CHUNK, npair), F32)],
        scratch_shapes=[pltpu.VMEM((nsq, npair, 2 * LANES, B_V_DIM), F32), pltpu.VMEM((nsq, nh, LANES), F32),
                        pltpu.VMEM((nsq, 2 * CHUNK, npair), F32)],
        compiler_params=_params("parallel", "arbitrary"),
        name="mlstm",
    )(p.reshape(nseq, seqlen, m), gs, gr, gbc, gbc.T, h_gain.reshape(1, B_V_DIM), *_pair_triangles(),
      c0, n0, m0s)
    m_heads = m_new[:, ::CHUNK, :].transpose(0, 2, 1).reshape(nseq, nh)
    return o.reshape(n, nh * B_V_DIM), c_new, n_new, m_heads


def _split2(x):
    hi = _bf(x)
    return hi, _bf(x - hi.astype(F32))


def _cat3_lhs(x):
    hi, mid = _split2(x)
    return jnp.concatenate([hi, hi, mid], axis=1)


def _cat3_rhs(x):
    hi, mid = _split2(x)
    return jnp.concatenate([hi, mid, hi], axis=0)


def _gdn_kernel(p_ref, gs_ref, gr_ref, cw_ref, alc_ref, alr_ref, dtc_ref, dtr_ref, og_ref, tl_ref, tu_ref,
                s0_ref, cb0_ref, o_ref, s_ref, cb_ref, ss, tail):
    c_id = pl.program_id(1)
    nh = C_HEADS
    cdim = nh * (2 * C_DK + C_DV)
    L = p_ref.shape[0]
    L2 = 2 * L
    nprev = CONV_W - 1

    @pl.when(c_id == 0)
    def _():
        ss[...] = s0_ref[0]
        tail[...] = jnp.zeros(tail.shape, F32)
        tail[8 - nprev:8, :] = cb0_ref[0]

    x = p_ref[:, 0:cdim]
    ext = jnp.concatenate([tail[...], x], axis=0)
    conv = ext[8:8 + L] * cw_ref[CONV_W - 1:CONV_W, :]
    for j in range(CONV_W - 1):
        conv = conv + ext[8 - nprev + j:8 - nprev + j + L] * cw_ref[j:j + 1, :]
    tail[...] = x[L - 8:L, :]
    cf = _silu(conv)

    npair = nh // 2
    gs = gs_ref[0]
    gr = gr_ref[0]
    beta = _sigmoid(gs[:, 0:npair])
    g_col = -jnp.exp(alc_ref[...]) * _softplus(gs[:, npair:] + dtc_ref[...])
    g_row = -jnp.exp(alr_ref[...]) * _softplus(gr[npair:, :] + dtr_ref[...])
    r = lax.broadcasted_iota(I32, (L2, L2), 0)
    c = lax.broadcasted_iota(I32, (L2, L2), 1)
    same = (r >> CHUNK_SHIFT) == (c >> CHUNK_SHIFT)
    incl = same & (c <= r)
    strict = same & (c < r)
    eye = (c == r).astype(F32)
    gc_col = _dot_exact_lhs01(tl_ref[...], g_col)
    gc_row = _dot_exact_rhs01(g_row, tu_ref[...])
    rowc = lax.broadcasted_iota(I32, (L2, 1), 0)
    top = (rowc < L).astype(F32)
    bot = 1.0 - top
    row2 = lax.broadcasted_iota(I32, (2 * C_DK, 1), 0)
    og = og_ref[...]

    def stack(base, pr):
        a = base + 2 * pr * C_DK
        return jnp.concatenate([cf[:, a:a + C_DK], cf[:, a + C_DK:a + 2 * C_DK]], axis=0)

    def bd(x):
        return _bf(jnp.concatenate([x * top, x * bot], axis=1))

    pairs = range(npair)
    qc, kc, kcb, dm, amat, rhs, gcols, egcs = [], [], [], [], [], [], [], []
    for pr in pairs:
        qraw = stack(0, pr)
        kraw = stack(nh * C_DK, pr)
        vc = stack(2 * nh * C_DK, pr)
        qc.append(qraw * lax.rsqrt(jnp.sum(qraw * qraw, axis=-1, keepdims=True) + EPS) * (C_DK ** -0.5))
        kc.append(kraw * lax.rsqrt(jnp.sum(kraw * kraw, axis=-1, keepdims=True) + EPS))
        bc = beta[:, pr:pr + 1]
        gcol = gc_col[:, pr:pr + 1]
        grow = gc_row[pr:pr + 1, :]
        dm.append(jnp.where(incl, jnp.exp(jnp.where(incl, gcol - grow, 0.0)), 0.0))
        kb = kc[pr] * bc
        kcb.append(_bf(kc[pr]))
        amat.append(jnp.where(strict, _dot_nt(_bf(kb), kcb[pr]) * dm[pr], 0.0))
        egc = jnp.exp(gcol)
        rhs.append(jnp.concatenate([vc * bc, kb * egc], axis=-1))
        gcols.append(gcol)
        egcs.append(egc)
    tinv = [eye - amat[pr] for pr in pairs]
    pw_l = [_cat3_lhs(-amat[pr]) for pr in pairs]
    pw_r = [_cat3_rhs(-amat[pr]) for pr in pairs]
    for _ in range(CHUNK_SHIFT - 1):
        pw = [_dot(pw_l[pr], pw_r[pr]) for pr in pairs]
        pw_l = [_cat3_lhs(pw[pr]) for pr in pairs]
        pw_r = [_cat3_rhs(pw[pr]) for pr in pairs]
        tinv = [tinv[pr] + _dot(_cat3_lhs(tinv[pr]), pw_r[pr]) for pr in pairs]
    sol = [_dot(_cat3_lhs(tinv[pr]), _cat3_rhs(rhs[pr])) for pr in pairs]
    attn = [_dot_nt(_bf(qc[pr]), kcb[pr]) * dm[pr] for pr in pairs]
    smat = [ss[pr] for pr in pairs]
    sb = [_bf(smat[pr]) for pr in pairs]
    vnew = [sol[pr][:, :C_DV] - _dot(bd(sol[pr][:, C_DV:]), sb[pr]) for pr in pairs]
    o = [_dot(bd(qc[pr] * egcs[pr]), sb[pr]) + _dot(_bf(attn[pr]), _bf(vnew[pr])) for pr in pairs]
    for pr in pairs:
        gl0 = gcols[pr][L - 1:L, :]
        gl1 = gcols[pr][L2 - 1:L2, :]
        ke = kc[pr] * jnp.exp(jnp.where(rowc < L, gl0, gl1) - gcols[pr])
        decay = jnp.exp(jnp.where(row2 < C_DK, gl0, gl1))
        ss[pr] = smat[pr] * decay + _dot_tn(bd(ke), _bf(vnew[pr]))
    for pr in pairs:
        za = cdim + 2 * pr * C_DV
        z = jnp.concatenate([p_ref[:, za:za + C_DV], p_ref[:, za + C_DV:za + 2 * C_DV]], axis=0)
        on = _bf(_rms(o[pr], og) * _silu(z))
        oa = 2 * pr * C_DV
        o_ref[:, oa:oa + C_DV] = on[0:L]
        o_ref[:, oa + C_DV:oa + 2 * C_DV] = on[L:L2]

    @pl.when(c_id == pl.num_programs(1) - 1)
    def _():
        s_ref[0] = ss[...]
        cb_ref[0] = tail[8 - nprev:8, :]


def gdn_weight(w_in):
    main = C_HEADS * (2 * C_DK + C_DV) + C_HEADS * C_DV
    tail = jnp.pad(w_in[:, main:], ((0, 0), (0, LANES - 2 * C_HEADS)))
    return _bf(jnp.concatenate([w_in[:, :main], tail], axis=1))


def gdn_core(p, gates, conv_w, a_log, dt_bias, o_gain, s0, cb0, nseq, nchunk):
    n, m = p.shape
    nh = C_HEADS
    npair = nh // 2
    cdim = nh * (2 * C_DK + C_DV)
    gcolumn = cdim + nh * C_DV
    pre = gates[:, :2 * nh].reshape(nseq * nchunk, CHUNK, 2, npair, 2)
    gs = pre.transpose(0, 4, 1, 2, 3).reshape(nseq * nchunk, 2 * CHUNK, 2 * npair)
    gr = pre.transpose(0, 2, 3, 4, 1).reshape(nseq * nchunk, 2 * npair, 2 * CHUNK)

    def col(v):
        return jnp.repeat(v.reshape(npair, 2).T, CHUNK, axis=0)

    o, s_new, cb_new = pl.pallas_call(
        _gdn_kernel,
        grid=(nseq, nchunk),
        in_specs=[pl.BlockSpec((CHUNK, m), lambda s, c: (s * nchunk + c, 0)),
                  pl.BlockSpec((1, 2 * CHUNK, 2 * npair), lambda s, c: (s * nchunk + c, 0, 0)),
                  pl.BlockSpec((1, 2 * npair, 2 * CHUNK), lambda s, c: (s * nchunk + c, 0, 0)),
                  _full((CONV_W, cdim)),
                  _full((2 * CHUNK, npair)), _full((npair, 2 * CHUNK)),
                  _full((2 * CHUNK, npair)), _full((npair, 2 * CHUNK)),
                  _full((1, C_DV)), _full((2 * CHUNK, 2 * CHUNK)), _full((2 * CHUNK, 2 * CHUNK)),
                  pl.BlockSpec((1, npair, 2 * C_DK, C_DV), lambda s, c: (s, 0, 0, 0)),
                  pl.BlockSpec((1, CONV_W - 1, cdim), lambda s, c: (s, 0, 0))],
        out_specs=[pl.BlockSpec((CHUNK, nh * C_DV), lambda s, c: (s * nchunk + c, 0)),
                   pl.BlockSpec((1, npair, 2 * C_DK, C_DV), lambda s, c: (s, 0, 0, 0)),
                   pl.BlockSpec((1, CONV_W - 1, cdim), lambda s, c: (s, 0, 0))],
        out_shape=[jax.ShapeDtypeStruct((n, nh * C_DV), BF16),
                   jax.ShapeDtypeStruct((nseq, npair, 2 * C_DK, C_DV), F32),
                   jax.ShapeDtypeStruct((nseq, CONV_W - 1, cdim), F32)],
        scratch_shapes=[pltpu.VMEM((npair, 2 * C_DK, C_DV), F32), pltpu.VMEM((8, cdim), F32)],
        compiler_params=_params("parallel", "arbitrary"),
        name="gdn",
    )(p, gs, gr, conv_w, col(a_log), col(a_log).T, col(dt_bias), col(dt_bias).T, o_gain.reshape(1, C_DV),
      *_pair_triangles(),
      s0.reshape(nseq, npair, 2 * C_DK, C_DV), cb0)
    return o, s_new.reshape(nseq, nh, C_DK, C_DV), cb_new


def _trunk(x, nseq, seqlen, mem_k, mem_v, st, W, bias, is_prompt):
    d = x.shape[-1]
    n = nseq * seqlen
    x = x.reshape(n, d)
    tm = min(PROJ_ROWS, n)
    tm_mem = min(MEM_ROWS, n)
    new = {}
    for i in range(4):
        kind = i % 3
        mx = W["mixer"][i]
        if kind == 0:
            tstate = is_prompt and seqlen % tm == 0
            pr = proj_dsa(x, W["norm_mix"][i], mx["w_in"], mx["q_gain"], mx["k_gain"], mx["ki_gain"], tm,
                          nseq, tstate)
            if is_prompt:
                o = dsa_core_prompt(pr[:7], nseq, seqlen, bias)
            else:
                o = dsa_core_sample(pr, *st[i], nseq, seqlen, bias)
            if tstate:
                new[i] = (pr[1].transpose(0, 3, 1, 2), pr[3].transpose(0, 3, 1, 2), pr[7].transpose(0, 2, 1))
            else:
                new[i] = (pr[1].reshape(nseq, seqlen, A_HEADS, A_HEAD_DIM),
                          pr[3].reshape(nseq, seqlen, A_HEADS, A_HEAD_DIM),
                          pr[6][:, :IDX_DIM].reshape(nseq, seqlen, IDX_DIM))
        elif kind == 1:
            p, gates = proj(x, W["norm_mix"][i], mx["w_in"], tm)
            o, c_new, n_new, m_new = mlstm_core(p, gates, mx["gate_bias"], mx["h_gain"], *st[i],
                                                nseq, seqlen // CHUNK)
            new[i] = (c_new, n_new, m_new.reshape(nseq, B_HEADS))
        else:
            p, gates = proj(x, W["norm_mix"][i], mx["w_in"], tm)
            o, s_new, cb_new = gdn_core(p, gates, mx["conv_w"], mx["a_log"], mx["dt_bias"], mx["o_gain"], *st[i],
                                        nseq, seqlen // CHUNK)
            new[i] = (s_new, cb_new)
        x = mem_attend(x, o, mx["w_out"], W["norm_mem"][i], W["w_mq"][i], W["mq_gain"][i],
                       mem_k[i], mem_v[i], W["w_mo"][i], tm_mem, seqlen)
        x = ffn(x, W["norm_ffn"][i], W["w_ffn1"][i], W["w_ffn3"][i], W["w_ffn2"][i], min(FFN_ROWS, n))
    return x.reshape(nseq, seqlen, d), new


def kernel(x_prompt, x_sample, mem_prompt, cache_l0_k, cache_l0_v, cache_l0_kidx, state_l1_C, state_l1_n, state_l1_m, state_l2_S, state_l2_conv, cache_l3_k, cache_l3_v, cache_l3_kidx, cache_mem_k, cache_mem_v, rel_bias, norm_mix, norm_mem, norm_ffn, mem_norm, w_mq, w_mk, w_mv, w_mo, mq_gain, mk_gain, w_ffn1, w_ffn3, w_ffn2, a0_w_in, a0_w_out, a0_q_gain, a0_k_gain, a0_kidx_gain, b1_w_in, b1_gate_bias, b1_h_gain, b1_w_out, c2_w_in, c2_conv_w, c2_a_log, c2_dt_bias, c2_o_gain, c2_w_out, a3_w_in, a3_w_out, a3_q_gain, a3_k_gain, a3_kidx_gain):
    B, T, D = x_prompt.shape
    S, Ts, _ = x_sample.shape
    depth = w_mq.shape[0]
    mlen = mem_prompt.shape[1]
    mw = MEM_HEADS * MEM_HEAD_DIM

    def dsa_w(w_in, w_out, qg, kg, kig):
        return dict(w_in=dsa_weight(w_in), w_out=_bf(w_out), q_gain=qg, k_gain=kg, ki_gain=kig)

    W = dict(
        norm_mix=norm_mix, norm_mem=norm_mem, norm_ffn=norm_ffn,
        w_mq=_bf(w_mq), w_mo=_bf(w_mo), mq_gain=mq_gain,
        w_ffn1=_bf(w_ffn1), w_ffn3=_bf(w_ffn3), w_ffn2=_bf(w_ffn2),
        mixer={
            0: dsa_w(a0_w_in, a0_w_out, a0_q_gain, a0_k_gain, a0_kidx_gain),
            1: dict(w_in=mlstm_weight(b1_w_in), gate_bias=b1_gate_bias, h_gain=b1_h_gain, w_out=_bf(b1_w_out)),
            2: dict(w_in=gdn_weight(c2_w_in), conv_w=c2_conv_w, a_log=c2_a_log, dt_bias=c2_dt_bias,
                    o_gain=c2_o_gain, w_out=_bf(c2_w_out)),
            3: dsa_w(a3_w_in, a3_w_out, a3_q_gain, a3_k_gain, a3_kidx_gain),
        },
    )
    bias = bias_tiles(rel_bias)

    mk_p, mv_p = mem_kv(mem_prompt.reshape(B * mlen, D), mem_norm, w_mk, w_mv, mk_gain)
    mk_p = mk_p.reshape(depth, B, mlen, mw)
    mv_p = mv_p.reshape(depth, B, mlen, mw)
    st_p = {
        0: None,
        1: (jnp.zeros((B, B_HEADS, B_QK_DIM, B_V_DIM), F32), jnp.zeros((B, B_HEADS, B_QK_DIM), F32),
            jnp.full((B, B_HEADS), NEG, F32)),
        2: (jnp.zeros((B, C_HEADS, C_DK, C_DV), F32), jnp.zeros((B, CONV_W - 1, state_l2_conv.shape[-1]), F32)),
        3: None,
    }
    y_p, np_ = _trunk(x_prompt, B, T, mk_p, mv_p, st_p, W, bias, True)

    st_s = {
        0: (cache_l0_k, cache_l0_v, cache_l0_kidx),
        1: (state_l1_C, state_l1_n, state_l1_m),
        2: (state_l2_S, state_l2_conv),
        3: (cache_l3_k, cache_l3_v, cache_l3_kidx),
    }
    mk_s = cache_mem_k.reshape(depth, S, mlen, mw)
    mv_s = cache_mem_v.reshape(depth, S, mlen, mw)
    y_s, ns_ = _trunk(x_sample, S, Ts, mk_s, mv_s, st_s, W, bias, False)

    shp = (depth, B, mlen, MEM_HEADS, MEM_HEAD_DIM)
    return (y_p, y_s,
            *np_[0], *np_[1], *np_[2], *np_[3], mk_p.reshape(shp), mv_p.reshape(shp),
            *ns_[0], *ns_[1], *ns_[2], *ns_[3])
```
